```python
import jax, jax.numpy as jnp
from jax import lax
import numpy as np

D_MODEL = 2048
BATCH = 1
SEQ = 8192
DEPTH = 4

N_MIXERS = 3
EPS = 1e-6
ROPE_THETA = 10000.0
N_HEADS = 16
HEAD_DIM = 128
N_KV_HEADS = 4
IDX_HEADS = 16
IDX_DIM = 128
IDX_ROPE_DIM = 64
TOPK_MAX = 256
Q_BLOCK = 128
A_SPLITS = [N_HEADS * HEAD_DIM, N_KV_HEADS * HEAD_DIM, N_KV_HEADS * HEAD_DIM, IDX_HEADS * IDX_DIM, IDX_DIM]
A_IN = sum(A_SPLITS) + IDX_HEADS
A_CUTS = [int(c) for c in np.cumsum(A_SPLITS)]
LRU_WIDTH = D_MODEL
LRU_BLOCKS = 8
LRU_BLOCK = LRU_WIDTH // LRU_BLOCKS
CONV_WIDTH = 4
LRU_C = 8.0
POOL_WINDOWS = (2, 4, 8, 16)
POOL_GROUPS = len(POOL_WINDOWS)
POOL_GROUP = D_MODEL // POOL_GROUPS
D_FF = 4 * D_MODEL

kernel_name = 'hybrid_dsa_rglru_pool_trunk'

F32 = jnp.float32


def rms_norm(x, g):
    xf = x.astype(F32)
    y = xf * lax.rsqrt(jnp.mean(xf * xf, axis=-1, keepdims=True) + EPS)
    return (y * g.astype(F32)).astype(x.dtype)


def rope_tables(pos, dim):
    inv = 1.0 / (ROPE_THETA ** (jnp.arange(0, dim, 2, dtype=F32) / dim))
    ang = pos.astype(F32)[..., None] * inv
    return jnp.cos(ang), jnp.sin(ang)


def apply_rope(x, cos, sin):
    xf = x.astype(F32)
    x1, x2 = jnp.split(xf, 2, axis=-1)
    c = cos[:, :, None, :]
    s = sin[:, :, None, :]
    return jnp.concatenate([x1 * c - x2 * s, x2 * c + x1 * s], axis=-1).astype(x.dtype)


def partial_rope(x, cos, sin):
    return jnp.concatenate([apply_rope(x[..., :IDX_ROPE_DIM], cos, sin), x[..., IDX_ROPE_DIM:]], axis=-1)


def dsa_mixer(h, pos, w_in, q_norm, k_norm, w_out):
    B, T, _ = h.shape
    topk = min(TOPK_MAX, T // 4)
    proj = h @ w_in
    q, k, v, iq, ik, iw = jnp.split(proj, A_CUTS, axis=-1)
    q = rms_norm(q.reshape(B, T, N_HEADS, HEAD_DIM), q_norm)
    k = rms_norm(k.reshape(B, T, N_KV_HEADS, HEAD_DIM), k_norm)
    v = v.reshape(B, T, N_KV_HEADS, HEAD_DIM)
    cos, sin = rope_tables(pos, HEAD_DIM)
    q = apply_rope(q, cos, sin)
    k = apply_rope(k, cos, sin)
    icos, isin = rope_tables(pos, IDX_ROPE_DIM)
    iq = partial_rope(iq.reshape(B, T, IDX_HEADS, IDX_DIM), icos, isin)
    ik = partial_rope(ik.reshape(B, T, 1, IDX_DIM), icos, isin)[:, :, 0]
    iw = iw * IDX_HEADS ** -0.5
    n_blk = T // Q_BLOCK
    key_idx = jnp.arange(T, dtype=jnp.int32)
    ik_f = ik.astype(F32)
    group = N_HEADS // N_KV_HEADS

    def to_blocks(a):
        return jnp.moveaxis(a.reshape((B, n_blk, Q_BLOCK) + a.shape[2:]), 1, 0)

    def block_fn(args):
        qb, iqb, iwb, tb = args
        logits = jnp.einsum('bqhd,bsd->bqhs', iqb.astype(F32), ik_f) * IDX_DIM ** -0.5
        score = jnp.einsum('bqh,bqhs->bqs', iwb.astype(F32), jax.nn.relu(logits))
        causal = key_idx[None, None, :] <= tb[None, :, None]
        score = jnp.where(causal, score, -jnp.inf)
        _, sel = lax.top_k(score, topk)
        valid = sel <= tb[None, :, None]
        kg = jax.vmap(lambda kk, ii: kk[ii])(k, sel)
        vg = jax.vmap(lambda vv, ii: vv[ii])(v, sel)
        qg = qb.reshape(B, Q_BLOCK, N_KV_HEADS, group, HEAD_DIM)
        s = jnp.einsum('bqgrd,bqkgd->bqgrk', qg, kg).astype(F32) * HEAD_DIM ** -0.5
        s = jnp.where(valid[:, :, None, None, :], s, -jnp.inf)
        p = jax.nn.softmax(s, axis=-1).astype(vg.dtype)
        o = jnp.einsum('bqgrk,bqkgd->bqgrd', p, vg)
        return o.reshape(B, Q_BLOCK, N_HEADS * HEAD_DIM)

    out = lax.map(block_fn, (to_blocks(q), to_blocks(iq), to_blocks(iw), key_idx.reshape(n_blk, Q_BLOCK)))
    out = jnp.moveaxis(out, 0, 1).reshape(B, T, N_HEADS * HEAD_DIM)
    return out @ w_out


def rglru_mixer(h, w_in, conv_w, conv_b, gate_a_w, gate_a_b, gate_x_w, gate_x_b, lam, w_out):
    B, T, _ = h.shape
    proj = h @ w_in
    gate, xr = jnp.split(proj, 2, axis=-1)
    xp = jnp.pad(xr, ((0, 0), (CONV_WIDTH - 1, 0), (0, 0)))
    xc = conv_b + xp[:, 0:T] * conv_w[0]
    for j in range(1, CONV_WIDTH):
        xc = xc + xp[:, j:j + T] * conv_w[j]
    xb = xc.reshape(B, T, LRU_BLOCKS, LRU_BLOCK)
    r = jax.nn.sigmoid(jnp.einsum('btnc,ncd->btnd', xb, gate_a_w).reshape(B, T, LRU_WIDTH) + gate_a_b)
    i = jax.nn.sigmoid(jnp.einsum('btnc,ncd->btnd', xb, gate_x_w).reshape(B, T, LRU_WIDTH) + gate_x_b)
    log_a = -LRU_C * r.astype(F32) * jax.nn.softplus(-lam.astype(F32))
    a = jnp.exp(log_a)
    mult = jnp.sqrt(-jnp.expm1(2.0 * log_a))
    b = xc.astype(F32) * i.astype(F32) * mult

    def combine(left, right):
        a1, b1 = left
        a2, b2 = right
        return a1 * a2, a2 * b1 + b2

    _, hs = lax.associative_scan(combine, (a, b), axis=1)
    y = hs.astype(h.dtype) * jax.nn.gelu(gate)
    return y @ w_out


def pool_mixer(h, w_group, b_group, scale):
    B, T, D = h.shape
    hf = h.astype(F32)
    csum = jnp.concatenate([jnp.zeros((B, 1, D), F32), jnp.cumsum(hf, axis=1)], axis=1)
    t1 = jnp.arange(1, T + 1, dtype=F32)[None, :, None]
    outs = []
    for g, w in enumerate(POOL_WINDOWS):
        sl = slice(g * POOL_GROUP, (g + 1) * POOL_GROUP)
        c = csum[:, :, sl]
        lag = jnp.concatenate([jnp.zeros((B, w - 1, POOL_GROUP), F32), c[:, :T + 1 - w]], axis=1)
        mean = (c[:, 1:] - lag) / jnp.minimum(t1, float(w))
        outs.append(mean - hf[:, :, sl])
    y = jnp.stack(outs, axis=2).astype(h.dtype)
    y = jnp.einsum('btgc,gcd->btgd', y, w_group) + b_group
    return y.reshape(B, T, D) * scale


def channel_mlp(h, w_up, w_down):
    return jnp.square(jax.nn.relu(h @ w_up)) @ w_down


def setup_inputs(seed: int = 0) -> dict:
    key = jax.random.key(seed)
    ks = jax.random.split(key, 32)
    n_a = (DEPTH + 2) // 3
    n_b = (DEPTH + 1) // 3
    n_c = DEPTH // 3
    nrm = jax.random.normal

    def gain(k, n, d):
        return 1.0 + 0.02 * nrm(k, (n, d), F32)

    x = nrm(ks[0], (BATCH, SEQ, D_MODEL), F32)
    positions = jnp.broadcast_to(jnp.arange(SEQ, dtype=jnp.int32)[None, :], (BATCH, SEQ))
    u = jax.random.uniform(ks[14], (n_b, LRU_WIDTH), F32, 0.9, 0.999)
    a0 = u ** (1.0 / LRU_C)
    lam = jnp.log(a0) - jnp.log1p(-a0)
    return {
        'x': x,
        'positions': positions,
        'attn_norm': gain(ks[1], n_a, D_MODEL),
        'attn_w_in': nrm(ks[2], (n_a, D_MODEL, A_IN), F32) * D_MODEL ** -0.5,
        'attn_q_norm': gain(ks[3], n_a, HEAD_DIM),
        'attn_k_norm': gain(ks[4], n_a, HEAD_DIM),
        'attn_w_out': nrm(ks[5], (n_a, N_HEADS * HEAD_DIM, D_MODEL), F32) * (N_HEADS * HEAD_DIM) ** -0.5,
        'rnn_norm': gain(ks[6], n_b, D_MODEL),
        'rnn_w_in': nrm(ks[7], (n_b, D_MODEL, 2 * LRU_WIDTH), F32) * D_MODEL ** -0.5,
        'rnn_conv_w': nrm(ks[8], (n_b, CONV_WIDTH, LRU_WIDTH), F32) * CONV_WIDTH ** -0.5,
        'rnn_conv_b': 0.02 * nrm(ks[9], (n_b, LRU_WIDTH), F32),
        'rnn_gate_a_w': nrm(ks[10], (n_b, LRU_BLOCKS, LRU_BLOCK, LRU_BLOCK), F32) * LRU_BLOCK ** -0.5,
        'rnn_gate_a_b': 0.02 * nrm(ks[11], (n_b, LRU_WIDTH), F32),
        'rnn_gate_x_w': nrm(ks[12], (n_b, LRU_BLOCKS, LRU_BLOCK, LRU_BLOCK), F32) * LRU_BLOCK ** -0.5,
        'rnn_gate_x_b': 0.02 * nrm(ks[13], (n_b, LRU_WIDTH), F32),
        'rnn_lambda': lam,
        'rnn_w_out': nrm(ks[15], (n_b, LRU_WIDTH, D_MODEL), F32) * LRU_WIDTH ** -0.5,
        'pool_norm': gain(ks[16], n_c, D_MODEL),
        'pool_w': nrm(ks[17], (n_c, POOL_GROUPS, POOL_GROUP, POOL_GROUP), F32) * POOL_GROUP ** -0.5,
        'pool_b': 0.02 * nrm(ks[18], (n_c, POOL_GROUPS, POOL_GROUP), F32),
        'pool_scale': 1.0 + 0.1 * nrm(ks[19], (n_c, D_MODEL), F32),
        'mlp_norm': gain(ks[20], DEPTH, D_MODEL),
        'mlp_w_up': nrm(ks[21], (DEPTH, D_MODEL, D_FF), F32) * D_MODEL ** -0.5,
        'mlp_w_down': nrm(ks[22], (DEPTH, D_FF, D_MODEL), F32) * D_FF ** -0.5,
    }


def reference(x, positions, attn_norm, attn_w_in, attn_q_norm, attn_k_norm, attn_w_out,
              rnn_norm, rnn_w_in, rnn_conv_w, rnn_conv_b, rnn_gate_a_w, rnn_gate_a_b,
              rnn_gate_x_w, rnn_gate_x_b, rnn_lambda, rnn_w_out,
              pool_norm, pool_w, pool_b, pool_scale,
              mlp_norm, mlp_w_up, mlp_w_down):
    for i in range(DEPTH):
        kind, j = i % N_MIXERS, i // N_MIXERS
        if kind == 0:
            x = x + dsa_mixer(rms_norm(x, attn_norm[j]), positions, attn_w_in[j],
                              attn_q_norm[j], attn_k_norm[j], attn_w_out[j])
        elif kind == 1:
            x = x + rglru_mixer(rms_norm(x, rnn_norm[j]), rnn_w_in[j], rnn_conv_w[j], rnn_conv_b[j],
                                rnn_gate_a_w[j], rnn_gate_a_b[j], rnn_gate_x_w[j], rnn_gate_x_b[j],
                                rnn_lambda[j], rnn_w_out[j])
        else:
            x = x + pool_mixer(rms_norm(x, pool_norm[j]), pool_w[j], pool_b[j], pool_scale[j])
        x = x + channel_mlp(rms_norm(x, mlp_norm[i]), mlp_w_up[i], mlp_w_down[i])
    return x
```

```python
import functools
import math

import jax
import jax.numpy as jnp
from jax import lax
from jax.experimental import pallas as pl
from jax.experimental.pallas import tpu as pltpu

F32 = jnp.float32
MXU_DTYPE = jnp.bfloat16

N_MIXERS = 3
EPS = 1e-6
ROPE_THETA = 10000.0
HEAD_DIM = 128
N_HEADS = 16
N_KV_HEADS = 4
IDX_HEADS = 16
IDX_DIM = 128
IDX_ROPE_DIM = 64
TOPK_MAX = 256
CONV_WIDTH = 4
LRU_C = 8.0
POOL_WINDOWS = (2, 4, 8, 16)

LANES = 128
SUBLANES = 8
VMEM_BYTES_V7X = 64 * 1024 * 1024
VMEM_CAP_BYTES = VMEM_BYTES_V7X - 8 * 1024 * 1024

INT_MIN = -(2 ** 31)
NEG_BIG = -1e30


def _vmem_limit(block_bytes):
    return int(min(VMEM_CAP_BYTES, block_bytes * 3 // 2 + (4 << 20)))


def _nbytes(shape, dtype):
    return math.prod(shape) * jnp.dtype(dtype).itemsize


def _rms_norm_rows(x, g):
    ms = jnp.mean(x * x, axis=-1, keepdims=True)
    return x * lax.rsqrt(ms + EPS) * g


def _tile_lanes(x, width):
    reps = width // LANES
    return x if reps == 1 else jnp.concatenate([x] * reps, axis=1)


def _norm_matmul_kernel(x_ref, g_ref, w_ref, o_ref, h_ref):
    @pl.when(pl.program_id(1) == 0)
    def _():
        h_ref[...] = _rms_norm_rows(x_ref[...], g_ref[...]).astype(h_ref.dtype)

    o_ref[...] = jnp.dot(h_ref[...], w_ref[...], preferred_element_type=F32).astype(o_ref.dtype)


def norm_matmul(x, g, w, *, tm, tn, out_dtype):
    t, d = x.shape
    n = w.shape[1]
    need = (2 * _nbytes((tm, d), F32) + 2 * _nbytes((d, tn), w.dtype)
            + 2 * _nbytes((tm, tn), out_dtype) + _nbytes((tm, d), w.dtype))
    return pl.pallas_call(
        _norm_matmul_kernel,
        grid=(t // tm, n // tn),
        in_specs=[
            pl.BlockSpec((tm, d), lambda i, j: (i, 0)),
            pl.BlockSpec((1, d), lambda i, j: (0, 0)),
            pl.BlockSpec((d, tn), lambda i, j: (0, j)),
        ],
        out_specs=pl.BlockSpec((tm, tn), lambda i, j: (i, j)),
        out_shape=jax.ShapeDtypeStruct((t, n), out_dtype),
        scratch_shapes=[pltpu.VMEM((tm, d), w.dtype)],
        compiler_params=pltpu.CompilerParams(
            dimension_semantics=("parallel", "arbitrary"),
            vmem_limit_bytes=_vmem_limit(need)),
        name="norm_matmul",
    )(x, g.reshape(1, d), w)


def _matmul_residual_kernel(a_ref, w_ref, x_ref, o_ref):
    o_ref[...] = x_ref[...] + jnp.dot(a_ref[...], w_ref[...], preferred_element_type=F32)


def matmul_residual(a, w, x, *, tm, tn):
    t, k = a.shape
    n = w.shape[1]
    need = (2 * _nbytes((tm, k), a.dtype) + 2 * _nbytes((k, tn), w.dtype)
            + 4 * _nbytes((tm, tn), F32))
    return pl.pallas_call(
        _matmul_residual_kernel,
        grid=(t // tm, n // tn),
        in_specs=[
            pl.BlockSpec((tm, k), lambda i, j: (i, 0)),
            pl.BlockSpec((k, tn), lambda i, j: (0, j)),
            pl.BlockSpec((tm, tn), lambda i, j: (i, j)),
        ],
        out_specs=pl.BlockSpec((tm, tn), lambda i, j: (i, j)),
        out_shape=jax.ShapeDtypeStruct((t, n), F32),
        compiler_params=pltpu.CompilerParams(
            dimension_semantics=("parallel", "arbitrary"),
            vmem_limit_bytes=_vmem_limit(need)),
        name="matmul_residual",
    )(a, w, x)


def _mlp_kernel(x_ref, g_ref, wu_ref, wd_ref, o_ref, h_ref, acc_ref):
    f = pl.program_id(1)

    @pl.when(f == 0)
    def _():
        h_ref[...] = _rms_norm_rows(x_ref[...], g_ref[...]).astype(h_ref.dtype)
        acc_ref[...] = jnp.zeros_like(acc_ref)

    u = jnp.dot(h_ref[...], wu_ref[...], preferred_element_type=F32)
    u = jnp.square(jnp.maximum(u, 0.0)).astype(wd_ref.dtype)
    acc_ref[...] += jnp.dot(u, wd_ref[...], preferred_element_type=F32)

    @pl.when(f == pl.num_programs(1) - 1)
    def _():
        o_ref[...] = x_ref[...] + acc_ref[...]


def mlp_block(x, g, w_up, w_down, *, tm, tf):
    t, d = x.shape
    ff = w_up.shape[1]
    need = (4 * _nbytes((tm, d), F32) + 2 * _nbytes((d, tf), w_up.dtype)
            + 2 * _nbytes((tf, d), w_down.dtype) + _nbytes((tm, d), w_up.dtype)
            + _nbytes((tm, d), F32) + 2 * _nbytes((tm, tf), F32))
    return pl.pallas_call(
        _mlp_kernel,
        grid=(t // tm, ff // tf),
        in_specs=[
            pl.BlockSpec((tm, d), lambda i, f: (i, 0)),
            pl.BlockSpec((1, d), lambda i, f: (0, 0)),
            pl.BlockSpec((d, tf), lambda i, f: (0, f)),
            pl.BlockSpec((tf, d), lambda i, f: (f, 0)),
        ],
        out_specs=pl.BlockSpec((tm, d), lambda i, f: (i, 0)),
        out_shape=jax.ShapeDtypeStruct((t, d), F32),
        scratch_shapes=[pltpu.VMEM((tm, d), w_up.dtype), pltpu.VMEM((tm, d), F32)],
        compiler_params=pltpu.CompilerParams(
            dimension_semantics=("parallel", "arbitrary"),
            vmem_limit_bytes=_vmem_limit(need)),
        name="mlp_block",
    )(x, g.reshape(1, d), w_up, w_down)


DSA_TN = 2 * LANES
Q_COLS = N_HEADS * HEAD_DIM
IQ_COLS = IDX_HEADS * IDX_DIM
KV_COLS = N_KV_HEADS * HEAD_DIM
DSA_COLS = Q_COLS + IQ_COLS + 2 * KV_COLS + IDX_DIM + LANES
Q_BLOCKS = Q_COLS // DSA_TN
IQ_BLOCKS = IQ_COLS // DSA_TN
KV_BLOCKS = KV_COLS // DSA_TN


def _dsa_proj_kernel(x_ref, g_ref, w_ref, qn_ref, kn_ref, rc_ref, rs_ref,
                     ic_ref, ia_ref, ib_ref, o_ref, iw_ref, h_ref):
    j = pl.program_id(1)

    @pl.when(j == 0)
    def _():
        h_ref[...] = _rms_norm_rows(x_ref[...], g_ref[...]).astype(h_ref.dtype)

    y = jnp.dot(h_ref[...], w_ref[...], preferred_element_type=F32)

    def rope(z):
        return z * rc_ref[...] + pltpu.roll(z, HEAD_DIM // 2, 1) * rs_ref[...]

    def idx_rope(z):
        half = IDX_ROPE_DIM // 2
        return (z * ic_ref[...] + pltpu.roll(z, LANES - half, 1) * ia_ref[...]
                + pltpu.roll(z, half, 1) * ib_ref[...])

    def store_heads(fn):
        o_ref[...] = jnp.concatenate(
            [fn(y[:, :LANES]), fn(y[:, LANES:])], axis=1).astype(o_ref.dtype)

    @pl.when(j < Q_BLOCKS)
    def _():
        scale = HEAD_DIM ** -0.5
        store_heads(lambda z: rope(_rms_norm_rows(z, qn_ref[...])) * scale)

    @pl.when((j >= Q_BLOCKS) & (j < Q_BLOCKS + IQ_BLOCKS))
    def _():
        store_heads(idx_rope)

    @pl.when((j >= Q_BLOCKS + IQ_BLOCKS) & (j < Q_BLOCKS + IQ_BLOCKS + KV_BLOCKS))
    def _():
        store_heads(lambda z: rope(_rms_norm_rows(z, kn_ref[...])))

    @pl.when((j >= Q_BLOCKS + IQ_BLOCKS + KV_BLOCKS) & (j < Q_BLOCKS + IQ_BLOCKS + 2 * KV_BLOCKS))
    def _():
        o_ref[...] = y.astype(o_ref.dtype)

    @pl.when(j == Q_BLOCKS + IQ_BLOCKS + 2 * KV_BLOCKS)
    def _():
        o_ref[...] = jnp.concatenate(
            [idx_rope(y[:, :LANES]), jnp.zeros_like(y[:, LANES:])], axis=1).astype(o_ref.dtype)
        iw_ref[...] = y[:, LANES:] * (IDX_HEADS ** -0.5 * IDX_DIM ** -0.5)


def dsa_projection(x, g, w, q_norm, k_norm, tables, *, tm):
    t, d = x.shape
    rc, rs, ic, ia, ib = tables
    tab_spec = pl.BlockSpec((tm, LANES), lambda i, j: (i, 0))
    vec_spec = pl.BlockSpec((1, LANES), lambda i, j: (0, 0))
    need = (2 * _nbytes((tm, d), F32) + 2 * _nbytes((d, DSA_TN), w.dtype)
            + 2 * _nbytes((tm, DSA_TN), w.dtype) + _nbytes((tm, d), w.dtype)
            + 12 * _nbytes((tm, LANES), F32))
    return pl.pallas_call(
        _dsa_proj_kernel,
        grid=(t // tm, DSA_COLS // DSA_TN),
        in_specs=[
            pl.BlockSpec((tm, d), lambda i, j: (i, 0)),
            pl.BlockSpec((1, d), lambda i, j: (0, 0)),
            pl.BlockSpec((d, DSA_TN), lambda i, j: (0, j)),
            vec_spec, vec_spec, tab_spec, tab_spec, tab_spec, tab_spec, tab_spec,
        ],
        out_specs=[
            pl.BlockSpec((tm, DSA_TN), lambda i, j: (i, j)),
            pl.BlockSpec((tm, LANES), lambda i, j: (i, 0)),
        ],
        out_shape=[
            jax.ShapeDtypeStruct((t, DSA_COLS), w.dtype),
            jax.ShapeDtypeStruct((t, LANES), F32),
        ],
        scratch_shapes=[pltpu.VMEM((tm, d), w.dtype)],
        compiler_params=pltpu.CompilerParams(
            dimension_semantics=("parallel", "arbitrary"),
            vmem_limit_bytes=_vmem_limit(need)),
        name="dsa_projection",
    )(x, g.reshape(1, d), w, q_norm.reshape(1, LANES), k_norm.reshape(1, LANES), rc, rs, ic, ia, ib)


def _dsa_core_kernel(q_ref, iq_ref, iw_ref, k_ref, v_ref, ik_ref, o_ref,
                     key_ref, wrep_ref, m_ref, l_ref, acc_ref, *, tq, topk):
    i = pl.program_id(0)
    nc = i + 1
    group = N_HEADS // N_KV_HEADS
    nt = (((1,), (1,)), ((), ()))

    iw = iw_ref[...]
    for h in range(IDX_HEADS):
        wrep_ref[h] = jnp.broadcast_to(iw[:, h:h + 1], (tq, LANES))

    row = lax.broadcasted_iota(jnp.int32, (tq, tq), 0)
    col = lax.broadcasted_iota(jnp.int32, (tq, tq), 1)
    lower_tri = col <= row

    def index_chunk(c, carry):
        start = pl.multiple_of(c * tq, tq)
        ikc = ik_ref[pl.ds(start, tq), :]
        score = jnp.zeros((tq, tq), F32)
        for h in range(IDX_HEADS):
            logits = lax.dot_general(iq_ref[:, h * IDX_DIM:(h + 1) * IDX_DIM], ikc, nt,
                                     preferred_element_type=F32)
            score = score + _tile_lanes(wrep_ref[h], tq) * jnp.maximum(logits, 0.0)
        bits = lax.bitcast_convert_type(score, jnp.int32)
        key = bits ^ ((bits >> 31) & 0x7FFFFFFF)
        key_ref[c] = jnp.where((c < i) | lower_tri, key, INT_MIN)
        return carry

    lax.fori_loop(0, nc, index_chunk, 0)

    def count_ge(cand):
        cand_t = _tile_lanes(cand, tq)

        def body(c, cnt):
            hit = jnp.where(key_ref[c] >= cand_t, 1.0, 0.0)
            for s in range(tq // LANES):
                cnt = cnt + hit[:, s * LANES:(s + 1) * LANES]
            return cnt

        cnt = lax.fori_loop(0, nc, body, jnp.zeros((tq, LANES), F32))
        return jnp.broadcast_to(jnp.sum(cnt, axis=1, keepdims=True), (tq, LANES))

    n_valid = (lax.broadcasted_iota(jnp.int32, (tq, LANES), 0) + (i * tq + 1)).astype(F32)
    cnt0 = count_ge(jnp.zeros((tq, LANES), jnp.int32))
    take0 = cnt0 >= topk
    thr0 = jnp.where(take0, 0, INT_MIN)
    cnt_thr0 = jnp.where(take0, cnt0, n_valid)

    def search_cond(st):
        b, _, cnt_thr = st
        return (b >= 0) & (jnp.max(cnt_thr) > topk)

    def search_body(st):
        b, thr, cnt_thr = st
        cand = thr | lax.shift_left(jnp.int32(1), b)
        cnt = count_ge(cand)
        take = cnt >= topk
        return b - 1, jnp.where(take, cand, thr), jnp.where(take, cnt, cnt_thr)

    _, thr, _ = lax.while_loop(search_cond, search_body, (jnp.int32(30), thr0, cnt_thr0))
    thr_t = _tile_lanes(jnp.maximum(thr, INT_MIN + 1), tq)

    m_ref[...] = jnp.full(m_ref.shape, NEG_BIG, F32)
    l_ref[...] = jnp.zeros(l_ref.shape, F32)
    acc_ref[...] = jnp.zeros(acc_ref.shape, F32)

    def attend_chunk(c, carry):
        start = pl.multiple_of(c * tq, tq)
        sel = key_ref[c] >= thr_t
        for g in range(N_KV_HEADS):
            kc = k_ref[pl.ds(start, tq), g * HEAD_DIM:(g + 1) * HEAD_DIM]
            vc = v_ref[pl.ds(start, tq), g * HEAD_DIM:(g + 1) * HEAD_DIM]
            for r in range(group):
                h = g * group + r
                s = lax.dot_general(q_ref[:, h * HEAD_DIM:(h + 1) * HEAD_DIM], kc, nt,
                                    preferred_element_type=F32)
                s = jnp.where(sel, s, NEG_BIG)
                m_prev = m_ref[h]
                m_cur = jnp.maximum(m_prev, jnp.max(s, axis=1, keepdims=True))
                alpha = jnp.exp(m_prev - m_cur)
                p = jnp.exp(s - _tile_lanes(m_cur, tq))
                l_ref[h] = alpha * l_ref[h] + jnp.sum(p, axis=1, keepdims=True)
                acc_ref[h] = alpha * acc_ref[h] + jnp.dot(
                    p.astype(vc.dtype), vc, preferred_element_type=F32)
                m_ref[h] = m_cur
        return carry

    lax.fori_loop(0, nc, attend_chunk, 0)

    for h in range(N_HEADS):
        o_ref[:, h * HEAD_DIM:(h + 1) * HEAD_DIM] = (acc_ref[h] / l_ref[h]).astype(o_ref.dtype)


def dsa_core(proj, iw, *, tq, topk):
    t = proj.shape[0]
    k_block = (Q_COLS + IQ_COLS) // KV_COLS
    ik_block = (Q_COLS + IQ_COLS + 2 * KV_COLS) // IDX_DIM
    resident = dict(pipeline_mode=pl.Buffered(1))
    need = (4 * _nbytes((tq, Q_COLS), proj.dtype) + 2 * _nbytes((tq, LANES), F32)
            + 2 * _nbytes((t, KV_COLS), proj.dtype) + _nbytes((t, IDX_DIM), proj.dtype)
            + 2 * _nbytes((tq, Q_COLS), proj.dtype)
            + _nbytes((t // tq, tq, tq), jnp.int32)
            + (IDX_HEADS + 3 * N_HEADS) * _nbytes((tq, LANES), F32))
    return pl.pallas_call(
        functools.partial(_dsa_core_kernel, tq=tq, topk=topk),
        grid=(t // tq,),
        in_specs=[
            pl.BlockSpec((tq, Q_COLS), lambda i: (i, 0)),
            pl.BlockSpec((tq, IQ_COLS), lambda i: (i, 1)),
            pl.BlockSpec((tq, LANES), lambda i: (i, 0)),
            pl.BlockSpec((t, KV_COLS), lambda i: (0, k_block), **resident),
            pl.BlockSpec((t, KV_COLS), lambda i: (0, k_block + 1), **resident),
            pl.BlockSpec((t, IDX_DIM), lambda i: (0, ik_block), **resident),
        ],
        out_specs=pl.BlockSpec((tq, Q_COLS), lambda i: (i, 0)),
        out_shape=jax.ShapeDtypeStruct((t, Q_COLS), proj.dtype),
        scratch_shapes=[
            pltpu.VMEM((t // tq, tq, tq), jnp.int32),
            pltpu.VMEM((IDX_HEADS, tq, LANES), F32),
            pltpu.VMEM((N_HEADS, tq, LANES), F32),
            pltpu.VMEM((N_HEADS, tq, LANES), F32),
            pltpu.VMEM((N_HEADS, tq, HEAD_DIM), F32),
        ],
        compiler_params=pltpu.CompilerParams(
            dimension_semantics=("arbitrary",),
            vmem_limit_bytes=_vmem_limit(need)),
        name="dsa_core",
    )(proj, proj, iw, proj, proj, proj)


def _rope_tables(pos):
    def angles(dim):
        inv = 1.0 / (ROPE_THETA ** (jnp.arange(0, dim, 2, dtype=F32) / dim))
        return pos.astype(F32)[:, None] * inv

    ang = angles(HEAD_DIM)
    cos, sin = jnp.cos(ang), jnp.sin(ang)
    rc = jnp.concatenate([cos, cos], axis=1)
    rs = jnp.concatenate([-sin, sin], axis=1)
    iang = angles(IDX_ROPE_DIM)
    icos, isin = jnp.cos(iang), jnp.sin(iang)
    rest = LANES - IDX_ROPE_DIM
    zeros = jnp.zeros_like(isin)
    ic = jnp.concatenate([icos, icos, jnp.ones((pos.shape[0], rest), F32)], axis=1)
    ia = jnp.concatenate([-isin, zeros, jnp.zeros((pos.shape[0], rest), F32)], axis=1)
    ib = jnp.concatenate([zeros, isin, jnp.zeros((pos.shape[0], rest), F32)], axis=1)
    return rc, rs, ic, ia, ib


def _reorder_dsa_weight(w_in):
    cuts = [Q_COLS, Q_COLS + KV_COLS, Q_COLS + 2 * KV_COLS, Q_COLS + 2 * KV_COLS + IQ_COLS,
            Q_COLS + 2 * KV_COLS + IQ_COLS + IDX_DIM]
    wq, wk, wv, wiq, wik, wiw = jnp.split(w_in, cuts, axis=1)
    pad = jnp.zeros((w_in.shape[0], LANES - wiw.shape[1]), w_in.dtype)
    return jnp.concatenate([wq, wiq, wk, wv, wik, wiw, pad], axis=1).astype(MXU_DTYPE)


def dsa_layer(x, pos, norm_g, w_in, q_norm, k_norm, w_out, tiles):
    t = x.shape[0]
    proj, iw = dsa_projection(x, norm_g, _reorder_dsa_weight(w_in), q_norm, k_norm,
                              _rope_tables(pos), tm=tiles["proj_tm"])
    attn = dsa_core(proj, iw, tq=tiles["tq"], topk=min(TOPK_MAX, t // 4))
    return matmul_residual(attn, w_out.astype(MXU_DTYPE), x, tm=tiles["res_tm"], tn=tiles["res_tn"])


def _shift_rows(x, d, fill):
    rows = lax.broadcasted_iota(jnp.int32, x.shape, 0)
    return jnp.where(rows >= d, pltpu.roll(x, d, 0), fill)


def _rglru_kernel(gate_ref, xr_ref, cw_ref, cb_ref, wa_ref, ba_ref, wx_ref, bx_ref, lam_ref,
                  y_ref, hcar_ref, xprev_ref, *, tt):
    @pl.when(pl.program_id(1) == 0)
    def _():
        hcar_ref[...] = jnp.zeros_like(hcar_ref)
        xprev_ref[...] = jnp.zeros_like(xprev_ref)

    xr = xr_ref[...]
    ext = jnp.concatenate([xprev_ref[...], xr], axis=0)
    cw = cw_ref[...]
    xc = cb_ref[...] + xr * cw[CONV_WIDTH - 1:CONV_WIDTH, :]
    for d in range(1, CONV_WIDTH):
        xc = xc + pltpu.roll(ext, d, 0)[SUBLANES:, :] * cw[CONV_WIDTH - 1 - d:CONV_WIDTH - d, :]
    xprev_ref[...] = xr[tt - SUBLANES:, :]

    xcb = xc.astype(wa_ref.dtype)
    r = jax.nn.sigmoid(jnp.dot(xcb, wa_ref[...], preferred_element_type=F32) + ba_ref[...])
    ig = jax.nn.sigmoid(jnp.dot(xcb, wx_ref[...], preferred_element_type=F32) + bx_ref[...])
    nlam = -lam_ref[...]
    softplus = jnp.maximum(nlam, 0.0) + jnp.log(1.0 + jnp.exp(-jnp.abs(nlam)))
    log_a = -LRU_C * r * softplus
    a = jnp.exp(log_a)
    mult = jnp.sqrt(1.0 - jnp.exp(2.0 * log_a))
    b = xc * ig * mult

    d = 1
    while d < tt:
        b = a * _shift_rows(b, d, 0.0) + b
        a = a * _shift_rows(a, d, 1.0)
        d *= 2
    h = b + a * hcar_ref[...]
    hcar_ref[...] = h[tt - 1:tt, :]
    y_ref[...] = (h * jax.nn.gelu(gate_ref[...])).astype(y_ref.dtype)


def rglru_scan(proj, conv_w, conv_b, wa, ba, wx, bx, lam, *, tt):
    t = proj.shape[0]
    width = conv_w.shape[1]
    nb, blk, _ = wa.shape
    vec = lambda: pl.BlockSpec((1, blk), lambda n, s: (0, n))
    need = (4 * _nbytes((tt, blk), F32) + 2 * _nbytes((tt, blk), MXU_DTYPE)
            + 4 * _nbytes((blk, blk), wa.dtype) + 24 * _nbytes((tt, blk), F32))
    return pl.pallas_call(
        functools.partial(_rglru_kernel, tt=tt),
        grid=(nb, t // tt),
        in_specs=[
            pl.BlockSpec((tt, blk), lambda n, s: (s, n)),
            pl.BlockSpec((tt, blk), lambda n, s: (s, nb + n)),
            pl.BlockSpec((CONV_WIDTH, blk), lambda n, s: (0, n)),
            vec(),
            pl.BlockSpec((None, blk, blk), lambda n, s: (n, 0, 0)),
            vec(),
            pl.BlockSpec((None, blk, blk), lambda n, s: (n, 0, 0)),
            vec(),
            vec(),
        ],
        out_specs=pl.BlockSpec((tt, blk), lambda n, s: (s, n)),
        out_shape=jax.ShapeDtypeStruct((t, width), MXU_DTYPE),
        scratch_shapes=[pltpu.VMEM((1, blk), F32), pltpu.VMEM((SUBLANES, blk), F32)],
        compiler_params=pltpu.CompilerParams(
            dimension_semantics=("parallel", "arbitrary"),
            vmem_limit_bytes=_vmem_limit(need)),
        name="rglru_scan",
    )(proj, proj, conv_w, conv_b.reshape(1, width), wa, ba.reshape(1, width),
      wx, bx.reshape(1, width), lam.reshape(1, width))


def rglru_layer(x, norm_g, w_in, conv_w, conv_b, wa, ba, wx, bx, lam, w_out, tiles):
    proj = norm_matmul(x, norm_g, w_in.astype(MXU_DTYPE), tm=tiles["proj_tm"], tn=tiles["proj_tn"],
                       out_dtype=F32)
    y = rglru_scan(proj, conv_w, conv_b, wa.astype(MXU_DTYPE), ba, wx.astype(MXU_DTYPE), bx, lam,
                   tt=tiles["scan_tt"])
    return matmul_residual(y, w_out.astype(MXU_DTYPE), x, tm=tiles["res_tm"], tn=tiles["res_tn"])


POOL_HALO = max(POOL_WINDOWS)


def _pool_kernel(x_ref, g_ref, w_ref, b_ref, s_ref, o_ref, halo_ref, *, tt):
    blk = pl.program_id(0)

    @pl.when(blk == 0)
    def _():
        halo_ref[...] = jnp.zeros_like(halo_ref)

    x = x_ref[...]
    h = _rms_norm_rows(x, g_ref[...])
    ext = jnp.concatenate([halo_ref[...], h], axis=0)
    halo_ref[...] = h[tt - POOL_HALO:, :]

    gw = h.shape[1] // len(POOL_WINDOWS)
    t1 = (lax.broadcasted_iota(jnp.int32, (tt, gw), 0) + (blk * tt + 1)).astype(F32)
    for gi, win in enumerate(POOL_WINDOWS):
        cols = slice(gi * gw, (gi + 1) * gw)
        acc = ext[:, cols]
        d = 1
        while d < win:
            acc = acc + pltpu.roll(acc, d, 0)
            d *= 2
        mean = acc[POOL_HALO:, :] / jnp.minimum(t1, float(win))
        y = (mean - h[:, cols]).astype(w_ref.dtype)
        z = jnp.dot(y, w_ref[gi], preferred_element_type=F32) + b_ref[gi:gi + 1, :]
        o_ref[:, cols] = x[:, cols] + z * s_ref[:, cols]


def pool_layer(x, norm_g, w_group, b_group, scale, tiles):
    t, d = x.shape
    tt = tiles["pool_tt"]
    ng, gw, _ = w_group.shape
    need = (4 * _nbytes((tt, d), F32) + 2 * _nbytes((ng, gw, gw), MXU_DTYPE)
            + 6 * _nbytes((tt, d), F32))
    return pl.pallas_call(
        functools.partial(_pool_kernel, tt=tt),
        grid=(t // tt,),
        in_specs=[
            pl.BlockSpec((tt, d), lambda i: (i, 0)),
            pl.BlockSpec((1, d), lambda i: (0, 0)),
            pl.BlockSpec((ng, gw, gw), lambda i: (0, 0, 0)),
            pl.BlockSpec((ng, gw), lambda i: (0, 0)),
            pl.BlockSpec((1, d), lambda i: (0, 0)),
        ],
        out_specs=pl.BlockSpec((tt, d), lambda i: (i, 0)),
        out_shape=jax.ShapeDtypeStruct((t, d), F32),
        scratch_shapes=[pltpu.VMEM((POOL_HALO, d), F32)],
        compiler_params=pltpu.CompilerParams(
            dimension_semantics=("arbitrary",),
            vmem_limit_bytes=_vmem_limit(need)),
        name="pool_mixer",
    )(x, norm_g.reshape(1, d), w_group.astype(MXU_DTYPE), b_group, scale.reshape(1, d))


def _tiles(t):
    big = min(t, 512)
    return {
        "proj_tm": big, "proj_tn": 512,
        "res_tm": big, "res_tn": 512,
        "tq": min(t, 256),
        "scan_tt": min(t, 256),
        "pool_tt": min(t, 256),
        "mlp_tm": big, "mlp_tf": 512,
    }


def kernel(x, positions, attn_norm, attn_w_in, attn_q_norm, attn_k_norm, attn_w_out, rnn_norm, rnn_w_in, rnn_conv_w, rnn_conv_b, rnn_gate_a_w, rnn_gate_a_b, rnn_gate_x_w, rnn_gate_x_b, rnn_lambda, rnn_w_out, pool_norm, pool_w, pool_b, pool_scale, mlp_norm, mlp_w_up, mlp_w_down):
    batch, t, d = x.shape
    depth = mlp_norm.shape[0]
    tiles = _tiles(t)
    outs = []
    for bi in range(batch):
        xb = x[bi]
        pos = positions[bi]
        for i in range(depth):
            kind, j = i % N_MIXERS, i // N_MIXERS
            if kind == 0:
                xb = dsa_layer(xb, pos, attn_norm[j], attn_w_in[j], attn_q_norm[j], attn_k_norm[j],
                               attn_w_out[j], tiles)
            elif kind == 1:
                xb = rglru_layer(xb, rnn_norm[j], rnn_w_in[j], rnn_conv_w[j], rnn_conv_b[j],
                                 rnn_gate_a_w[j], rnn_gate_a_b[j], rnn_gate_x_w[j], rnn_gate_x_b[j],
                                 rnn_lambda[j], rnn_w_out[j], tiles)
            else:
                xb = pool_layer(xb, pool_norm[j], pool_w[j], pool_b[j], pool_scale[j], tiles)
            xb = mlp_block(xb, mlp_norm[i], mlp_w_up[i].astype(MXU_DTYPE),
                           mlp_w_down[i].astype(MXU_DTYPE), tm=tiles["mlp_tm"], tf=tiles["mlp_tf"])
        outs.append(xb)
    return jnp.stack(outs, axis=0)
```

```python
import functools
import math

import jax
import jax.numpy as jnp
from jax import lax
from jax.experimental import pallas as pl
from jax.experimental.pallas import tpu as pltpu

F32 = jnp.float32
MXU_DTYPE = jnp.bfloat16

N_MIXERS = 3
EPS = 1e-6
ROPE_THETA = 10000.0
HEAD_DIM = 128
N_HEADS = 16
N_KV_HEADS = 4
IDX_HEADS = 16
IDX_DIM = 128
IDX_ROPE_DIM = 64
TOPK_MAX = 256
CONV_WIDTH = 4
LRU_C = 8.0
POOL_WINDOWS = (2, 4, 8, 16)

LANES = 128
SUBLANES = 8
VMEM_BYTES_V7X = 64 * 1024 * 1024
VMEM_CAP_BYTES = VMEM_BYTES_V7X - 8 * 1024 * 1024

INT_MIN = -(2 ** 31)
NEG_BIG = -1e30


def _vmem_limit(block_bytes):
    return int(min(VMEM_CAP_BYTES, block_bytes * 3 // 2 + (4 << 20)))


def _nbytes(shape, dtype):
    return math.prod(shape) * jnp.dtype(dtype).itemsize


def _rms_norm_rows(x, g):
    ms = jnp.mean(x * x, axis=-1, keepdims=True)
    return x * lax.rsqrt(ms + EPS) * g


def _tile_lanes(x, width):
    reps = width // LANES
    return x if reps == 1 else jnp.concatenate([x] * reps, axis=1)


def _norm_matmul_kernel(x_ref, g_ref, w_ref, o_ref, h_ref):
    @pl.when(pl.program_id(1) == 0)
    def _():
        h_ref[...] = _rms_norm_rows(x_ref[...], g_ref[...]).astype(h_ref.dtype)

    o_ref[...] = jnp.dot(h_ref[...], w_ref[...], preferred_element_type=F32).astype(o_ref.dtype)


def norm_matmul(x, g, w, *, tm, tn, out_dtype):
    t, d = x.shape
    n = w.shape[1]
    need = (2 * _nbytes((tm, d), F32) + 2 * _nbytes((d, tn), w.dtype)
            + 2 * _nbytes((tm, tn), out_dtype) + _nbytes((tm, d), w.dtype))
    return pl.pallas_call(
        _norm_matmul_kernel,
        grid=(t // tm, n // tn),
        in_specs=[
            pl.BlockSpec((tm, d), lambda i, j: (i, 0)),
            pl.BlockSpec((1, d), lambda i, j: (0, 0)),
            pl.BlockSpec((d, tn), lambda i, j: (0, j)),
        ],
        out_specs=pl.BlockSpec((tm, tn), lambda i, j: (i, j)),
        out_shape=jax.ShapeDtypeStruct((t, n), out_dtype),
        scratch_shapes=[pltpu.VMEM((tm, d), w.dtype)],
        compiler_params=pltpu.CompilerParams(
            dimension_semantics=("parallel", "arbitrary"),
            vmem_limit_bytes=_vmem_limit(need)),
        name="norm_matmul",
    )(x, g.reshape(1, d), w)


def _matmul_residual_kernel(a_ref, w_ref, x_ref, o_ref):
    o_ref[...] = x_ref[...] + jnp.dot(a_ref[...], w_ref[...], preferred_element_type=F32)


def matmul_residual(a, w, x, *, tm, tn):
    t, k = a.shape
    n = w.shape[1]
    need = (2 * _nbytes((tm, k), a.dtype) + 2 * _nbytes((k, tn), w.dtype)
            + 4 * _nbytes((tm, tn), F32))
    return pl.pallas_call(
        _matmul_residual_kernel,
        grid=(t // tm, n // tn),
        in_specs=[
            pl.BlockSpec((tm, k), lambda i, j: (i, 0)),
            pl.BlockSpec((k, tn), lambda i, j: (0, j)),
            pl.BlockSpec((tm, tn), lambda i, j: (i, j)),
        ],
        out_specs=pl.BlockSpec((tm, tn), lambda i, j: (i, j)),
        out_shape=jax.ShapeDtypeStruct((t, n), F32),
        compiler_params=pltpu.CompilerParams(
            dimension_semantics=("parallel", "arbitrary"),
            vmem_limit_bytes=_vmem_limit(need)),
        name="matmul_residual",
    )(a, w, x)


def _mlp_kernel(x_ref, g_ref, wu_ref, wd_ref, o_ref, h_ref, acc_ref):
    f = pl.program_id(1)

    @pl.when(f == 0)
    def _():
        h_ref[...] = _rms_norm_rows(x_ref[...], g_ref[...]).astype(h_ref.dtype)
        acc_ref[...] = jnp.zeros_like(acc_ref)

    u = jnp.dot(h_ref[...], wu_ref[...], preferred_element_type=F32)
    u = jnp.square(jnp.maximum(u, 0.0)).astype(wd_ref.dtype)
    acc_ref[...] += jnp.dot(u, wd_ref[...], preferred_element_type=F32)

    @pl.when(f == pl.num_programs(1) - 1)
    def _():
        o_ref[...] = x_ref[...] + acc_ref[...]


def mlp_block(x, g, w_up, w_down, *, tm, tf):
    t, d = x.shape
    ff = w_up.shape[1]
    need = (4 * _nbytes((tm, d), F32) + 2 * _nbytes((d, tf), w_up.dtype)
            + 2 * _nbytes((tf, d), w_down.dtype) + _nbytes((tm, d), w_up.dtype)
            + _nbytes((tm, d), F32) + 2 * _nbytes((tm, tf), F32))
    return pl.pallas_call(
        _mlp_kernel,
        grid=(t // tm, ff // tf),
        in_specs=[
            pl.BlockSpec((tm, d), lambda i, f: (i, 0)),
            pl.BlockSpec((1, d), lambda i, f: (0, 0)),
            pl.BlockSpec((d, tf), lambda i, f: (0, f)),
            pl.BlockSpec((tf, d), lambda i, f: (f, 0)),
        ],
        out_specs=pl.BlockSpec((tm, d), lambda i, f: (i, 0)),
        out_shape=jax.ShapeDtypeStruct((t, d), F32),
        scratch_shapes=[pltpu.VMEM((tm, d), w_up.dtype), pltpu.VMEM((tm, d), F32)],
        compiler_params=pltpu.CompilerParams(
            dimension_semantics=("parallel", "arbitrary"),
            vmem_limit_bytes=_vmem_limit(need)),
        name="mlp_block",
    )(x, g.reshape(1, d), w_up, w_down)


DSA_TN = 2 * LANES
Q_COLS = N_HEADS * HEAD_DIM
IQ_COLS = IDX_HEADS * IDX_DIM
KV_COLS = N_KV_HEADS * HEAD_DIM
DSA_COLS = Q_COLS + IQ_COLS + 2 * KV_COLS + IDX_DIM + LANES
Q_BLOCKS = Q_COLS // DSA_TN
IQ_BLOCKS = IQ_COLS // DSA_TN
KV_BLOCKS = KV_COLS // DSA_TN


V_FIRST_BLOCK = Q_BLOCKS + IQ_BLOCKS + KV_BLOCKS


def _dsa_proj_kernel(x_ref, g_ref, w_ref, qn_ref, kn_ref, rc_ref, rs_ref,
                     ic_ref, ia_ref, ib_ref, o_ref, iw_ref, vt_ref, h_ref):
    j = pl.program_id(1)

    @pl.when(j == 0)
    def _():
        h_ref[...] = _rms_norm_rows(x_ref[...], g_ref[...]).astype(h_ref.dtype)

    y = jnp.dot(h_ref[...], w_ref[...], preferred_element_type=F32)

    def rope(z):
        return z * rc_ref[...] + pltpu.roll(z, HEAD_DIM // 2, 1) * rs_ref[...]

    def idx_rope(z):
        half = IDX_ROPE_DIM // 2
        return (z * ic_ref[...] + pltpu.roll(z, LANES - half, 1) * ia_ref[...]
                + pltpu.roll(z, half, 1) * ib_ref[...])

    def store_heads(fn):
        o_ref[...] = jnp.concatenate(
            [fn(y[:, :LANES]), fn(y[:, LANES:])], axis=1).astype(o_ref.dtype)

    @pl.when(j < Q_BLOCKS)
    def _():
        scale = HEAD_DIM ** -0.5 * math.log2(math.e)
        store_heads(lambda z: rope(_rms_norm_rows(z, qn_ref[...])) * scale)

    @pl.when((j >= Q_BLOCKS) & (j < Q_BLOCKS + IQ_BLOCKS))
    def _():
        store_heads(idx_rope)

    @pl.when((j >= Q_BLOCKS + IQ_BLOCKS) & (j < Q_BLOCKS + IQ_BLOCKS + KV_BLOCKS))
    def _():
        store_heads(lambda z: rope(_rms_norm_rows(z, kn_ref[...])))

    @pl.when((j >= V_FIRST_BLOCK) & (j < V_FIRST_BLOCK + KV_BLOCKS))
    def _():
        o_ref[...] = y.astype(o_ref.dtype)
        vt_ref[...] = jnp.transpose(y).astype(vt_ref.dtype)

    @pl.when(j == V_FIRST_BLOCK + KV_BLOCKS)
    def _():
        o_ref[...] = jnp.concatenate(
            [idx_rope(y[:, :LANES]), jnp.zeros_like(y[:, LANES:])], axis=1).astype(o_ref.dtype)
        iw_ref[...] = y[:, LANES:] * (IDX_HEADS ** -0.5 * IDX_DIM ** -0.5)


def dsa_projection(x, g, w, q_norm, k_norm, tables, *, tm):
    t, d = x.shape
    rc, rs, ic, ia, ib = tables
    tab_spec = pl.BlockSpec((tm, LANES), lambda i, j: (i, 0))
    vec_spec = pl.BlockSpec((1, LANES), lambda i, j: (0, 0))
    need = (2 * _nbytes((tm, d), F32) + 2 * _nbytes((d, DSA_TN), w.dtype)
            + 2 * _nbytes((tm, DSA_TN), w.dtype) + _nbytes((tm, d), w.dtype)
            + 12 * _nbytes((tm, LANES), F32))
    return pl.pallas_call(
        _dsa_proj_kernel,
        grid=(t // tm, DSA_COLS // DSA_TN),
        in_specs=[
            pl.BlockSpec((tm, d), lambda i, j: (i, 0)),
            pl.BlockSpec((1, d), lambda i, j: (0, 0)),
            pl.BlockSpec((d, DSA_TN), lambda i, j: (0, j)),
            vec_spec, vec_spec, tab_spec, tab_spec, tab_spec, tab_spec, tab_spec,
        ],
        out_specs=[
            pl.BlockSpec((tm, DSA_TN), lambda i, j: (i, j)),
            pl.BlockSpec((tm, LANES), lambda i, j: (i, 0)),
            pl.BlockSpec((DSA_TN, tm), lambda i, j: (jnp.clip(j - V_FIRST_BLOCK, 0, KV_BLOCKS - 1), i)),
        ],
        out_shape=[
            jax.ShapeDtypeStruct((t, DSA_COLS), w.dtype),
            jax.ShapeDtypeStruct((t, LANES), F32),
            jax.ShapeDtypeStruct((KV_COLS, t), w.dtype),
        ],
        scratch_shapes=[pltpu.VMEM((tm, d), w.dtype)],
        compiler_params=pltpu.CompilerParams(
            dimension_semantics=("parallel", "arbitrary"),
            vmem_limit_bytes=_vmem_limit(need)),
        name="dsa_projection",
    )(x, g.reshape(1, d), w, q_norm.reshape(1, LANES), k_norm.reshape(1, LANES), rc, rs, ic, ia, ib)


def _float_key(v):
    b = lax.bitcast_convert_type(v, jnp.int32)
    return b ^ ((b >> 31) & 0x7FFFFFFF)


def _key_float(k):
    return lax.bitcast_convert_type(k ^ ((k >> 31) & 0x7FFFFFFF), F32)


_KLO, _KHI, _CLO = range(3)
_SEARCH_CAP = 96


def _dsa_core_kernel(q_ref, iq_ref, iw_ref, k_ref, vt_ref, ik_ref, o_ref,
                     key_ref, bias_ref, wt_ref, st_ref, m_ref, l_ref, acc_ref, *, tq, topk, kt):
    i = pl.program_id(0)
    n_tiles = (i + kt) // kt
    group = N_HEADS // N_KV_HEADS
    nt = (((1,), (1,)), ((), ()))

    wt_ref[...] = jnp.transpose(iw_ref[...])

    krow = lax.broadcasted_iota(jnp.int32, (tq, tq), 0)
    qcol = lax.broadcasted_iota(jnp.int32, (tq, tq), 1)
    diag_causal = krow <= qcol

    def chunk_start(c):
        return pl.multiple_of(c * tq, tq)

    def index_tile(ti, carry):
        smin, smax = carry
        iks = [ik_ref[pl.ds(chunk_start(ti * kt + u), tq), :] for u in range(kt)]
        scores = [jnp.zeros((tq, tq), F32) for _ in range(kt)]
        for h in range(IDX_HEADS):
            iqh = iq_ref[:, h * IDX_DIM:(h + 1) * IDX_DIM]
            w = jnp.broadcast_to(wt_ref[h:h + 1, :], (tq, tq))
            for u in range(kt):
                logits = lax.dot_general(iks[u], iqh, nt, preferred_element_type=F32)
                scores[u] = scores[u] + w * jnp.maximum(logits, 0.0)
        for u in range(kt):
            c = ti * kt + u
            valid = (c < i) | ((c == i) & diag_causal)
            key_ref[c] = jnp.where(valid, _float_key(scores[u]), INT_MIN)
            smin = jnp.minimum(smin, jnp.min(jnp.where(valid, scores[u], jnp.inf), axis=0, keepdims=True))
            smax = jnp.maximum(smax, jnp.max(jnp.where(valid, scores[u], -jnp.inf), axis=0, keepdims=True))
        return smin, smax

    smin, smax = lax.fori_loop(
        0, n_tiles, index_tile,
        (jnp.full((1, tq), jnp.inf, F32), jnp.full((1, tq), -jnp.inf, F32)))

    def count_ge(cand):
        cand_b = jnp.broadcast_to(cand, (tq, tq))

        def body(ti, cnt):
            for u in range(kt):
                hit = jnp.where(key_ref[ti * kt + u] >= cand_b, 1.0, 0.0)
                cnt = cnt + jnp.sum(hit, axis=0, keepdims=True)
            return cnt

        return lax.fori_loop(0, n_tiles, body, jnp.zeros((1, tq), F32))

    st_ref[_KLO] = _float_key(smin)
    st_ref[_KHI] = _float_key(smax) + 1
    st_ref[_CLO] = lax.broadcasted_iota(jnp.int32, (1, tq), 1) + (i * tq + 1)

    def unfinished(klo, khi, clo):
        return (clo > topk) & ((khi - klo) != 1)

    def search_cond(st):
        step, pending = st
        return (step < _SEARCH_CAP) & (pending > 0)

    def search_body(st):
        step, _ = st
        klo, khi, clo = st_ref[_KLO], st_ref[_KHI], st_ref[_CLO]
        live = unfinished(klo, khi, clo)
        vmid = 0.5 * _key_float(klo) + 0.5 * _key_float(khi)
        cmid = _float_key(vmid)
        kmid = klo + lax.shift_right_logical(khi - klo, 1)
        cand = jnp.where((cmid > klo) & (cmid < khi), cmid, kmid)
        cand = jnp.where(live, cand, klo)
        cnt = count_ge(cand).astype(jnp.int32)
        take = cnt >= topk
        klo = jnp.where(live & take, cand, klo)
        clo = jnp.where(live & take, cnt, clo)
        khi = jnp.where(live & jnp.logical_not(take), cand, khi)
        st_ref[_KLO], st_ref[_KHI], st_ref[_CLO] = klo, khi, clo
        pending = jnp.max(jnp.where(unfinished(klo, khi, clo), 1, 0))
        return step + 1, pending

    pending0 = jnp.max(jnp.where(unfinished(st_ref[_KLO], st_ref[_KHI], st_ref[_CLO]), 1, 0))
    lax.while_loop(search_cond, search_body, (jnp.int32(0), pending0))
    thr_b = jnp.broadcast_to(st_ref[_KLO], (tq, tq))

    m_ref[...] = jnp.full(m_ref.shape, NEG_BIG, F32)
    l_ref[...] = jnp.zeros(l_ref.shape, F32)
    acc_ref[...] = jnp.zeros(acc_ref.shape, F32)

    def attend_tile(ti, carry):
        for u in range(kt):
            bias_ref[u] = jnp.where(key_ref[ti * kt + u] >= thr_b, 0.0, NEG_BIG)
        starts = [chunk_start(ti * kt + u) for u in range(kt)]
        for g in range(N_KV_HEADS):
            cols = slice(g * HEAD_DIM, (g + 1) * HEAD_DIM)
            for r in range(group):
                h = g * group + r
                qh = q_ref[:, h * HEAD_DIM:(h + 1) * HEAD_DIM]
                ss = [lax.dot_general(k_ref[pl.ds(starts[u], tq), cols], qh, nt,
                                      preferred_element_type=F32) + bias_ref[u]
                      for u in range(kt)]
                top = functools.reduce(jnp.maximum, ss)
                m_prev = m_ref[h]
                m_cur = jnp.maximum(m_prev, jnp.max(top, axis=0, keepdims=True))
                alpha = jnp.exp2(m_prev - m_cur)
                ps = [jnp.exp2(s - m_cur) for s in ss]
                l_ref[h] = alpha * l_ref[h] + jnp.sum(functools.reduce(jnp.add, ps), axis=0, keepdims=True)
                pv = functools.reduce(jnp.add, [
                    jnp.dot(vt_ref[cols, pl.ds(starts[u], tq)], ps[u].astype(vt_ref.dtype),
                            preferred_element_type=F32) for u in range(kt)])
                acc_ref[h] = alpha * acc_ref[h] + pv
                m_ref[h] = m_cur
        return carry

    lax.fori_loop(0, n_tiles, attend_tile, 0)

    for h in range(N_HEADS):
        o_ref[:, h * HEAD_DIM:(h + 1) * HEAD_DIM] = jnp.transpose(acc_ref[h] / l_ref[h]).astype(o_ref.dtype)


def dsa_core(proj, iw, vt, *, tq, topk, kt):
    t = proj.shape[0]
    assert (t // tq) % kt == 0
    k_block = (Q_COLS + IQ_COLS) // KV_COLS
    ik_block = (Q_COLS + IQ_COLS + 2 * KV_COLS) // IDX_DIM
    resident = dict(pipeline_mode=pl.Buffered(1))
    need = (4 * _nbytes((tq, Q_COLS), proj.dtype) + 2 * _nbytes((tq, LANES), F32)
            + 2 * _nbytes((t, KV_COLS), proj.dtype) + _nbytes((t, IDX_DIM), proj.dtype)
            + 2 * _nbytes((tq, Q_COLS), proj.dtype)
            + _nbytes((t // tq, tq, tq), jnp.int32) + _nbytes((kt, tq, tq), F32)
            + _nbytes((LANES, tq), F32) + (3 + 2 * N_HEADS) * _nbytes((SUBLANES, tq), F32)
            + _nbytes((N_HEADS, HEAD_DIM, tq), F32)
            + 3 * _nbytes((kt, tq, tq), F32))
    return pl.pallas_call(
        functools.partial(_dsa_core_kernel, tq=tq, topk=topk, kt=kt),
        grid=(t // tq,),
        in_specs=[
            pl.BlockSpec((tq, Q_COLS), lambda i: (i, 0)),
            pl.BlockSpec((tq, IQ_COLS), lambda i: (i, 1)),
            pl.BlockSpec((tq, LANES), lambda i: (i, 0)),
            pl.BlockSpec((t, KV_COLS), lambda i: (0, k_block), **resident),
            pl.BlockSpec((KV_COLS, t), lambda i: (0, 0), **resident),
            pl.BlockSpec((t, IDX_DIM), lambda i: (0, ik_block), **resident),
        ],
        out_specs=pl.BlockSpec((tq, Q_COLS), lambda i: (i, 0)),
        out_shape=jax.ShapeDtypeStruct((t, Q_COLS), proj.dtype),
        scratch_shapes=[
            pltpu.VMEM((t // tq, tq, tq), jnp.int32),
            pltpu.VMEM((kt, tq, tq), F32),
            pltpu.VMEM((LANES, tq), F32),
            pltpu.VMEM((3, 1, tq), jnp.int32),
            pltpu.VMEM((N_HEADS, 1, tq), F32),
            pltpu.VMEM((N_HEADS, 1, tq), F32),
            pltpu.VMEM((N_HEADS, HEAD_DIM, tq), F32),
        ],
        compiler_params=pltpu.CompilerParams(
            dimension_semantics=("arbitrary",),
            vmem_limit_bytes=_vmem_limit(need)),
        name="dsa_core",
    )(proj, proj, iw, proj, vt, proj)


def _rope_tables(pos):
    def angles(dim):
        inv = 1.0 / (ROPE_THETA ** (jnp.arange(0, dim, 2, dtype=F32) / dim))
        return pos.astype(F32)[:, None] * inv

    ang = angles(HEAD_DIM)
    cos, sin = jnp.cos(ang), jnp.sin(ang)
    rc = jnp.concatenate([cos, cos], axis=1)
    rs = jnp.concatenate([-sin, sin], axis=1)
    iang = angles(IDX_ROPE_DIM)
    icos, isin = jnp.cos(iang), jnp.sin(iang)
    rest = LANES - IDX_ROPE_DIM
    zeros = jnp.zeros_like(isin)
    ic = jnp.concatenate([icos, icos, jnp.ones((pos.shape[0], rest), F32)], axis=1)
    ia = jnp.concatenate([-isin, zeros, jnp.zeros((pos.shape[0], rest), F32)], axis=1)
    ib = jnp.concatenate([zeros, isin, jnp.zeros((pos.shape[0], rest), F32)], axis=1)
    return rc, rs, ic, ia, ib


def _reorder_dsa_weight(w_in):
    cuts = [Q_COLS, Q_COLS + KV_COLS, Q_COLS + 2 * KV_COLS, Q_COLS + 2 * KV_COLS + IQ_COLS,
            Q_COLS + 2 * KV_COLS + IQ_COLS + IDX_DIM]
    wq, wk, wv, wiq, wik, wiw = jnp.split(w_in, cuts, axis=1)
    pad = jnp.zeros((w_in.shape[0], LANES - wiw.shape[1]), w_in.dtype)
    return jnp.concatenate([wq, wiq, wk, wv, wik, wiw, pad], axis=1).astype(MXU_DTYPE)


def dsa_layer(x, pos, norm_g, w_in, q_norm, k_norm, w_out, tiles):
    t = x.shape[0]
    proj, iw, vt = dsa_projection(x, norm_g, _reorder_dsa_weight(w_in), q_norm, k_norm,
                                  _rope_tables(pos), tm=tiles["proj_tm"])
    attn = dsa_core(proj, iw, vt, tq=tiles["tq"], topk=min(TOPK_MAX, t // 4), kt=tiles["kt"])
    return matmul_residual(attn, w_out.astype(MXU_DTYPE), x, tm=tiles["res_tm"], tn=tiles["res_tn"])


def _shift_rows(x, d, fill):
    rows = lax.broadcasted_iota(jnp.int32, x.shape, 0)
    return jnp.where(rows >= d, pltpu.roll(x, d, 0), fill)


def _rglru_kernel(gate_ref, xr_ref, cw_ref, cb_ref, wa_ref, ba_ref, wx_ref, bx_ref, lam_ref,
                  y_ref, hcar_ref, xprev_ref, *, tt):
    @pl.when(pl.program_id(1) == 0)
    def _():
        hcar_ref[...] = jnp.zeros_like(hcar_ref)
        xprev_ref[...] = jnp.zeros_like(xprev_ref)

    xr = xr_ref[...]
    ext = jnp.concatenate([xprev_ref[...], xr], axis=0)
    cw = cw_ref[...]
    xc = cb_ref[...] + xr * cw[CONV_WIDTH - 1:CONV_WIDTH, :]
    for d in range(1, CONV_WIDTH):
        xc = xc + pltpu.roll(ext, d, 0)[SUBLANES:, :] * cw[CONV_WIDTH - 1 - d:CONV_WIDTH - d, :]
    xprev_ref[...] = xr[tt - SUBLANES:, :]

    xcb = xc.astype(wa_ref.dtype)
    r = jax.nn.sigmoid(jnp.dot(xcb, wa_ref[...], preferred_element_type=F32) + ba_ref[...])
    ig = jax.nn.sigmoid(jnp.dot(xcb, wx_ref[...], preferred_element_type=F32) + bx_ref[...])
    nlam = -lam_ref[...]
    softplus = jnp.maximum(nlam, 0.0) + jnp.log(1.0 + jnp.exp(-jnp.abs(nlam)))
    log_a = -LRU_C * r * softplus
    a = jnp.exp(log_a)
    mult = jnp.sqrt(1.0 - jnp.exp(2.0 * log_a))
    b = xc * ig * mult

    d = 1
    while d < tt:
        b = a * _shift_rows(b, d, 0.0) + b
        a = a * _shift_rows(a, d, 1.0)
        d *= 2
    h = b + a * hcar_ref[...]
    hcar_ref[...] = h[tt - 1:tt, :]
    y_ref[...] = (h * jax.nn.gelu(gate_ref[...])).astype(y_ref.dtype)


def rglru_scan(proj, conv_w, conv_b, wa, ba, wx, bx, lam, *, tt):
    t = proj.shape[0]
    width = conv_w.shape[1]
    nb, blk, _ = wa.shape
    vec = lambda: pl.BlockSpec((1, blk), lambda n, s: (0, n))
    need = (4 * _nbytes((tt, blk), F32) + 2 * _nbytes((tt, blk), MXU_DTYPE)
            + 4 * _nbytes((blk, blk), wa.dtype) + 24 * _nbytes((tt, blk), F32))
    return pl.pallas_call(
        functools.partial(_rglru_kernel, tt=tt),
        grid=(nb, t // tt),
        in_specs=[
            pl.BlockSpec((tt, blk), lambda n, s: (s, n)),
            pl.BlockSpec((tt, blk), lambda n, s: (s, nb + n)),
            pl.BlockSpec((CONV_WIDTH, blk), lambda n, s: (0, n)),
            vec(),
            pl.BlockSpec((None, blk, blk), lambda n, s: (n, 0, 0)),
            vec(),
            pl.BlockSpec((None, blk, blk), lambda n, s: (n, 0, 0)),
            vec(),
            vec(),
        ],
        out_specs=pl.BlockSpec((tt, blk), lambda n, s: (s, n)),
        out_shape=jax.ShapeDtypeStruct((t, width), MXU_DTYPE),
        scratch_shapes=[pltpu.VMEM((1, blk), F32), pltpu.VMEM((SUBLANES, blk), F32)],
        compiler_params=pltpu.CompilerParams(
            dimension_semantics=("parallel", "arbitrary"),
            vmem_limit_bytes=_vmem_limit(need)),
        name="rglru_scan",
    )(proj, proj, conv_w, conv_b.reshape(1, width), wa, ba.reshape(1, width),
      wx, bx.reshape(1, width), lam.reshape(1, width))


def rglru_layer(x, norm_g, w_in, conv_w, conv_b, wa, ba, wx, bx, lam, w_out, tiles):
    proj = norm_matmul(x, norm_g, w_in.astype(MXU_DTYPE), tm=tiles["proj_tm"], tn=tiles["proj_tn"],
                       out_dtype=F32)
    y = rglru_scan(proj, conv_w, conv_b, wa.astype(MXU_DTYPE), ba, wx.astype(MXU_DTYPE), bx, lam,
                   tt=tiles["scan_tt"])
    return matmul_residual(y, w_out.astype(MXU_DTYPE), x, tm=tiles["res_tm"], tn=tiles["res_tn"])


POOL_HALO = max(POOL_WINDOWS)


def _pool_kernel(x_ref, g_ref, w_ref, b_ref, s_ref, o_ref, halo_ref, *, tt):
    blk = pl.program_id(0)

    @pl.when(blk == 0)
    def _():
        halo_ref[...] = jnp.zeros_like(halo_ref)

    x = x_ref[...]
    h = _rms_norm_rows(x, g_ref[...])
    ext = jnp.concatenate([halo_ref[...], h], axis=0)
    halo_ref[...] = h[tt - POOL_HALO:, :]

    gw = h.shape[1] // len(POOL_WINDOWS)
    t1 = (lax.broadcasted_iota(jnp.int32, (tt, gw), 0) + (blk * tt + 1)).astype(F32)
    for gi, win in enumerate(POOL_WINDOWS):
        cols = slice(gi * gw, (gi + 1) * gw)
        acc = ext[:, cols]
        d = 1
        while d < win:
            acc = acc + pltpu.roll(acc, d, 0)
            d *= 2
        mean = acc[POOL_HALO:, :] / jnp.minimum(t1, float(win))
        y = (mean - h[:, cols]).astype(w_ref.dtype)
        z = jnp.dot(y, w_ref[gi], preferred_element_type=F32) + b_ref[gi:gi + 1, :]
        o_ref[:, cols] = x[:, cols] + z * s_ref[:, cols]


def pool_layer(x, norm_g, w_group, b_group, scale, tiles):
    t, d = x.shape
    tt = tiles["pool_tt"]
    ng, gw, _ = w_group.shape
    need = (4 * _nbytes((tt, d), F32) + 2 * _nbytes((ng, gw, gw), MXU_DTYPE)
            + 6 * _nbytes((tt, d), F32))
    return pl.pallas_call(
        functools.partial(_pool_kernel, tt=tt),
        grid=(t // tt,),
        in_specs=[
            pl.BlockSpec((tt, d), lambda i: (i, 0)),
            pl.BlockSpec((1, d), lambda i: (0, 0)),
            pl.BlockSpec((ng, gw, gw), lambda i: (0, 0, 0)),
            pl.BlockSpec((ng, gw), lambda i: (0, 0)),
            pl.BlockSpec((1, d), lambda i: (0, 0)),
        ],
        out_specs=pl.BlockSpec((tt, d), lambda i: (i, 0)),
        out_shape=jax.ShapeDtypeStruct((t, d), F32),
        scratch_shapes=[pltpu.VMEM((POOL_HALO, d), F32)],
        compiler_params=pltpu.CompilerParams(
            dimension_semantics=("arbitrary",),
            vmem_limit_bytes=_vmem_limit(need)),
        name="pool_mixer",
    )(x, norm_g.reshape(1, d), w_group.astype(MXU_DTYPE), b_group, scale.reshape(1, d))


def _tiles(t):
    big = min(t, 512)
    return {
        "proj_tm": big, "proj_tn": 512,
        "res_tm": big, "res_tn": 512,
        "tq": min(t, 256), "kt": min(4, t // min(t, 256)),
        "scan_tt": min(t, 256),
        "pool_tt": min(t, 256),
        "mlp_tm": big, "mlp_tf": 512,
    }


def kernel(x, positions, attn_norm, attn_w_in, attn_q_norm, attn_k_norm, attn_w_out, rnn_norm, rnn_w_in, rnn_conv_w, rnn_conv_b, rnn_gate_a_w, rnn_gate_a_b, rnn_gate_x_w, rnn_gate_x_b, rnn_lambda, rnn_w_out, pool_norm, pool_w, pool_b, pool_scale, mlp_norm, mlp_w_up, mlp_w_down):
    batch, t, d = x.shape
    depth = mlp_norm.shape[0]
    tiles = _tiles(t)
    outs = []
    for bi in range(batch):
        xb = x[bi]
        pos = positions[bi]
        for i in range(depth):
            kind, j = i % N_MIXERS, i // N_MIXERS
            if kind == 0:
                xb = dsa_layer(xb, pos, attn_norm[j], attn_w_in[j], attn_q_norm[j], attn_k_norm[j],
                               attn_w_out[j], tiles)
            elif kind == 1:
                xb = rglru_layer(xb, rnn_norm[j], rnn_w_in[j], rnn_conv_w[j], rnn_conv_b[j],
                                 rnn_gate_a_w[j], rnn_gate_a_b[j], rnn_gate_x_w[j], rnn_gate_x_b[j],
                                 rnn_lambda[j], rnn_w_out[j], tiles)
            else:
                xb = pool_layer(xb, pool_norm[j], pool_w[j], pool_b[j], pool_scale[j], tiles)
            xb = mlp_block(xb, mlp_norm[i], mlp_w_up[i].astype(MXU_DTYPE),
                           mlp_w_down[i].astype(MXU_DTYPE), tm=tiles["mlp_tm"], tf=tiles["mlp_tf"])
        outs.append(xb)
    return jnp.stack(outs, axis=0)
```

```python
import functools
import math

import jax
import jax.numpy as jnp
from jax import lax
from jax.experimental import pallas as pl
from jax.experimental.pallas import tpu as pltpu

F32 = jnp.float32
MXU_DTYPE = jnp.bfloat16

N_MIXERS = 3
EPS = 1e-6
ROPE_THETA = 10000.0
HEAD_DIM = 128
N_HEADS = 16
N_KV_HEADS = 4
IDX_HEADS = 16
IDX_DIM = 128
IDX_ROPE_DIM = 64
TOPK_MAX = 256
CONV_WIDTH = 4
LRU_C = 8.0
POOL_WINDOWS = (2, 4, 8, 16)

LANES = 128
SUBLANES = 8
VMEM_BYTES_V7X = 64 * 1024 * 1024
VMEM_CAP_BYTES = VMEM_BYTES_V7X - 8 * 1024 * 1024

INT_MIN = -(2 ** 31)
NEG_BIG = -1e30


def _vmem_limit(block_bytes):
    return int(min(VMEM_CAP_BYTES, block_bytes * 3 // 2 + (4 << 20)))


def _nbytes(shape, dtype):
    return math.prod(shape) * jnp.dtype(dtype).itemsize


def _rms_norm_rows(x, g):
    ms = jnp.mean(x * x, axis=-1, keepdims=True)
    return x * lax.rsqrt(ms + EPS) * g


def _tile_lanes(x, width):
    reps = width // LANES
    return x if reps == 1 else jnp.concatenate([x] * reps, axis=1)


def _norm_matmul_kernel(x_ref, g_ref, w_ref, o_ref, h_ref):
    @pl.when(pl.program_id(1) == 0)
    def _():
        h_ref[...] = _rms_norm_rows(x_ref[...], g_ref[...]).astype(h_ref.dtype)

    o_ref[...] = jnp.dot(h_ref[...], w_ref[...], preferred_element_type=F32).astype(o_ref.dtype)


def norm_matmul(x, g, w, *, tm, tn, out_dtype):
    t, d = x.shape
    n = w.shape[1]
    need = (2 * _nbytes((tm, d), F32) + 2 * _nbytes((d, tn), w.dtype)
            + 2 * _nbytes((tm, tn), out_dtype) + _nbytes((tm, d), w.dtype))
    return pl.pallas_call(
        _norm_matmul_kernel,
        grid=(t // tm, n // tn),
        in_specs=[
            pl.BlockSpec((tm, d), lambda i, j: (i, 0)),
            pl.BlockSpec((1, d), lambda i, j: (0, 0)),
            pl.BlockSpec((d, tn), lambda i, j: (0, j)),
        ],
        out_specs=pl.BlockSpec((tm, tn), lambda i, j: (i, j)),
        out_shape=jax.ShapeDtypeStruct((t, n), out_dtype),
        scratch_shapes=[pltpu.VMEM((tm, d), w.dtype)],
        compiler_params=pltpu.CompilerParams(
            dimension_semantics=("parallel", "arbitrary"),
            vmem_limit_bytes=_vmem_limit(need)),
        name="norm_matmul",
    )(x, g.reshape(1, d), w)


def _matmul_residual_kernel(a_ref, w_ref, x_ref, o_ref):
    o_ref[...] = x_ref[...] + jnp.dot(a_ref[...], w_ref[...], preferred_element_type=F32)


def matmul_residual(a, w, x, *, tm, tn):
    t, k = a.shape
    n = w.shape[1]
    need = (2 * _nbytes((tm, k), a.dtype) + 2 * _nbytes((k, tn), w.dtype)
            + 4 * _nbytes((tm, tn), F32))
    return pl.pallas_call(
        _matmul_residual_kernel,
        grid=(t // tm, n // tn),
        in_specs=[
            pl.BlockSpec((tm, k), lambda i, j: (i, 0)),
            pl.BlockSpec((k, tn), lambda i, j: (0, j)),
            pl.BlockSpec((tm, tn), lambda i, j: (i, j)),
        ],
        out_specs=pl.BlockSpec((tm, tn), lambda i, j: (i, j)),
        out_shape=jax.ShapeDtypeStruct((t, n), F32),
        compiler_params=pltpu.CompilerParams(
            dimension_semantics=("parallel", "arbitrary"),
            vmem_limit_bytes=_vmem_limit(need)),
        name="matmul_residual",
    )(a, w, x)


def _mlp_kernel(x_ref, g_ref, wu_ref, wd_ref, o_ref, h_ref, acc_ref):
    f = pl.program_id(1)

    @pl.when(f == 0)
    def _():
        h_ref[...] = _rms_norm_rows(x_ref[...], g_ref[...]).astype(h_ref.dtype)
        acc_ref[...] = jnp.zeros_like(acc_ref)

    u = jnp.dot(h_ref[...], wu_ref[...], preferred_element_type=F32)
    u = jnp.square(jnp.maximum(u, 0.0)).astype(wd_ref.dtype)
    acc_ref[...] += jnp.dot(u, wd_ref[...], preferred_element_type=F32)

    @pl.when(f == pl.num_programs(1) - 1)
    def _():
        o_ref[...] = x_ref[...] + acc_ref[...]


def mlp_block(x, g, w_up, w_down, *, tm, tf):
    t, d = x.shape
    ff = w_up.shape[1]
    need = (4 * _nbytes((tm, d), F32) + 2 * _nbytes((d, tf), w_up.dtype)
            + 2 * _nbytes((tf, d), w_down.dtype) + _nbytes((tm, d), w_up.dtype)
            + _nbytes((tm, d), F32) + 2 * _nbytes((tm, tf), F32))
    return pl.pallas_call(
        _mlp_kernel,
        grid=(t // tm, ff // tf),
        in_specs=[
            pl.BlockSpec((tm, d), lambda i, f: (i, 0)),
            pl.BlockSpec((1, d), lambda i, f: (0, 0)),
            pl.BlockSpec((d, tf), lambda i, f: (0, f)),
            pl.BlockSpec((tf, d), lambda i, f: (f, 0)),
        ],
        out_specs=pl.BlockSpec((tm, d), lambda i, f: (i, 0)),
        out_shape=jax.ShapeDtypeStruct((t, d), F32),
        scratch_shapes=[pltpu.VMEM((tm, d), w_up.dtype), pltpu.VMEM((tm, d), F32)],
        compiler_params=pltpu.CompilerParams(
            dimension_semantics=("parallel", "arbitrary"),
            vmem_limit_bytes=_vmem_limit(need)),
        name="mlp_block",
    )(x, g.reshape(1, d), w_up, w_down)


DSA_TN = 2 * LANES
Q_COLS = N_HEADS * HEAD_DIM
IQ_COLS = IDX_HEADS * IDX_DIM
KV_COLS = N_KV_HEADS * HEAD_DIM
DSA_COLS = Q_COLS + IQ_COLS + 2 * KV_COLS + IDX_DIM + LANES
Q_BLOCKS = Q_COLS // DSA_TN
IQ_BLOCKS = IQ_COLS // DSA_TN
KV_BLOCKS = KV_COLS // DSA_TN


V_FIRST_BLOCK = Q_BLOCKS + IQ_BLOCKS + KV_BLOCKS


def _dsa_proj_kernel(x_ref, g_ref, w_ref, qn_ref, kn_ref, rc_ref, rs_ref,
                     ic_ref, ia_ref, ib_ref, o_ref, iw_ref, vt_ref, h_ref):
    j = pl.program_id(1)

    @pl.when(j == 0)
    def _():
        h_ref[...] = _rms_norm_rows(x_ref[...], g_ref[...]).astype(h_ref.dtype)

    y = jnp.dot(h_ref[...], w_ref[...], preferred_element_type=F32)

    def rope(z):
        return z * rc_ref[...] + pltpu.roll(z, HEAD_DIM // 2, 1) * rs_ref[...]

    def idx_rope(z):
        half = IDX_ROPE_DIM // 2
        return (z * ic_ref[...] + pltpu.roll(z, LANES - half, 1) * ia_ref[...]
                + pltpu.roll(z, half, 1) * ib_ref[...])

    def store_heads(fn):
        o_ref[...] = jnp.concatenate(
            [fn(y[:, :LANES]), fn(y[:, LANES:])], axis=1).astype(o_ref.dtype)

    @pl.when(j < Q_BLOCKS)
    def _():
        scale = HEAD_DIM ** -0.5 * math.log2(math.e)
        store_heads(lambda z: rope(_rms_norm_rows(z, qn_ref[...])) * scale)

    @pl.when((j >= Q_BLOCKS) & (j < Q_BLOCKS + IQ_BLOCKS))
    def _():
        store_heads(idx_rope)

    @pl.when((j >= Q_BLOCKS + IQ_BLOCKS) & (j < Q_BLOCKS + IQ_BLOCKS + KV_BLOCKS))
    def _():
        store_heads(lambda z: rope(_rms_norm_rows(z, kn_ref[...])))

    @pl.when((j >= V_FIRST_BLOCK) & (j < V_FIRST_BLOCK + KV_BLOCKS))
    def _():
        o_ref[...] = y.astype(o_ref.dtype)
        vt_ref[...] = jnp.transpose(y).astype(vt_ref.dtype)

    @pl.when(j == V_FIRST_BLOCK + KV_BLOCKS)
    def _():
        o_ref[...] = jnp.concatenate(
            [idx_rope(y[:, :LANES]), jnp.zeros_like(y[:, LANES:])], axis=1).astype(o_ref.dtype)
        iw_ref[...] = y[:, LANES:] * (IDX_HEADS ** -0.5 * IDX_DIM ** -0.5)


def dsa_projection(x, g, w, q_norm, k_norm, tables, *, tm):
    t, d = x.shape
    rc, rs, ic, ia, ib = tables
    tab_spec = pl.BlockSpec((tm, LANES), lambda i, j: (i, 0))
    vec_spec = pl.BlockSpec((1, LANES), lambda i, j: (0, 0))
    need = (2 * _nbytes((tm, d), F32) + 2 * _nbytes((d, DSA_TN), w.dtype)
            + 2 * _nbytes((tm, DSA_TN), w.dtype) + _nbytes((tm, d), w.dtype)
            + 12 * _nbytes((tm, LANES), F32))
    return pl.pallas_call(
        _dsa_proj_kernel,
        grid=(t // tm, DSA_COLS // DSA_TN),
        in_specs=[
            pl.BlockSpec((tm, d), lambda i, j: (i, 0)),
            pl.BlockSpec((1, d), lambda i, j: (0, 0)),
            pl.BlockSpec((d, DSA_TN), lambda i, j: (0, j)),
            vec_spec, vec_spec, tab_spec, tab_spec, tab_spec, tab_spec, tab_spec,
        ],
        out_specs=[
            pl.BlockSpec((tm, DSA_TN), lambda i, j: (i, j)),
            pl.BlockSpec((tm, LANES), lambda i, j: (i, 0)),
            pl.BlockSpec((DSA_TN, tm), lambda i, j: (jnp.clip(j - V_FIRST_BLOCK, 0, KV_BLOCKS - 1), i)),
        ],
        out_shape=[
            jax.ShapeDtypeStruct((t, DSA_COLS), w.dtype),
            jax.ShapeDtypeStruct((t, LANES), F32),
            jax.ShapeDtypeStruct((KV_COLS, t), w.dtype),
        ],
        scratch_shapes=[pltpu.VMEM((tm, d), w.dtype)],
        compiler_params=pltpu.CompilerParams(
            dimension_semantics=("parallel", "arbitrary"),
            vmem_limit_bytes=_vmem_limit(need)),
        name="dsa_projection",
    )(x, g.reshape(1, d), w, q_norm.reshape(1, LANES), k_norm.reshape(1, LANES), rc, rs, ic, ia, ib)


def _float_key(v):
    b = lax.bitcast_convert_type(v, jnp.int32)
    return b ^ ((b >> 31) & 0x7FFFFFFF)


def _key_float(k):
    return lax.bitcast_convert_type(k ^ ((k >> 31) & 0x7FFFFFFF), F32)


_KLO, _KHI, _CLO = range(3)
_SEARCH_CAP = 96
ONES_ROWS = 2 * SUBLANES
ATTN_LOOKAHEAD = 1


def _dsa_core_kernel(q_ref, iq_ref, iw_ref, k_ref, vt_ref, ik_ref, o_ref,
                     key_ref, bias_ref, s_ref, wt_ref, st_ref, m_ref, l_ref, acc_ref, *, tq, topk, kt, ka):
    i = pl.program_id(0)
    n_tiles = (i + kt) // kt
    group = N_HEADS // N_KV_HEADS
    nt = (((1,), (1,)), ((), ()))

    wt_ref[...] = jnp.transpose(iw_ref[...])

    krow = lax.broadcasted_iota(jnp.int32, (tq, tq), 0)
    qcol = lax.broadcasted_iota(jnp.int32, (tq, tq), 1)
    diag_causal = krow <= qcol

    def chunk_start(c):
        return pl.multiple_of(c * tq, tq)

    def index_tile(ti, carry):
        smin, smax = carry
        iks = [ik_ref[pl.ds(chunk_start(ti * kt + u), tq), :] for u in range(kt)]
        scores = [jnp.zeros((tq, tq), F32) for _ in range(kt)]
        for h in range(IDX_HEADS):
            iqh = iq_ref[:, h * IDX_DIM:(h + 1) * IDX_DIM]
            w = jnp.broadcast_to(wt_ref[h:h + 1, :], (tq, tq))
            for u in range(kt):
                logits = lax.dot_general(iks[u], iqh, nt, preferred_element_type=F32)
                scores[u] = scores[u] + w * jnp.maximum(logits, 0.0)
        for u in range(kt):
            c = ti * kt + u
            valid = (c < i) | ((c == i) & diag_causal)
            key_ref[c] = jnp.where(valid, _float_key(scores[u]), INT_MIN)
            smin = jnp.minimum(smin, jnp.min(jnp.where(valid, scores[u], jnp.inf), axis=0, keepdims=True))
            smax = jnp.maximum(smax, jnp.max(jnp.where(valid, scores[u], -jnp.inf), axis=0, keepdims=True))
        return smin, smax

    smin, smax = lax.fori_loop(
        0, n_tiles, index_tile,
        (jnp.full((1, tq), jnp.inf, F32), jnp.full((1, tq), -jnp.inf, F32)))

    def count_ge(cand):
        cand_b = jnp.broadcast_to(cand, (tq, tq))

        def body(ti, cnt):
            for u in range(kt):
                hit = jnp.where(key_ref[ti * kt + u] >= cand_b, 1.0, 0.0)
                cnt = cnt + jnp.sum(hit, axis=0, keepdims=True)
            return cnt

        return lax.fori_loop(0, n_tiles, body, jnp.zeros((1, tq), F32))

    st_ref[_KLO] = _float_key(smin)
    st_ref[_KHI] = _float_key(smax) + 1
    st_ref[_CLO] = lax.broadcasted_iota(jnp.int32, (1, tq), 1) + (i * tq + 1)

    def unfinished(klo, khi, clo):
        return (clo > topk) & ((khi - klo) != 1)

    def search_cond(st):
        step, pending = st
        return (step < _SEARCH_CAP) & (pending > 0)

    def search_body(st):
        step, _ = st
        klo, khi, clo = st_ref[_KLO], st_ref[_KHI], st_ref[_CLO]
        live = unfinished(klo, khi, clo)
        vmid = 0.5 * _key_float(klo) + 0.5 * _key_float(khi)
        cmid = _float_key(vmid)
        kmid = klo + lax.shift_right_logical(khi - klo, 1)
        cand = jnp.where((cmid > klo) & (cmid < khi), cmid, kmid)
        cand = jnp.where(live, cand, klo)
        cnt = count_ge(cand).astype(jnp.int32)
        take = cnt >= topk
        klo = jnp.where(live & take, cand, klo)
        clo = jnp.where(live & take, cnt, clo)
        khi = jnp.where(live & jnp.logical_not(take), cand, khi)
        st_ref[_KLO], st_ref[_KHI], st_ref[_CLO] = klo, khi, clo
        pending = jnp.max(jnp.where(unfinished(klo, khi, clo), 1, 0))
        return step + 1, pending

    pending0 = jnp.max(jnp.where(unfinished(st_ref[_KLO], st_ref[_KHI], st_ref[_CLO]), 1, 0))
    lax.while_loop(search_cond, search_body, (jnp.int32(0), pending0))
    thr_b = jnp.broadcast_to(st_ref[_KLO], (tq, tq))

    m_ref[...] = jnp.full(m_ref.shape, NEG_BIG, F32)
    l_ref[...] = jnp.zeros(l_ref.shape, F32)
    acc_ref[...] = jnp.zeros(acc_ref.shape, F32)

    def attend_tile(ti, carry):
        for u in range(ka):
            bias_ref[u] = jnp.where(key_ref[ti * ka + u] >= thr_b, 0.0, NEG_BIG)
        starts = [chunk_start(ti * ka + u) for u in range(ka)]
        slots = ATTN_LOOKAHEAD + 1
        dyn0 = jnp.minimum(ti, 0)
        ones_rows = jnp.ones((ONES_ROWS, tq), vt_ref.dtype)

        def logits(h):
            cols = slice(h // group * HEAD_DIM, (h // group + 1) * HEAD_DIM)
            qh = q_ref[:, h * HEAD_DIM:(h + 1) * HEAD_DIM]
            top = None
            for u in range(ka):
                s = lax.dot_general(k_ref[pl.ds(starts[u], tq), cols], qh, nt,
                                    preferred_element_type=F32) + bias_ref[u]
                s_ref[h % slots + dyn0, u] = s
                top = s if top is None else jnp.maximum(top, s)
            return jnp.max(top, axis=0, keepdims=True)

        queued = [logits(h) for h in range(ATTN_LOOKAHEAD)]
        for h in range(N_HEADS):
            tile_max = queued.pop(0)
            if h + ATTN_LOOKAHEAD < N_HEADS:
                queued.append(logits(h + ATTN_LOOKAHEAD))
            cols = slice(h // group * HEAD_DIM, (h // group + 1) * HEAD_DIM)
            m_prev = m_ref[h]
            m_cur = jnp.maximum(m_prev, tile_max)
            alpha = jnp.exp2(m_prev - m_cur)
            pv = jnp.zeros((HEAD_DIM + ONES_ROWS, tq), F32)
            for u in range(ka):
                p = jnp.exp2(s_ref[h % slots + dyn0, u] - m_cur)
                v_aug = jnp.concatenate([vt_ref[cols, pl.ds(starts[u], tq)], ones_rows], axis=0)
                pv = pv + jnp.dot(v_aug, p.astype(vt_ref.dtype), preferred_element_type=F32)
            l_ref[h] = alpha * l_ref[h] + pv[HEAD_DIM:HEAD_DIM + 1, :]
            pv = pv[:HEAD_DIM, :]
            acc_ref[h] = alpha * acc_ref[h] + pv
            m_ref[h] = m_cur
        return carry

    lax.fori_loop(0, (i + ka) // ka, attend_tile, 0)

    for h in range(N_HEADS):
        o_ref[:, h * HEAD_DIM:(h + 1) * HEAD_DIM] = jnp.transpose(acc_ref[h] / l_ref[h]).astype(o_ref.dtype)


def dsa_core(proj, iw, vt, *, tq, topk, kt, ka):
    t = proj.shape[0]
    assert (t // tq) % kt == 0 and kt % ka == 0
    k_block = (Q_COLS + IQ_COLS) // KV_COLS
    ik_block = (Q_COLS + IQ_COLS + 2 * KV_COLS) // IDX_DIM
    resident = dict(pipeline_mode=pl.Buffered(1))
    need = (4 * _nbytes((tq, Q_COLS), proj.dtype) + 2 * _nbytes((tq, LANES), F32)
            + 2 * _nbytes((t, KV_COLS), proj.dtype) + _nbytes((t, IDX_DIM), proj.dtype)
            + 2 * _nbytes((tq, Q_COLS), proj.dtype)
            + _nbytes((t // tq, tq, tq), jnp.int32) + _nbytes((kt, tq, tq), F32)
            + _nbytes((LANES, tq), F32) + (3 + 2 * N_HEADS) * _nbytes((SUBLANES, tq), F32)
            + _nbytes((N_HEADS, HEAD_DIM, tq), F32)
            + (ATTN_LOOKAHEAD + 3) * _nbytes((ka, tq, tq), F32))
    return pl.pallas_call(
        functools.partial(_dsa_core_kernel, tq=tq, topk=topk, kt=kt, ka=ka),
        grid=(t // tq,),
        in_specs=[
            pl.BlockSpec((tq, Q_COLS), lambda i: (i, 0)),
            pl.BlockSpec((tq, IQ_COLS), lambda i: (i, 1)),
            pl.BlockSpec((tq, LANES), lambda i: (i, 0)),
            pl.BlockSpec((t, KV_COLS), lambda i: (0, k_block), **resident),
            pl.BlockSpec((KV_COLS, t), lambda i: (0, 0), **resident),
            pl.BlockSpec((t, IDX_DIM), lambda i: (0, ik_block), **resident),
        ],
        out_specs=pl.BlockSpec((tq, Q_COLS), lambda i: (i, 0)),
        out_shape=jax.ShapeDtypeStruct((t, Q_COLS), proj.dtype),
        scratch_shapes=[
            pltpu.VMEM((t // tq, tq, tq), jnp.int32),
            pltpu.VMEM((ka, tq, tq), F32),
            pltpu.VMEM((ATTN_LOOKAHEAD + 1, ka, tq, tq), F32),
            pltpu.VMEM((LANES, tq), F32),
            pltpu.VMEM((3, 1, tq), jnp.int32),
            pltpu.VMEM((N_HEADS, 1, tq), F32),
            pltpu.VMEM((N_HEADS, 1, tq), F32),
            pltpu.VMEM((N_HEADS, HEAD_DIM, tq), F32),
        ],
        compiler_params=pltpu.CompilerParams(
            dimension_semantics=("arbitrary",),
            vmem_limit_bytes=_vmem_limit(need)),
        name="dsa_core",
    )(proj, proj, iw, proj, vt, proj)


def _rope_tables(pos):
    def angles(dim):
        inv = 1.0 / (ROPE_THETA ** (jnp.arange(0, dim, 2, dtype=F32) / dim))
        return pos.astype(F32)[:, None] * inv

    ang = angles(HEAD_DIM)
    cos, sin = jnp.cos(ang), jnp.sin(ang)
    rc = jnp.concatenate([cos, cos], axis=1)
    rs = jnp.concatenate([-sin, sin], axis=1)
    iang = angles(IDX_ROPE_DIM)
    icos, isin = jnp.cos(iang), jnp.sin(iang)
    rest = LANES - IDX_ROPE_DIM
    zeros = jnp.zeros_like(isin)
    ic = jnp.concatenate([icos, icos, jnp.ones((pos.shape[0], rest), F32)], axis=1)
    ia = jnp.concatenate([-isin, zeros, jnp.zeros((pos.shape[0], rest), F32)], axis=1)
    ib = jnp.concatenate([zeros, isin, jnp.zeros((pos.shape[0], rest), F32)], axis=1)
    return rc, rs, ic, ia, ib


def _reorder_dsa_weight(w_in):
    cuts = [Q_COLS, Q_COLS + KV_COLS, Q_COLS + 2 * KV_COLS, Q_COLS + 2 * KV_COLS + IQ_COLS,
            Q_COLS + 2 * KV_COLS + IQ_COLS + IDX_DIM]
    wq, wk, wv, wiq, wik, wiw = jnp.split(w_in, cuts, axis=1)
    pad = jnp.zeros((w_in.shape[0], LANES - wiw.shape[1]), w_in.dtype)
    return jnp.concatenate([wq, wiq, wk, wv, wik, wiw, pad], axis=1).astype(MXU_DTYPE)


def dsa_layer(x, pos, norm_g, w_in, q_norm, k_norm, w_out, tiles):
    t = x.shape[0]
    proj, iw, vt = dsa_projection(x, norm_g, _reorder_dsa_weight(w_in), q_norm, k_norm,
                                  _rope_tables(pos), tm=tiles["proj_tm"])
    attn = dsa_core(proj, iw, vt, tq=tiles["tq"], topk=min(TOPK_MAX, t // 4), kt=tiles["kt"], ka=tiles["ka"])
    return matmul_residual(attn, w_out.astype(MXU_DTYPE), x, tm=tiles["res_tm"], tn=tiles["res_tn"])


def _shift_rows(x, d, fill):
    rows = lax.broadcasted_iota(jnp.int32, x.shape, 0)
    return jnp.where(rows >= d, pltpu.roll(x, d, 0), fill)


def _rglru_kernel(gate_ref, xr_ref, cw_ref, cb_ref, wa_ref, ba_ref, wx_ref, bx_ref, lam_ref,
                  y_ref, hcar_ref, xprev_ref, *, tt):
    @pl.when(pl.program_id(1) == 0)
    def _():
        hcar_ref[...] = jnp.zeros_like(hcar_ref)
        xprev_ref[...] = jnp.zeros_like(xprev_ref)

    xr = xr_ref[...]
    ext = jnp.concatenate([xprev_ref[...], xr], axis=0)
    cw = cw_ref[...]
    xc = cb_ref[...] + xr * cw[CONV_WIDTH - 1:CONV_WIDTH, :]
    for d in range(1, CONV_WIDTH):
        xc = xc + pltpu.roll(ext, d, 0)[SUBLANES:, :] * cw[CONV_WIDTH - 1 - d:CONV_WIDTH - d, :]
    xprev_ref[...] = xr[tt - SUBLANES:, :]

    xcb = xc.astype(wa_ref.dtype)
    r = jax.nn.sigmoid(jnp.dot(xcb, wa_ref[...], preferred_element_type=F32) + ba_ref[...])
    ig = jax.nn.sigmoid(jnp.dot(xcb, wx_ref[...], preferred_element_type=F32) + bx_ref[...])
    nlam = -lam_ref[...]
    softplus = jnp.maximum(nlam, 0.0) + jnp.log(1.0 + jnp.exp(-jnp.abs(nlam)))
    log_a = -LRU_C * r * softplus
    a = jnp.exp(log_a)
    mult = jnp.sqrt(1.0 - jnp.exp(2.0 * log_a))
    b = xc * ig * mult

    d = 1
    while d < tt:
        b = a * _shift_rows(b, d, 0.0) + b
        a = a * _shift_rows(a, d, 1.0)
        d *= 2
    h = b + a * hcar_ref[...]
    hcar_ref[...] = h[tt - 1:tt, :]
    y_ref[...] = (h * jax.nn.gelu(gate_ref[...])).astype(y_ref.dtype)


def rglru_scan(proj, conv_w, conv_b, wa, ba, wx, bx, lam, *, tt):
    t = proj.shape[0]
    width = conv_w.shape[1]
    nb, blk, _ = wa.shape
    vec = lambda: pl.BlockSpec((1, blk), lambda n, s: (0, n))
    need = (4 * _nbytes((tt, blk), F32) + 2 * _nbytes((tt, blk), MXU_DTYPE)
            + 4 * _nbytes((blk, blk), wa.dtype) + 24 * _nbytes((tt, blk), F32))
    return pl.pallas_call(
        functools.partial(_rglru_kernel, tt=tt),
        grid=(nb, t // tt),
        in_specs=[
            pl.BlockSpec((tt, blk), lambda n, s: (s, n)),
            pl.BlockSpec((tt, blk), lambda n, s: (s, nb + n)),
            pl.BlockSpec((CONV_WIDTH, blk), lambda n, s: (0, n)),
            vec(),
            pl.BlockSpec((None, blk, blk), lambda n, s: (n, 0, 0)),
            vec(),
            pl.BlockSpec((None, blk, blk), lambda n, s: (n, 0, 0)),
            vec(),
            vec(),
        ],
        out_specs=pl.BlockSpec((tt, blk), lambda n, s: (s, n)),
        out_shape=jax.ShapeDtypeStruct((t, width), MXU_DTYPE),
        scratch_shapes=[pltpu.VMEM((1, blk), F32), pltpu.VMEM((SUBLANES, blk), F32)],
        compiler_params=pltpu.CompilerParams(
            dimension_semantics=("parallel", "arbitrary"),
            vmem_limit_bytes=_vmem_limit(need)),
        name="rglru_scan",
    )(proj, proj, conv_w, conv_b.reshape(1, width), wa, ba.reshape(1, width),
      wx, bx.reshape(1, width), lam.reshape(1, width))


def rglru_layer(x, norm_g, w_in, conv_w, conv_b, wa, ba, wx, bx, lam, w_out, tiles):
    proj = norm_matmul(x, norm_g, w_in.astype(MXU_DTYPE), tm=tiles["proj_tm"], tn=tiles["proj_tn"],
                       out_dtype=F32)
    y = rglru_scan(proj, conv_w, conv_b, wa.astype(MXU_DTYPE), ba, wx.astype(MXU_DTYPE), bx, lam,
                   tt=tiles["scan_tt"])
    return matmul_residual(y, w_out.astype(MXU_DTYPE), x, tm=tiles["res_tm"], tn=tiles["res_tn"])


POOL_HALO = max(POOL_WINDOWS)


def _pool_kernel(x_ref, g_ref, w_ref, b_ref, s_ref, o_ref, halo_ref, *, tt):
    blk = pl.program_id(0)

    @pl.when(blk == 0)
    def _():
        halo_ref[...] = jnp.zeros_like(halo_ref)

    x = x_ref[...]
    h = _rms_norm_rows(x, g_ref[...])
    ext = jnp.concatenate([halo_ref[...], h], axis=0)
    halo_ref[...] = h[tt - POOL_HALO:, :]

    gw = h.shape[1] // len(POOL_WINDOWS)
    t1 = (lax.broadcasted_iota(jnp.int32, (tt, gw), 0) + (blk * tt + 1)).astype(F32)
    for gi, win in enumerate(POOL_WINDOWS):
        cols = slice(gi * gw, (gi + 1) * gw)
        acc = ext[:, cols]
        d = 1
        while d < win:
            acc = acc + pltpu.roll(acc, d, 0)
            d *= 2
        mean = acc[POOL_HALO:, :] / jnp.minimum(t1, float(win))
        y = (mean - h[:, cols]).astype(w_ref.dtype)
        z = jnp.dot(y, w_ref[gi], preferred_element_type=F32) + b_ref[gi:gi + 1, :]
        o_ref[:, cols] = x[:, cols] + z * s_ref[:, cols]


def pool_layer(x, norm_g, w_group, b_group, scale, tiles):
    t, d = x.shape
    tt = tiles["pool_tt"]
    ng, gw, _ = w_group.shape
    need = (4 * _nbytes((tt, d), F32) + 2 * _nbytes((ng, gw, gw), MXU_DTYPE)
            + 6 * _nbytes((tt, d), F32))
    return pl.pallas_call(
        functools.partial(_pool_kernel, tt=tt),
        grid=(t // tt,),
        in_specs=[
            pl.BlockSpec((tt, d), lambda i: (i, 0)),
            pl.BlockSpec((1, d), lambda i: (0, 0)),
            pl.BlockSpec((ng, gw, gw), lambda i: (0, 0, 0)),
            pl.BlockSpec((ng, gw), lambda i: (0, 0)),
            pl.BlockSpec((1, d), lambda i: (0, 0)),
        ],
        out_specs=pl.BlockSpec((tt, d), lambda i: (i, 0)),
        out_shape=jax.ShapeDtypeStruct((t, d), F32),
        scratch_shapes=[pltpu.VMEM((POOL_HALO, d), F32)],
        compiler_params=pltpu.CompilerParams(
            dimension_semantics=("arbitrary",),
            vmem_limit_bytes=_vmem_limit(need)),
        name="pool_mixer",
    )(x, norm_g.reshape(1, d), w_group.astype(MXU_DTYPE), b_group, scale.reshape(1, d))


def _tiles(t):
    big = min(t, 512)
    return {
        "proj_tm": big, "proj_tn": 512,
        "res_tm": big, "res_tn": 512,
        "tq": min(t, 256), "kt": min(4, t // min(t, 256)), "ka": min(4, t // min(t, 256)),
        "scan_tt": min(t, 256),
        "pool_tt": min(t, 256),
        "mlp_tm": big, "mlp_tf": 512,
    }


def kernel(x, positions, attn_norm, attn_w_in, attn_q_norm, attn_k_norm, attn_w_out, rnn_norm, rnn_w_in, rnn_conv_w, rnn_conv_b, rnn_gate_a_w, rnn_gate_a_b, rnn_gate_x_w, rnn_gate_x_b, rnn_lambda, rnn_w_out, pool_norm, pool_w, pool_b, pool_scale, mlp_norm, mlp_w_up, mlp_w_down):
    batch, t, d = x.shape
    depth = mlp_norm.shape[0]
    tiles = _tiles(t)
    outs = []
    for bi in range(batch):
        xb = x[bi]
        pos = positions[bi]
        for i in range(depth):
            kind, j = i % N_MIXERS, i // N_MIXERS
            if kind == 0:
                xb = dsa_layer(xb, pos, attn_norm[j], attn_w_in[j], attn_q_norm[j], attn_k_norm[j],
                               attn_w_out[j], tiles)
            elif kind == 1:
                xb = rglru_layer(xb, rnn_norm[j], rnn_w_in[j], rnn_conv_w[j], rnn_conv_b[j],
                                 rnn_gate_a_w[j], rnn_gate_a_b[j], rnn_gate_x_w[j], rnn_gate_x_b[j],
                                 rnn_lambda[j], rnn_w_out[j], tiles)
            else:
                xb = pool_layer(xb, pool_norm[j], pool_w[j], pool_b[j], pool_scale[j], tiles)
            xb = mlp_block(xb, mlp_norm[i], mlp_w_up[i].astype(MXU_DTYPE),
                           mlp_w_down[i].astype(MXU_DTYPE), tm=tiles["mlp_tm"], tf=tiles["mlp_tf"])
        outs.append(xb)
    return jnp.stack(outs, axis=0)
```

```python
import functools
import math

import jax
import jax.numpy as jnp
from jax import lax
from jax.experimental import pallas as pl
from jax.experimental.pallas import tpu as pltpu

F32 = jnp.float32
MXU_DTYPE = jnp.bfloat16

N_MIXERS = 3
EPS = 1e-6
ROPE_THETA = 10000.0
HEAD_DIM = 128
N_HEADS = 16
N_KV_HEADS = 4
IDX_HEADS = 16
IDX_DIM = 128
IDX_ROPE_DIM = 64
TOPK_MAX = 256
CONV_WIDTH = 4
LRU_C = 8.0
POOL_WINDOWS = (2, 4, 8, 16)

LANES = 128
SUBLANES = 8
VMEM_BYTES_V7X = 64 * 1024 * 1024
VMEM_CAP_BYTES = VMEM_BYTES_V7X - 8 * 1024 * 1024

INT_MIN = -(2 ** 31)
NEG_BIG = -1e30


def _vmem_limit(block_bytes):
    return int(min(VMEM_CAP_BYTES, block_bytes * 3 // 2 + (4 << 20)))


def _nbytes(shape, dtype):
    return math.prod(shape) * jnp.dtype(dtype).itemsize


def _rms_norm_rows(x, g):
    ms = jnp.mean(x * x, axis=-1, keepdims=True)
    return x * lax.rsqrt(ms + EPS) * g


def _tile_lanes(x, width):
    reps = width // LANES
    return x if reps == 1 else jnp.concatenate([x] * reps, axis=1)


def _norm_matmul_kernel(x_ref, g_ref, w_ref, o_ref, h_ref):
    @pl.when(pl.program_id(1) == 0)
    def _():
        h_ref[...] = _rms_norm_rows(x_ref[...], g_ref[...]).astype(h_ref.dtype)

    o_ref[...] = jnp.dot(h_ref[...], w_ref[...], preferred_element_type=F32).astype(o_ref.dtype)


def norm_matmul(x, g, w, *, tm, tn, out_dtype):
    t, d = x.shape
    n = w.shape[1]
    need = (2 * _nbytes((tm, d), F32) + 2 * _nbytes((d, tn), w.dtype)
            + 2 * _nbytes((tm, tn), out_dtype) + _nbytes((tm, d), w.dtype))
    return pl.pallas_call(
        _norm_matmul_kernel,
        grid=(t // tm, n // tn),
        in_specs=[
            pl.BlockSpec((tm, d), lambda i, j: (i, 0)),
            pl.BlockSpec((1, d), lambda i, j: (0, 0)),
            pl.BlockSpec((d, tn), lambda i, j: (0, j)),
        ],
        out_specs=pl.BlockSpec((tm, tn), lambda i, j: (i, j)),
        out_shape=jax.ShapeDtypeStruct((t, n), out_dtype),
        scratch_shapes=[pltpu.VMEM((tm, d), w.dtype)],
        compiler_params=pltpu.CompilerParams(
            dimension_semantics=("parallel", "arbitrary"),
            vmem_limit_bytes=_vmem_limit(need)),
        name="norm_matmul",
    )(x, g.reshape(1, d), w)


def _matmul_residual_kernel(a_ref, w_ref, x_ref, o_ref):
    o_ref[...] = x_ref[...] + jnp.dot(a_ref[...], w_ref[...], preferred_element_type=F32)


def matmul_residual(a, w, x, *, tm, tn):
    t, k = a.shape
    n = w.shape[1]
    need = (2 * _nbytes((tm, k), a.dtype) + 2 * _nbytes((k, tn), w.dtype)
            + 4 * _nbytes((tm, tn), F32))
    return pl.pallas_call(
        _matmul_residual_kernel,
        grid=(t // tm, n // tn),
        in_specs=[
            pl.BlockSpec((tm, k), lambda i, j: (i, 0)),
            pl.BlockSpec((k, tn), lambda i, j: (0, j)),
            pl.BlockSpec((tm, tn), lambda i, j: (i, j)),
        ],
        out_specs=pl.BlockSpec((tm, tn), lambda i, j: (i, j)),
        out_shape=jax.ShapeDtypeStruct((t, n), F32),
        compiler_params=pltpu.CompilerParams(
            dimension_semantics=("parallel", "arbitrary"),
            vmem_limit_bytes=_vmem_limit(need)),
        name="matmul_residual",
    )(a, w, x)


def _mlp_kernel(x_ref, g_ref, wu_ref, wd_ref, o_ref, h_ref, acc_ref):
    f = pl.program_id(1)

    @pl.when(f == 0)
    def _():
        h_ref[...] = _rms_norm_rows(x_ref[...], g_ref[...]).astype(h_ref.dtype)
        acc_ref[...] = jnp.zeros_like(acc_ref)

    u = jnp.dot(h_ref[...], wu_ref[...], preferred_element_type=F32)
    u = jnp.square(jnp.maximum(u, 0.0)).astype(wd_ref.dtype)
    acc_ref[...] += jnp.dot(u, wd_ref[...], preferred_element_type=F32)

    @pl.when(f == pl.num_programs(1) - 1)
    def _():
        o_ref[...] = x_ref[...] + acc_ref[...]


def mlp_block(x, g, w_up, w_down, *, tm, tf):
    t, d = x.shape
    ff = w_up.shape[1]
    need = (4 * _nbytes((tm, d), F32) + 2 * _nbytes((d, tf), w_up.dtype)
            + 2 * _nbytes((tf, d), w_down.dtype) + _nbytes((tm, d), w_up.dtype)
            + _nbytes((tm, d), F32) + 2 * _nbytes((tm, tf), F32))
    return pl.pallas_call(
        _mlp_kernel,
        grid=(t // tm, ff // tf),
        in_specs=[
            pl.BlockSpec((tm, d), lambda i, f: (i, 0)),
            pl.BlockSpec((1, d), lambda i, f: (0, 0)),
            pl.BlockSpec((d, tf), lambda i, f: (0, f)),
            pl.BlockSpec((tf, d), lambda i, f: (f, 0)),
        ],
        out_specs=pl.BlockSpec((tm, d), lambda i, f: (i, 0)),
        out_shape=jax.ShapeDtypeStruct((t, d), F32),
        scratch_shapes=[pltpu.VMEM((tm, d), w_up.dtype), pltpu.VMEM((tm, d), F32)],
        compiler_params=pltpu.CompilerParams(
            dimension_semantics=("parallel", "arbitrary"),
            vmem_limit_bytes=_vmem_limit(need)),
        name="mlp_block",
    )(x, g.reshape(1, d), w_up, w_down)


DSA_TN = 2 * LANES
Q_COLS = N_HEADS * HEAD_DIM
IQ_COLS = IDX_HEADS * IDX_DIM
KV_COLS = N_KV_HEADS * HEAD_DIM
DSA_COLS = Q_COLS + IQ_COLS + 2 * KV_COLS + IDX_DIM + LANES
Q_BLOCKS = Q_COLS // DSA_TN
IQ_BLOCKS = IQ_COLS // DSA_TN
KV_BLOCKS = KV_COLS // DSA_TN


W_K_FIRST = Q_BLOCKS
W_V_FIRST = W_K_FIRST + KV_BLOCKS
W_IQ_FIRST = W_V_FIRST + KV_BLOCKS
W_LAST = W_IQ_FIRST + IQ_BLOCKS
OUT_K_FIRST = Q_BLOCKS + IQ_BLOCKS
OUT_V_FIRST = OUT_K_FIRST + KV_BLOCKS


def _dsa_out_block(jw):
    return jnp.where(jw < W_K_FIRST, jw,
                     jnp.where(jw < W_IQ_FIRST, jw + (OUT_K_FIRST - W_K_FIRST),
                               jnp.where(jw < W_LAST, jw - (W_IQ_FIRST - Q_BLOCKS), W_LAST)))


def _dsa_proj_kernel(x_ref, g_ref, w_ref, qn_ref, kn_ref, rc_ref, rs_ref,
                     ic_ref, ia_ref, ib_ref, o_ref, iw_ref, vt_ref, h_ref, y_ref):
    j = pl.program_id(1)
    jp = j - 1

    @pl.when(j == 0)
    def _():
        h_ref[...] = _rms_norm_rows(x_ref[...], g_ref[...]).astype(h_ref.dtype)

    def rope(z):
        return z * rc_ref[...] + pltpu.roll(z, HEAD_DIM // 2, 1) * rs_ref[...]

    def idx_rope(z):
        half = IDX_ROPE_DIM // 2
        return (z * ic_ref[...] + pltpu.roll(z, LANES - half, 1) * ia_ref[...]
                + pltpu.roll(z, half, 1) * ib_ref[...])

    def per_head(fn):
        def finish(y):
            o_ref[...] = jnp.concatenate(
                [fn(y[:, :LANES]), fn(y[:, LANES:])], axis=1).astype(o_ref.dtype)
        return finish

    def step(finish, multiply=True):
        y_prev = y_ref[...]
        if multiply:
            y_ref[...] = jnp.dot(h_ref[...], w_ref[...], preferred_element_type=F32)
        if finish is not None:
            finish(y_prev)

    @pl.when(j == 0)
    def _():
        step(None)

    @pl.when((jp >= 0) & (jp < W_K_FIRST))
    def _():
        scale = HEAD_DIM ** -0.5 * math.log2(math.e)
        step(per_head(lambda z: rope(_rms_norm_rows(z, qn_ref[...])) * scale))

    @pl.when((jp >= W_K_FIRST) & (jp < W_V_FIRST))
    def _():
        step(per_head(lambda z: rope(_rms_norm_rows(z, kn_ref[...]))))

    @pl.when((jp >= W_V_FIRST) & (jp < W_IQ_FIRST))
    def _():
        def finish(y):
            o_ref[...] = y.astype(o_ref.dtype)
            vt_ref[...] = jnp.transpose(y).astype(vt_ref.dtype)
        step(finish)

    @pl.when((jp >= W_IQ_FIRST) & (jp < W_LAST))
    def _():
        step(per_head(idx_rope))

    @pl.when(jp == W_LAST)
    def _():
        def finish(y):
            o_ref[...] = jnp.concatenate(
                [idx_rope(y[:, :LANES]), jnp.zeros_like(y[:, LANES:])], axis=1).astype(o_ref.dtype)
            lane = lax.broadcasted_iota(jnp.int32, (y.shape[0], LANES), 1)
            iw_ref[...] = jnp.where(lane < IDX_HEADS,
                                    y[:, LANES:] * (IDX_HEADS ** -0.5 * IDX_DIM ** -0.5), 0.0)
        step(finish, multiply=False)


def dsa_projection(x, g, w, q_norm, k_norm, tables, *, tm):
    t, d = x.shape
    assert w.shape[1] == Q_COLS + 2 * KV_COLS + IQ_COLS + IDX_DIM + IDX_HEADS
    rc, rs, ic, ia, ib = tables
    tab_spec = pl.BlockSpec((tm, LANES), lambda i, j: (i, 0))
    vec_spec = pl.BlockSpec((1, LANES), lambda i, j: (0, 0))
    need = (2 * _nbytes((tm, d), F32) + 2 * _nbytes((d, DSA_TN), w.dtype)
            + 2 * _nbytes((tm, DSA_TN), w.dtype) + _nbytes((tm, d), w.dtype)
            + 12 * _nbytes((tm, LANES), F32) + 3 * _nbytes((tm, DSA_TN), F32))
    return pl.pallas_call(
        _dsa_proj_kernel,
        grid=(t // tm, W_LAST + 2),
        in_specs=[
            pl.BlockSpec((tm, d), lambda i, j: (i, 0)),
            pl.BlockSpec((1, d), lambda i, j: (0, 0)),
            pl.BlockSpec((d, DSA_TN), lambda i, j: (0, jnp.minimum(j, W_LAST))),
            vec_spec, vec_spec, tab_spec, tab_spec, tab_spec, tab_spec, tab_spec,
        ],
        out_specs=[
            pl.BlockSpec((tm, DSA_TN), lambda i, j: (i, _dsa_out_block(jnp.maximum(j - 1, 0)))),
            pl.BlockSpec((tm, LANES), lambda i, j: (i, 0)),
            pl.BlockSpec((DSA_TN, tm), lambda i, j: (jnp.clip(j - 1 - W_V_FIRST, 0, KV_BLOCKS - 1), i)),
        ],
        out_shape=[
            jax.ShapeDtypeStruct((t, DSA_COLS), w.dtype),
            jax.ShapeDtypeStruct((t, LANES), F32),
            jax.ShapeDtypeStruct((KV_COLS, t), w.dtype),
        ],
        scratch_shapes=[pltpu.VMEM((tm, d), w.dtype), pltpu.VMEM((tm, DSA_TN), F32)],
        compiler_params=pltpu.CompilerParams(
            dimension_semantics=("parallel", "arbitrary"),
            vmem_limit_bytes=_vmem_limit(need)),
        name="dsa_projection",
    )(x, g.reshape(1, d), w, q_norm.reshape(1, LANES), k_norm.reshape(1, LANES), rc, rs, ic, ia, ib)


def _float_key(v):
    b = lax.bitcast_convert_type(v, jnp.int32)
    return b ^ ((b >> 31) & 0x7FFFFFFF)


def _key_float(k):
    return lax.bitcast_convert_type(k ^ ((k >> 31) & 0x7FFFFFFF), F32)


_KLO, _KHI, _CLO = range(3)
_SEARCH_CAP = 96
ONES_ROWS = 2 * SUBLANES
ATTN_LOOKAHEAD = 1


def _dsa_core_kernel(q_ref, iq_ref, iw_ref, k_ref, vt_ref, ik_ref, o_ref,
                     key_ref, bias_ref, s_ref, wt_ref, st_ref, m_ref, l_ref, acc_ref, *, tq, topk, kt, ka):
    i = pl.program_id(0)
    n_tiles = (i + kt) // kt
    group = N_HEADS // N_KV_HEADS
    nt = (((1,), (1,)), ((), ()))

    wt_ref[...] = jnp.transpose(iw_ref[...])

    krow = lax.broadcasted_iota(jnp.int32, (tq, tq), 0)
    qcol = lax.broadcasted_iota(jnp.int32, (tq, tq), 1)
    diag_causal = krow <= qcol

    def chunk_start(c):
        return pl.multiple_of(c * tq, tq)

    def index_tile(ti, carry):
        smin, smax = carry
        iks = [ik_ref[pl.ds(chunk_start(ti * kt + u), tq), :] for u in range(kt)]
        scores = [jnp.zeros((tq, tq), F32) for _ in range(kt)]
        for h in range(IDX_HEADS):
            iqh = iq_ref[:, h * IDX_DIM:(h + 1) * IDX_DIM]
            w = jnp.broadcast_to(wt_ref[h:h + 1, :], (tq, tq))
            for u in range(kt):
                logits = lax.dot_general(iks[u], iqh, nt, preferred_element_type=F32)
                scores[u] = scores[u] + w * jnp.maximum(logits, 0.0)
        for u in range(kt):
            c = ti * kt + u
            valid = (c < i) | ((c == i) & diag_causal)
            key_ref[c] = jnp.where(valid, _float_key(scores[u]), INT_MIN)
            smin = jnp.minimum(smin, jnp.min(jnp.where(valid, scores[u], jnp.inf), axis=0, keepdims=True))
            smax = jnp.maximum(smax, jnp.max(jnp.where(valid, scores[u], -jnp.inf), axis=0, keepdims=True))
        return smin, smax

    smin, smax = lax.fori_loop(
        0, n_tiles, index_tile,
        (jnp.full((1, tq), jnp.inf, F32), jnp.full((1, tq), -jnp.inf, F32)))

    def count_ge(cand):
        cand_b = jnp.broadcast_to(cand, (tq, tq))

        def body(ti, cnt):
            for u in range(kt):
                hit = jnp.where(key_ref[ti * kt + u] >= cand_b, 1.0, 0.0)
                cnt = cnt + jnp.sum(hit, axis=0, keepdims=True)
            return cnt

        return lax.fori_loop(0, n_tiles, body, jnp.zeros((1, tq), F32))

    st_ref[_KLO] = _float_key(smin)
    st_ref[_KHI] = _float_key(smax) + 1
    st_ref[_CLO] = lax.broadcasted_iota(jnp.int32, (1, tq), 1) + (i * tq + 1)

    def unfinished(klo, khi, clo):
        return (clo > topk) & ((khi - klo) != 1)

    def search_cond(st):
        step, pending = st
        return (step < _SEARCH_CAP) & (pending > 0)

    def search_body(st):
        step, _ = st
        klo, khi, clo = st_ref[_KLO], st_ref[_KHI], st_ref[_CLO]
        live = unfinished(klo, khi, clo)
        vmid = 0.5 * _key_float(klo) + 0.5 * _key_float(khi)
        cmid = _float_key(vmid)
        kmid = klo + lax.shift_right_logical(khi - klo, 1)
        cand = jnp.where((cmid > klo) & (cmid < khi), cmid, kmid)
        cand = jnp.where(live, cand, klo)
        cnt = count_ge(cand).astype(jnp.int32)
        take = cnt >= topk
        klo = jnp.where(live & take, cand, klo)
        clo = jnp.where(live & take, cnt, clo)
        khi = jnp.where(live & jnp.logical_not(take), cand, khi)
        st_ref[_KLO], st_ref[_KHI], st_ref[_CLO] = klo, khi, clo
        pending = jnp.max(jnp.where(unfinished(klo, khi, clo), 1, 0))
        return step + 1, pending

    pending0 = jnp.max(jnp.where(unfinished(st_ref[_KLO], st_ref[_KHI], st_ref[_CLO]), 1, 0))
    lax.while_loop(search_cond, search_body, (jnp.int32(0), pending0))
    thr_b = jnp.broadcast_to(st_ref[_KLO], (tq, tq))

    m_ref[...] = jnp.full(m_ref.shape, NEG_BIG, F32)
    l_ref[...] = jnp.zeros(l_ref.shape, F32)
    acc_ref[...] = jnp.zeros(acc_ref.shape, F32)

    def attend_tile(ti, carry):
        for u in range(ka):
            bias_ref[u] = jnp.where(key_ref[ti * ka + u] >= thr_b, 0.0, NEG_BIG)
        starts = [chunk_start(ti * ka + u) for u in range(ka)]
        slots = ATTN_LOOKAHEAD + 1
        dyn0 = jnp.minimum(ti, 0)
        ones_rows = jnp.ones((ONES_ROWS, tq), vt_ref.dtype)

        def logits(h):
            cols = slice(h // group * HEAD_DIM, (h // group + 1) * HEAD_DIM)
            qh = q_ref[:, h * HEAD_DIM:(h + 1) * HEAD_DIM]
            top = None
            for u in range(ka):
                s = lax.dot_general(k_ref[pl.ds(starts[u], tq), cols], qh, nt,
                                    preferred_element_type=F32) + bias_ref[u]
                s_ref[h % slots + dyn0, u] = s
                top = s if top is None else jnp.maximum(top, s)
            return jnp.max(top, axis=0, keepdims=True)

        queued = [logits(h) for h in range(ATTN_LOOKAHEAD)]
        for h in range(N_HEADS):
            tile_max = queued.pop(0)
            if h + ATTN_LOOKAHEAD < N_HEADS:
                queued.append(logits(h + ATTN_LOOKAHEAD))
            cols = slice(h // group * HEAD_DIM, (h // group + 1) * HEAD_DIM)
            m_prev = m_ref[h]
            m_cur = jnp.maximum(m_prev, tile_max)
            alpha = jnp.exp2(m_prev - m_cur)
            pv = jnp.zeros((HEAD_DIM + ONES_ROWS, tq), F32)
            for u in range(ka):
                p = jnp.exp2(s_ref[h % slots + dyn0, u] - m_cur)
                v_aug = jnp.concatenate([vt_ref[cols, pl.ds(starts[u], tq)], ones_rows], axis=0)
                pv = pv + jnp.dot(v_aug, p.astype(vt_ref.dtype), preferred_element_type=F32)
            l_ref[h] = alpha * l_ref[h] + pv[HEAD_DIM:HEAD_DIM + 1, :]
            pv = pv[:HEAD_DIM, :]
            acc_ref[h] = alpha * acc_ref[h] + pv
            m_ref[h] = m_cur
        return carry

    lax.fori_loop(0, (i + ka) // ka, attend_tile, 0)

    for h in range(N_HEADS):
        o_ref[:, h * HEAD_DIM:(h + 1) * HEAD_DIM] = jnp.transpose(acc_ref[h] / l_ref[h]).astype(o_ref.dtype)


def dsa_core(proj, iw, vt, *, tq, topk, kt, ka):
    t = proj.shape[0]
    assert (t // tq) % kt == 0 and kt % ka == 0
    k_block = (Q_COLS + IQ_COLS) // KV_COLS
    ik_block = (Q_COLS + IQ_COLS + 2 * KV_COLS) // IDX_DIM
    resident = dict(pipeline_mode=pl.Buffered(1))
    need = (4 * _nbytes((tq, Q_COLS), proj.dtype) + 2 * _nbytes((tq, LANES), F32)
            + 2 * _nbytes((t, KV_COLS), proj.dtype) + _nbytes((t, IDX_DIM), proj.dtype)
            + 2 * _nbytes((tq, Q_COLS), proj.dtype)
            + _nbytes((t // tq, tq, tq), jnp.int32) + _nbytes((kt, tq, tq), F32)
            + _nbytes((LANES, tq), F32) + (3 + 2 * N_HEADS) * _nbytes((SUBLANES, tq), F32)
            + _nbytes((N_HEADS, HEAD_DIM, tq), F32)
            + (ATTN_LOOKAHEAD + 3) * _nbytes((ka, tq, tq), F32))
    return pl.pallas_call(
        functools.partial(_dsa_core_kernel, tq=tq, topk=topk, kt=kt, ka=ka),
        grid=(t // tq,),
        in_specs=[
            pl.BlockSpec((tq, Q_COLS), lambda i: (i, 0)),
            pl.BlockSpec((tq, IQ_COLS), lambda i: (i, 1)),
            pl.BlockSpec((tq, LANES), lambda i: (i, 0)),
            pl.BlockSpec((t, KV_COLS), lambda i: (0, k_block), **resident),
            pl.BlockSpec((KV_COLS, t), lambda i: (0, 0), **resident),
            pl.BlockSpec((t, IDX_DIM), lambda i: (0, ik_block), **resident),
        ],
        out_specs=pl.BlockSpec((tq, Q_COLS), lambda i: (i, 0)),
        out_shape=jax.ShapeDtypeStruct((t, Q_COLS), proj.dtype),
        scratch_shapes=[
            pltpu.VMEM((t // tq, tq, tq), jnp.int32),
            pltpu.VMEM((ka, tq, tq), F32),
            pltpu.VMEM((ATTN_LOOKAHEAD + 1, ka, tq, tq), F32),
            pltpu.VMEM((LANES, tq), F32),
            pltpu.VMEM((3, 1, tq), jnp.int32),
            pltpu.VMEM((N_HEADS, 1, tq), F32),
            pltpu.VMEM((N_HEADS, 1, tq), F32),
            pltpu.VMEM((N_HEADS, HEAD_DIM, tq), F32),
        ],
        compiler_params=pltpu.CompilerParams(
            dimension_semantics=("arbitrary",),
            vmem_limit_bytes=_vmem_limit(need)),
        name="dsa_core",
    )(proj, proj, iw, proj, vt, proj)


def _rope_tables(pos):
    def angles(dim):
        inv = 1.0 / (ROPE_THETA ** (jnp.arange(0, dim, 2, dtype=F32) / dim))
        return pos.astype(F32)[:, None] * inv

    ang = angles(HEAD_DIM)
    cos, sin = jnp.cos(ang), jnp.sin(ang)
    rc = jnp.concatenate([cos, cos], axis=1)
    rs = jnp.concatenate([-sin, sin], axis=1)
    iang = angles(IDX_ROPE_DIM)
    icos, isin = jnp.cos(iang), jnp.sin(iang)
    rest = LANES - IDX_ROPE_DIM
    zeros = jnp.zeros_like(isin)
    ic = jnp.concatenate([icos, icos, jnp.ones((pos.shape[0], rest), F32)], axis=1)
    ia = jnp.concatenate([-isin, zeros, jnp.zeros((pos.shape[0], rest), F32)], axis=1)
    ib = jnp.concatenate([zeros, isin, jnp.zeros((pos.shape[0], rest), F32)], axis=1)
    return rc, rs, ic, ia, ib


def dsa_layer(x, pos, norm_g, w_in, q_norm, k_norm, w_out, tiles):
    t = x.shape[0]
    proj, iw, vt = dsa_projection(x, norm_g, w_in.astype(MXU_DTYPE), q_norm, k_norm,
                                  _rope_tables(pos), tm=tiles["proj_tm"])
    attn = dsa_core(proj, iw, vt, tq=tiles["tq"], topk=min(TOPK_MAX, t // 4), kt=tiles["kt"], ka=tiles["ka"])
    return matmul_residual(attn, w_out.astype(MXU_DTYPE), x, tm=tiles["res_tm"], tn=tiles["res_tn"])


def _shift_rows(x, d, fill):
    rows = lax.broadcasted_iota(jnp.int32, x.shape, 0)
    return jnp.where(rows >= d, pltpu.roll(x, d, 0), fill)


def _rglru_kernel(gate_ref, xr_ref, cw_ref, cb_ref, wa_ref, ba_ref, wx_ref, bx_ref, lam_ref,
                  y_ref, hcar_ref, xprev_ref, *, tt):
    @pl.when(pl.program_id(1) == 0)
    def _():
        hcar_ref[...] = jnp.zeros_like(hcar_ref)
        xprev_ref[...] = jnp.zeros_like(xprev_ref)

    xr = xr_ref[...]
    ext = jnp.concatenate([xprev_ref[...], xr], axis=0)
    cw = cw_ref[...]
    xc = cb_ref[...] + xr * cw[CONV_WIDTH - 1:CONV_WIDTH, :]
    for d in range(1, CONV_WIDTH):
        xc = xc + pltpu.roll(ext, d, 0)[SUBLANES:, :] * cw[CONV_WIDTH - 1 - d:CONV_WIDTH - d, :]
    xprev_ref[...] = xr[tt - SUBLANES:, :]

    xcb = xc.astype(wa_ref.dtype)
    r = jax.nn.sigmoid(jnp.dot(xcb, wa_ref[...], preferred_element_type=F32) + ba_ref[...])
    ig = jax.nn.sigmoid(jnp.dot(xcb, wx_ref[...], preferred_element_type=F32) + bx_ref[...])
    nlam = -lam_ref[...]
    softplus = jnp.maximum(nlam, 0.0) + jnp.log(1.0 + jnp.exp(-jnp.abs(nlam)))
    log_a = -LRU_C * r * softplus
    a = jnp.exp(log_a)
    mult = jnp.sqrt(1.0 - jnp.exp(2.0 * log_a))
    b = xc * ig * mult

    d = 1
    while d < tt:
        b = a * _shift_rows(b, d, 0.0) + b
        a = a * _shift_rows(a, d, 1.0)
        d *= 2
    h = b + a * hcar_ref[...]
    hcar_ref[...] = h[tt - 1:tt, :]
    y_ref[...] = (h * jax.nn.gelu(gate_ref[...])).astype(y_ref.dtype)


def rglru_scan(proj, conv_w, conv_b, wa, ba, wx, bx, lam, *, tt):
    t = proj.shape[0]
    width = conv_w.shape[1]
    nb, blk, _ = wa.shape
    vec = lambda: pl.BlockSpec((1, blk), lambda n, s: (0, n))
    need = (4 * _nbytes((tt, blk), F32) + 2 * _nbytes((tt, blk), MXU_DTYPE)
            + 4 * _nbytes((blk, blk), wa.dtype) + 24 * _nbytes((tt, blk), F32))
    return pl.pallas_call(
        functools.partial(_rglru_kernel, tt=tt),
        grid=(nb, t // tt),
        in_specs=[
            pl.BlockSpec((tt, blk), lambda n, s: (s, n)),
            pl.BlockSpec((tt, blk), lambda n, s: (s, nb + n)),
            pl.BlockSpec((CONV_WIDTH, blk), lambda n, s: (0, n)),
            vec(),
            pl.BlockSpec((None, blk, blk), lambda n, s: (n, 0, 0)),
            vec(),
            pl.BlockSpec((None, blk, blk), lambda n, s: (n, 0, 0)),
            vec(),
            vec(),
        ],
        out_specs=pl.BlockSpec((tt, blk), lambda n, s: (s, n)),
        out_shape=jax.ShapeDtypeStruct((t, width), MXU_DTYPE),
        scratch_shapes=[pltpu.VMEM((1, blk), F32), pltpu.VMEM((SUBLANES, blk), F32)],
        compiler_params=pltpu.CompilerParams(
            dimension_semantics=("parallel", "arbitrary"),
            vmem_limit_bytes=_vmem_limit(need)),
        name="rglru_scan",
    )(proj, proj, conv_w, conv_b.reshape(1, width), wa, ba.reshape(1, width),
      wx, bx.reshape(1, width), lam.reshape(1, width))


def rglru_layer(x, norm_g, w_in, conv_w, conv_b, wa, ba, wx, bx, lam, w_out, tiles):
    proj = norm_matmul(x, norm_g, w_in.astype(MXU_DTYPE), tm=tiles["proj_tm"], tn=tiles["proj_tn"],
                       out_dtype=F32)
    y = rglru_scan(proj, conv_w, conv_b, wa.astype(MXU_DTYPE), ba, wx.astype(MXU_DTYPE), bx, lam,
                   tt=tiles["scan_tt"])
    return matmul_residual(y, w_out.astype(MXU_DTYPE), x, tm=tiles["res_tm"], tn=tiles["res_tn"])


POOL_HALO = max(POOL_WINDOWS)


def _pool_kernel(x_ref, g_ref, w_ref, b_ref, s_ref, o_ref, halo_ref, *, tt):
    blk = pl.program_id(0)

    @pl.when(blk == 0)
    def _():
        halo_ref[...] = jnp.zeros_like(halo_ref)

    x = x_ref[...]
    h = _rms_norm_rows(x, g_ref[...])
    ext = jnp.concatenate([halo_ref[...], h], axis=0)
    halo_ref[...] = h[tt - POOL_HALO:, :]

    gw = h.shape[1] // len(POOL_WINDOWS)
    t1 = (lax.broadcasted_iota(jnp.int32, (tt, gw), 0) + (blk * tt + 1)).astype(F32)
    for gi, win in enumerate(POOL_WINDOWS):
        cols = slice(gi * gw, (gi + 1) * gw)
        acc = ext[:, cols]
        d = 1
        while d < win:
            acc = acc + pltpu.roll(acc, d, 0)
            d *= 2
        mean = acc[POOL_HALO:, :] / jnp.minimum(t1, float(win))
        y = (mean - h[:, cols]).astype(w_ref.dtype)
        z = jnp.dot(y, w_ref[gi], preferred_element_type=F32) + b_ref[gi:gi + 1, :]
        o_ref[:, cols] = x[:, cols] + z * s_ref[:, cols]


def pool_layer(x, norm_g, w_group, b_group, scale, tiles):
    t, d = x.shape
    tt = tiles["pool_tt"]
    ng, gw, _ = w_group.shape
    need = (4 * _nbytes((tt, d), F32) + 2 * _nbytes((ng, gw, gw), MXU_DTYPE)
            + 6 * _nbytes((tt, d), F32))
    return pl.pallas_call(
        functools.partial(_pool_kernel, tt=tt),
        grid=(t // tt,),
        in_specs=[
            pl.BlockSpec((tt, d), lambda i: (i, 0)),
            pl.BlockSpec((1, d), lambda i: (0, 0)),
            pl.BlockSpec((ng, gw, gw), lambda i: (0, 0, 0)),
            pl.BlockSpec((ng, gw), lambda i: (0, 0)),
            pl.BlockSpec((1, d), lambda i: (0, 0)),
        ],
        out_specs=pl.BlockSpec((tt, d), lambda i: (i, 0)),
        out_shape=jax.ShapeDtypeStruct((t, d), F32),
        scratch_shapes=[pltpu.VMEM((POOL_HALO, d), F32)],
        compiler_params=pltpu.CompilerParams(
            dimension_semantics=("arbitrary",),
            vmem_limit_bytes=_vmem_limit(need)),
        name="pool_mixer",
    )(x, norm_g.reshape(1, d), w_group.astype(MXU_DTYPE), b_group, scale.reshape(1, d))


def _tiles(t):
    big = min(t, 512)
    return {
        "proj_tm": big, "proj_tn": 512,
        "res_tm": big, "res_tn": 512,
        "tq": min(t, 256), "kt": min(4, t // min(t, 256)), "ka": min(4, t // min(t, 256)),
        "scan_tt": min(t, 256),
        "pool_tt": min(t, 256),
        "mlp_tm": big, "mlp_tf": 512,
    }


def kernel(x, positions, attn_norm, attn_w_in, attn_q_norm, attn_k_norm, attn_w_out, rnn_norm, rnn_w_in, rnn_conv_w, rnn_conv_b, rnn_gate_a_w, rnn_gate_a_b, rnn_gate_x_w, rnn_gate_x_b, rnn_lambda, rnn_w_out, pool_norm, pool_w, pool_b, pool_scale, mlp_norm, mlp_w_up, mlp_w_down):
    batch, t, d = x.shape
    depth = mlp_norm.shape[0]
    tiles = _tiles(t)
    outs = []
    for bi in range(batch):
        xb = x[bi]
        pos = positions[bi]
        for i in range(depth):
            kind, j = i % N_MIXERS, i // N_MIXERS
            if kind == 0:
                xb = dsa_layer(xb, pos, attn_norm[j], attn_w_in[j], attn_q_norm[j], attn_k_norm[j],
                               attn_w_out[j], tiles)
            elif kind == 1:
                xb = rglru_layer(xb, rnn_norm[j], rnn_w_in[j], rnn_conv_w[j], rnn_conv_b[j],
                                 rnn_gate_a_w[j], rnn_gate_a_b[j], rnn_gate_x_w[j], rnn_gate_x_b[j],
                                 rnn_lambda[j], rnn_w_out[j], tiles)
            else:
                xb = pool_layer(xb, pool_norm[j], pool_w[j], pool_b[j], pool_scale[j], tiles)
            xb = mlp_block(xb, mlp_norm[i], mlp_w_up[i].astype(MXU_DTYPE),
                           mlp_w_down[i].astype(MXU_DTYPE), tm=tiles["mlp_tm"], tf=tiles["mlp_tf"])
        outs.append(xb)
    return jnp.stack(outs, axis=0)
```

```python
import functools
import math

import jax
import jax.numpy as jnp
from jax import lax
from jax.experimental import pallas as pl
from jax.experimental.pallas import tpu as pltpu

F32 = jnp.float32
MXU_DTYPE = jnp.bfloat16

N_MIXERS = 3
EPS = 1e-6
ROPE_THETA = 10000.0
HEAD_DIM = 128
N_HEADS = 16
N_KV_HEADS = 4
IDX_HEADS = 16
IDX_DIM = 128
IDX_ROPE_DIM = 64
TOPK_MAX = 256
CONV_WIDTH = 4
LRU_C = 8.0
POOL_WINDOWS = (2, 4, 8, 16)

LANES = 128
SUBLANES = 8
VMEM_BYTES_V7X = 64 * 1024 * 1024
VMEM_CAP_BYTES = VMEM_BYTES_V7X - 8 * 1024 * 1024

INT_MIN = -(2 ** 31)
NEG_BIG = -1e30


def _vmem_limit(block_bytes):
    return int(min(VMEM_CAP_BYTES, block_bytes * 3 // 2 + (4 << 20)))


def _nbytes(shape, dtype):
    return math.prod(shape) * jnp.dtype(dtype).itemsize


def _rms_norm_rows(x, g):
    ms = jnp.mean(x * x, axis=-1, keepdims=True)
    return x * lax.rsqrt(ms + EPS) * g


def _tile_lanes(x, width):
    reps = width // LANES
    return x if reps == 1 else jnp.concatenate([x] * reps, axis=1)


def _norm_matmul_kernel(x_ref, g_ref, w_ref, o_ref, h_ref):
    @pl.when(pl.program_id(1) == 0)
    def _():
        h_ref[...] = _rms_norm_rows(x_ref[...], g_ref[...]).astype(h_ref.dtype)

    o_ref[...] = jnp.dot(h_ref[...], w_ref[...], preferred_element_type=F32).astype(o_ref.dtype)


def norm_matmul(x, g, w, *, tm, tn, out_dtype):
    t, d = x.shape
    n = w.shape[1]
    need = (2 * _nbytes((tm, d), F32) + 2 * _nbytes((d, tn), w.dtype)
            + 2 * _nbytes((tm, tn), out_dtype) + _nbytes((tm, d), w.dtype))
    return pl.pallas_call(
        _norm_matmul_kernel,
        grid=(t // tm, n // tn),
        in_specs=[
            pl.BlockSpec((tm, d), lambda i, j: (i, 0)),
            pl.BlockSpec((1, d), lambda i, j: (0, 0)),
            pl.BlockSpec((d, tn), lambda i, j: (0, j)),
        ],
        out_specs=pl.BlockSpec((tm, tn), lambda i, j: (i, j)),
        out_shape=jax.ShapeDtypeStruct((t, n), out_dtype),
        scratch_shapes=[pltpu.VMEM((tm, d), w.dtype)],
        compiler_params=pltpu.CompilerParams(
            dimension_semantics=("parallel", "arbitrary"),
            vmem_limit_bytes=_vmem_limit(need)),
        name="norm_matmul",
    )(x, g.reshape(1, d), w)


def _matmul_residual_kernel(a_ref, w_ref, x_ref, o_ref):
    o_ref[...] = x_ref[...] + jnp.dot(a_ref[...], w_ref[...], preferred_element_type=F32)


def matmul_residual(a, w, x, *, tm, tn):
    t, k = a.shape
    n = w.shape[1]
    need = (2 * _nbytes((tm, k), a.dtype) + 2 * _nbytes((k, tn), w.dtype)
            + 4 * _nbytes((tm, tn), F32))
    return pl.pallas_call(
        _matmul_residual_kernel,
        grid=(t // tm, n // tn),
        in_specs=[
            pl.BlockSpec((tm, k), lambda i, j: (i, 0)),
            pl.BlockSpec((k, tn), lambda i, j: (0, j)),
            pl.BlockSpec((tm, tn), lambda i, j: (i, j)),
        ],
        out_specs=pl.BlockSpec((tm, tn), lambda i, j: (i, j)),
        out_shape=jax.ShapeDtypeStruct((t, n), F32),
        compiler_params=pltpu.CompilerParams(
            dimension_semantics=("parallel", "arbitrary"),
            vmem_limit_bytes=_vmem_limit(need)),
        name="matmul_residual",
    )(a, w, x)


def _mlp_kernel(x_ref, g_ref, wu_ref, wd_ref, o_ref, h_ref, acc_ref):
    f = pl.program_id(1)

    @pl.when(f == 0)
    def _():
        h_ref[...] = _rms_norm_rows(x_ref[...], g_ref[...]).astype(h_ref.dtype)
        acc_ref[...] = jnp.zeros_like(acc_ref)

    u = jnp.dot(h_ref[...], wu_ref[...], preferred_element_type=F32)
    u = jnp.square(jnp.maximum(u, 0.0)).astype(wd_ref.dtype)
    acc_ref[...] += jnp.dot(u, wd_ref[...], preferred_element_type=F32)

    @pl.when(f == pl.num_programs(1) - 1)
    def _():
        o_ref[...] = x_ref[...] + acc_ref[...]


def mlp_block(x, g, w_up, w_down, *, tm, tf):
    t, d = x.shape
    ff = w_up.shape[1]
    need = (4 * _nbytes((tm, d), F32) + 2 * _nbytes((d, tf), w_up.dtype)
            + 2 * _nbytes((tf, d), w_down.dtype) + _nbytes((tm, d), w_up.dtype)
            + _nbytes((tm, d), F32) + 2 * _nbytes((tm, tf), F32))
    return pl.pallas_call(
        _mlp_kernel,
        grid=(t // tm, ff // tf),
        in_specs=[
            pl.BlockSpec((tm, d), lambda i, f: (i, 0)),
            pl.BlockSpec((1, d), lambda i, f: (0, 0)),
            pl.BlockSpec((d, tf), lambda i, f: (0, f)),
            pl.BlockSpec((tf, d), lambda i, f: (f, 0)),
        ],
        out_specs=pl.BlockSpec((tm, d), lambda i, f: (i, 0)),
        out_shape=jax.ShapeDtypeStruct((t, d), F32),
        scratch_shapes=[pltpu.VMEM((tm, d), w_up.dtype), pltpu.VMEM((tm, d), F32)],
        compiler_params=pltpu.CompilerParams(
            dimension_semantics=("parallel", "arbitrary"),
            vmem_limit_bytes=_vmem_limit(need)),
        name="mlp_block",
    )(x, g.reshape(1, d), w_up, w_down)


DSA_TN = 2 * LANES
Q_COLS = N_HEADS * HEAD_DIM
IQ_COLS = IDX_HEADS * IDX_DIM
KV_COLS = N_KV_HEADS * HEAD_DIM
DSA_COLS = Q_COLS + IQ_COLS + 2 * KV_COLS + IDX_DIM + LANES
Q_BLOCKS = Q_COLS // DSA_TN
IQ_BLOCKS = IQ_COLS // DSA_TN
KV_BLOCKS = KV_COLS // DSA_TN


W_K_FIRST = Q_BLOCKS
W_V_FIRST = W_K_FIRST + KV_BLOCKS
W_IQ_FIRST = W_V_FIRST + KV_BLOCKS
W_LAST = W_IQ_FIRST + IQ_BLOCKS
OUT_K_FIRST = Q_BLOCKS + IQ_BLOCKS
OUT_V_FIRST = OUT_K_FIRST + KV_BLOCKS


def _dsa_out_block(jw):
    return jnp.where(jw < W_K_FIRST, jw,
                     jnp.where(jw < W_IQ_FIRST, jw + (OUT_K_FIRST - W_K_FIRST),
                               jnp.where(jw < W_LAST, jw - (W_IQ_FIRST - Q_BLOCKS), W_LAST)))


def _dsa_proj_kernel(x_ref, g_ref, w_ref, qn_ref, kn_ref, rc_ref, rs_ref,
                     ic_ref, ia_ref, ib_ref, o_ref, iw_ref, vt_ref, h_ref, y_ref):
    j = pl.program_id(1)
    jp = j - 1

    @pl.when(j == 0)
    def _():
        h_ref[...] = _rms_norm_rows(x_ref[...], g_ref[...]).astype(h_ref.dtype)

    def rope(z):
        return z * rc_ref[...] + pltpu.roll(z, HEAD_DIM // 2, 1) * rs_ref[...]

    def idx_rope(z):
        half = IDX_ROPE_DIM // 2
        return (z * ic_ref[...] + pltpu.roll(z, LANES - half, 1) * ia_ref[...]
                + pltpu.roll(z, half, 1) * ib_ref[...])

    def per_head(fn):
        def finish(y):
            o_ref[...] = jnp.concatenate(
                [fn(y[:, :LANES]), fn(y[:, LANES:])], axis=1).astype(o_ref.dtype)
        return finish

    def step(finish, multiply=True):
        y_prev = y_ref[...]
        if multiply:
            y_ref[...] = jnp.dot(h_ref[...], w_ref[...], preferred_element_type=F32)
        if finish is not None:
            finish(y_prev)

    @pl.when(j == 0)
    def _():
        step(None)

    @pl.when((jp >= 0) & (jp < W_K_FIRST))
    def _():
        scale = HEAD_DIM ** -0.5 * math.log2(math.e)
        step(per_head(lambda z: rope(_rms_norm_rows(z, qn_ref[...])) * scale))

    @pl.when((jp >= W_K_FIRST) & (jp < W_V_FIRST))
    def _():
        step(per_head(lambda z: rope(_rms_norm_rows(z, kn_ref[...]))))

    @pl.when((jp >= W_V_FIRST) & (jp < W_IQ_FIRST))
    def _():
        def finish(y):
            o_ref[...] = y.astype(o_ref.dtype)
            vt_ref[...] = jnp.transpose(y).astype(vt_ref.dtype)
        step(finish)

    @pl.when((jp >= W_IQ_FIRST) & (jp < W_LAST))
    def _():
        step(per_head(idx_rope))

    @pl.when(jp == W_LAST)
    def _():
        def finish(y):
            o_ref[...] = jnp.concatenate(
                [idx_rope(y[:, :LANES]), jnp.zeros_like(y[:, LANES:])], axis=1).astype(o_ref.dtype)
            lane = lax.broadcasted_iota(jnp.int32, (y.shape[0], LANES), 1)
            iw_ref[...] = jnp.where(lane < IDX_HEADS,
                                    y[:, LANES:] * (IDX_HEADS ** -0.5 * IDX_DIM ** -0.5), 0.0)
        step(finish, multiply=False)


def dsa_projection(x, g, w, q_norm, k_norm, tables, *, tm):
    t, d = x.shape
    assert w.shape[1] == Q_COLS + 2 * KV_COLS + IQ_COLS + IDX_DIM + IDX_HEADS
    rc, rs, ic, ia, ib = tables
    tab_spec = pl.BlockSpec((tm, LANES), lambda i, j: (i, 0))
    vec_spec = pl.BlockSpec((1, LANES), lambda i, j: (0, 0))
    need = (2 * _nbytes((tm, d), F32) + 2 * _nbytes((d, DSA_TN), w.dtype)
            + 2 * _nbytes((tm, DSA_TN), w.dtype) + _nbytes((tm, d), w.dtype)
            + 12 * _nbytes((tm, LANES), F32) + 3 * _nbytes((tm, DSA_TN), F32))
    return pl.pallas_call(
        _dsa_proj_kernel,
        grid=(t // tm, W_LAST + 2),
        in_specs=[
            pl.BlockSpec((tm, d), lambda i, j: (i, 0)),
            pl.BlockSpec((1, d), lambda i, j: (0, 0)),
            pl.BlockSpec((d, DSA_TN), lambda i, j: (0, jnp.minimum(j, W_LAST))),
            vec_spec, vec_spec, tab_spec, tab_spec, tab_spec, tab_spec, tab_spec,
        ],
        out_specs=[
            pl.BlockSpec((tm, DSA_TN), lambda i, j: (i, _dsa_out_block(jnp.maximum(j - 1, 0)))),
            pl.BlockSpec((tm, LANES), lambda i, j: (i, 0)),
            pl.BlockSpec((DSA_TN, tm), lambda i, j: (jnp.clip(j - 1 - W_V_FIRST, 0, KV_BLOCKS - 1), i)),
        ],
        out_shape=[
            jax.ShapeDtypeStruct((t, DSA_COLS), w.dtype),
            jax.ShapeDtypeStruct((t, LANES), F32),
            jax.ShapeDtypeStruct((KV_COLS, t), w.dtype),
        ],
        scratch_shapes=[pltpu.VMEM((tm, d), w.dtype), pltpu.VMEM((tm, DSA_TN), F32)],
        compiler_params=pltpu.CompilerParams(
            dimension_semantics=("parallel", "arbitrary"),
            vmem_limit_bytes=_vmem_limit(need)),
        name="dsa_projection",
    )(x, g.reshape(1, d), w, q_norm.reshape(1, LANES), k_norm.reshape(1, LANES), rc, rs, ic, ia, ib)


def _float_key(v):
    b = lax.bitcast_convert_type(v, jnp.int32)
    return b ^ ((b >> 31) & 0x7FFFFFFF)


def _key_float(k):
    return lax.bitcast_convert_type(k ^ ((k >> 31) & 0x7FFFFFFF), F32)


_KLO, _KHI, _CLO = range(3)
_SEARCH_CAP = 96
ONES_ROWS = 2 * SUBLANES
ATTN_LOOKAHEAD = 1


def _dsa_core_kernel(q_ref, iq_ref, iw_ref, k_ref, vt_ref, ik_ref, o_ref,
                     key_ref, bias_ref, s_ref, wt_ref, st_ref, m_ref, l_ref, acc_ref, *, tq, topk, kt, ka):
    i = pl.program_id(0)
    n_tiles = (i + kt) // kt
    group = N_HEADS // N_KV_HEADS
    nt = (((1,), (1,)), ((), ()))

    wt_ref[...] = jnp.transpose(iw_ref[...])

    krow = lax.broadcasted_iota(jnp.int32, (tq, tq), 0)
    qcol = lax.broadcasted_iota(jnp.int32, (tq, tq), 1)
    diag_causal = krow <= qcol

    def chunk_start(c):
        return pl.multiple_of(c * tq, tq)

    def index_tile(ti, carry):
        smin, smax = carry
        iks = [ik_ref[pl.ds(chunk_start(ti * kt + u), tq), :] for u in range(kt)]
        scores = [jnp.zeros((tq, tq), F32) for _ in range(kt)]
        for h in range(IDX_HEADS):
            iqh = iq_ref[:, h * IDX_DIM:(h + 1) * IDX_DIM]
            w = jnp.broadcast_to(wt_ref[h:h + 1, :], (tq, tq))
            for u in range(kt):
                logits = lax.dot_general(iks[u], iqh, nt, preferred_element_type=F32)
                scores[u] = scores[u] + w * jnp.maximum(logits, 0.0)
        for u in range(kt):
            c = ti * kt + u
            valid = (c < i) | ((c == i) & diag_causal)
            key_ref[c] = jnp.where(valid, _float_key(scores[u]), INT_MIN)
            smin = jnp.minimum(smin, jnp.min(jnp.where(valid, scores[u], jnp.inf), axis=0, keepdims=True))
            smax = jnp.maximum(smax, jnp.max(jnp.where(valid, scores[u], -jnp.inf), axis=0, keepdims=True))
        return smin, smax

    smin, smax = lax.fori_loop(
        0, n_tiles, index_tile,
        (jnp.full((1, tq), jnp.inf, F32), jnp.full((1, tq), -jnp.inf, F32)))

    def count_ge(cand):
        cand_b = jnp.broadcast_to(cand, (tq, tq))

        def body(ti, cnt):
            for u in range(kt):
                hit = jnp.where(key_ref[ti * kt + u] >= cand_b, 1.0, 0.0)
                cnt = cnt + jnp.sum(hit, axis=0, keepdims=True)
            return cnt

        return lax.fori_loop(0, n_tiles, body, jnp.zeros((1, tq), F32))

    st_ref[_KLO] = _float_key(smin)
    st_ref[_KHI] = _float_key(smax) + 1
    st_ref[_CLO] = lax.broadcasted_iota(jnp.int32, (1, tq), 1) + (i * tq + 1)

    def unfinished(klo, khi, clo):
        return (clo > topk) & ((khi - klo) != 1)

    def search_cond(st):
        step, pending = st
        return (step < _SEARCH_CAP) & (pending > 0)

    def search_body(st):
        step, _ = st
        klo, khi, clo = st_ref[_KLO], st_ref[_KHI], st_ref[_CLO]
        live = unfinished(klo, khi, clo)
        vmid = 0.5 * _key_float(klo) + 0.5 * _key_float(khi)
        cmid = _float_key(vmid)
        kmid = klo + lax.shift_right_logical(khi - klo, 1)
        cand = jnp.where((cmid > klo) & (cmid < khi), cmid, kmid)
        cand = jnp.where(live, cand, klo)
        cnt = count_ge(cand).astype(jnp.int32)
        take = cnt >= topk
        klo = jnp.where(live & take, cand, klo)
        clo = jnp.where(live & take, cnt, clo)
        khi = jnp.where(live & jnp.logical_not(take), cand, khi)
        st_ref[_KLO], st_ref[_KHI], st_ref[_CLO] = klo, khi, clo
        pending = jnp.max(jnp.where(unfinished(klo, khi, clo), 1, 0))
        return step + 1, pending

    pending0 = jnp.max(jnp.where(unfinished(st_ref[_KLO], st_ref[_KHI], st_ref[_CLO]), 1, 0))
    lax.while_loop(search_cond, search_body, (jnp.int32(0), pending0))
    thr_b = jnp.broadcast_to(st_ref[_KLO], (tq, tq))

    m_ref[...] = jnp.full(m_ref.shape, NEG_BIG, F32)
    l_ref[...] = jnp.zeros(l_ref.shape, F32)
    acc_ref[...] = jnp.zeros(acc_ref.shape, F32)

    def attend_tile(ti, carry):
        for u in range(ka):
            bias_ref[u] = jnp.where(key_ref[ti * ka + u] >= thr_b, 0.0, NEG_BIG)
        starts = [chunk_start(ti * ka + u) for u in range(ka)]
        slots = ATTN_LOOKAHEAD + 1
        dyn0 = jnp.minimum(ti, 0)
        ones_rows = jnp.ones((ONES_ROWS, tq), vt_ref.dtype)

        def logits(h):
            cols = slice(h // group * HEAD_DIM, (h // group + 1) * HEAD_DIM)
            qh = q_ref[:, h * HEAD_DIM:(h + 1) * HEAD_DIM]
            top = None
            for u in range(ka):
                s = lax.dot_general(k_ref[pl.ds(starts[u], tq), cols], qh, nt,
                                    preferred_element_type=F32) + bias_ref[u]
                s_ref[h % slots + dyn0, u] = s
                top = s if top is None else jnp.maximum(top, s)
            return jnp.max(top, axis=0, keepdims=True)

        queued = [logits(h) for h in range(ATTN_LOOKAHEAD)]
        for h in range(N_HEADS):
            tile_max = queued.pop(0)
            if h + ATTN_LOOKAHEAD < N_HEADS:
                queued.append(logits(h + ATTN_LOOKAHEAD))
            cols = slice(h // group * HEAD_DIM, (h // group + 1) * HEAD_DIM)
            m_prev = m_ref[h]
            m_cur = jnp.maximum(m_prev, tile_max)
            alpha = jnp.exp2(m_prev - m_cur)
            pv = jnp.zeros((HEAD_DIM + ONES_ROWS, tq), F32)
            for u in range(ka):
                p = jnp.exp2(s_ref[h % slots + dyn0, u] - m_cur)
                v_aug = jnp.concatenate([vt_ref[cols, pl.ds(starts[u], tq)], ones_rows], axis=0)
                pv = pv + jnp.dot(v_aug, p.astype(vt_ref.dtype), preferred_element_type=F32)
            l_ref[h] = alpha * l_ref[h] + pv[HEAD_DIM:HEAD_DIM + 1, :]
            pv = pv[:HEAD_DIM, :]
            acc_ref[h] = alpha * acc_ref[h] + pv
            m_ref[h] = m_cur
        return carry

    lax.fori_loop(0, (i + ka) // ka, attend_tile, 0)

    for h in range(N_HEADS):
        o_ref[:, h * HEAD_DIM:(h + 1) * HEAD_DIM] = jnp.transpose(acc_ref[h] / l_ref[h]).astype(o_ref.dtype)


def dsa_core(proj, iw, vt, *, tq, topk, kt, ka):
    t = proj.shape[0]
    assert (t // tq) % kt == 0 and kt % ka == 0
    k_block = (Q_COLS + IQ_COLS) // KV_COLS
    ik_block = (Q_COLS + IQ_COLS + 2 * KV_COLS) // IDX_DIM
    resident = dict(pipeline_mode=pl.Buffered(1))
    need = (4 * _nbytes((tq, Q_COLS), proj.dtype) + 2 * _nbytes((tq, LANES), F32)
            + 2 * _nbytes((t, KV_COLS), proj.dtype) + _nbytes((t, IDX_DIM), proj.dtype)
            + 2 * _nbytes((tq, Q_COLS), proj.dtype)
            + _nbytes((t // tq, tq, tq), jnp.int32) + _nbytes((kt, tq, tq), F32)
            + _nbytes((LANES, tq), F32) + (3 + 2 * N_HEADS) * _nbytes((SUBLANES, tq), F32)
            + _nbytes((N_HEADS, HEAD_DIM, tq), F32)
            + (ATTN_LOOKAHEAD + 3) * _nbytes((ka, tq, tq), F32))
    return pl.pallas_call(
        functools.partial(_dsa_core_kernel, tq=tq, topk=topk, kt=kt, ka=ka),
        grid=(t // tq,),
        in_specs=[
            pl.BlockSpec((tq, Q_COLS), lambda i: (i, 0)),
            pl.BlockSpec((tq, IQ_COLS), lambda i: (i, 1)),
            pl.BlockSpec((tq, LANES), lambda i: (i, 0)),
            pl.BlockSpec((t, KV_COLS), lambda i: (0, k_block), **resident),
            pl.BlockSpec((KV_COLS, t), lambda i: (0, 0), **resident),
            pl.BlockSpec((t, IDX_DIM), lambda i: (0, ik_block), **resident),
        ],
        out_specs=pl.BlockSpec((tq, Q_COLS), lambda i: (i, 0)),
        out_shape=jax.ShapeDtypeStruct((t, Q_COLS), proj.dtype),
        scratch_shapes=[
            pltpu.VMEM((t // tq, tq, tq), jnp.int32),
            pltpu.VMEM((ka, tq, tq), F32),
            pltpu.VMEM((ATTN_LOOKAHEAD + 1, ka, tq, tq), F32),
            pltpu.VMEM((LANES, tq), F32),
            pltpu.VMEM((3, 1, tq), jnp.int32),
            pltpu.VMEM((N_HEADS, 1, tq), F32),
            pltpu.VMEM((N_HEADS, 1, tq), F32),
            pltpu.VMEM((N_HEADS, HEAD_DIM, tq), F32),
        ],
        compiler_params=pltpu.CompilerParams(
            dimension_semantics=("arbitrary",),
            vmem_limit_bytes=_vmem_limit(need)),
        name="dsa_core",
    )(proj, proj, iw, proj, vt, proj)


def _rope_tables(pos):
    def angles(dim):
        inv = 1.0 / (ROPE_THETA ** (jnp.arange(0, dim, 2, dtype=F32) / dim))
        return pos.astype(F32)[:, None] * inv

    ang = angles(HEAD_DIM)
    cos, sin = jnp.cos(ang), jnp.sin(ang)
    rc = jnp.concatenate([cos, cos], axis=1)
    rs = jnp.concatenate([-sin, sin], axis=1)
    iang = angles(IDX_ROPE_DIM)
    icos, isin = jnp.cos(iang), jnp.sin(iang)
    rest = LANES - IDX_ROPE_DIM
    zeros = jnp.zeros_like(isin)
    ic = jnp.concatenate([icos, icos, jnp.ones((pos.shape[0], rest), F32)], axis=1)
    ia = jnp.concatenate([-isin, zeros, jnp.zeros((pos.shape[0], rest), F32)], axis=1)
    ib = jnp.concatenate([zeros, isin, jnp.zeros((pos.shape[0], rest), F32)], axis=1)
    return rc, rs, ic, ia, ib


def dsa_layer(x, pos, norm_g, w_in, q_norm, k_norm, w_out, tiles):
    t = x.shape[0]
    proj, iw, vt = dsa_projection(x, norm_g, w_in.astype(MXU_DTYPE), q_norm, k_norm,
                                  _rope_tables(pos), tm=tiles["proj_tm"])
    attn = dsa_core(proj, iw, vt, tq=tiles["tq"], topk=min(TOPK_MAX, t // 4), kt=tiles["kt"], ka=tiles["ka"])
    return matmul_residual(attn, w_out.astype(MXU_DTYPE), x, tm=tiles["res_tm"], tn=tiles["res_tn"])


def _shift_rows(x, d, fill):
    rows = lax.broadcasted_iota(jnp.int32, x.shape, 0)
    return jnp.where(rows >= d, pltpu.roll(x, d, 0), fill)


def _rglru_kernel(gate_ref, xr_ref, cw_ref, cb_ref, wa_ref, ba_ref, wx_ref, bx_ref, lam_ref,
                  y_ref, hcar_ref, xprev_ref, *, tt):
    @pl.when(pl.program_id(1) == 0)
    def _():
        hcar_ref[...] = jnp.zeros_like(hcar_ref)
        xprev_ref[...] = jnp.zeros_like(xprev_ref)

    xr = xr_ref[...]
    ext = jnp.concatenate([xprev_ref[...], xr], axis=0)
    cw = cw_ref[...]
    xc = cb_ref[...] + xr * cw[CONV_WIDTH - 1:CONV_WIDTH, :]
    for d in range(1, CONV_WIDTH):
        xc = xc + pltpu.roll(ext, d, 0)[SUBLANES:, :] * cw[CONV_WIDTH - 1 - d:CONV_WIDTH - d, :]
    xprev_ref[...] = xr[tt - SUBLANES:, :]

    xcb = xc.astype(wa_ref.dtype)
    r = jax.nn.sigmoid(jnp.dot(xcb, wa_ref[...], preferred_element_type=F32) + ba_ref[...])
    ig = jax.nn.sigmoid(jnp.dot(xcb, wx_ref[...], preferred_element_type=F32) + bx_ref[...])
    nlam = -lam_ref[...]
    softplus = jnp.maximum(nlam, 0.0) + jnp.log(1.0 + jnp.exp(-jnp.abs(nlam)))
    log_a = -LRU_C * r * softplus
    a = jnp.exp(log_a)
    mult = jnp.sqrt(1.0 - jnp.exp(2.0 * log_a))
    b = xc * ig * mult

    d = 1
    while d < tt:
        b = a * _shift_rows(b, d, 0.0) + b
        a = a * _shift_rows(a, d, 1.0)
        d *= 2
    h = b + a * hcar_ref[...]
    hcar_ref[...] = h[tt - 1:tt, :]
    y_ref[...] = (h * jax.nn.gelu(gate_ref[...])).astype(y_ref.dtype)


def rglru_scan(proj, conv_w, conv_b, wa, ba, wx, bx, lam, *, tt):
    t = proj.shape[0]
    width = conv_w.shape[1]
    nb, blk, _ = wa.shape
    vec = lambda: pl.BlockSpec((1, blk), lambda n, s: (0, n))
    need = (4 * _nbytes((tt, blk), F32) + 2 * _nbytes((tt, blk), MXU_DTYPE)
            + 4 * _nbytes((blk, blk), wa.dtype) + 24 * _nbytes((tt, blk), F32))
    return pl.pallas_call(
        functools.partial(_rglru_kernel, tt=tt),
        grid=(nb, t // tt),
        in_specs=[
            pl.BlockSpec((tt, blk), lambda n, s: (s, n)),
            pl.BlockSpec((tt, blk), lambda n, s: (s, nb + n)),
            pl.BlockSpec((CONV_WIDTH, blk), lambda n, s: (0, n)),
            vec(),
            pl.BlockSpec((None, blk, blk), lambda n, s: (n, 0, 0)),
            vec(),
            pl.BlockSpec((None, blk, blk), lambda n, s: (n, 0, 0)),
            vec(),
            vec(),
        ],
        out_specs=pl.BlockSpec((tt, blk), lambda n, s: (s, n)),
        out_shape=jax.ShapeDtypeStruct((t, width), MXU_DTYPE),
        scratch_shapes=[pltpu.VMEM((1, blk), F32), pltpu.VMEM((SUBLANES, blk), F32)],
        compiler_params=pltpu.CompilerParams(
            dimension_semantics=("parallel", "arbitrary"),
            vmem_limit_bytes=_vmem_limit(need)),
        name="rglru_scan",
    )(proj, proj, conv_w, conv_b.reshape(1, width), wa, ba.reshape(1, width),
      wx, bx.reshape(1, width), lam.reshape(1, width))


def rglru_layer(x, norm_g, w_in, conv_w, conv_b, wa, ba, wx, bx, lam, w_out, tiles):
    proj = norm_matmul(x, norm_g, w_in.astype(MXU_DTYPE), tm=tiles["proj_tm"], tn=tiles["proj_tn"],
                       out_dtype=F32)
    y = rglru_scan(proj, conv_w, conv_b, wa.astype(MXU_DTYPE), ba, wx.astype(MXU_DTYPE), bx, lam,
                   tt=tiles["scan_tt"])
    return matmul_residual(y, w_out.astype(MXU_DTYPE), x, tm=tiles["res_tm"], tn=tiles["res_tn"])


POOL_HALO = max(POOL_WINDOWS)


def _pool_kernel(x_ref, g_ref, w_ref, b_ref, s_ref, o_ref, halo_ref, *, tt):
    blk = pl.program_id(0)

    @pl.when(blk == 0)
    def _():
        halo_ref[...] = jnp.zeros_like(halo_ref)

    x = x_ref[...]
    h = _rms_norm_rows(x, g_ref[...])
    ext = jnp.concatenate([halo_ref[...], h], axis=0)
    halo_ref[...] = h[tt - POOL_HALO:, :]

    gw = h.shape[1] // len(POOL_WINDOWS)
    t1 = (lax.broadcasted_iota(jnp.int32, (tt, gw), 0) + (blk * tt + 1)).astype(F32)
    for gi, win in enumerate(POOL_WINDOWS):
        cols = slice(gi * gw, (gi + 1) * gw)
        acc = ext[:, cols]
        d = 1
        while d < win:
            acc = acc + pltpu.roll(acc, d, 0)
            d *= 2
        mean = acc[POOL_HALO:, :] / jnp.minimum(t1, float(win))
        y = (mean - h[:, cols]).astype(w_ref.dtype)
        z = jnp.dot(y, w_ref[gi], preferred_element_type=F32) + b_ref[gi:gi + 1, :]
        o_ref[:, cols] = x[:, cols] + z * s_ref[:, cols]


def pool_layer(x, norm_g, w_group, b_group, scale, tiles):
    t, d = x.shape
    tt = tiles["pool_tt"]
    ng, gw, _ = w_group.shape
    need = (4 * _nbytes((tt, d), F32) + 2 * _nbytes((ng, gw, gw), MXU_DTYPE)
            + 6 * _nbytes((tt, d), F32))
    return pl.pallas_call(
        functools.partial(_pool_kernel, tt=tt),
        grid=(t // tt,),
        in_specs=[
            pl.BlockSpec((tt, d), lambda i: (i, 0)),
            pl.BlockSpec((1, d), lambda i: (0, 0)),
            pl.BlockSpec((ng, gw, gw), lambda i: (0, 0, 0)),
            pl.BlockSpec((ng, gw), lambda i: (0, 0)),
            pl.BlockSpec((1, d), lambda i: (0, 0)),
        ],
        out_specs=pl.BlockSpec((tt, d), lambda i: (i, 0)),
        out_shape=jax.ShapeDtypeStruct((t, d), F32),
        scratch_shapes=[pltpu.VMEM((POOL_HALO, d), F32)],
        compiler_params=pltpu.CompilerParams(
            dimension_semantics=("arbitrary",),
            vmem_limit_bytes=_vmem_limit(need)),
        name="pool_mixer",
    )(x, norm_g.reshape(1, d), w_group.astype(MXU_DTYPE), b_group, scale.reshape(1, d))


def _tiles(t):
    big = min(t, 512)
    tall = min(t, 1024)
    return {
        "proj_tm": tall, "proj_tn": 512,
        "res_tm": tall, "res_tn": 512,
        "tq": min(t, 256), "kt": min(4, t // min(t, 256)), "ka": min(4, t // min(t, 256)),
        "scan_tt": min(t, 256),
        "pool_tt": min(t, 256),
        "mlp_tm": big, "mlp_tf": 512,
    }


def kernel(x, positions, attn_norm, attn_w_in, attn_q_norm, attn_k_norm, attn_w_out, rnn_norm, rnn_w_in, rnn_conv_w, rnn_conv_b, rnn_gate_a_w, rnn_gate_a_b, rnn_gate_x_w, rnn_gate_x_b, rnn_lambda, rnn_w_out, pool_norm, pool_w, pool_b, pool_scale, mlp_norm, mlp_w_up, mlp_w_down):
    batch, t, d = x.shape
    depth = mlp_norm.shape[0]
    tiles = _tiles(t)
    outs = []
    for bi in range(batch):
        xb = x[bi]
        pos = positions[bi]
        for i in range(depth):
            kind, j = i % N_MIXERS, i // N_MIXERS
            if kind == 0:
                xb = dsa_layer(xb, pos, attn_norm[j], attn_w_in[j], attn_q_norm[j], attn_k_norm[j],
                               attn_w_out[j], tiles)
            elif kind == 1:
                xb = rglru_layer(xb, rnn_norm[j], rnn_w_in[j], rnn_conv_w[j], rnn_conv_b[j],
                                 rnn_gate_a_w[j], rnn_gate_a_b[j], rnn_gate_x_w[j], rnn_gate_x_b[j],
                                 rnn_lambda[j], rnn_w_out[j], tiles)
            else:
                xb = pool_layer(xb, pool_norm[j], pool_w[j], pool_b[j], pool_scale[j], tiles)
            xb = mlp_block(xb, mlp_norm[i], mlp_w_up[i].astype(MXU_DTYPE),
                           mlp_w_down[i].astype(MXU_DTYPE), tm=tiles["mlp_tm"], tf=tiles["mlp_tf"])
        outs.append(xb)
    return jnp.stack(outs, axis=0)
```

```python
import functools
import math

import jax
import jax.numpy as jnp
from jax import lax
from jax.experimental import pallas as pl
from jax.experimental.pallas import tpu as pltpu

F32 = jnp.float32
MXU_DTYPE = jnp.bfloat16

N_MIXERS = 3
EPS = 1e-6
ROPE_THETA = 10000.0
HEAD_DIM = 128
N_HEADS = 16
N_KV_HEADS = 4
IDX_HEADS = 16
IDX_DIM = 128
IDX_ROPE_DIM = 64
TOPK_MAX = 256
CONV_WIDTH = 4
LRU_C = 8.0
POOL_WINDOWS = (2, 4, 8, 16)

LANES = 128
SUBLANES = 8
VMEM_BYTES_V7X = 64 * 1024 * 1024
VMEM_CAP_BYTES = VMEM_BYTES_V7X - 8 * 1024 * 1024

INT_MIN = -(2 ** 31)
NEG_BIG = -1e30


def _vmem_limit(block_bytes):
    return int(min(VMEM_CAP_BYTES, block_bytes * 3 // 2 + (4 << 20)))


def _nbytes(shape, dtype):
    return math.prod(shape) * jnp.dtype(dtype).itemsize


def _rms_norm_rows(x, g):
    ms = jnp.mean(x * x, axis=-1, keepdims=True)
    return x * lax.rsqrt(ms + EPS) * g


def _tile_lanes(x, width):
    reps = width // LANES
    return x if reps == 1 else jnp.concatenate([x] * reps, axis=1)


def _norm_matmul_kernel(x_ref, g_ref, w_ref, o_ref, h_ref):
    @pl.when(pl.program_id(1) == 0)
    def _():
        h_ref[...] = _rms_norm_rows(x_ref[...], g_ref[...]).astype(h_ref.dtype)

    o_ref[...] = jnp.dot(h_ref[...], w_ref[...], preferred_element_type=F32).astype(o_ref.dtype)


def norm_matmul(x, g, w, *, tm, tn, out_dtype):
    t, d = x.shape
    n = w.shape[1]
    need = (2 * _nbytes((tm, d), F32) + 2 * _nbytes((d, tn), w.dtype)
            + 2 * _nbytes((tm, tn), out_dtype) + _nbytes((tm, d), w.dtype))
    return pl.pallas_call(
        _norm_matmul_kernel,
        grid=(t // tm, n // tn),
        in_specs=[
            pl.BlockSpec((tm, d), lambda i, j: (i, 0)),
            pl.BlockSpec((1, d), lambda i, j: (0, 0)),
            pl.BlockSpec((d, tn), lambda i, j: (0, j)),
        ],
        out_specs=pl.BlockSpec((tm, tn), lambda i, j: (i, j)),
        out_shape=jax.ShapeDtypeStruct((t, n), out_dtype),
        scratch_shapes=[pltpu.VMEM((tm, d), w.dtype)],
        compiler_params=pltpu.CompilerParams(
            dimension_semantics=("parallel", "arbitrary"),
            vmem_limit_bytes=_vmem_limit(need)),
        name="norm_matmul",
    )(x, g.reshape(1, d), w)


def _matmul_residual_kernel(a_ref, w_ref, x_ref, o_ref):
    o_ref[...] = x_ref[...] + jnp.dot(a_ref[...], w_ref[...], preferred_element_type=F32)


def matmul_residual(a, w, x, *, tm, tn):
    t, k = a.shape
    n = w.shape[1]
    need = (2 * _nbytes((tm, k), a.dtype) + 2 * _nbytes((k, tn), w.dtype)
            + 4 * _nbytes((tm, tn), F32))
    return pl.pallas_call(
        _matmul_residual_kernel,
        grid=(t // tm, n // tn),
        in_specs=[
            pl.BlockSpec((tm, k), lambda i, j: (i, 0)),
            pl.BlockSpec((k, tn), lambda i, j: (0, j)),
            pl.BlockSpec((tm, tn), lambda i, j: (i, j)),
        ],
        out_specs=pl.BlockSpec((tm, tn), lambda i, j: (i, j)),
        out_shape=jax.ShapeDtypeStruct((t, n), F32),
        compiler_params=pltpu.CompilerParams(
            dimension_semantics=("parallel", "arbitrary"),
            vmem_limit_bytes=_vmem_limit(need)),
        name="matmul_residual",
    )(a, w, x)


def _mlp_kernel(x_ref, g_ref, wu_ref, wd_ref, o_ref, h_ref, acc_ref):
    f = pl.program_id(1)

    @pl.when(f == 0)
    def _():
        h_ref[...] = _rms_norm_rows(x_ref[...], g_ref[...]).astype(h_ref.dtype)
        acc_ref[...] = jnp.zeros_like(acc_ref)

    u = jnp.dot(h_ref[...], wu_ref[...], preferred_element_type=F32)
    u = jnp.square(jnp.maximum(u, 0.0)).astype(wd_ref.dtype)
    acc_ref[...] += jnp.dot(u, wd_ref[...], preferred_element_type=F32)

    @pl.when(f == pl.num_programs(1) - 1)
    def _():
        o_ref[...] = x_ref[...] + acc_ref[...]


def mlp_block(x, g, w_up, w_down, *, tm, tf):
    t, d = x.shape
    ff = w_up.shape[1]
    need = (4 * _nbytes((tm, d), F32) + 2 * _nbytes((d, tf), w_up.dtype)
            + 2 * _nbytes((tf, d), w_down.dtype) + _nbytes((tm, d), w_up.dtype)
            + _nbytes((tm, d), F32) + 2 * _nbytes((tm, tf), F32))
    return pl.pallas_call(
        _mlp_kernel,
        grid=(t // tm, ff // tf),
        in_specs=[
            pl.BlockSpec((tm, d), lambda i, f: (i, 0)),
            pl.BlockSpec((1, d), lambda i, f: (0, 0)),
            pl.BlockSpec((d, tf), lambda i, f: (0, f)),
            pl.BlockSpec((tf, d), lambda i, f: (f, 0)),
        ],
        out_specs=pl.BlockSpec((tm, d), lambda i, f: (i, 0)),
        out_shape=jax.ShapeDtypeStruct((t, d), F32),
        scratch_shapes=[pltpu.VMEM((tm, d), w_up.dtype), pltpu.VMEM((tm, d), F32)],
        compiler_params=pltpu.CompilerParams(
            dimension_semantics=("parallel", "arbitrary"),
            vmem_limit_bytes=_vmem_limit(need)),
        name="mlp_block",
    )(x, g.reshape(1, d), w_up, w_down)


DSA_TN = 2 * LANES
Q_COLS = N_HEADS * HEAD_DIM
IQ_COLS = IDX_HEADS * IDX_DIM
KV_COLS = N_KV_HEADS * HEAD_DIM
DSA_COLS = Q_COLS + IQ_COLS + 2 * KV_COLS + IDX_DIM + LANES
Q_BLOCKS = Q_COLS // DSA_TN
IQ_BLOCKS = IQ_COLS // DSA_TN
KV_BLOCKS = KV_COLS // DSA_TN


W_K_FIRST = Q_BLOCKS
W_V_FIRST = W_K_FIRST + KV_BLOCKS
W_IQ_FIRST = W_V_FIRST + KV_BLOCKS
W_LAST = W_IQ_FIRST + IQ_BLOCKS
OUT_K_FIRST = Q_BLOCKS + IQ_BLOCKS
OUT_V_FIRST = OUT_K_FIRST + KV_BLOCKS


def _dsa_out_block(jw):
    return jnp.where(jw < W_K_FIRST, jw,
                     jnp.where(jw < W_IQ_FIRST, jw + (OUT_K_FIRST - W_K_FIRST),
                               jnp.where(jw < W_LAST, jw - (W_IQ_FIRST - Q_BLOCKS), W_LAST)))


def _dsa_proj_kernel(x_ref, g_ref, w_ref, qn_ref, kn_ref, rc_ref, rs_ref,
                     ic_ref, ia_ref, ib_ref, o_ref, iw_ref, vt_ref, h_ref, y_ref):
    j = pl.program_id(1)
    jp = j - 1

    @pl.when(j == 0)
    def _():
        h_ref[...] = _rms_norm_rows(x_ref[...], g_ref[...]).astype(h_ref.dtype)

    def rope(z):
        return z * rc_ref[...] + pltpu.roll(z, HEAD_DIM // 2, 1) * rs_ref[...]

    def idx_rope(z):
        half = IDX_ROPE_DIM // 2
        return (z * ic_ref[...] + pltpu.roll(z, LANES - half, 1) * ia_ref[...]
                + pltpu.roll(z, half, 1) * ib_ref[...])

    def per_head(fn):
        def finish(y):
            o_ref[...] = jnp.concatenate(
                [fn(y[:, :LANES]), fn(y[:, LANES:])], axis=1).astype(o_ref.dtype)
        return finish

    def step(finish, multiply=True):
        y_prev = y_ref[...]
        if multiply:
            y_ref[...] = jnp.dot(h_ref[...], w_ref[...], preferred_element_type=F32)
        if finish is not None:
            finish(y_prev)

    @pl.when(j == 0)
    def _():
        step(None)

    @pl.when((jp >= 0) & (jp < W_K_FIRST))
    def _():
        scale = HEAD_DIM ** -0.5 * math.log2(math.e)
        step(per_head(lambda z: rope(_rms_norm_rows(z, qn_ref[...])) * scale))

    @pl.when((jp >= W_K_FIRST) & (jp < W_V_FIRST))
    def _():
        step(per_head(lambda z: rope(_rms_norm_rows(z, kn_ref[...]))))

    @pl.when((jp >= W_V_FIRST) & (jp < W_IQ_FIRST))
    def _():
        def finish(y):
            o_ref[...] = y.astype(o_ref.dtype)
            vt_ref[...] = jnp.transpose(y).astype(vt_ref.dtype)
        step(finish)

    @pl.when((jp >= W_IQ_FIRST) & (jp < W_LAST))
    def _():
        step(per_head(idx_rope))

    @pl.when(jp == W_LAST)
    def _():
        def finish(y):
            o_ref[...] = jnp.concatenate(
                [idx_rope(y[:, :LANES]), jnp.zeros_like(y[:, LANES:])], axis=1).astype(o_ref.dtype)
            lane = lax.broadcasted_iota(jnp.int32, (y.shape[0], LANES), 1)
            iw_ref[...] = jnp.where(lane < IDX_HEADS,
                                    y[:, LANES:] * (IDX_HEADS ** -0.5 * IDX_DIM ** -0.5), 0.0)
        step(finish, multiply=False)


def dsa_projection(x, g, w, q_norm, k_norm, tables, *, tm):
    t, d = x.shape
    assert w.shape[1] == Q_COLS + 2 * KV_COLS + IQ_COLS + IDX_DIM + IDX_HEADS
    rc, rs, ic, ia, ib = tables
    tab_spec = pl.BlockSpec((tm, LANES), lambda i, j: (i, 0))
    vec_spec = pl.BlockSpec((1, LANES), lambda i, j: (0, 0))
    need = (2 * _nbytes((tm, d), F32) + 2 * _nbytes((d, DSA_TN), w.dtype)
            + 2 * _nbytes((tm, DSA_TN), w.dtype) + _nbytes((tm, d), w.dtype)
            + 12 * _nbytes((tm, LANES), F32) + 3 * _nbytes((tm, DSA_TN), F32))
    return pl.pallas_call(
        _dsa_proj_kernel,
        grid=(t // tm, W_LAST + 2),
        in_specs=[
            pl.BlockSpec((tm, d), lambda i, j: (i, 0)),
            pl.BlockSpec((1, d), lambda i, j: (0, 0)),
            pl.BlockSpec((d, DSA_TN), lambda i, j: (0, jnp.minimum(j, W_LAST))),
            vec_spec, vec_spec, tab_spec, tab_spec, tab_spec, tab_spec, tab_spec,
        ],
        out_specs=[
            pl.BlockSpec((tm, DSA_TN), lambda i, j: (i, _dsa_out_block(jnp.maximum(j - 1, 0)))),
            pl.BlockSpec((tm, LANES), lambda i, j: (i, 0)),
            pl.BlockSpec((DSA_TN, tm), lambda i, j: (jnp.clip(j - 1 - W_V_FIRST, 0, KV_BLOCKS - 1), i)),
        ],
        out_shape=[
            jax.ShapeDtypeStruct((t, DSA_COLS), w.dtype),
            jax.ShapeDtypeStruct((t, LANES), F32),
            jax.ShapeDtypeStruct((KV_COLS, t), w.dtype),
        ],
        scratch_shapes=[pltpu.VMEM((tm, d), w.dtype), pltpu.VMEM((tm, DSA_TN), F32)],
        compiler_params=pltpu.CompilerParams(
            dimension_semantics=("parallel", "arbitrary"),
            vmem_limit_bytes=_vmem_limit(need)),
        name="dsa_projection",
    )(x, g.reshape(1, d), w, q_norm.reshape(1, LANES), k_norm.reshape(1, LANES), rc, rs, ic, ia, ib)


def _float_key(v):
    b = lax.bitcast_convert_type(v, jnp.int32)
    return b ^ ((b >> 31) & 0x7FFFFFFF)


def _key_float(k):
    return lax.bitcast_convert_type(k ^ ((k >> 31) & 0x7FFFFFFF), F32)


_KLO, _KHI, _CLO = range(3)
_SEARCH_CAP = 96
ONES_ROWS = 2 * SUBLANES
ATTN_LOOKAHEAD = 1


def _dsa_core_kernel(q_ref, iq_ref, iw_ref, k_ref, vt_ref, ik_ref, o_ref,
                     key_ref, bias_ref, s_ref, wt_ref, st_ref, m_ref, l_ref, acc_ref, *, tq, topk, kt, ka):
    i = pl.program_id(0)
    n_tiles = (i + kt) // kt
    group = N_HEADS // N_KV_HEADS
    nt = (((1,), (1,)), ((), ()))

    wt_ref[...] = jnp.transpose(iw_ref[...])

    krow = lax.broadcasted_iota(jnp.int32, (tq, tq), 0)
    qcol = lax.broadcasted_iota(jnp.int32, (tq, tq), 1)
    diag_causal = krow <= qcol

    def chunk_start(c):
        return pl.multiple_of(c * tq, tq)

    def index_tile(ti, carry):
        smin, smax = carry
        iks = [ik_ref[pl.ds(chunk_start(ti * kt + u), tq), :] for u in range(kt)]
        scores = [jnp.zeros((tq, tq), F32) for _ in range(kt)]
        for h in range(IDX_HEADS):
            iqh = iq_ref[:, h * IDX_DIM:(h + 1) * IDX_DIM]
            w = jnp.broadcast_to(wt_ref[h:h + 1, :], (tq, tq))
            for u in range(kt):
                logits = lax.dot_general(iks[u], iqh, nt, preferred_element_type=F32)
                scores[u] = scores[u] + w * jnp.maximum(logits, 0.0)
        for u in range(kt):
            c = ti * kt + u
            valid = (c < i) | ((c == i) & diag_causal)
            key_ref[c] = jnp.where(valid, _float_key(scores[u]), INT_MIN)
            smin = jnp.minimum(smin, jnp.min(jnp.where(valid, scores[u], jnp.inf), axis=0, keepdims=True))
            smax = jnp.maximum(smax, jnp.max(jnp.where(valid, scores[u], -jnp.inf), axis=0, keepdims=True))
        return smin, smax

    smin, smax = lax.fori_loop(
        0, n_tiles, index_tile,
        (jnp.full((1, tq), jnp.inf, F32), jnp.full((1, tq), -jnp.inf, F32)))

    def count_ge(cand):
        cand_b = jnp.broadcast_to(cand, (tq, tq))

        def body(ti, cnt):
            for u in range(kt):
                hit = jnp.where(key_ref[ti * kt + u] >= cand_b, 1.0, 0.0)
                cnt = cnt + jnp.sum(hit, axis=0, keepdims=True)
            return cnt

        return lax.fori_loop(0, n_tiles, body, jnp.zeros((1, tq), F32))

    st_ref[_KLO] = _float_key(smin)
    st_ref[_KHI] = _float_key(smax) + 1
    st_ref[_CLO] = lax.broadcasted_iota(jnp.int32, (1, tq), 1) + (i * tq + 1)

    def unfinished(klo, khi, clo):
        return (clo > topk) & ((khi - klo) != 1)

    def search_cond(st):
        step, pending = st
        return (step < _SEARCH_CAP) & (pending > 0)

    def any_lane(mask):
        return jnp.max(jnp.where(mask, 1.0, 0.0))

    def search_body(st):
        step, _ = st
        klo, khi, clo = st_ref[_KLO], st_ref[_KHI], st_ref[_CLO]
        live = unfinished(klo, khi, clo)
        pending = any_lane(live)
        vmid = 0.5 * _key_float(klo) + 0.5 * _key_float(khi)
        cmid = _float_key(vmid)
        kmid = klo + lax.shift_right_logical(khi - klo, 1)
        cand = jnp.where((cmid > klo) & (cmid < khi), cmid, kmid)
        cand = jnp.where(live, cand, klo)
        cnt = count_ge(cand).astype(jnp.int32)
        take = cnt >= topk
        klo = jnp.where(live & take, cand, klo)
        clo = jnp.where(live & take, cnt, clo)
        khi = jnp.where(live & jnp.logical_not(take), cand, khi)
        st_ref[_KLO], st_ref[_KHI], st_ref[_CLO] = klo, khi, clo
        return step + 1, pending

    lax.while_loop(search_cond, search_body, (jnp.int32(0), jnp.float32(1.0)))
    thr_b = jnp.broadcast_to(st_ref[_KLO], (tq, tq))

    m_ref[...] = jnp.full(m_ref.shape, NEG_BIG, F32)
    l_ref[...] = jnp.zeros(l_ref.shape, F32)
    acc_ref[...] = jnp.zeros(acc_ref.shape, F32)

    def attend_tile(ti, carry):
        for u in range(ka):
            bias_ref[u] = jnp.where(key_ref[ti * ka + u] >= thr_b, 0.0, NEG_BIG)
        starts = [chunk_start(ti * ka + u) for u in range(ka)]
        slots = ATTN_LOOKAHEAD + 1
        dyn0 = jnp.minimum(ti, 0)
        ones_rows = jnp.ones((ONES_ROWS, tq), vt_ref.dtype)

        def logits(h):
            cols = slice(h // group * HEAD_DIM, (h // group + 1) * HEAD_DIM)
            qh = q_ref[:, h * HEAD_DIM:(h + 1) * HEAD_DIM]
            top = None
            for u in range(ka):
                s = lax.dot_general(k_ref[pl.ds(starts[u], tq), cols], qh, nt,
                                    preferred_element_type=F32) + bias_ref[u]
                s_ref[h % slots + dyn0, u] = s
                top = s if top is None else jnp.maximum(top, s)
            return jnp.max(top, axis=0, keepdims=True)

        queued = [logits(h) for h in range(ATTN_LOOKAHEAD)]
        for h in range(N_HEADS):
            tile_max = queued.pop(0)
            if h + ATTN_LOOKAHEAD < N_HEADS:
                queued.append(logits(h + ATTN_LOOKAHEAD))
            cols = slice(h // group * HEAD_DIM, (h // group + 1) * HEAD_DIM)
            m_prev = m_ref[h]
            m_cur = jnp.maximum(m_prev, tile_max)
            alpha = jnp.exp2(m_prev - m_cur)
            pv = jnp.zeros((HEAD_DIM + ONES_ROWS, tq), F32)
            for u in range(ka):
                p = jnp.exp2(s_ref[h % slots + dyn0, u] - m_cur)
                v_aug = jnp.concatenate([vt_ref[cols, pl.ds(starts[u], tq)], ones_rows], axis=0)
                pv = pv + jnp.dot(v_aug, p.astype(vt_ref.dtype), preferred_element_type=F32)
            l_ref[h] = alpha * l_ref[h] + pv[HEAD_DIM:HEAD_DIM + 1, :]
            pv = pv[:HEAD_DIM, :]
            acc_ref[h] = alpha * acc_ref[h] + pv
            m_ref[h] = m_cur
        return carry

    lax.fori_loop(0, (i + ka) // ka, attend_tile, 0)

    for h in range(N_HEADS):
        o_ref[:, h * HEAD_DIM:(h + 1) * HEAD_DIM] = jnp.transpose(acc_ref[h] / l_ref[h]).astype(o_ref.dtype)


def dsa_core(proj, iw, vt, *, tq, topk, kt, ka):
    t = proj.shape[0]
    assert (t // tq) % kt == 0 and kt % ka == 0
    k_block = (Q_COLS + IQ_COLS) // KV_COLS
    ik_block = (Q_COLS + IQ_COLS + 2 * KV_COLS) // IDX_DIM
    resident = dict(pipeline_mode=pl.Buffered(1))
    need = (4 * _nbytes((tq, Q_COLS), proj.dtype) + 2 * _nbytes((tq, LANES), F32)
            + 2 * _nbytes((t, KV_COLS), proj.dtype) + _nbytes((t, IDX_DIM), proj.dtype)
            + 2 * _nbytes((tq, Q_COLS), proj.dtype)
            + _nbytes((t // tq, tq, tq), jnp.int32) + _nbytes((kt, tq, tq), F32)
            + _nbytes((LANES, tq), F32) + (3 + 2 * N_HEADS) * _nbytes((SUBLANES, tq), F32)
            + _nbytes((N_HEADS, HEAD_DIM, tq), F32)
            + (ATTN_LOOKAHEAD + 3) * _nbytes((ka, tq, tq), F32))
    return pl.pallas_call(
        functools.partial(_dsa_core_kernel, tq=tq, topk=topk, kt=kt, ka=ka),
        grid=(t // tq,),
        in_specs=[
            pl.BlockSpec((tq, Q_COLS), lambda i: (i, 0)),
            pl.BlockSpec((tq, IQ_COLS), lambda i: (i, 1)),
            pl.BlockSpec((tq, LANES), lambda i: (i, 0)),
            pl.BlockSpec((t, KV_COLS), lambda i: (0, k_block), **resident),
            pl.BlockSpec((KV_COLS, t), lambda i: (0, 0), **resident),
            pl.BlockSpec((t, IDX_DIM), lambda i: (0, ik_block), **resident),
        ],
        out_specs=pl.BlockSpec((tq, Q_COLS), lambda i: (i, 0)),
        out_shape=jax.ShapeDtypeStruct((t, Q_COLS), proj.dtype),
        scratch_shapes=[
            pltpu.VMEM((t // tq, tq, tq), jnp.int32),
            pltpu.VMEM((ka, tq, tq), F32),
            pltpu.VMEM((ATTN_LOOKAHEAD + 1, ka, tq, tq), F32),
            pltpu.VMEM((LANES, tq), F32),
            pltpu.VMEM((3, 1, tq), jnp.int32),
            pltpu.VMEM((N_HEADS, 1, tq), F32),
            pltpu.VMEM((N_HEADS, 1, tq), F32),
            pltpu.VMEM((N_HEADS, HEAD_DIM, tq), F32),
        ],
        compiler_params=pltpu.CompilerParams(
            dimension_semantics=("arbitrary",),
            vmem_limit_bytes=_vmem_limit(need)),
        name="dsa_core",
    )(proj, proj, iw, proj, vt, proj)


def _rope_tables(pos):
    def angles(dim):
        inv = 1.0 / (ROPE_THETA ** (jnp.arange(0, dim, 2, dtype=F32) / dim))
        return pos.astype(F32)[:, None] * inv

    ang = angles(HEAD_DIM)
    cos, sin = jnp.cos(ang), jnp.sin(ang)
    rc = jnp.concatenate([cos, cos], axis=1)
    rs = jnp.concatenate([-sin, sin], axis=1)
    iang = angles(IDX_ROPE_DIM)
    icos, isin = jnp.cos(iang), jnp.sin(iang)
    rest = LANES - IDX_ROPE_DIM
    zeros = jnp.zeros_like(isin)
    ic = jnp.concatenate([icos, icos, jnp.ones((pos.shape[0], rest), F32)], axis=1)
    ia = jnp.concatenate([-isin, zeros, jnp.zeros((pos.shape[0], rest), F32)], axis=1)
    ib = jnp.concatenate([zeros, isin, jnp.zeros((pos.shape[0], rest), F32)], axis=1)
    return rc, rs, ic, ia, ib


def dsa_layer(x, pos, norm_g, w_in, q_norm, k_norm, w_out, tiles):
    t = x.shape[0]
    proj, iw, vt = dsa_projection(x, norm_g, w_in.astype(MXU_DTYPE), q_norm, k_norm,
                                  _rope_tables(pos), tm=tiles["proj_tm"])
    attn = dsa_core(proj, iw, vt, tq=tiles["tq"], topk=min(TOPK_MAX, t // 4), kt=tiles["kt"], ka=tiles["ka"])
    return matmul_residual(attn, w_out.astype(MXU_DTYPE), x, tm=tiles["res_tm"], tn=tiles["res_tn"])


def _rglru_kernel(gate_ref, xr_ref, cw_ref, cb_ref, wa_ref, ba_ref, wx_ref, bx_ref, lam_ref,
                  y_ref, hcar_ref, xprev_ref, *, tt):
    @pl.when(pl.program_id(1) == 0)
    def _():
        hcar_ref[...] = jnp.zeros_like(hcar_ref)
        xprev_ref[...] = jnp.zeros_like(xprev_ref)

    xr = xr_ref[...]
    ext = jnp.concatenate([xprev_ref[...], xr], axis=0)
    cw = cw_ref[...]
    xc = cb_ref[...] + xr * cw[CONV_WIDTH - 1:CONV_WIDTH, :]
    for d in range(1, CONV_WIDTH):
        xc = xc + pltpu.roll(ext, d, 0)[SUBLANES:, :] * cw[CONV_WIDTH - 1 - d:CONV_WIDTH - d, :]
    xprev_ref[...] = xr[tt - SUBLANES:, :]

    xcb = xc.astype(wa_ref.dtype)
    r = jax.nn.sigmoid(jnp.dot(xcb, wa_ref[...], preferred_element_type=F32) + ba_ref[...])
    ig = jax.nn.sigmoid(jnp.dot(xcb, wx_ref[...], preferred_element_type=F32) + bx_ref[...])
    nlam = -lam_ref[...]
    softplus = jnp.maximum(nlam, 0.0) + jnp.log(1.0 + jnp.exp(-jnp.abs(nlam)))
    log_a = -LRU_C * r * softplus
    a = jnp.exp(log_a)
    mult = jnp.sqrt(1.0 - jnp.exp(2.0 * log_a))
    b = xc * ig * mult

    n_groups = tt // SUBLANES
    a = a.reshape(n_groups, SUBLANES, a.shape[1])
    b = b.reshape(a.shape)
    in_group = lax.broadcasted_iota(jnp.int32, a.shape, 1)
    d = 1
    while d < SUBLANES:
        b = a * jnp.where(in_group >= d, pltpu.roll(b, d, 1), 0.0) + b
        a = a * jnp.where(in_group >= d, pltpu.roll(a, d, 1), 1.0)
        d *= 2
    carry = hcar_ref[...]
    groups = []
    for gi in range(n_groups):
        hg = b[gi] + a[gi] * carry
        carry = hg[SUBLANES - 1:SUBLANES, :]
        groups.append(hg)
    hcar_ref[...] = carry
    h = jnp.concatenate(groups, axis=0)
    y_ref[...] = (h * jax.nn.gelu(gate_ref[...])).astype(y_ref.dtype)


def rglru_scan(proj, conv_w, conv_b, wa, ba, wx, bx, lam, *, tt):
    t = proj.shape[0]
    width = conv_w.shape[1]
    nb, blk, _ = wa.shape
    vec = lambda: pl.BlockSpec((1, blk), lambda n, s: (0, n))
    need = (4 * _nbytes((tt, blk), F32) + 2 * _nbytes((tt, blk), MXU_DTYPE)
            + 4 * _nbytes((blk, blk), wa.dtype) + 24 * _nbytes((tt, blk), F32))
    return pl.pallas_call(
        functools.partial(_rglru_kernel, tt=tt),
        grid=(nb, t // tt),
        in_specs=[
            pl.BlockSpec((tt, blk), lambda n, s: (s, n)),
            pl.BlockSpec((tt, blk), lambda n, s: (s, nb + n)),
            pl.BlockSpec((CONV_WIDTH, blk), lambda n, s: (0, n)),
            vec(),
            pl.BlockSpec((None, blk, blk), lambda n, s: (n, 0, 0)),
            vec(),
            pl.BlockSpec((None, blk, blk), lambda n, s: (n, 0, 0)),
            vec(),
            vec(),
        ],
        out_specs=pl.BlockSpec((tt, blk), lambda n, s: (s, n)),
        out_shape=jax.ShapeDtypeStruct((t, width), MXU_DTYPE),
        scratch_shapes=[pltpu.VMEM((1, blk), F32), pltpu.VMEM((SUBLANES, blk), F32)],
        compiler_params=pltpu.CompilerParams(
            dimension_semantics=("parallel", "arbitrary"),
            vmem_limit_bytes=_vmem_limit(need)),
        name="rglru_scan",
    )(proj, proj, conv_w, conv_b.reshape(1, width), wa, ba.reshape(1, width),
      wx, bx.reshape(1, width), lam.reshape(1, width))


def rglru_layer(x, norm_g, w_in, conv_w, conv_b, wa, ba, wx, bx, lam, w_out, tiles):
    proj = norm_matmul(x, norm_g, w_in.astype(MXU_DTYPE), tm=tiles["proj_tm"], tn=tiles["proj_tn"],
                       out_dtype=F32)
    y = rglru_scan(proj, conv_w, conv_b, wa.astype(MXU_DTYPE), ba, wx.astype(MXU_DTYPE), bx, lam,
                   tt=tiles["scan_tt"])
    return matmul_residual(y, w_out.astype(MXU_DTYPE), x, tm=tiles["res_tm"], tn=tiles["res_tn"])


POOL_HALO = max(POOL_WINDOWS)


def _pool_kernel(x_ref, g_ref, w_ref, b_ref, s_ref, o_ref, halo_ref, *, tt):
    blk = pl.program_id(0)

    @pl.when(blk == 0)
    def _():
        halo_ref[...] = jnp.zeros_like(halo_ref)

    x = x_ref[...]
    h = _rms_norm_rows(x, g_ref[...])
    ext = jnp.concatenate([halo_ref[...], h], axis=0)
    halo_ref[...] = h[tt - POOL_HALO:, :]

    gw = h.shape[1] // len(POOL_WINDOWS)
    t1 = (lax.broadcasted_iota(jnp.int32, (tt, gw), 0) + (blk * tt + 1)).astype(F32)
    for gi, win in enumerate(POOL_WINDOWS):
        cols = slice(gi * gw, (gi + 1) * gw)
        acc = ext[:, cols]
        d = 1
        while d < win:
            acc = acc + pltpu.roll(acc, d, 0)
            d *= 2
        mean = acc[POOL_HALO:, :] / jnp.minimum(t1, float(win))
        y = (mean - h[:, cols]).astype(w_ref.dtype)
        z = jnp.dot(y, w_ref[gi], preferred_element_type=F32) + b_ref[gi:gi + 1, :]
        o_ref[:, cols] = x[:, cols] + z * s_ref[:, cols]


def pool_layer(x, norm_g, w_group, b_group, scale, tiles):
    t, d = x.shape
    tt = tiles["pool_tt"]
    ng, gw, _ = w_group.shape
    need = (4 * _nbytes((tt, d), F32) + 2 * _nbytes((ng, gw, gw), MXU_DTYPE)
            + 6 * _nbytes((tt, d), F32))
    return pl.pallas_call(
        functools.partial(_pool_kernel, tt=tt),
        grid=(t // tt,),
        in_specs=[
            pl.BlockSpec((tt, d), lambda i: (i, 0)),
            pl.BlockSpec((1, d), lambda i: (0, 0)),
            pl.BlockSpec((ng, gw, gw), lambda i: (0, 0, 0)),
            pl.BlockSpec((ng, gw), lambda i: (0, 0)),
            pl.BlockSpec((1, d), lambda i: (0, 0)),
        ],
        out_specs=pl.BlockSpec((tt, d), lambda i: (i, 0)),
        out_shape=jax.ShapeDtypeStruct((t, d), F32),
        scratch_shapes=[pltpu.VMEM((POOL_HALO, d), F32)],
        compiler_params=pltpu.CompilerParams(
            dimension_semantics=("arbitrary",),
            vmem_limit_bytes=_vmem_limit(need)),
        name="pool_mixer",
    )(x, norm_g.reshape(1, d), w_group.astype(MXU_DTYPE), b_group, scale.reshape(1, d))


def _tiles(t):
    big = min(t, 512)
    tall = min(t, 1024)
    return {
        "proj_tm": tall, "proj_tn": 512,
        "res_tm": tall, "res_tn": 512,
        "tq": min(t, 256), "kt": min(4, t // min(t, 256)), "ka": min(4, t // min(t, 256)),
        "scan_tt": min(t, 256),
        "pool_tt": min(t, 256),
        "mlp_tm": big, "mlp_tf": 512,
    }


def kernel(x, positions, attn_norm, attn_w_in, attn_q_norm, attn_k_norm, attn_w_out, rnn_norm, rnn_w_in, rnn_conv_w, rnn_conv_b, rnn_gate_a_w, rnn_gate_a_b, rnn_gate_x_w, rnn_gate_x_b, rnn_lambda, rnn_w_out, pool_norm, pool_w, pool_b, pool_scale, mlp_norm, mlp_w_up, mlp_w_down):
    batch, t, d = x.shape
    depth = mlp_norm.shape[0]
    tiles = _tiles(t)
    outs = []
    for bi in range(batch):
        xb = x[bi]
        pos = positions[bi]
        for i in range(depth):
            kind, j = i % N_MIXERS, i // N_MIXERS
            if kind == 0:
                xb = dsa_layer(xb, pos, attn_norm[j], attn_w_in[j], attn_q_norm[j], attn_k_norm[j],
                               attn_w_out[j], tiles)
            elif kind == 1:
                xb = rglru_layer(xb, rnn_norm[j], rnn_w_in[j], rnn_conv_w[j], rnn_conv_b[j],
                                 rnn_gate_a_w[j], rnn_gate_a_b[j], rnn_gate_x_w[j], rnn_gate_x_b[j],
                                 rnn_lambda[j], rnn_w_out[j], tiles)
            else:
                xb = pool_layer(xb, pool_norm[j], pool_w[j], pool_b[j], pool_scale[j], tiles)
            xb = mlp_block(xb, mlp_norm[i], mlp_w_up[i].astype(MXU_DTYPE),
                           mlp_w_down[i].astype(MXU_DTYPE), tm=tiles["mlp_tm"], tf=tiles["mlp_tf"])
        outs.append(xb)
    return outs[0][None] if batch == 1 else jnp.stack(outs, axis=0)
```

```python
import functools
import math

import jax
import jax.numpy as jnp
from jax import lax
from jax.experimental import pallas as pl
from jax.experimental.pallas import tpu as pltpu

F32 = jnp.float32
MXU_DTYPE = jnp.bfloat16

N_MIXERS = 3
EPS = 1e-6
ROPE_THETA = 10000.0
HEAD_DIM = 128
N_HEADS = 16
N_KV_HEADS = 4
IDX_HEADS = 16
IDX_DIM = 128
IDX_ROPE_DIM = 64
TOPK_MAX = 256
CONV_WIDTH = 4
LRU_C = 8.0
POOL_WINDOWS = (2, 4, 8, 16)

LANES = 128
SUBLANES = 8
VMEM_BYTES_V7X = 64 * 1024 * 1024
VMEM_CAP_BYTES = VMEM_BYTES_V7X - 8 * 1024 * 1024

INT_MIN = -(2 ** 31)
NEG_BIG = -1e30


def _vmem_limit(block_bytes):
    return int(min(VMEM_CAP_BYTES, block_bytes * 3 // 2 + (4 << 20)))


def _nbytes(shape, dtype):
    return math.prod(shape) * jnp.dtype(dtype).itemsize


def _rms_norm_rows(x, g):
    ms = jnp.mean(x * x, axis=-1, keepdims=True)
    return x * lax.rsqrt(ms + EPS) * g


def _tile_lanes(x, width):
    reps = width // LANES
    return x if reps == 1 else jnp.concatenate([x] * reps, axis=1)


def _norm_matmul_kernel(x_ref, g_ref, w_ref, o_ref, h_ref):
    @pl.when(pl.program_id(1) == 0)
    def _():
        h_ref[...] = _rms_norm_rows(x_ref[...], g_ref[...]).astype(h_ref.dtype)

    o_ref[...] = jnp.dot(h_ref[...], w_ref[...], preferred_element_type=F32).astype(o_ref.dtype)


def norm_matmul(x, g, w, *, tm, tn, out_dtype):
    t, d = x.shape
    n = w.shape[1]
    need = (2 * _nbytes((tm, d), F32) + 2 * _nbytes((d, tn), w.dtype)
            + 2 * _nbytes((tm, tn), out_dtype) + _nbytes((tm, d), w.dtype))
    return pl.pallas_call(
        _norm_matmul_kernel,
        grid=(t // tm, n // tn),
        in_specs=[
            pl.BlockSpec((tm, d), lambda i, j: (i, 0)),
            pl.BlockSpec((1, d), lambda i, j: (0, 0)),
            pl.BlockSpec((d, tn), lambda i, j: (0, j)),
        ],
        out_specs=pl.BlockSpec((tm, tn), lambda i, j: (i, j)),
        out_shape=jax.ShapeDtypeStruct((t, n), out_dtype),
        scratch_shapes=[pltpu.VMEM((tm, d), w.dtype)],
        compiler_params=pltpu.CompilerParams(
            dimension_semantics=("parallel", "arbitrary"),
            vmem_limit_bytes=_vmem_limit(need)),
        name="norm_matmul",
    )(x, g.reshape(1, d), w)


def _matmul_residual_kernel(a_ref, w_ref, x_ref, o_ref):
    o_ref[...] = x_ref[...] + jnp.dot(a_ref[...], w_ref[...], preferred_element_type=F32)


def matmul_residual(a, w, x, *, tm, tn):
    t, k = a.shape
    n = w.shape[1]
    need = (2 * _nbytes((tm, k), a.dtype) + 2 * _nbytes((k, tn), w.dtype)
            + 4 * _nbytes((tm, tn), F32))
    return pl.pallas_call(
        _matmul_residual_kernel,
        grid=(t // tm, n // tn),
        in_specs=[
            pl.BlockSpec((tm, k), lambda i, j: (i, 0)),
            pl.BlockSpec((k, tn), lambda i, j: (0, j)),
            pl.BlockSpec((tm, tn), lambda i, j: (i, j)),
        ],
        out_specs=pl.BlockSpec((tm, tn), lambda i, j: (i, j)),
        out_shape=jax.ShapeDtypeStruct((t, n), F32),
        compiler_params=pltpu.CompilerParams(
            dimension_semantics=("parallel", "arbitrary"),
            vmem_limit_bytes=_vmem_limit(need)),
        name="matmul_residual",
    )(a, w, x)


def _mlp_kernel(x_ref, g_ref, wu_ref, wd_ref, o_ref, h_ref, acc_ref):
    f = pl.program_id(1)

    @pl.when(f == 0)
    def _():
        h_ref[...] = _rms_norm_rows(x_ref[...], g_ref[...]).astype(h_ref.dtype)
        acc_ref[...] = jnp.zeros_like(acc_ref)

    u = jnp.dot(h_ref[...], wu_ref[...], preferred_element_type=F32)
    u = jnp.square(jnp.maximum(u, 0.0)).astype(wd_ref.dtype)
    acc_ref[...] += jnp.dot(u, wd_ref[...], preferred_element_type=F32)

    @pl.when(f == pl.num_programs(1) - 1)
    def _():
        o_ref[...] = x_ref[...] + acc_ref[...]


def mlp_block(x, g, w_up, w_down, *, tm, tf):
    t, d = x.shape
    ff = w_up.shape[1]
    need = (4 * _nbytes((tm, d), F32) + 2 * _nbytes((d, tf), w_up.dtype)
            + 2 * _nbytes((tf, d), w_down.dtype) + _nbytes((tm, d), w_up.dtype)
            + _nbytes((tm, d), F32) + 2 * _nbytes((tm, tf), F32))
    return pl.pallas_call(
        _mlp_kernel,
        grid=(t // tm, ff // tf),
        in_specs=[
            pl.BlockSpec((tm, d), lambda i, f: (i, 0)),
            pl.BlockSpec((1, d), lambda i, f: (0, 0)),
            pl.BlockSpec((d, tf), lambda i, f: (0, f)),
            pl.BlockSpec((tf, d), lambda i, f: (f, 0)),
        ],
        out_specs=pl.BlockSpec((tm, d), lambda i, f: (i, 0)),
        out_shape=jax.ShapeDtypeStruct((t, d), F32),
        scratch_shapes=[pltpu.VMEM((tm, d), w_up.dtype), pltpu.VMEM((tm, d), F32)],
        compiler_params=pltpu.CompilerParams(
            dimension_semantics=("parallel", "arbitrary"),
            vmem_limit_bytes=_vmem_limit(need)),
        name="mlp_block",
    )(x, g.reshape(1, d), w_up, w_down)


DSA_TN = 2 * LANES
Q_COLS = N_HEADS * HEAD_DIM
IQ_COLS = IDX_HEADS * IDX_DIM
KV_COLS = N_KV_HEADS * HEAD_DIM
DSA_COLS = Q_COLS + IQ_COLS + 2 * KV_COLS + IDX_DIM + LANES
Q_BLOCKS = Q_COLS // DSA_TN
IQ_BLOCKS = IQ_COLS // DSA_TN
KV_BLOCKS = KV_COLS // DSA_TN


W_K_FIRST = Q_BLOCKS
W_V_FIRST = W_K_FIRST + KV_BLOCKS
W_IQ_FIRST = W_V_FIRST + KV_BLOCKS
W_LAST = W_IQ_FIRST + IQ_BLOCKS
OUT_K_FIRST = Q_BLOCKS + IQ_BLOCKS
OUT_V_FIRST = OUT_K_FIRST + KV_BLOCKS


def _dsa_out_block(jw):
    return jnp.where(jw < W_K_FIRST, jw,
                     jnp.where(jw < W_IQ_FIRST, jw + (OUT_K_FIRST - W_K_FIRST),
                               jnp.where(jw < W_LAST, jw - (W_IQ_FIRST - Q_BLOCKS), W_LAST)))


def _dsa_proj_kernel(x_ref, g_ref, w_ref, qn_ref, kn_ref, rc_ref, rs_ref,
                     ic_ref, ia_ref, ib_ref, o_ref, iw_ref, vt_ref, h_ref, y_ref):
    j = pl.program_id(1)
    jp = j - 1

    @pl.when(j == 0)
    def _():
        h_ref[...] = _rms_norm_rows(x_ref[...], g_ref[...]).astype(h_ref.dtype)

    def rope(z):
        return z * rc_ref[...] + pltpu.roll(z, HEAD_DIM // 2, 1) * rs_ref[...]

    def idx_rope(z):
        half = IDX_ROPE_DIM // 2
        return (z * ic_ref[...] + pltpu.roll(z, LANES - half, 1) * ia_ref[...]
                + pltpu.roll(z, half, 1) * ib_ref[...])

    def per_head(fn):
        def finish(y):
            o_ref[...] = jnp.concatenate(
                [fn(y[:, :LANES]), fn(y[:, LANES:])], axis=1).astype(o_ref.dtype)
        return finish

    def step(finish, multiply=True):
        y_prev = y_ref[...]
        if multiply:
            y_ref[...] = jnp.dot(h_ref[...], w_ref[...].astype(h_ref.dtype),
                                 preferred_element_type=F32)
        if finish is not None:
            finish(y_prev)

    @pl.when(j == 0)
    def _():
        step(None)

    @pl.when((jp >= 0) & (jp < W_K_FIRST))
    def _():
        scale = HEAD_DIM ** -0.5 * math.log2(math.e)
        step(per_head(lambda z: rope(_rms_norm_rows(z, qn_ref[...])) * scale))

    @pl.when((jp >= W_K_FIRST) & (jp < W_V_FIRST))
    def _():
        step(per_head(lambda z: rope(_rms_norm_rows(z, kn_ref[...]))))

    @pl.when((jp >= W_V_FIRST) & (jp < W_IQ_FIRST))
    def _():
        def finish(y):
            o_ref[...] = y.astype(o_ref.dtype)
            vt_ref[...] = jnp.transpose(y).astype(vt_ref.dtype)
        step(finish)

    @pl.when((jp >= W_IQ_FIRST) & (jp < W_LAST))
    def _():
        step(per_head(idx_rope))

    @pl.when(jp == W_LAST)
    def _():
        def finish(y):
            o_ref[...] = jnp.concatenate(
                [idx_rope(y[:, :LANES]), jnp.zeros_like(y[:, LANES:])], axis=1).astype(o_ref.dtype)
            lane = lax.broadcasted_iota(jnp.int32, (y.shape[0], LANES), 1)
            iw_ref[...] = jnp.where(lane < IDX_HEADS,
                                    y[:, LANES:] * (IDX_HEADS ** -0.5 * IDX_DIM ** -0.5), 0.0)
        step(finish, multiply=False)


def dsa_projection(x, g, w_layers, layer, q_norm, k_norm, tables, *, tm):
    t, d = x.shape
    assert w_layers.shape[2] == Q_COLS + 2 * KV_COLS + IQ_COLS + IDX_DIM + IDX_HEADS
    rc, rs, ic, ia, ib = tables
    tab_spec = pl.BlockSpec((tm, LANES), lambda i, j: (i, 0))
    vec_spec = pl.BlockSpec((1, LANES), lambda i, j: (0, 0))
    need = (2 * _nbytes((tm, d), F32) + 3 * _nbytes((d, DSA_TN), F32)
            + 2 * _nbytes((tm, DSA_TN), MXU_DTYPE) + _nbytes((tm, d), MXU_DTYPE)
            + 12 * _nbytes((tm, LANES), F32) + 3 * _nbytes((tm, DSA_TN), F32))
    return pl.pallas_call(
        _dsa_proj_kernel,
        grid=(t // tm, W_LAST + 2),
        in_specs=[
            pl.BlockSpec((tm, d), lambda i, j: (i, 0)),
            pl.BlockSpec((1, d), lambda i, j: (0, 0)),
            pl.BlockSpec((None, d, DSA_TN), lambda i, j: (layer, 0, jnp.minimum(j, W_LAST))),
            vec_spec, vec_spec, tab_spec, tab_spec, tab_spec, tab_spec, tab_spec,
        ],
        out_specs=[
            pl.BlockSpec((tm, DSA_TN), lambda i, j: (i, _dsa_out_block(jnp.maximum(j - 1, 0)))),
            pl.BlockSpec((tm, LANES), lambda i, j: (i, 0)),
            pl.BlockSpec((DSA_TN, tm), lambda i, j: (jnp.clip(j - 1 - W_V_FIRST, 0, KV_BLOCKS - 1), i)),
        ],
        out_shape=[
            jax.ShapeDtypeStruct((t, DSA_COLS), MXU_DTYPE),
            jax.ShapeDtypeStruct((t, LANES), F32),
            jax.ShapeDtypeStruct((KV_COLS, t), MXU_DTYPE),
        ],
        scratch_shapes=[pltpu.VMEM((tm, d), MXU_DTYPE), pltpu.VMEM((tm, DSA_TN), F32)],
        compiler_params=pltpu.CompilerParams(
            dimension_semantics=("parallel", "arbitrary"),
            vmem_limit_bytes=_vmem_limit(need)),
        name="dsa_projection",
    )(x, g.reshape(1, d), w_layers, q_norm.reshape(1, LANES), k_norm.reshape(1, LANES), rc, rs, ic, ia, ib)


def _float_key(v):
    b = lax.bitcast_convert_type(v, jnp.int32)
    return b ^ ((b >> 31) & 0x7FFFFFFF)


def _key_float(k):
    return lax.bitcast_convert_type(k ^ ((k >> 31) & 0x7FFFFFFF), F32)


_KLO, _KHI, _CLO = range(3)
_SEARCH_CAP = 96
ONES_ROWS = 2 * SUBLANES
ATTN_LOOKAHEAD = 1


def _dsa_core_kernel(q_ref, iq_ref, iw_ref, k_ref, vt_ref, ik_ref, o_ref,
                     key_ref, bias_ref, s_ref, wt_ref, st_ref, m_ref, l_ref, acc_ref, *, tq, topk, kt, ka):
    i = pl.program_id(0)
    n_tiles = (i + kt) // kt
    group = N_HEADS // N_KV_HEADS
    nt = (((1,), (1,)), ((), ()))

    wt_ref[...] = jnp.transpose(iw_ref[...])

    krow = lax.broadcasted_iota(jnp.int32, (tq, tq), 0)
    qcol = lax.broadcasted_iota(jnp.int32, (tq, tq), 1)
    diag_causal = krow <= qcol

    def chunk_start(c):
        return pl.multiple_of(c * tq, tq)

    def index_tile(ti, carry):
        smin, smax = carry
        iks = [ik_ref[pl.ds(chunk_start(ti * kt + u), tq), :] for u in range(kt)]
        scores = [jnp.zeros((tq, tq), F32) for _ in range(kt)]
        for h in range(IDX_HEADS):
            iqh = iq_ref[:, h * IDX_DIM:(h + 1) * IDX_DIM]
            w = jnp.broadcast_to(wt_ref[h:h + 1, :], (tq, tq))
            for u in range(kt):
                logits = lax.dot_general(iks[u], iqh, nt, preferred_element_type=F32)
                scores[u] = scores[u] + w * jnp.maximum(logits, 0.0)
        for u in range(kt):
            c = ti * kt + u
            valid = (c < i) | ((c == i) & diag_causal)
            key_ref[c] = jnp.where(valid, _float_key(scores[u]), INT_MIN)
            smin = jnp.minimum(smin, jnp.min(jnp.where(valid, scores[u], jnp.inf), axis=0, keepdims=True))
            smax = jnp.maximum(smax, jnp.max(jnp.where(valid, scores[u], -jnp.inf), axis=0, keepdims=True))
        return smin, smax

    smin, smax = lax.fori_loop(
        0, n_tiles, index_tile,
        (jnp.full((1, tq), jnp.inf, F32), jnp.full((1, tq), -jnp.inf, F32)))

    def count_ge(cand):
        cand_b = jnp.broadcast_to(cand, (tq, tq))

        def body(ti, cnt):
            for u in range(kt):
                hit = jnp.where(key_ref[ti * kt + u] >= cand_b, 1.0, 0.0)
                cnt = cnt + jnp.sum(hit, axis=0, keepdims=True)
            return cnt

        return lax.fori_loop(0, n_tiles, body, jnp.zeros((1, tq), F32))

    st_ref[_KLO] = _float_key(smin)
    st_ref[_KHI] = _float_key(smax) + 1
    st_ref[_CLO] = lax.broadcasted_iota(jnp.int32, (1, tq), 1) + (i * tq + 1)

    def unfinished(klo, khi, clo):
        return (clo > topk) & ((khi - klo) != 1)

    def search_cond(st):
        step, pending = st
        return (step < _SEARCH_CAP) & (pending > 0)

    def any_lane(mask):
        return jnp.max(jnp.where(mask, 1.0, 0.0))

    def search_body(st):
        step, _ = st
        klo, khi, clo = st_ref[_KLO], st_ref[_KHI], st_ref[_CLO]
        live = unfinished(klo, khi, clo)
        pending = any_lane(live)
        vmid = 0.5 * _key_float(klo) + 0.5 * _key_float(khi)
        cmid = _float_key(vmid)
        kmid = klo + lax.shift_right_logical(khi - klo, 1)
        cand = jnp.where((cmid > klo) & (cmid < khi), cmid, kmid)
        cand = jnp.where(live, cand, klo)
        cnt = count_ge(cand).astype(jnp.int32)
        take = cnt >= topk
        klo = jnp.where(live & take, cand, klo)
        clo = jnp.where(live & take, cnt, clo)
        khi = jnp.where(live & jnp.logical_not(take), cand, khi)
        st_ref[_KLO], st_ref[_KHI], st_ref[_CLO] = klo, khi, clo
        return step + 1, pending

    lax.while_loop(search_cond, search_body, (jnp.int32(0), jnp.float32(1.0)))
    thr_b = jnp.broadcast_to(st_ref[_KLO], (tq, tq))

    m_ref[...] = jnp.full(m_ref.shape, NEG_BIG, F32)
    l_ref[...] = jnp.zeros(l_ref.shape, F32)
    acc_ref[...] = jnp.zeros(acc_ref.shape, F32)

    def attend_tile(ti, carry):
        for u in range(ka):
            bias_ref[u] = jnp.where(key_ref[ti * ka + u] >= thr_b, 0.0, NEG_BIG)
        starts = [chunk_start(ti * ka + u) for u in range(ka)]
        slots = ATTN_LOOKAHEAD + 1
        dyn0 = jnp.minimum(ti, 0)
        ones_rows = jnp.ones((ONES_ROWS, tq), vt_ref.dtype)

        def logits(h):
            cols = slice(h // group * HEAD_DIM, (h // group + 1) * HEAD_DIM)
            qh = q_ref[:, h * HEAD_DIM:(h + 1) * HEAD_DIM]
            top = None
            for u in range(ka):
                s = lax.dot_general(k_ref[pl.ds(starts[u], tq), cols], qh, nt,
                                    preferred_element_type=F32) + bias_ref[u]
                s_ref[h % slots + dyn0, u] = s
                top = s if top is None else jnp.maximum(top, s)
            return jnp.max(top, axis=0, keepdims=True)

        queued = [logits(h) for h in range(ATTN_LOOKAHEAD)]
        for h in range(N_HEADS):
            tile_max = queued.pop(0)
            if h + ATTN_LOOKAHEAD < N_HEADS:
                queued.append(logits(h + ATTN_LOOKAHEAD))
            cols = slice(h // group * HEAD_DIM, (h // group + 1) * HEAD_DIM)
            m_prev = m_ref[h]
            m_cur = jnp.maximum(m_prev, tile_max)
            alpha = jnp.exp2(m_prev - m_cur)
            pv = jnp.zeros((HEAD_DIM + ONES_ROWS, tq), F32)
            for u in range(ka):
                p = jnp.exp2(s_ref[h % slots + dyn0, u] - m_cur)
                v_aug = jnp.concatenate([vt_ref[cols, pl.ds(starts[u], tq)], ones_rows], axis=0)
                pv = pv + jnp.dot(v_aug, p.astype(vt_ref.dtype), preferred_element_type=F32)
            l_ref[h] = alpha * l_ref[h] + pv[HEAD_DIM:HEAD_DIM + 1, :]
            pv = pv[:HEAD_DIM, :]
            acc_ref[h] = alpha * acc_ref[h] + pv
            m_ref[h] = m_cur
        return carry

    lax.fori_loop(0, (i + ka) // ka, attend_tile, 0)

    for h in range(N_HEADS):
        o_ref[:, h * HEAD_DIM:(h + 1) * HEAD_DIM] = jnp.transpose(acc_ref[h] / l_ref[h]).astype(o_ref.dtype)


def dsa_core(proj, iw, vt, *, tq, topk, kt, ka):
    t = proj.shape[0]
    assert (t // tq) % kt == 0 and kt % ka == 0
    k_block = (Q_COLS + IQ_COLS) // KV_COLS
    ik_block = (Q_COLS + IQ_COLS + 2 * KV_COLS) // IDX_DIM
    resident = dict(pipeline_mode=pl.Buffered(1))
    need = (4 * _nbytes((tq, Q_COLS), proj.dtype) + 2 * _nbytes((tq, LANES), F32)
            + 2 * _nbytes((t, KV_COLS), proj.dtype) + _nbytes((t, IDX_DIM), proj.dtype)
            + 2 * _nbytes((tq, Q_COLS), proj.dtype)
            + _nbytes((t // tq, tq, tq), jnp.int32) + _nbytes((kt, tq, tq), F32)
            + _nbytes((LANES, tq), F32) + (3 + 2 * N_HEADS) * _nbytes((SUBLANES, tq), F32)
            + _nbytes((N_HEADS, HEAD_DIM, tq), F32)
            + (ATTN_LOOKAHEAD + 3) * _nbytes((ka, tq, tq), F32))
    return pl.pallas_call(
        functools.partial(_dsa_core_kernel, tq=tq, topk=topk, kt=kt, ka=ka),
        grid=(t // tq,),
        in_specs=[
            pl.BlockSpec((tq, Q_COLS), lambda i: (i, 0)),
            pl.BlockSpec((tq, IQ_COLS), lambda i: (i, 1)),
            pl.BlockSpec((tq, LANES), lambda i: (i, 0)),
            pl.BlockSpec((t, KV_COLS), lambda i: (0, k_block), **resident),
            pl.BlockSpec((KV_COLS, t), lambda i: (0, 0), **resident),
            pl.BlockSpec((t, IDX_DIM), lambda i: (0, ik_block), **resident),
        ],
        out_specs=pl.BlockSpec((tq, Q_COLS), lambda i: (i, 0)),
        out_shape=jax.ShapeDtypeStruct((t, Q_COLS), proj.dtype),
        scratch_shapes=[
            pltpu.VMEM((t // tq, tq, tq), jnp.int32),
            pltpu.VMEM((ka, tq, tq), F32),
            pltpu.VMEM((ATTN_LOOKAHEAD + 1, ka, tq, tq), F32),
            pltpu.VMEM((LANES, tq), F32),
            pltpu.VMEM((3, 1, tq), jnp.int32),
            pltpu.VMEM((N_HEADS, 1, tq), F32),
            pltpu.VMEM((N_HEADS, 1, tq), F32),
            pltpu.VMEM((N_HEADS, HEAD_DIM, tq), F32),
        ],
        compiler_params=pltpu.CompilerParams(
            dimension_semantics=("arbitrary",),
            vmem_limit_bytes=_vmem_limit(need)),
        name="dsa_core",
    )(proj, proj, iw, proj, vt, proj)


def _rope_tables(pos):
    def angles(dim):
        inv = 1.0 / (ROPE_THETA ** (jnp.arange(0, dim, 2, dtype=F32) / dim))
        return pos.astype(F32)[:, None] * inv

    ang = angles(HEAD_DIM)
    cos, sin = jnp.cos(ang), jnp.sin(ang)
    rc = jnp.concatenate([cos, cos], axis=1)
    rs = jnp.concatenate([-sin, sin], axis=1)
    iang = angles(IDX_ROPE_DIM)
    icos, isin = jnp.cos(iang), jnp.sin(iang)
    rest = LANES - IDX_ROPE_DIM
    zeros = jnp.zeros_like(isin)
    ic = jnp.concatenate([icos, icos, jnp.ones((pos.shape[0], rest), F32)], axis=1)
    ia = jnp.concatenate([-isin, zeros, jnp.zeros((pos.shape[0], rest), F32)], axis=1)
    ib = jnp.concatenate([zeros, isin, jnp.zeros((pos.shape[0], rest), F32)], axis=1)
    return rc, rs, ic, ia, ib


def dsa_layer(x, pos, norm_g, w_in_layers, layer, q_norm, k_norm, w_out, tiles):
    t = x.shape[0]
    proj, iw, vt = dsa_projection(x, norm_g, w_in_layers, layer, q_norm, k_norm,
                                  _rope_tables(pos), tm=tiles["proj_tm"])
    attn = dsa_core(proj, iw, vt, tq=tiles["tq"], topk=min(TOPK_MAX, t // 4), kt=tiles["kt"], ka=tiles["ka"])
    return matmul_residual(attn, w_out.astype(MXU_DTYPE), x, tm=tiles["res_tm"], tn=tiles["res_tn"])


def _rglru_kernel(gate_ref, xr_ref, cw_ref, cb_ref, wa_ref, ba_ref, wx_ref, bx_ref, lam_ref,
                  y_ref, hcar_ref, xprev_ref, *, tt):
    @pl.when(pl.program_id(1) == 0)
    def _():
        hcar_ref[...] = jnp.zeros_like(hcar_ref)
        xprev_ref[...] = jnp.zeros_like(xprev_ref)

    xr = xr_ref[...]
    ext = jnp.concatenate([xprev_ref[...], xr], axis=0)
    cw = cw_ref[...]
    xc = cb_ref[...] + xr * cw[CONV_WIDTH - 1:CONV_WIDTH, :]
    for d in range(1, CONV_WIDTH):
        xc = xc + pltpu.roll(ext, d, 0)[SUBLANES:, :] * cw[CONV_WIDTH - 1 - d:CONV_WIDTH - d, :]
    xprev_ref[...] = xr[tt - SUBLANES:, :]

    xcb = xc.astype(wa_ref.dtype)
    r = jax.nn.sigmoid(jnp.dot(xcb, wa_ref[...], preferred_element_type=F32) + ba_ref[...])
    ig = jax.nn.sigmoid(jnp.dot(xcb, wx_ref[...], preferred_element_type=F32) + bx_ref[...])
    nlam = -lam_ref[...]
    softplus = jnp.maximum(nlam, 0.0) + jnp.log(1.0 + jnp.exp(-jnp.abs(nlam)))
    log_a = -LRU_C * r * softplus
    a = jnp.exp(log_a)
    mult = jnp.sqrt(1.0 - jnp.exp(2.0 * log_a))
    b = xc * ig * mult

    n_groups = tt // SUBLANES
    a = a.reshape(n_groups, SUBLANES, a.shape[1])
    b = b.reshape(a.shape)
    in_group = lax.broadcasted_iota(jnp.int32, a.shape, 1)
    d = 1
    while d < SUBLANES:
        b = a * jnp.where(in_group >= d, pltpu.roll(b, d, 1), 0.0) + b
        a = a * jnp.where(in_group >= d, pltpu.roll(a, d, 1), 1.0)
        d *= 2
    carry = hcar_ref[...]
    groups = []
    for gi in range(n_groups):
        hg = b[gi] + a[gi] * carry
        carry = hg[SUBLANES - 1:SUBLANES, :]
        groups.append(hg)
    hcar_ref[...] = carry
    h = jnp.concatenate(groups, axis=0)
    y_ref[...] = (h * jax.nn.gelu(gate_ref[...])).astype(y_ref.dtype)


def rglru_scan(proj, conv_w, conv_b, wa, ba, wx, bx, lam, *, tt):
    t = proj.shape[0]
    width = conv_w.shape[1]
    nb, blk, _ = wa.shape
    vec = lambda: pl.BlockSpec((1, blk), lambda n, s: (0, n))
    need = (4 * _nbytes((tt, blk), F32) + 2 * _nbytes((tt, blk), MXU_DTYPE)
            + 4 * _nbytes((blk, blk), wa.dtype) + 24 * _nbytes((tt, blk), F32))
    return pl.pallas_call(
        functools.partial(_rglru_kernel, tt=tt),
        grid=(nb, t // tt),
        in_specs=[
            pl.BlockSpec((tt, blk), lambda n, s: (s, n)),
            pl.BlockSpec((tt, blk), lambda n, s: (s, nb + n)),
            pl.BlockSpec((CONV_WIDTH, blk), lambda n, s: (0, n)),
            vec(),
            pl.BlockSpec((None, blk, blk), lambda n, s: (n, 0, 0)),
            vec(),
            pl.BlockSpec((None, blk, blk), lambda n, s: (n, 0, 0)),
            vec(),
            vec(),
        ],
        out_specs=pl.BlockSpec((tt, blk), lambda n, s: (s, n)),
        out_shape=jax.ShapeDtypeStruct((t, width), MXU_DTYPE),
        scratch_shapes=[pltpu.VMEM((1, blk), F32), pltpu.VMEM((SUBLANES, blk), F32)],
        compiler_params=pltpu.CompilerParams(
            dimension_semantics=("parallel", "arbitrary"),
            vmem_limit_bytes=_vmem_limit(need)),
        name="rglru_scan",
    )(proj, proj, conv_w, conv_b.reshape(1, width), wa, ba.reshape(1, width),
      wx, bx.reshape(1, width), lam.reshape(1, width))


def rglru_layer(x, norm_g, w_in, conv_w, conv_b, wa, ba, wx, bx, lam, w_out, tiles):
    proj = norm_matmul(x, norm_g, w_in.astype(MXU_DTYPE), tm=tiles["proj_tm"], tn=tiles["proj_tn"],
                       out_dtype=F32)
    y = rglru_scan(proj, conv_w, conv_b, wa.astype(MXU_DTYPE), ba, wx.astype(MXU_DTYPE), bx, lam,
                   tt=tiles["scan_tt"])
    return matmul_residual(y, w_out.astype(MXU_DTYPE), x, tm=tiles["res_tm"], tn=tiles["res_tn"])


POOL_HALO = max(POOL_WINDOWS)


def _pool_kernel(x_ref, g_ref, w_ref, b_ref, s_ref, o_ref, halo_ref, *, tt):
    blk = pl.program_id(0)

    @pl.when(blk == 0)
    def _():
        halo_ref[...] = jnp.zeros_like(halo_ref)

    x = x_ref[...]
    h = _rms_norm_rows(x, g_ref[...])
    ext = jnp.concatenate([halo_ref[...], h], axis=0)
    halo_ref[...] = h[tt - POOL_HALO:, :]

    gw = h.shape[1] // len(POOL_WINDOWS)
    t1 = (lax.broadcasted_iota(jnp.int32, (tt, gw), 0) + (blk * tt + 1)).astype(F32)
    for gi, win in enumerate(POOL_WINDOWS):
        cols = slice(gi * gw, (gi + 1) * gw)
        acc = ext[:, cols]
        d = 1
        while d < win:
            acc = acc + pltpu.roll(acc, d, 0)
            d *= 2
        mean = acc[POOL_HALO:, :] / jnp.minimum(t1, float(win))
        y = (mean - h[:, cols]).astype(w_ref.dtype)
        z = jnp.dot(y, w_ref[gi], preferred_element_type=F32) + b_ref[gi:gi + 1, :]
        o_ref[:, cols] = x[:, cols] + z * s_ref[:, cols]


def pool_layer(x, norm_g, w_group, b_group, scale, tiles):
    t, d = x.shape
    tt = tiles["pool_tt"]
    ng, gw, _ = w_group.shape
    need = (4 * _nbytes((tt, d), F32) + 2 * _nbytes((ng, gw, gw), MXU_DTYPE)
            + 6 * _nbytes((tt, d), F32))
    return pl.pallas_call(
        functools.partial(_pool_kernel, tt=tt),
        grid=(t // tt,),
        in_specs=[
            pl.BlockSpec((tt, d), lambda i: (i, 0)),
            pl.BlockSpec((1, d), lambda i: (0, 0)),
            pl.BlockSpec((ng, gw, gw), lambda i: (0, 0, 0)),
            pl.BlockSpec((ng, gw), lambda i: (0, 0)),
            pl.BlockSpec((1, d), lambda i: (0, 0)),
        ],
        out_specs=pl.BlockSpec((tt, d), lambda i: (i, 0)),
        out_shape=jax.ShapeDtypeStruct((t, d), F32),
        scratch_shapes=[pltpu.VMEM((POOL_HALO, d), F32)],
        compiler_params=pltpu.CompilerParams(
            dimension_semantics=("arbitrary",),
            vmem_limit_bytes=_vmem_limit(need)),
        name="pool_mixer",
    )(x, norm_g.reshape(1, d), w_group.astype(MXU_DTYPE), b_group, scale.reshape(1, d))


def _tiles(t):
    big = min(t, 512)
    tall = min(t, 1024)
    return {
        "proj_tm": tall, "proj_tn": 512,
        "res_tm": tall, "res_tn": 512,
        "tq": min(t, 256), "kt": min(4, t // min(t, 256)), "ka": min(4, t // min(t, 256)),
        "scan_tt": min(t, 256),
        "pool_tt": min(t, 256),
        "mlp_tm": big, "mlp_tf": 512,
    }


def kernel(x, positions, attn_norm, attn_w_in, attn_q_norm, attn_k_norm, attn_w_out, rnn_norm, rnn_w_in, rnn_conv_w, rnn_conv_b, rnn_gate_a_w, rnn_gate_a_b, rnn_gate_x_w, rnn_gate_x_b, rnn_lambda, rnn_w_out, pool_norm, pool_w, pool_b, pool_scale, mlp_norm, mlp_w_up, mlp_w_down):
    batch, t, d = x.shape
    depth = mlp_norm.shape[0]
    tiles = _tiles(t)
    outs = []
    for bi in range(batch):
        xb = x[bi]
        pos = positions[bi]
        for i in range(depth):
            kind, j = i % N_MIXERS, i // N_MIXERS
            if kind == 0:
                xb = dsa_layer(xb, pos, attn_norm[j], attn_w_in, j, attn_q_norm[j], attn_k_norm[j],
                               attn_w_out[j], tiles)
            elif kind == 1:
                xb = rglru_layer(xb, rnn_norm[j], rnn_w_in[j], rnn_conv_w[j], rnn_conv_b[j],
                                 rnn_gate_a_w[j], rnn_gate_a_b[j], rnn_gate_x_w[j], rnn_gate_x_b[j],
                                 rnn_lambda[j], rnn_w_out[j], tiles)
            else:
                xb = pool_layer(xb, pool_norm[j], pool_w[j], pool_b[j], pool_scale[j], tiles)
            xb = mlp_block(xb, mlp_norm[i], mlp_w_up[i].astype(MXU_DTYPE),
                           mlp_w_down[i].astype(MXU_DTYPE), tm=tiles["mlp_tm"], tf=tiles["mlp_tf"])
        outs.append(xb)
    return outs[0][None] if batch == 1 else jnp.stack(outs, axis=0)
```

```python
import functools
import math

import jax
import jax.numpy as jnp
from jax import lax
from jax.experimental import pallas as pl
from jax.experimental.pallas import tpu as pltpu

F32 = jnp.float32
MXU_DTYPE = jnp.bfloat16

N_MIXERS = 3
EPS = 1e-6
ROPE_THETA = 10000.0
HEAD_DIM = 128
N_HEADS = 16
N_KV_HEADS = 4
IDX_HEADS = 16
IDX_DIM = 128
IDX_ROPE_DIM = 64
TOPK_MAX = 256
CONV_WIDTH = 4
LRU_C = 8.0
POOL_WINDOWS = (2, 4, 8, 16)

LANES = 128
SUBLANES = 8
VMEM_BYTES_V7X = 64 * 1024 * 1024
VMEM_CAP_BYTES = VMEM_BYTES_V7X - 8 * 1024 * 1024

INT_MIN = -(2 ** 31)
NEG_BIG = -1e30


def _vmem_limit(block_bytes):
    return int(min(VMEM_CAP_BYTES, block_bytes * 3 // 2 + (4 << 20)))


def _nbytes(shape, dtype):
    return math.prod(shape) * jnp.dtype(dtype).itemsize


def _rms_norm_rows(x, g):
    ms = jnp.mean(x * x, axis=-1, keepdims=True)
    return x * lax.rsqrt(ms + EPS) * g


def _tile_lanes(x, width):
    reps = width // LANES
    return x if reps == 1 else jnp.concatenate([x] * reps, axis=1)


def _norm_matmul_kernel(x_ref, g_ref, w_ref, o_ref, h_ref):
    @pl.when(pl.program_id(1) == 0)
    def _():
        h_ref[...] = _rms_norm_rows(x_ref[...], g_ref[...]).astype(h_ref.dtype)

    o_ref[...] = jnp.dot(h_ref[...], w_ref[...], preferred_element_type=F32).astype(o_ref.dtype)


def norm_matmul(x, g, w, *, tm, tn, out_dtype):
    t, d = x.shape
    n = w.shape[1]
    need = (2 * _nbytes((tm, d), F32) + 2 * _nbytes((d, tn), w.dtype)
            + 2 * _nbytes((tm, tn), out_dtype) + _nbytes((tm, d), w.dtype))
    return pl.pallas_call(
        _norm_matmul_kernel,
        grid=(t // tm, n // tn),
        in_specs=[
            pl.BlockSpec((tm, d), lambda i, j: (i, 0)),
            pl.BlockSpec((1, d), lambda i, j: (0, 0)),
            pl.BlockSpec((d, tn), lambda i, j: (0, j)),
        ],
        out_specs=pl.BlockSpec((tm, tn), lambda i, j: (i, j)),
        out_shape=jax.ShapeDtypeStruct((t, n), out_dtype),
        scratch_shapes=[pltpu.VMEM((tm, d), w.dtype)],
        compiler_params=pltpu.CompilerParams(
            dimension_semantics=("parallel", "arbitrary"),
            vmem_limit_bytes=_vmem_limit(need)),
        name="norm_matmul",
    )(x, g.reshape(1, d), w)


def _matmul_residual_kernel(a_ref, w_ref, x_ref, o_ref):
    o_ref[...] = x_ref[...] + jnp.dot(a_ref[...], w_ref[...], preferred_element_type=F32)


def matmul_residual(a, w, x, *, tm, tn):
    t, k = a.shape
    n = w.shape[1]
    need = (2 * _nbytes((tm, k), a.dtype) + 2 * _nbytes((k, tn), w.dtype)
            + 4 * _nbytes((tm, tn), F32))
    return pl.pallas_call(
        _matmul_residual_kernel,
        grid=(t // tm, n // tn),
        in_specs=[
            pl.BlockSpec((tm, k), lambda i, j: (i, 0)),
            pl.BlockSpec((k, tn), lambda i, j: (0, j)),
            pl.BlockSpec((tm, tn), lambda i, j: (i, j)),
        ],
        out_specs=pl.BlockSpec((tm, tn), lambda i, j: (i, j)),
        out_shape=jax.ShapeDtypeStruct((t, n), F32),
        compiler_params=pltpu.CompilerParams(
            dimension_semantics=("parallel", "arbitrary"),
            vmem_limit_bytes=_vmem_limit(need)),
        name="matmul_residual",
    )(a, w, x)


def _mlp_kernel(x_ref, g_ref, wu_ref, wd_ref, o_ref, h_ref, acc_ref):
    f = pl.program_id(1)

    @pl.when(f == 0)
    def _():
        h_ref[...] = _rms_norm_rows(x_ref[...], g_ref[...]).astype(h_ref.dtype)
        acc_ref[...] = jnp.zeros_like(acc_ref)

    u = jnp.dot(h_ref[...], wu_ref[...], preferred_element_type=F32)
    u = jnp.square(jnp.maximum(u, 0.0)).astype(wd_ref.dtype)
    acc_ref[...] += jnp.dot(u, wd_ref[...], preferred_element_type=F32)

    @pl.when(f == pl.num_programs(1) - 1)
    def _():
        o_ref[...] = x_ref[...] + acc_ref[...]


def mlp_block(x, g, w_up, w_down, layer, *, tm, tf):
    t, d = x.shape
    ff = w_up.shape[2]
    need = (4 * _nbytes((tm, d), F32) + 2 * _nbytes((d, tf), w_up.dtype)
            + 2 * _nbytes((tf, d), w_down.dtype) + _nbytes((tm, d), w_up.dtype)
            + _nbytes((tm, d), F32) + 2 * _nbytes((tm, tf), F32))
    return pl.pallas_call(
        _mlp_kernel,
        grid=(t // tm, ff // tf),
        in_specs=[
            pl.BlockSpec((tm, d), lambda i, f: (i, 0)),
            pl.BlockSpec((1, d), lambda i, f: (0, 0)),
            pl.BlockSpec((None, d, tf), lambda i, f: (layer, 0, f)),
            pl.BlockSpec((None, tf, d), lambda i, f: (layer, f, 0)),
        ],
        out_specs=pl.BlockSpec((tm, d), lambda i, f: (i, 0)),
        out_shape=jax.ShapeDtypeStruct((t, d), F32),
        scratch_shapes=[pltpu.VMEM((tm, d), w_up.dtype), pltpu.VMEM((tm, d), F32)],
        compiler_params=pltpu.CompilerParams(
            dimension_semantics=("parallel", "arbitrary"),
            vmem_limit_bytes=_vmem_limit(need)),
        name="mlp_block",
    )(x, g.reshape(1, d), w_up, w_down)


DSA_TN = 2 * LANES
Q_COLS = N_HEADS * HEAD_DIM
IQ_COLS = IDX_HEADS * IDX_DIM
KV_COLS = N_KV_HEADS * HEAD_DIM
DSA_COLS = Q_COLS + IQ_COLS + 2 * KV_COLS + IDX_DIM + LANES
Q_BLOCKS = Q_COLS // DSA_TN
IQ_BLOCKS = IQ_COLS // DSA_TN
KV_BLOCKS = KV_COLS // DSA_TN


W_K_FIRST = Q_BLOCKS
W_V_FIRST = W_K_FIRST + KV_BLOCKS
W_IQ_FIRST = W_V_FIRST + KV_BLOCKS
W_LAST = W_IQ_FIRST + IQ_BLOCKS
OUT_K_FIRST = Q_BLOCKS + IQ_BLOCKS
OUT_V_FIRST = OUT_K_FIRST + KV_BLOCKS


def _dsa_out_block(jw):
    return jnp.where(jw < W_K_FIRST, jw,
                     jnp.where(jw < W_IQ_FIRST, jw + (OUT_K_FIRST - W_K_FIRST),
                               jnp.where(jw < W_LAST, jw - (W_IQ_FIRST - Q_BLOCKS), W_LAST)))


def _dsa_proj_kernel(x_ref, g_ref, w_ref, qn_ref, kn_ref, rc_ref, rs_ref,
                     ic_ref, ia_ref, ib_ref, o_ref, iw_ref, vt_ref, h_ref, y_ref):
    j = pl.program_id(1)
    jp = j - 1

    @pl.when(j == 0)
    def _():
        h_ref[...] = _rms_norm_rows(x_ref[...], g_ref[...]).astype(h_ref.dtype)

    def rope(z):
        return z * rc_ref[...] + pltpu.roll(z, HEAD_DIM // 2, 1) * rs_ref[...]

    def idx_rope(z):
        half = IDX_ROPE_DIM // 2
        return (z * ic_ref[...] + pltpu.roll(z, LANES - half, 1) * ia_ref[...]
                + pltpu.roll(z, half, 1) * ib_ref[...])

    def per_head(fn):
        def finish(y):
            o_ref[...] = jnp.concatenate(
                [fn(y[:, :LANES]), fn(y[:, LANES:])], axis=1).astype(o_ref.dtype)
        return finish

    def step(finish, multiply=True):
        y_prev = y_ref[...]
        if multiply:
            y_ref[...] = lax.dot_general(
                h_ref[...], w_ref[...].astype(h_ref.dtype), (((1,), (1,)), ((), ())),
                preferred_element_type=F32)
        if finish is not None:
            finish(y_prev)

    @pl.when(j == 0)
    def _():
        step(None)

    @pl.when((jp >= 0) & (jp < W_K_FIRST))
    def _():
        scale = HEAD_DIM ** -0.5 * math.log2(math.e)
        step(per_head(lambda z: rope(_rms_norm_rows(z, qn_ref[...])) * scale))

    @pl.when((jp >= W_K_FIRST) & (jp < W_V_FIRST))
    def _():
        step(per_head(lambda z: rope(_rms_norm_rows(z, kn_ref[...]))))

    @pl.when((jp >= W_V_FIRST) & (jp < W_IQ_FIRST))
    def _():
        def finish(y):
            o_ref[...] = y.astype(o_ref.dtype)
            vt_ref[...] = jnp.transpose(y).astype(vt_ref.dtype)
        step(finish)

    @pl.when((jp >= W_IQ_FIRST) & (jp < W_LAST))
    def _():
        step(per_head(idx_rope))

    @pl.when(jp == W_LAST)
    def _():
        def finish(y):
            o_ref[...] = jnp.concatenate(
                [idx_rope(y[:, :LANES]), jnp.zeros_like(y[:, LANES:])], axis=1).astype(o_ref.dtype)
            lane = lax.broadcasted_iota(jnp.int32, (y.shape[0], LANES), 1)
            iw_ref[...] = jnp.where(lane < IDX_HEADS,
                                    y[:, LANES:] * (IDX_HEADS ** -0.5 * IDX_DIM ** -0.5), 0.0)
        step(finish, multiply=False)


def dsa_projection(x, g, w_layers, layer, q_norm, k_norm, tables, *, tm):
    t, d = x.shape
    assert w_layers.shape[1] == Q_COLS + 2 * KV_COLS + IQ_COLS + IDX_DIM + IDX_HEADS
    rc, rs, ic, ia, ib = tables
    tab_spec = pl.BlockSpec((tm, LANES), lambda i, j: (i, 0))
    vec_spec = pl.BlockSpec((1, LANES), lambda i, j: (0, 0))
    need = (2 * _nbytes((tm, d), F32) + 3 * _nbytes((d, DSA_TN), F32)
            + 2 * _nbytes((tm, DSA_TN), MXU_DTYPE) + _nbytes((tm, d), MXU_DTYPE)
            + 12 * _nbytes((tm, LANES), F32) + 3 * _nbytes((tm, DSA_TN), F32))
    return pl.pallas_call(
        _dsa_proj_kernel,
        grid=(t // tm, W_LAST + 2),
        in_specs=[
            pl.BlockSpec((tm, d), lambda i, j: (i, 0)),
            pl.BlockSpec((1, d), lambda i, j: (0, 0)),
            pl.BlockSpec((None, DSA_TN, d), lambda i, j: (layer, jnp.minimum(j, W_LAST), 0)),
            vec_spec, vec_spec, tab_spec, tab_spec, tab_spec, tab_spec, tab_spec,
        ],
        out_specs=[
            pl.BlockSpec((tm, DSA_TN), lambda i, j: (i, _dsa_out_block(jnp.maximum(j - 1, 0)))),
            pl.BlockSpec((tm, LANES), lambda i, j: (i, 0)),
            pl.BlockSpec((DSA_TN, tm), lambda i, j: (jnp.clip(j - 1 - W_V_FIRST, 0, KV_BLOCKS - 1), i)),
        ],
        out_shape=[
            jax.ShapeDtypeStruct((t, DSA_COLS), MXU_DTYPE),
            jax.ShapeDtypeStruct((t, LANES), F32),
            jax.ShapeDtypeStruct((KV_COLS, t), MXU_DTYPE),
        ],
        scratch_shapes=[pltpu.VMEM((tm, d), MXU_DTYPE), pltpu.VMEM((tm, DSA_TN), F32)],
        compiler_params=pltpu.CompilerParams(
            dimension_semantics=("parallel", "arbitrary"),
            vmem_limit_bytes=_vmem_limit(need)),
        name="dsa_projection",
    )(x, g.reshape(1, d), w_layers, q_norm.reshape(1, LANES), k_norm.reshape(1, LANES), rc, rs, ic, ia, ib)


def _float_key(v):
    b = lax.bitcast_convert_type(v, jnp.int32)
    return b ^ ((b >> 31) & 0x7FFFFFFF)


def _key_float(k):
    return lax.bitcast_convert_type(k ^ ((k >> 31) & 0x7FFFFFFF), F32)


_KLO, _KHI, _CLO = range(3)
_SEARCH_CAP = 96
ONES_ROWS = 2 * SUBLANES
ATTN_LOOKAHEAD = 1


def _dsa_core_kernel(q_ref, iq_ref, iw_ref, k_ref, vt_ref, ik_ref, o_ref,
                     key_ref, bias_ref, s_ref, wt_ref, st_ref, m_ref, l_ref, acc_ref, *, tq, topk, kt, ka):
    i = pl.program_id(0)
    n_tiles = (i + kt) // kt
    group = N_HEADS // N_KV_HEADS
    nt = (((1,), (1,)), ((), ()))

    wt_ref[...] = jnp.transpose(iw_ref[...])

    krow = lax.broadcasted_iota(jnp.int32, (tq, tq), 0)
    qcol = lax.broadcasted_iota(jnp.int32, (tq, tq), 1)
    diag_causal = krow <= qcol

    def chunk_start(c):
        return pl.multiple_of(c * tq, tq)

    def index_tile(ti, carry):
        smin, smax = carry
        iks = [ik_ref[pl.ds(chunk_start(ti * kt + u), tq), :] for u in range(kt)]
        scores = [jnp.zeros((tq, tq), F32) for _ in range(kt)]
        for h in range(IDX_HEADS):
            iqh = iq_ref[:, h * IDX_DIM:(h + 1) * IDX_DIM]
            w = jnp.broadcast_to(wt_ref[h:h + 1, :], (tq, tq))
            for u in range(kt):
                logits = lax.dot_general(iks[u], iqh, nt, preferred_element_type=F32)
                scores[u] = scores[u] + w * jnp.maximum(logits, 0.0)
        for u in range(kt):
            c = ti * kt + u
            valid = (c < i) | ((c == i) & diag_causal)
            key_ref[c] = jnp.where(valid, _float_key(scores[u]), INT_MIN)
            smin = jnp.minimum(smin, jnp.min(jnp.where(valid, scores[u], jnp.inf), axis=0, keepdims=True))
            smax = jnp.maximum(smax, jnp.max(jnp.where(valid, scores[u], -jnp.inf), axis=0, keepdims=True))
        return smin, smax

    smin, smax = lax.fori_loop(
        0, n_tiles, index_tile,
        (jnp.full((1, tq), jnp.inf, F32), jnp.full((1, tq), -jnp.inf, F32)))

    def count_ge(cand):
        cand_b = jnp.broadcast_to(cand, (tq, tq))

        def body(ti, cnt):
            for u in range(kt):
                hit = jnp.where(key_ref[ti * kt + u] >= cand_b, 1.0, 0.0)
                cnt = cnt + jnp.sum(hit, axis=0, keepdims=True)
            return cnt

        return lax.fori_loop(0, n_tiles, body, jnp.zeros((1, tq), F32))

    st_ref[_KLO] = _float_key(smin)
    st_ref[_KHI] = _float_key(smax) + 1
    st_ref[_CLO] = lax.broadcasted_iota(jnp.int32, (1, tq), 1) + (i * tq + 1)

    def unfinished(klo, khi, clo):
        return (clo > topk) & ((khi - klo) != 1)

    def search_cond(st):
        step, pending = st
        return (step < _SEARCH_CAP) & (pending > 0)

    def any_lane(mask):
        return jnp.max(jnp.where(mask, 1.0, 0.0))

    def search_body(st):
        step, _ = st
        klo, khi, clo = st_ref[_KLO], st_ref[_KHI], st_ref[_CLO]
        live = unfinished(klo, khi, clo)
        pending = any_lane(live)
        vmid = 0.5 * _key_float(klo) + 0.5 * _key_float(khi)
        cmid = _float_key(vmid)
        kmid = klo + lax.shift_right_logical(khi - klo, 1)
        cand = jnp.where((cmid > klo) & (cmid < khi), cmid, kmid)
        cand = jnp.where(live, cand, klo)
        cnt = count_ge(cand).astype(jnp.int32)
        take = cnt >= topk
        klo = jnp.where(live & take, cand, klo)
        clo = jnp.where(live & take, cnt, clo)
        khi = jnp.where(live & jnp.logical_not(take), cand, khi)
        st_ref[_KLO], st_ref[_KHI], st_ref[_CLO] = klo, khi, clo
        return step + 1, pending

    lax.while_loop(search_cond, search_body, (jnp.int32(0), jnp.float32(1.0)))
    thr_b = jnp.broadcast_to(st_ref[_KLO], (tq, tq))

    m_ref[...] = jnp.full(m_ref.shape, NEG_BIG, F32)
    l_ref[...] = jnp.zeros(l_ref.shape, F32)
    acc_ref[...] = jnp.zeros(acc_ref.shape, F32)

    def attend_tile(ti, carry):
        for u in range(ka):
            bias_ref[u] = jnp.where(key_ref[ti * ka + u] >= thr_b, 0.0, NEG_BIG)
        starts = [chunk_start(ti * ka + u) for u in range(ka)]
        slots = ATTN_LOOKAHEAD + 1
        dyn0 = jnp.minimum(ti, 0)
        ones_rows = jnp.ones((ONES_ROWS, tq), vt_ref.dtype)

        def logits(h):
            cols = slice(h // group * HEAD_DIM, (h // group + 1) * HEAD_DIM)
            qh = q_ref[:, h * HEAD_DIM:(h + 1) * HEAD_DIM]
            top = None
            for u in range(ka):
                s = lax.dot_general(k_ref[pl.ds(starts[u], tq), cols], qh, nt,
                                    preferred_element_type=F32) + bias_ref[u]
                s_ref[h % slots + dyn0, u] = s
                top = s if top is None else jnp.maximum(top, s)
            return jnp.max(top, axis=0, keepdims=True)

        queued = [logits(h) for h in range(ATTN_LOOKAHEAD)]
        for h in range(N_HEADS):
            tile_max = queued.pop(0)
            if h + ATTN_LOOKAHEAD < N_HEADS:
                queued.append(logits(h + ATTN_LOOKAHEAD))
            cols = slice(h // group * HEAD_DIM, (h // group + 1) * HEAD_DIM)
            m_prev = m_ref[h]
            m_cur = jnp.maximum(m_prev, tile_max)
            alpha = jnp.exp2(m_prev - m_cur)
            pv = jnp.zeros((HEAD_DIM + ONES_ROWS, tq), F32)
            for u in range(ka):
                p = jnp.exp2(s_ref[h % slots + dyn0, u] - m_cur)
                v_aug = jnp.concatenate([vt_ref[cols, pl.ds(starts[u], tq)], ones_rows], axis=0)
                pv = pv + jnp.dot(v_aug, p.astype(vt_ref.dtype), preferred_element_type=F32)
            l_ref[h] = alpha * l_ref[h] + pv[HEAD_DIM:HEAD_DIM + 1, :]
            pv = pv[:HEAD_DIM, :]
            acc_ref[h] = alpha * acc_ref[h] + pv
            m_ref[h] = m_cur
        return carry

    lax.fori_loop(0, (i + ka) // ka, attend_tile, 0)

    for h in range(N_HEADS):
        o_ref[:, h * HEAD_DIM:(h + 1) * HEAD_DIM] = jnp.transpose(acc_ref[h] / l_ref[h]).astype(o_ref.dtype)


def dsa_core(proj, iw, vt, *, tq, topk, kt, ka):
    t = proj.shape[0]
    assert (t // tq) % kt == 0 and kt % ka == 0
    k_block = (Q_COLS + IQ_COLS) // KV_COLS
    ik_block = (Q_COLS + IQ_COLS + 2 * KV_COLS) // IDX_DIM
    resident = dict(pipeline_mode=pl.Buffered(1))
    need = (4 * _nbytes((tq, Q_COLS), proj.dtype) + 2 * _nbytes((tq, LANES), F32)
            + 2 * _nbytes((t, KV_COLS), proj.dtype) + _nbytes((t, IDX_DIM), proj.dtype)
            + 2 * _nbytes((tq, Q_COLS), proj.dtype)
            + _nbytes((t // tq, tq, tq), jnp.int32) + _nbytes((kt, tq, tq), F32)
            + _nbytes((LANES, tq), F32) + (3 + 2 * N_HEADS) * _nbytes((SUBLANES, tq), F32)
            + _nbytes((N_HEADS, HEAD_DIM, tq), F32)
            + (ATTN_LOOKAHEAD + 3) * _nbytes((ka, tq, tq), F32))
    return pl.pallas_call(
        functools.partial(_dsa_core_kernel, tq=tq, topk=topk, kt=kt, ka=ka),
        grid=(t // tq,),
        in_specs=[
            pl.BlockSpec((tq, Q_COLS), lambda i: (i, 0)),
            pl.BlockSpec((tq, IQ_COLS), lambda i: (i, 1)),
            pl.BlockSpec((tq, LANES), lambda i: (i, 0)),
            pl.BlockSpec((t, KV_COLS), lambda i: (0, k_block), **resident),
            pl.BlockSpec((KV_COLS, t), lambda i: (0, 0), **resident),
            pl.BlockSpec((t, IDX_DIM), lambda i: (0, ik_block), **resident),
        ],
        out_specs=pl.BlockSpec((tq, Q_COLS), lambda i: (i, 0)),
        out_shape=jax.ShapeDtypeStruct((t, Q_COLS), proj.dtype),
        scratch_shapes=[
            pltpu.VMEM((t // tq, tq, tq), jnp.int32),
            pltpu.VMEM((ka, tq, tq), F32),
            pltpu.VMEM((ATTN_LOOKAHEAD + 1, ka, tq, tq), F32),
            pltpu.VMEM((LANES, tq), F32),
            pltpu.VMEM((3, 1, tq), jnp.int32),
            pltpu.VMEM((N_HEADS, 1, tq), F32),
            pltpu.VMEM((N_HEADS, 1, tq), F32),
            pltpu.VMEM((N_HEADS, HEAD_DIM, tq), F32),
        ],
        compiler_params=pltpu.CompilerParams(
            dimension_semantics=("arbitrary",),
            vmem_limit_bytes=_vmem_limit(need)),
        name="dsa_core",
    )(proj, proj, iw, proj, vt, proj)


def _rope_tables(pos):
    def angles(dim):
        inv = 1.0 / (ROPE_THETA ** (jnp.arange(0, dim, 2, dtype=F32) / dim))
        return pos.astype(F32)[:, None] * inv

    ang = angles(HEAD_DIM)
    cos, sin = jnp.cos(ang), jnp.sin(ang)
    rc = jnp.concatenate([cos, cos], axis=1)
    rs = jnp.concatenate([-sin, sin], axis=1)
    iang = angles(IDX_ROPE_DIM)
    icos, isin = jnp.cos(iang), jnp.sin(iang)
    rest = LANES - IDX_ROPE_DIM
    zeros = jnp.zeros_like(isin)
    ic = jnp.concatenate([icos, icos, jnp.ones((pos.shape[0], rest), F32)], axis=1)
    ia = jnp.concatenate([-isin, zeros, jnp.zeros((pos.shape[0], rest), F32)], axis=1)
    ib = jnp.concatenate([zeros, isin, jnp.zeros((pos.shape[0], rest), F32)], axis=1)
    return rc, rs, ic, ia, ib


def dsa_layer(x, pos, norm_g, w_in_layers, layer, q_norm, k_norm, w_out, tiles):
    t = x.shape[0]
    proj, iw, vt = dsa_projection(x, norm_g, jnp.swapaxes(w_in_layers, 1, 2), layer, q_norm, k_norm,
                                  _rope_tables(pos), tm=tiles["proj_tm"])
    attn = dsa_core(proj, iw, vt, tq=tiles["tq"], topk=min(TOPK_MAX, t // 4), kt=tiles["kt"], ka=tiles["ka"])
    return matmul_residual(attn, w_out.astype(MXU_DTYPE), x, tm=tiles["res_tm"], tn=tiles["res_tn"])


def _rglru_kernel(gate_ref, xr_ref, cw_ref, cb_ref, wa_ref, ba_ref, wx_ref, bx_ref, lam_ref,
                  y_ref, hcar_ref, xprev_ref, *, tt):
    @pl.when(pl.program_id(1) == 0)
    def _():
        hcar_ref[...] = jnp.zeros_like(hcar_ref)
        xprev_ref[...] = jnp.zeros_like(xprev_ref)

    xr = xr_ref[...]
    ext = jnp.concatenate([xprev_ref[...], xr], axis=0)
    cw = cw_ref[...]
    xc = cb_ref[...] + xr * cw[CONV_WIDTH - 1:CONV_WIDTH, :]
    for d in range(1, CONV_WIDTH):
        xc = xc + pltpu.roll(ext, d, 0)[SUBLANES:, :] * cw[CONV_WIDTH - 1 - d:CONV_WIDTH - d, :]
    xprev_ref[...] = xr[tt - SUBLANES:, :]

    xcb = xc.astype(wa_ref.dtype)
    r = jax.nn.sigmoid(jnp.dot(xcb, wa_ref[...], preferred_element_type=F32) + ba_ref[...])
    ig = jax.nn.sigmoid(jnp.dot(xcb, wx_ref[...], preferred_element_type=F32) + bx_ref[...])
    nlam = -lam_ref[...]
    softplus = jnp.maximum(nlam, 0.0) + jnp.log(1.0 + jnp.exp(-jnp.abs(nlam)))
    log_a = -LRU_C * r * softplus
    a = jnp.exp(log_a)
    mult = jnp.sqrt(1.0 - jnp.exp(2.0 * log_a))
    b = xc * ig * mult

    n_groups = tt // SUBLANES
    a = a.reshape(n_groups, SUBLANES, a.shape[1])
    b = b.reshape(a.shape)
    in_group = lax.broadcasted_iota(jnp.int32, a.shape, 1)
    d = 1
    while d < SUBLANES:
        b = a * jnp.where(in_group >= d, pltpu.roll(b, d, 1), 0.0) + b
        a = a * jnp.where(in_group >= d, pltpu.roll(a, d, 1), 1.0)
        d *= 2
    carry = hcar_ref[...]
    groups = []
    for gi in range(n_groups):
        hg = b[gi] + a[gi] * carry
        carry = hg[SUBLANES - 1:SUBLANES, :]
        groups.append(hg)
    hcar_ref[...] = carry
    h = jnp.concatenate(groups, axis=0)
    y_ref[...] = (h * jax.nn.gelu(gate_ref[...])).astype(y_ref.dtype)


def rglru_scan(proj, conv_w, conv_b, wa, ba, wx, bx, lam, *, tt):
    t = proj.shape[0]
    width = conv_w.shape[1]
    nb, blk, _ = wa.shape
    vec = lambda: pl.BlockSpec((1, blk), lambda n, s: (0, n))
    need = (4 * _nbytes((tt, blk), F32) + 2 * _nbytes((tt, blk), MXU_DTYPE)
            + 4 * _nbytes((blk, blk), wa.dtype) + 24 * _nbytes((tt, blk), F32))
    return pl.pallas_call(
        functools.partial(_rglru_kernel, tt=tt),
        grid=(nb, t // tt),
        in_specs=[
            pl.BlockSpec((tt, blk), lambda n, s: (s, n)),
            pl.BlockSpec((tt, blk), lambda n, s: (s, nb + n)),
            pl.BlockSpec((CONV_WIDTH, blk), lambda n, s: (0, n)),
            vec(),
            pl.BlockSpec((None, blk, blk), lambda n, s: (n, 0, 0)),
            vec(),
            pl.BlockSpec((None, blk, blk), lambda n, s: (n, 0, 0)),
            vec(),
            vec(),
        ],
        out_specs=pl.BlockSpec((tt, blk), lambda n, s: (s, n)),
        out_shape=jax.ShapeDtypeStruct((t, width), MXU_DTYPE),
        scratch_shapes=[pltpu.VMEM((1, blk), F32), pltpu.VMEM((SUBLANES, blk), F32)],
        compiler_params=pltpu.CompilerParams(
            dimension_semantics=("parallel", "arbitrary"),
            vmem_limit_bytes=_vmem_limit(need)),
        name="rglru_scan",
    )(proj, proj, conv_w, conv_b.reshape(1, width), wa, ba.reshape(1, width),
      wx, bx.reshape(1, width), lam.reshape(1, width))


def rglru_layer(x, norm_g, w_in, conv_w, conv_b, wa, ba, wx, bx, lam, w_out, tiles):
    proj = norm_matmul(x, norm_g, w_in.astype(MXU_DTYPE), tm=tiles["proj_tm"], tn=tiles["proj_tn"],
                       out_dtype=F32)
    y = rglru_scan(proj, conv_w, conv_b, wa.astype(MXU_DTYPE), ba, wx.astype(MXU_DTYPE), bx, lam,
                   tt=tiles["scan_tt"])
    return matmul_residual(y, w_out.astype(MXU_DTYPE), x, tm=tiles["res_tm"], tn=tiles["res_tn"])


POOL_HALO = max(POOL_WINDOWS)


def _pool_kernel(x_ref, g_ref, w_ref, b_ref, s_ref, o_ref, halo_ref, *, tt):
    blk = pl.program_id(0)

    @pl.when(blk == 0)
    def _():
        halo_ref[...] = jnp.zeros_like(halo_ref)

    x = x_ref[...]
    h = _rms_norm_rows(x, g_ref[...])
    ext = jnp.concatenate([halo_ref[...], h], axis=0)
    halo_ref[...] = h[tt - POOL_HALO:, :]

    gw = h.shape[1] // len(POOL_WINDOWS)
    t1 = (lax.broadcasted_iota(jnp.int32, (tt, gw), 0) + (blk * tt + 1)).astype(F32)
    for gi, win in enumerate(POOL_WINDOWS):
        cols = slice(gi * gw, (gi + 1) * gw)
        acc = ext[:, cols]
        d = 1
        while d < win:
            acc = acc + pltpu.roll(acc, d, 0)
            d *= 2
        mean = acc[POOL_HALO:, :] / jnp.minimum(t1, float(win))
        y = (mean - h[:, cols]).astype(w_ref.dtype)
        z = jnp.dot(y, w_ref[gi], preferred_element_type=F32) + b_ref[gi:gi + 1, :]
        o_ref[:, cols] = x[:, cols] + z * s_ref[:, cols]


def pool_layer(x, norm_g, w_group, b_group, scale, tiles):
    t, d = x.shape
    tt = tiles["pool_tt"]
    ng, gw, _ = w_group.shape
    need = (4 * _nbytes((tt, d), F32) + 2 * _nbytes((ng, gw, gw), MXU_DTYPE)
            + 6 * _nbytes((tt, d), F32))
    return pl.pallas_call(
        functools.partial(_pool_kernel, tt=tt),
        grid=(t // tt,),
        in_specs=[
            pl.BlockSpec((tt, d), lambda i: (i, 0)),
            pl.BlockSpec((1, d), lambda i: (0, 0)),
            pl.BlockSpec((ng, gw, gw), lambda i: (0, 0, 0)),
            pl.BlockSpec((ng, gw), lambda i: (0, 0)),
            pl.BlockSpec((1, d), lambda i: (0, 0)),
        ],
        out_specs=pl.BlockSpec((tt, d), lambda i: (i, 0)),
        out_shape=jax.ShapeDtypeStruct((t, d), F32),
        scratch_shapes=[pltpu.VMEM((POOL_HALO, d), F32)],
        compiler_params=pltpu.CompilerParams(
            dimension_semantics=("arbitrary",),
            vmem_limit_bytes=_vmem_limit(need)),
        name="pool_mixer",
    )(x, norm_g.reshape(1, d), w_group.astype(MXU_DTYPE), b_group, scale.reshape(1, d))


def _tiles(t):
    big = min(t, 512)
    tall = min(t, 1024)
    return {
        "proj_tm": tall, "proj_tn": 512,
        "res_tm": tall, "res_tn": 512,
        "tq": min(t, 256), "kt": min(4, t // min(t, 256)), "ka": min(4, t // min(t, 256)),
        "scan_tt": min(t, 256),
        "pool_tt": min(t, 256),
        "mlp_tm": big, "mlp_tf": 512,
    }


def kernel(x, positions, attn_norm, attn_w_in, attn_q_norm, attn_k_norm, attn_w_out, rnn_norm, rnn_w_in, rnn_conv_w, rnn_conv_b, rnn_gate_a_w, rnn_gate_a_b, rnn_gate_x_w, rnn_gate_x_b, rnn_lambda, rnn_w_out, pool_norm, pool_w, pool_b, pool_scale, mlp_norm, mlp_w_up, mlp_w_down):
    batch, t, d = x.shape
    depth = mlp_norm.shape[0]
    tiles = _tiles(t)
    w_up_all = mlp_w_up.astype(MXU_DTYPE)
    w_down_all = mlp_w_down.astype(MXU_DTYPE)
    outs = []
    for bi in range(batch):
        xb = x[bi]
        pos = positions[bi]
        for i in range(depth):
            kind, j = i % N_MIXERS, i // N_MIXERS
            if kind == 0:
                xb = dsa_layer(xb, pos, attn_norm[j], attn_w_in, j, attn_q_norm[j], attn_k_norm[j],
                               attn_w_out[j], tiles)
            elif kind == 1:
                xb = rglru_layer(xb, rnn_norm[j], rnn_w_in[j], rnn_conv_w[j], rnn_conv_b[j],
                                 rnn_gate_a_w[j], rnn_gate_a_b[j], rnn_gate_x_w[j], rnn_gate_x_b[j],
                                 rnn_lambda[j], rnn_w_out[j], tiles)
            else:
                xb = pool_layer(xb, pool_norm[j], pool_w[j], pool_b[j], pool_scale[j], tiles)
            xb = mlp_block(xb, mlp_norm[i], w_up_all, w_down_all, i,
                           tm=tiles["mlp_tm"], tf=tiles["mlp_tf"])
        outs.append(xb)
    return outs[0][None] if batch == 1 else jnp.stack(outs, axis=0)
```

```python
import functools
import math

import jax
import jax.numpy as jnp
from jax import lax
from jax.experimental import pallas as pl
from jax.experimental.pallas import tpu as pltpu

F32 = jnp.float32
MXU_DTYPE = jnp.bfloat16

N_MIXERS = 3
EPS = 1e-6
ROPE_THETA = 10000.0
HEAD_DIM = 128
N_HEADS = 16
N_KV_HEADS = 4
IDX_HEADS = 16
IDX_DIM = 128
IDX_ROPE_DIM = 64
TOPK_MAX = 256
CONV_WIDTH = 4
LRU_C = 8.0
POOL_WINDOWS = (2, 4, 8, 16)

LANES = 128
SUBLANES = 8
VMEM_BYTES_V7X = 64 * 1024 * 1024
VMEM_CAP_BYTES = VMEM_BYTES_V7X - 8 * 1024 * 1024

INT_MIN = -(2 ** 31)
NEG_BIG = -1e30


def _vmem_limit(block_bytes):
    return int(min(VMEM_CAP_BYTES, block_bytes * 3 // 2 + (4 << 20)))


def _nbytes(shape, dtype):
    return math.prod(shape) * jnp.dtype(dtype).itemsize


def _rms_norm_rows(x, g):
    ms = jnp.mean(x * x, axis=-1, keepdims=True)
    return x * lax.rsqrt(ms + EPS) * g


def _tile_lanes(x, width):
    reps = width // LANES
    return x if reps == 1 else jnp.concatenate([x] * reps, axis=1)


def _norm_matmul_kernel(x_ref, g_ref, w_ref, o_ref, h_ref):
    @pl.when(pl.program_id(1) == 0)
    def _():
        h_ref[...] = _rms_norm_rows(x_ref[...], g_ref[...]).astype(h_ref.dtype)

    o_ref[...] = jnp.dot(h_ref[...], w_ref[...], preferred_element_type=F32).astype(o_ref.dtype)


def norm_matmul(x, g, w, *, tm, tn, out_dtype):
    t, d = x.shape
    n = w.shape[1]
    need = (2 * _nbytes((tm, d), F32) + 2 * _nbytes((d, tn), w.dtype)
            + 2 * _nbytes((tm, tn), out_dtype) + _nbytes((tm, d), w.dtype))
    return pl.pallas_call(
        _norm_matmul_kernel,
        grid=(t // tm, n // tn),
        in_specs=[
            pl.BlockSpec((tm, d), lambda i, j: (i, 0)),
            pl.BlockSpec((1, d), lambda i, j: (0, 0)),
            pl.BlockSpec((d, tn), lambda i, j: (0, j)),
        ],
        out_specs=pl.BlockSpec((tm, tn), lambda i, j: (i, j)),
        out_shape=jax.ShapeDtypeStruct((t, n), out_dtype),
        scratch_shapes=[pltpu.VMEM((tm, d), w.dtype)],
        compiler_params=pltpu.CompilerParams(
            dimension_semantics=("parallel", "arbitrary"),
            vmem_limit_bytes=_vmem_limit(need)),
        name="norm_matmul",
    )(x, g.reshape(1, d), w)


def _matmul_residual_kernel(a_ref, w_ref, x_ref, o_ref):
    o_ref[...] = x_ref[...] + jnp.dot(a_ref[...], w_ref[...], preferred_element_type=F32)


def matmul_residual(a, w, x, *, tm, tn):
    t, k = a.shape
    n = w.shape[1]
    need = (2 * _nbytes((tm, k), a.dtype) + 2 * _nbytes((k, tn), w.dtype)
            + 4 * _nbytes((tm, tn), F32))
    return pl.pallas_call(
        _matmul_residual_kernel,
        grid=(t // tm, n // tn),
        in_specs=[
            pl.BlockSpec((tm, k), lambda i, j: (i, 0)),
            pl.BlockSpec((k, tn), lambda i, j: (0, j)),
            pl.BlockSpec((tm, tn), lambda i, j: (i, j)),
        ],
        out_specs=pl.BlockSpec((tm, tn), lambda i, j: (i, j)),
        out_shape=jax.ShapeDtypeStruct((t, n), F32),
        compiler_params=pltpu.CompilerParams(
            dimension_semantics=("parallel", "arbitrary"),
            vmem_limit_bytes=_vmem_limit(need)),
        name="matmul_residual",
    )(a, w, x)


def _mlp_kernel(x_ref, g_ref, wu_ref, wd_ref, o_ref, h_ref, acc_ref):
    f = pl.program_id(1)

    @pl.when(f == 0)
    def _():
        h_ref[...] = _rms_norm_rows(x_ref[...], g_ref[...]).astype(h_ref.dtype)
        acc_ref[...] = jnp.zeros_like(acc_ref)

    u = jnp.dot(h_ref[...], wu_ref[...], preferred_element_type=F32)
    u = jnp.square(jnp.maximum(u, 0.0)).astype(wd_ref.dtype)
    acc_ref[...] += jnp.dot(u, wd_ref[...], preferred_element_type=F32)

    @pl.when(f == pl.num_programs(1) - 1)
    def _():
        o_ref[...] = x_ref[...] + acc_ref[...]


def mlp_block(x, g, w_up, w_down, layer, *, tm, tf):
    t, d = x.shape
    ff = w_up.shape[2]
    need = (4 * _nbytes((tm, d), F32) + 2 * _nbytes((d, tf), w_up.dtype)
            + 2 * _nbytes((tf, d), w_down.dtype) + _nbytes((tm, d), w_up.dtype)
            + _nbytes((tm, d), F32) + 2 * _nbytes((tm, tf), F32))
    return pl.pallas_call(
        _mlp_kernel,
        grid=(t // tm, ff // tf),
        in_specs=[
            pl.BlockSpec((tm, d), lambda i, f: (i, 0)),
            pl.BlockSpec((1, d), lambda i, f: (0, 0)),
            pl.BlockSpec((None, d, tf), lambda i, f: (layer, 0, f)),
            pl.BlockSpec((None, tf, d), lambda i, f: (layer, f, 0)),
        ],
        out_specs=pl.BlockSpec((tm, d), lambda i, f: (i, 0)),
        out_shape=jax.ShapeDtypeStruct((t, d), F32),
        scratch_shapes=[pltpu.VMEM((tm, d), w_up.dtype), pltpu.VMEM((tm, d), F32)],
        compiler_params=pltpu.CompilerParams(
            dimension_semantics=("parallel", "arbitrary"),
            vmem_limit_bytes=_vmem_limit(need)),
        name="mlp_block",
    )(x, g.reshape(1, d), w_up, w_down)


DSA_TN = 2 * LANES
Q_COLS = N_HEADS * HEAD_DIM
IQ_COLS = IDX_HEADS * IDX_DIM
KV_COLS = N_KV_HEADS * HEAD_DIM
DSA_COLS = Q_COLS + IQ_COLS + 2 * KV_COLS + IDX_DIM + LANES
Q_BLOCKS = Q_COLS // DSA_TN
IQ_BLOCKS = IQ_COLS // DSA_TN
KV_BLOCKS = KV_COLS // DSA_TN


W_K_FIRST = Q_BLOCKS
W_V_FIRST = W_K_FIRST + KV_BLOCKS
W_IQ_FIRST = W_V_FIRST + KV_BLOCKS
W_LAST = W_IQ_FIRST + IQ_BLOCKS
OUT_K_FIRST = Q_BLOCKS + IQ_BLOCKS
OUT_V_FIRST = OUT_K_FIRST + KV_BLOCKS


def _dsa_out_block(jw):
    return jnp.where(jw < W_K_FIRST, jw,
                     jnp.where(jw < W_IQ_FIRST, jw + (OUT_K_FIRST - W_K_FIRST),
                               jnp.where(jw < W_LAST, jw - (W_IQ_FIRST - Q_BLOCKS), W_LAST)))


def _dsa_proj_kernel(x_ref, g_ref, w_ref, qn_ref, kn_ref, rc_ref, rs_ref,
                     ic_ref, ia_ref, ib_ref, o_ref, iw_ref, vt_ref, h_ref, y_ref):
    j = pl.program_id(1)
    jp = j - 1

    @pl.when(j == 0)
    def _():
        h_ref[...] = _rms_norm_rows(x_ref[...], g_ref[...]).astype(h_ref.dtype)

    def rope(z):
        return z * rc_ref[...] + pltpu.roll(z, HEAD_DIM // 2, 1) * rs_ref[...]

    def idx_rope(z):
        half = IDX_ROPE_DIM // 2
        return (z * ic_ref[...] + pltpu.roll(z, LANES - half, 1) * ia_ref[...]
                + pltpu.roll(z, half, 1) * ib_ref[...])

    def per_head(fn):
        def finish(y):
            o_ref[...] = jnp.concatenate(
                [fn(y[:, :LANES]), fn(y[:, LANES:])], axis=1).astype(o_ref.dtype)
        return finish

    def step(finish, multiply=True):
        y_prev = y_ref[...]
        if multiply:
            y_ref[...] = lax.dot_general(
                h_ref[...], w_ref[...].astype(h_ref.dtype), (((1,), (1,)), ((), ())),
                preferred_element_type=F32)
        if finish is not None:
            finish(y_prev)

    @pl.when(j == 0)
    def _():
        step(None)

    @pl.when((jp >= 0) & (jp < W_K_FIRST))
    def _():
        scale = HEAD_DIM ** -0.5 * math.log2(math.e)
        step(per_head(lambda z: rope(_rms_norm_rows(z, qn_ref[...])) * scale))

    @pl.when((jp >= W_K_FIRST) & (jp < W_V_FIRST))
    def _():
        step(per_head(lambda z: rope(_rms_norm_rows(z, kn_ref[...]))))

    @pl.when((jp >= W_V_FIRST) & (jp < W_IQ_FIRST))
    def _():
        def finish(y):
            o_ref[...] = y.astype(o_ref.dtype)
            vt_ref[...] = jnp.transpose(y).astype(vt_ref.dtype)
        step(finish)

    @pl.when((jp >= W_IQ_FIRST) & (jp < W_LAST))
    def _():
        step(per_head(idx_rope))

    @pl.when(jp == W_LAST)
    def _():
        def finish(y):
            o_ref[...] = jnp.concatenate(
                [idx_rope(y[:, :LANES]), jnp.zeros_like(y[:, LANES:])], axis=1).astype(o_ref.dtype)
            lane = lax.broadcasted_iota(jnp.int32, (y.shape[0], LANES), 1)
            iw_ref[...] = jnp.where(lane < IDX_HEADS,
                                    y[:, LANES:] * (IDX_HEADS ** -0.5 * IDX_DIM ** -0.5), 0.0)
        step(finish, multiply=False)


def dsa_projection(x, g, w_layers, layer, q_norm, k_norm, tables, *, tm):
    t, d = x.shape
    assert w_layers.shape[1] == Q_COLS + 2 * KV_COLS + IQ_COLS + IDX_DIM + IDX_HEADS
    rc, rs, ic, ia, ib = tables
    tab_spec = pl.BlockSpec((tm, LANES), lambda i, j: (i, 0))
    vec_spec = pl.BlockSpec((1, LANES), lambda i, j: (0, 0))
    need = (2 * _nbytes((tm, d), F32) + 3 * _nbytes((d, DSA_TN), F32)
            + 2 * _nbytes((tm, DSA_TN), MXU_DTYPE) + _nbytes((tm, d), MXU_DTYPE)
            + 12 * _nbytes((tm, LANES), F32) + 3 * _nbytes((tm, DSA_TN), F32))
    return pl.pallas_call(
        _dsa_proj_kernel,
        grid=(t // tm, W_LAST + 2),
        in_specs=[
            pl.BlockSpec((tm, d), lambda i, j: (i, 0)),
            pl.BlockSpec((1, d), lambda i, j: (0, 0)),
            pl.BlockSpec((None, DSA_TN, d), lambda i, j: (layer, jnp.minimum(j, W_LAST), 0)),
            vec_spec, vec_spec, tab_spec, tab_spec, tab_spec, tab_spec, tab_spec,
        ],
        out_specs=[
            pl.BlockSpec((tm, DSA_TN), lambda i, j: (i, _dsa_out_block(jnp.maximum(j - 1, 0)))),
            pl.BlockSpec((tm, LANES), lambda i, j: (i, 0)),
            pl.BlockSpec((DSA_TN, tm), lambda i, j: (jnp.clip(j - 1 - W_V_FIRST, 0, KV_BLOCKS - 1), i)),
        ],
        out_shape=[
            jax.ShapeDtypeStruct((t, DSA_COLS), MXU_DTYPE),
            jax.ShapeDtypeStruct((t, LANES), F32),
            jax.ShapeDtypeStruct((KV_COLS, t), MXU_DTYPE),
        ],
        scratch_shapes=[pltpu.VMEM((tm, d), MXU_DTYPE), pltpu.VMEM((tm, DSA_TN), F32)],
        compiler_params=pltpu.CompilerParams(
            dimension_semantics=("parallel", "arbitrary"),
            vmem_limit_bytes=_vmem_limit(need)),
        name="dsa_projection",
    )(x, g.reshape(1, d), w_layers, q_norm.reshape(1, LANES), k_norm.reshape(1, LANES), rc, rs, ic, ia, ib)


def _float_key(v):
    b = lax.bitcast_convert_type(v, jnp.int32)
    return b ^ ((b >> 31) & 0x7FFFFFFF)


def _key_float(k):
    return lax.bitcast_convert_type(k ^ ((k >> 31) & 0x7FFFFFFF), F32)


_KLO, _KHI, _CLO, _CHI, _JCUT = range(5)
_SEARCH_CAP = 96
ONES_ROWS = 2 * SUBLANES
ATTN_LOOKAHEAD = 1


def _dsa_core_kernel(q_ref, iq_ref, iw_ref, k_ref, vt_ref, ik_ref, o_ref,
                     key_ref, bias_ref, s_ref, wt_ref, st_ref, m_ref, l_ref, acc_ref, *, tq, topk, kt, ka):
    i = pl.program_id(0)
    n_tiles = (i + kt) // kt
    group = N_HEADS // N_KV_HEADS
    nt = (((1,), (1,)), ((), ()))

    wt_ref[...] = jnp.transpose(iw_ref[...])

    krow = lax.broadcasted_iota(jnp.int32, (tq, tq), 0)
    qcol = lax.broadcasted_iota(jnp.int32, (tq, tq), 1)
    diag_causal = krow <= qcol

    def chunk_start(c):
        return pl.multiple_of(c * tq, tq)

    def index_tile(ti, carry):
        smin, smax = carry
        iks = [ik_ref[pl.ds(chunk_start(ti * kt + u), tq), :] for u in range(kt)]
        scores = [jnp.zeros((tq, tq), F32) for _ in range(kt)]
        for h in range(IDX_HEADS):
            iqh = iq_ref[:, h * IDX_DIM:(h + 1) * IDX_DIM]
            w = jnp.broadcast_to(wt_ref[h:h + 1, :], (tq, tq))
            for u in range(kt):
                logits = lax.dot_general(iks[u], iqh, nt, preferred_element_type=F32)
                scores[u] = scores[u] + w * jnp.maximum(logits, 0.0)
        for u in range(kt):
            c = ti * kt + u
            valid = (c < i) | ((c == i) & diag_causal)
            key_ref[c] = jnp.where(valid, _float_key(scores[u]), INT_MIN)
            smin = jnp.minimum(smin, jnp.min(jnp.where(valid, scores[u], jnp.inf), axis=0, keepdims=True))
            smax = jnp.maximum(smax, jnp.max(jnp.where(valid, scores[u], -jnp.inf), axis=0, keepdims=True))
        return smin, smax

    smin, smax = lax.fori_loop(
        0, n_tiles, index_tile,
        (jnp.full((1, tq), jnp.inf, F32), jnp.full((1, tq), -jnp.inf, F32)))

    def count_ge(cand):
        cand_b = jnp.broadcast_to(cand, (tq, tq))

        def body(ti, cnt):
            for u in range(kt):
                hit = jnp.where(key_ref[ti * kt + u] >= cand_b, 1.0, 0.0)
                cnt = cnt + jnp.sum(hit, axis=0, keepdims=True)
            return cnt

        return lax.fori_loop(0, n_tiles, body, jnp.zeros((1, tq), F32))

    st_ref[_KLO] = _float_key(smin)
    st_ref[_KHI] = _float_key(smax) + 1
    st_ref[_CLO] = lax.broadcasted_iota(jnp.int32, (1, tq), 1) + (i * tq + 1)
    st_ref[_CHI] = jnp.zeros((1, tq), jnp.int32)

    def unfinished(klo, khi, clo):
        return (clo > topk) & ((khi - klo) != 1)

    def search_cond(st):
        step, pending = st
        return (step < _SEARCH_CAP) & (pending > 0)

    def any_lane(mask):
        return jnp.max(jnp.where(mask, 1.0, 0.0))

    def search_body(st):
        step, _ = st
        klo, khi, clo = st_ref[_KLO], st_ref[_KHI], st_ref[_CLO]
        live = unfinished(klo, khi, clo)
        pending = any_lane(live)
        vmid = 0.5 * _key_float(klo) + 0.5 * _key_float(khi)
        cmid = _float_key(vmid)
        kmid = klo + lax.shift_right_logical(khi - klo, 1)
        cand = jnp.where((cmid > klo) & (cmid < khi), cmid, kmid)
        cand = jnp.where(live, cand, klo)
        cnt = count_ge(cand).astype(jnp.int32)
        take = cnt >= topk
        klo = jnp.where(live & take, cand, klo)
        clo = jnp.where(live & take, cnt, clo)
        drop = live & jnp.logical_not(take)
        st_ref[_KLO], st_ref[_KHI], st_ref[_CLO] = klo, jnp.where(drop, cand, khi), clo
        st_ref[_CHI] = jnp.where(drop, cnt, st_ref[_CHI])
        return step + 1, pending

    lax.while_loop(search_cond, search_body, (jnp.int32(0), jnp.float32(1.0)))
    thr_b = jnp.broadcast_to(st_ref[_KLO], (tq, tq))

    n_keys = key_ref.shape[0] * tq
    tied = st_ref[_CLO] > topk
    has_ties = any_lane(tied) > 0
    st_ref[_JCUT] = jnp.full((1, tq), n_keys, jnp.int32)

    @pl.when(has_ties)
    def _():
        want = topk - st_ref[_CHI]

        def count_tied_upto(pos):
            pos_b = jnp.broadcast_to(pos, (tq, tq))

            def body(ti, cnt):
                for u in range(kt):
                    c = ti * kt + u
                    hit = (key_ref[c] == thr_b) & (krow + c * tq <= pos_b)
                    cnt = cnt + jnp.sum(jnp.where(hit, 1.0, 0.0), axis=0, keepdims=True)
                return cnt

            return lax.fori_loop(0, n_tiles, body, jnp.zeros((1, tq), F32)).astype(jnp.int32)

        def cut_step(_, st):
            lo, hi = st
            mid = lo + ((hi - lo) >> 1)
            enough = count_tied_upto(mid) >= want
            return jnp.where(enough, lo, mid), jnp.where(enough, mid, hi)

        _, cut = lax.fori_loop(0, n_keys.bit_length(), cut_step,
                               (jnp.full((1, tq), -1, jnp.int32), jnp.full((1, tq), n_keys - 1, jnp.int32)))
        st_ref[_JCUT] = jnp.where(tied, cut, n_keys)

    cut_b = jnp.broadcast_to(st_ref[_JCUT], (tq, tq))

    m_ref[...] = jnp.full(m_ref.shape, NEG_BIG, F32)
    l_ref[...] = jnp.zeros(l_ref.shape, F32)
    acc_ref[...] = jnp.zeros(acc_ref.shape, F32)

    def attend_tile(ti, carry):
        @pl.when(jnp.logical_not(has_ties))
        def _():
            for u in range(ka):
                bias_ref[u] = jnp.where(key_ref[ti * ka + u] >= thr_b, 0.0, NEG_BIG)

        @pl.when(has_ties)
        def _():
            for u in range(ka):
                c = ti * ka + u
                key = key_ref[c]
                keep = (key > thr_b) | ((key == thr_b) & (krow + c * tq <= cut_b))
                bias_ref[u] = jnp.where(keep, 0.0, NEG_BIG)

        starts = [chunk_start(ti * ka + u) for u in range(ka)]
        slots = ATTN_LOOKAHEAD + 1
        dyn0 = jnp.minimum(ti, 0)
        ones_rows = jnp.ones((ONES_ROWS, tq), vt_ref.dtype)

        def logits(h):
            cols = slice(h // group * HEAD_DIM, (h // group + 1) * HEAD_DIM)
            qh = q_ref[:, h * HEAD_DIM:(h + 1) * HEAD_DIM]
            top = None
            for u in range(ka):
                s = lax.dot_general(k_ref[pl.ds(starts[u], tq), cols], qh, nt,
                                    preferred_element_type=F32) + bias_ref[u]
                s_ref[h % slots + dyn0, u] = s
                top = s if top is None else jnp.maximum(top, s)
            return jnp.max(top, axis=0, keepdims=True)

        queued = [logits(h) for h in range(ATTN_LOOKAHEAD)]
        for h in range(N_HEADS):
            tile_max = queued.pop(0)
            if h + ATTN_LOOKAHEAD < N_HEADS:
                queued.append(logits(h + ATTN_LOOKAHEAD))
            cols = slice(h // group * HEAD_DIM, (h // group + 1) * HEAD_DIM)
            m_prev = m_ref[h]
            m_cur = jnp.maximum(m_prev, tile_max)
            alpha = jnp.exp2(m_prev - m_cur)
            pv = jnp.zeros((HEAD_DIM + ONES_ROWS, tq), F32)
            for u in range(ka):
                p = jnp.exp2(s_ref[h % slots + dyn0, u] - m_cur)
                v_aug = jnp.concatenate([vt_ref[cols, pl.ds(starts[u], tq)], ones_rows], axis=0)
                pv = pv + jnp.dot(v_aug, p.astype(vt_ref.dtype), preferred_element_type=F32)
            l_ref[h] = alpha * l_ref[h] + pv[HEAD_DIM:HEAD_DIM + 1, :]
            pv = pv[:HEAD_DIM, :]
            acc_ref[h] = alpha * acc_ref[h] + pv
            m_ref[h] = m_cur
        return carry

    lax.fori_loop(0, (i + ka) // ka, attend_tile, 0)

    for h in range(N_HEADS):
        o_ref[:, h * HEAD_DIM:(h + 1) * HEAD_DIM] = jnp.transpose(acc_ref[h] / l_ref[h]).astype(o_ref.dtype)


def dsa_core(proj, iw, vt, *, tq, topk, kt, ka):
    t = proj.shape[0]
    assert (t // tq) % kt == 0 and kt % ka == 0
    k_block = (Q_COLS + IQ_COLS) // KV_COLS
    ik_block = (Q_COLS + IQ_COLS + 2 * KV_COLS) // IDX_DIM
    resident = dict(pipeline_mode=pl.Buffered(1))
    need = (4 * _nbytes((tq, Q_COLS), proj.dtype) + 2 * _nbytes((tq, LANES), F32)
            + 2 * _nbytes((t, KV_COLS), proj.dtype) + _nbytes((t, IDX_DIM), proj.dtype)
            + 2 * _nbytes((tq, Q_COLS), proj.dtype)
            + _nbytes((t // tq, tq, tq), jnp.int32) + _nbytes((kt, tq, tq), F32)
            + _nbytes((LANES, tq), F32) + (3 + 2 * N_HEADS) * _nbytes((SUBLANES, tq), F32)
            + _nbytes((N_HEADS, HEAD_DIM, tq), F32)
            + (ATTN_LOOKAHEAD + 3) * _nbytes((ka, tq, tq), F32))
    return pl.pallas_call(
        functools.partial(_dsa_core_kernel, tq=tq, topk=topk, kt=kt, ka=ka),
        grid=(t // tq,),
        in_specs=[
            pl.BlockSpec((tq, Q_COLS), lambda i: (i, 0)),
            pl.BlockSpec((tq, IQ_COLS), lambda i: (i, 1)),
            pl.BlockSpec((tq, LANES), lambda i: (i, 0)),
            pl.BlockSpec((t, KV_COLS), lambda i: (0, k_block), **resident),
            pl.BlockSpec((KV_COLS, t), lambda i: (0, 0), **resident),
            pl.BlockSpec((t, IDX_DIM), lambda i: (0, ik_block), **resident),
        ],
        out_specs=pl.BlockSpec((tq, Q_COLS), lambda i: (i, 0)),
        out_shape=jax.ShapeDtypeStruct((t, Q_COLS), proj.dtype),
        scratch_shapes=[
            pltpu.VMEM((t // tq, tq, tq), jnp.int32),
            pltpu.VMEM((ka, tq, tq), F32),
            pltpu.VMEM((ATTN_LOOKAHEAD + 1, ka, tq, tq), F32),
            pltpu.VMEM((LANES, tq), F32),
            pltpu.VMEM((5, 1, tq), jnp.int32),
            pltpu.VMEM((N_HEADS, 1, tq), F32),
            pltpu.VMEM((N_HEADS, 1, tq), F32),
            pltpu.VMEM((N_HEADS, HEAD_DIM, tq), F32),
        ],
        compiler_params=pltpu.CompilerParams(
            dimension_semantics=("arbitrary",),
            vmem_limit_bytes=_vmem_limit(need)),
        name="dsa_core",
    )(proj, proj, iw, proj, vt, proj)


def _rope_tables(pos):
    def angles(dim):
        inv = 1.0 / (ROPE_THETA ** (jnp.arange(0, dim, 2, dtype=F32) / dim))
        return pos.astype(F32)[:, None] * inv

    ang = angles(HEAD_DIM)
    cos, sin = jnp.cos(ang), jnp.sin(ang)
    rc = jnp.concatenate([cos, cos], axis=1)
    rs = jnp.concatenate([-sin, sin], axis=1)
    iang = angles(IDX_ROPE_DIM)
    icos, isin = jnp.cos(iang), jnp.sin(iang)
    rest = LANES - IDX_ROPE_DIM
    zeros = jnp.zeros_like(isin)
    ic = jnp.concatenate([icos, icos, jnp.ones((pos.shape[0], rest), F32)], axis=1)
    ia = jnp.concatenate([-isin, zeros, jnp.zeros((pos.shape[0], rest), F32)], axis=1)
    ib = jnp.concatenate([zeros, isin, jnp.zeros((pos.shape[0], rest), F32)], axis=1)
    return rc, rs, ic, ia, ib


def dsa_layer(x, pos, norm_g, w_in_layers, layer, q_norm, k_norm, w_out, tiles):
    t = x.shape[0]
    proj, iw, vt = dsa_projection(x, norm_g, jnp.swapaxes(w_in_layers, 1, 2), layer, q_norm, k_norm,
                                  _rope_tables(pos), tm=tiles["proj_tm"])
    attn = dsa_core(proj, iw, vt, tq=tiles["tq"], topk=min(TOPK_MAX, t // 4), kt=tiles["kt"], ka=tiles["ka"])
    return matmul_residual(attn, w_out.astype(MXU_DTYPE), x, tm=tiles["res_tm"], tn=tiles["res_tn"])


def _rglru_kernel(gate_ref, xr_ref, cw_ref, cb_ref, wa_ref, ba_ref, wx_ref, bx_ref, lam_ref,
                  y_ref, hcar_ref, xprev_ref, *, tt):
    @pl.when(pl.program_id(1) == 0)
    def _():
        hcar_ref[...] = jnp.zeros_like(hcar_ref)
        xprev_ref[...] = jnp.zeros_like(xprev_ref)

    xr = xr_ref[...]
    ext = jnp.concatenate([xprev_ref[...], xr], axis=0)
    cw = cw_ref[...]
    xc = cb_ref[...] + xr * cw[CONV_WIDTH - 1:CONV_WIDTH, :]
    for d in range(1, CONV_WIDTH):
        xc = xc + pltpu.roll(ext, d, 0)[SUBLANES:, :] * cw[CONV_WIDTH - 1 - d:CONV_WIDTH - d, :]
    xprev_ref[...] = xr[tt - SUBLANES:, :]

    xcb = xc.astype(wa_ref.dtype)
    r = jax.nn.sigmoid(jnp.dot(xcb, wa_ref[...], preferred_element_type=F32) + ba_ref[...])
    ig = jax.nn.sigmoid(jnp.dot(xcb, wx_ref[...], preferred_element_type=F32) + bx_ref[...])
    nlam = -lam_ref[...]
    softplus = jnp.maximum(nlam, 0.0) + jnp.log(1.0 + jnp.exp(-jnp.abs(nlam)))
    log_a = -LRU_C * r * softplus
    a = jnp.exp(log_a)
    mult = jnp.sqrt(1.0 - jnp.exp(2.0 * log_a))
    b = xc * ig * mult

    n_groups = tt // SUBLANES
    a = a.reshape(n_groups, SUBLANES, a.shape[1])
    b = b.reshape(a.shape)
    in_group = lax.broadcasted_iota(jnp.int32, a.shape, 1)
    d = 1
    while d < SUBLANES:
        b = a * jnp.where(in_group >= d, pltpu.roll(b, d, 1), 0.0) + b
        a = a * jnp.where(in_group >= d, pltpu.roll(a, d, 1), 1.0)
        d *= 2
    carry = hcar_ref[...]
    groups = []
    for gi in range(n_groups):
        hg = b[gi] + a[gi] * carry
        carry = hg[SUBLANES - 1:SUBLANES, :]
        groups.append(hg)
    hcar_ref[...] = carry
    h = jnp.concatenate(groups, axis=0)
    y_ref[...] = (h * jax.nn.gelu(gate_ref[...])).astype(y_ref.dtype)


def rglru_scan(proj, conv_w, conv_b, wa, ba, wx, bx, lam, *, tt):
    t = proj.shape[0]
    width = conv_w.shape[1]
    nb, blk, _ = wa.shape
    vec = lambda: pl.BlockSpec((1, blk), lambda n, s: (0, n))
    need = (4 * _nbytes((tt, blk), F32) + 2 * _nbytes((tt, blk), MXU_DTYPE)
            + 4 * _nbytes((blk, blk), wa.dtype) + 24 * _nbytes((tt, blk), F32))
    return pl.pallas_call(
        functools.partial(_rglru_kernel, tt=tt),
        grid=(nb, t // tt),
        in_specs=[
            pl.BlockSpec((tt, blk), lambda n, s: (s, n)),
            pl.BlockSpec((tt, blk), lambda n, s: (s, nb + n)),
            pl.BlockSpec((CONV_WIDTH, blk), lambda n, s: (0, n)),
            vec(),
            pl.BlockSpec((None, blk, blk), lambda n, s: (n, 0, 0)),
            vec(),
            pl.BlockSpec((None, blk, blk), lambda n, s: (n, 0, 0)),
            vec(),
            vec(),
        ],
        out_specs=pl.BlockSpec((tt, blk), lambda n, s: (s, n)),
        out_shape=jax.ShapeDtypeStruct((t, width), MXU_DTYPE),
        scratch_shapes=[pltpu.VMEM((1, blk), F32), pltpu.VMEM((SUBLANES, blk), F32)],
        compiler_params=pltpu.CompilerParams(
            dimension_semantics=("parallel", "arbitrary"),
            vmem_limit_bytes=_vmem_limit(need)),
        name="rglru_scan",
    )(proj, proj, conv_w, conv_b.reshape(1, width), wa, ba.reshape(1, width),
      wx, bx.reshape(1, width), lam.reshape(1, width))


def rglru_layer(x, norm_g, w_in, conv_w, conv_b, wa, ba, wx, bx, lam, w_out, tiles):
    proj = norm_matmul(x, norm_g, w_in.astype(MXU_DTYPE), tm=tiles["proj_tm"], tn=tiles["proj_tn"],
                       out_dtype=F32)
    y = rglru_scan(proj, conv_w, conv_b, wa.astype(MXU_DTYPE), ba, wx.astype(MXU_DTYPE), bx, lam,
                   tt=tiles["scan_tt"])
    return matmul_residual(y, w_out.astype(MXU_DTYPE), x, tm=tiles["res_tm"], tn=tiles["res_tn"])


POOL_HALO = max(POOL_WINDOWS)


def _pool_kernel(x_ref, g_ref, w_ref, b_ref, s_ref, o_ref, halo_ref, *, tt):
    blk = pl.program_id(0)

    @pl.when(blk == 0)
    def _():
        halo_ref[...] = jnp.zeros_like(halo_ref)

    x = x_ref[...]
    h = _rms_norm_rows(x, g_ref[...])
    ext = jnp.concatenate([halo_ref[...], h], axis=0)
    halo_ref[...] = h[tt - POOL_HALO:, :]

    gw = h.shape[1] // len(POOL_WINDOWS)
    t1 = (lax.broadcasted_iota(jnp.int32, (tt, gw), 0) + (blk * tt + 1)).astype(F32)
    for gi, win in enumerate(POOL_WINDOWS):
        cols = slice(gi * gw, (gi + 1) * gw)
        acc = ext[:, cols]
        d = 1
        while d < win:
            acc = acc + pltpu.roll(acc, d, 0)
            d *= 2
        mean = acc[POOL_HALO:, :] / jnp.minimum(t1, float(win))
        y = (mean - h[:, cols]).astype(w_ref.dtype)
        z = jnp.dot(y, w_ref[gi], preferred_element_type=F32) + b_ref[gi:gi + 1, :]
        o_ref[:, cols] = x[:, cols] + z * s_ref[:, cols]


def pool_layer(x, norm_g, w_group, b_group, scale, tiles):
    t, d = x.shape
    tt = tiles["pool_tt"]
    ng, gw, _ = w_group.shape
    need = (4 * _nbytes((tt, d), F32) + 2 * _nbytes((ng, gw, gw), MXU_DTYPE)
            + 6 * _nbytes((tt, d), F32))
    return pl.pallas_call(
        functools.partial(_pool_kernel, tt=tt),
        grid=(t // tt,),
        in_specs=[
            pl.BlockSpec((tt, d), lambda i: (i, 0)),
            pl.BlockSpec((1, d), lambda i: (0, 0)),
            pl.BlockSpec((ng, gw, gw), lambda i: (0, 0, 0)),
            pl.BlockSpec((ng, gw), lambda i: (0, 0)),
            pl.BlockSpec((1, d), lambda i: (0, 0)),
        ],
        out_specs=pl.BlockSpec((tt, d), lambda i: (i, 0)),
        out_shape=jax.ShapeDtypeStruct((t, d), F32),
        scratch_shapes=[pltpu.VMEM((POOL_HALO, d), F32)],
        compiler_params=pltpu.CompilerParams(
            dimension_semantics=("arbitrary",),
            vmem_limit_bytes=_vmem_limit(need)),
        name="pool_mixer",
    )(x, norm_g.reshape(1, d), w_group.astype(MXU_DTYPE), b_group, scale.reshape(1, d))


def _tiles(t):
    big = min(t, 512)
    tall = min(t, 1024)
    return {
        "proj_tm": tall, "proj_tn": 512,
        "res_tm": tall, "res_tn": 512,
        "tq": min(t, 256), "kt": min(4, t // min(t, 256)), "ka": min(4, t // min(t, 256)),
        "scan_tt": min(t, 256),
        "pool_tt": min(t, 256),
        "mlp_tm": big, "mlp_tf": 512,
    }


def kernel(x, positions, attn_norm, attn_w_in, attn_q_norm, attn_k_norm, attn_w_out, rnn_norm, rnn_w_in, rnn_conv_w, rnn_conv_b, rnn_gate_a_w, rnn_gate_a_b, rnn_gate_x_w, rnn_gate_x_b, rnn_lambda, rnn_w_out, pool_norm, pool_w, pool_b, pool_scale, mlp_norm, mlp_w_up, mlp_w_down):
    batch, t, d = x.shape
    depth = mlp_norm.shape[0]
    tiles = _tiles(t)
    w_up_all = mlp_w_up.astype(MXU_DTYPE)
    w_down_all = mlp_w_down.astype(MXU_DTYPE)
    outs = []
    for bi in range(batch):
        xb = x[bi]
        pos = positions[bi]
        for i in range(depth):
            kind, j = i % N_MIXERS, i // N_MIXERS
            if kind == 0:
                xb = dsa_layer(xb, pos, attn_norm[j], attn_w_in, j, attn_q_norm[j], attn_k_norm[j],
                               attn_w_out[j], tiles)
            elif kind == 1:
                xb = rglru_layer(xb, rnn_norm[j], rnn_w_in[j], rnn_conv_w[j], rnn_conv_b[j],
                                 rnn_gate_a_w[j], rnn_gate_a_b[j], rnn_gate_x_w[j], rnn_gate_x_b[j],
                                 rnn_lambda[j], rnn_w_out[j], tiles)
            else:
                xb = pool_layer(xb, pool_norm[j], pool_w[j], pool_b[j], pool_scale[j], tiles)
            xb = mlp_block(xb, mlp_norm[i], w_up_all, w_down_all, i,
                           tm=tiles["mlp_tm"], tf=tiles["mlp_tf"])
        outs.append(xb)
    return outs[0][None] if batch == 1 else jnp.stack(outs, axis=0)
```

```python
import functools
import math

import jax
import jax.numpy as jnp
from jax import lax
from jax.experimental import pallas as pl
from jax.experimental.pallas import tpu as pltpu

F32 = jnp.float32
MXU_DTYPE = jnp.bfloat16

N_MIXERS = 3
EPS = 1e-6
ROPE_THETA = 10000.0
HEAD_DIM = 128
N_HEADS = 16
N_KV_HEADS = 4
IDX_HEADS = 16
IDX_DIM = 128
IDX_ROPE_DIM = 64
TOPK_MAX = 256
CONV_WIDTH = 4
LRU_C = 8.0
POOL_WINDOWS = (2, 4, 8, 16)

LANES = 128
SUBLANES = 8
VMEM_BYTES_V7X = 64 * 1024 * 1024
VMEM_CAP_BYTES = VMEM_BYTES_V7X - 8 * 1024 * 1024

INT_MIN = -(2 ** 31)
NEG_BIG = -1e30


def _vmem_limit(block_bytes):
    return int(min(VMEM_CAP_BYTES, block_bytes * 3 // 2 + (4 << 20)))


def _nbytes(shape, dtype):
    return math.prod(shape) * jnp.dtype(dtype).itemsize


def _rms_norm_rows(x, g):
    ms = jnp.mean(x * x, axis=-1, keepdims=True)
    return x * lax.rsqrt(ms + EPS) * g


def _tile_lanes(x, width):
    reps = width // LANES
    return x if reps == 1 else jnp.concatenate([x] * reps, axis=1)


def _norm_matmul_kernel(x_ref, g_ref, w_ref, o_ref, h_ref):
    @pl.when(pl.program_id(1) == 0)
    def _():
        h_ref[...] = _rms_norm_rows(x_ref[...], g_ref[...]).astype(h_ref.dtype)

    o_ref[...] = jnp.dot(h_ref[...], w_ref[...], preferred_element_type=F32).astype(o_ref.dtype)


def norm_matmul(x, g, w, *, tm, tn, out_dtype):
    t, d = x.shape
    n = w.shape[1]
    need = (2 * _nbytes((tm, d), F32) + 2 * _nbytes((d, tn), w.dtype)
            + 2 * _nbytes((tm, tn), out_dtype) + _nbytes((tm, d), w.dtype))
    return pl.pallas_call(
        _norm_matmul_kernel,
        grid=(t // tm, n // tn),
        in_specs=[
            pl.BlockSpec((tm, d), lambda i, j: (i, 0)),
            pl.BlockSpec((1, d), lambda i, j: (0, 0)),
            pl.BlockSpec((d, tn), lambda i, j: (0, j)),
        ],
        out_specs=pl.BlockSpec((tm, tn), lambda i, j: (i, j)),
        out_shape=jax.ShapeDtypeStruct((t, n), out_dtype),
        scratch_shapes=[pltpu.VMEM((tm, d), w.dtype)],
        compiler_params=pltpu.CompilerParams(
            dimension_semantics=("parallel", "arbitrary"),
            vmem_limit_bytes=_vmem_limit(need)),
        name="norm_matmul",
    )(x, g.reshape(1, d), w)


def _matmul_residual_kernel(a_ref, w_ref, x_ref, o_ref):
    o_ref[...] = x_ref[...] + jnp.dot(a_ref[...], w_ref[...], preferred_element_type=F32)


def matmul_residual(a, w, x, *, tm, tn):
    t, k = a.shape
    n = w.shape[1]
    need = (2 * _nbytes((tm, k), a.dtype) + 2 * _nbytes((k, tn), w.dtype)
            + 4 * _nbytes((tm, tn), F32))
    return pl.pallas_call(
        _matmul_residual_kernel,
        grid=(t // tm, n // tn),
        in_specs=[
            pl.BlockSpec((tm, k), lambda i, j: (i, 0)),
            pl.BlockSpec((k, tn), lambda i, j: (0, j)),
            pl.BlockSpec((tm, tn), lambda i, j: (i, j)),
        ],
        out_specs=pl.BlockSpec((tm, tn), lambda i, j: (i, j)),
        out_shape=jax.ShapeDtypeStruct((t, n), F32),
        compiler_params=pltpu.CompilerParams(
            dimension_semantics=("parallel", "arbitrary"),
            vmem_limit_bytes=_vmem_limit(need)),
        name="matmul_residual",
    )(a, w, x)


def _mlp_kernel(x_ref, g_ref, wu_ref, wd_ref, o_ref, h_ref, acc_ref):
    f = pl.program_id(1)

    @pl.when(f == 0)
    def _():
        h_ref[...] = _rms_norm_rows(x_ref[...], g_ref[...]).astype(h_ref.dtype)
        acc_ref[...] = jnp.zeros_like(acc_ref)

    u = jnp.dot(h_ref[...], wu_ref[...], preferred_element_type=F32)
    u = jnp.square(jnp.maximum(u, 0.0)).astype(wd_ref.dtype)
    acc_ref[...] += jnp.dot(u, wd_ref[...], preferred_element_type=F32)

    @pl.when(f == pl.num_programs(1) - 1)
    def _():
        o_ref[...] = x_ref[...] + acc_ref[...]


def mlp_block(x, g, w_up, w_down, layer, *, tm, tf):
    t, d = x.shape
    ff = w_up.shape[2]
    need = (4 * _nbytes((tm, d), F32) + 2 * _nbytes((d, tf), w_up.dtype)
            + 2 * _nbytes((tf, d), w_down.dtype) + _nbytes((tm, d), w_up.dtype)
            + _nbytes((tm, d), F32) + 2 * _nbytes((tm, tf), F32))
    return pl.pallas_call(
        _mlp_kernel,
        grid=(t // tm, ff // tf),
        in_specs=[
            pl.BlockSpec((tm, d), lambda i, f: (i, 0)),
            pl.BlockSpec((1, d), lambda i, f: (0, 0)),
            pl.BlockSpec((None, d, tf), lambda i, f: (layer, 0, f)),
            pl.BlockSpec((None, tf, d), lambda i, f: (layer, f, 0)),
        ],
        out_specs=pl.BlockSpec((tm, d), lambda i, f: (i, 0)),
        out_shape=jax.ShapeDtypeStruct((t, d), F32),
        scratch_shapes=[pltpu.VMEM((tm, d), w_up.dtype), pltpu.VMEM((tm, d), F32)],
        compiler_params=pltpu.CompilerParams(
            dimension_semantics=("parallel", "arbitrary"),
            vmem_limit_bytes=_vmem_limit(need)),
        name="mlp_block",
    )(x, g.reshape(1, d), w_up, w_down)


DSA_TN = 2 * LANES
Q_COLS = N_HEADS * HEAD_DIM
IQ_COLS = IDX_HEADS * IDX_DIM
KV_COLS = N_KV_HEADS * HEAD_DIM
DSA_COLS = Q_COLS + IQ_COLS + 2 * KV_COLS + IDX_DIM + LANES
Q_BLOCKS = Q_COLS // DSA_TN
IQ_BLOCKS = IQ_COLS // DSA_TN
KV_BLOCKS = KV_COLS // DSA_TN


W_K_FIRST = Q_BLOCKS
W_V_FIRST = W_K_FIRST + KV_BLOCKS
W_IQ_FIRST = W_V_FIRST + KV_BLOCKS
W_LAST = W_IQ_FIRST + IQ_BLOCKS
OUT_K_FIRST = Q_BLOCKS + IQ_BLOCKS
OUT_V_FIRST = OUT_K_FIRST + KV_BLOCKS


def _dsa_out_block(jw):
    return jnp.where(jw < W_K_FIRST, jw,
                     jnp.where(jw < W_IQ_FIRST, jw + (OUT_K_FIRST - W_K_FIRST),
                               jnp.where(jw < W_LAST, jw - (W_IQ_FIRST - Q_BLOCKS), W_LAST)))


def _dsa_proj_kernel(x_ref, g_ref, w_ref, qn_ref, kn_ref, rc_ref, rs_ref,
                     ic_ref, ia_ref, ib_ref, o_ref, iw_ref, vt_ref, h_ref, y_ref):
    j = pl.program_id(1)
    jp = j - 1

    @pl.when(j == 0)
    def _():
        h_ref[...] = _rms_norm_rows(x_ref[...], g_ref[...]).astype(h_ref.dtype)

    def rope(z):
        return z * rc_ref[...] + pltpu.roll(z, HEAD_DIM // 2, 1) * rs_ref[...]

    def idx_rope(z):
        half = IDX_ROPE_DIM // 2
        return (z * ic_ref[...] + pltpu.roll(z, LANES - half, 1) * ia_ref[...]
                + pltpu.roll(z, half, 1) * ib_ref[...])

    def per_head(fn):
        def finish(y):
            o_ref[...] = jnp.concatenate(
                [fn(y[:, :LANES]), fn(y[:, LANES:])], axis=1).astype(o_ref.dtype)
        return finish

    def step(finish, multiply=True):
        y_prev = y_ref[...]
        if multiply:
            y_ref[...] = lax.dot_general(
                h_ref[...], w_ref[...].astype(h_ref.dtype), (((1,), (1,)), ((), ())),
                preferred_element_type=F32)
        if finish is not None:
            finish(y_prev)

    @pl.when(j == 0)
    def _():
        step(None)

    @pl.when((jp >= 0) & (jp < W_K_FIRST))
    def _():
        scale = HEAD_DIM ** -0.5 * math.log2(math.e)
        step(per_head(lambda z: rope(_rms_norm_rows(z, qn_ref[...])) * scale))

    @pl.when((jp >= W_K_FIRST) & (jp < W_V_FIRST))
    def _():
        step(per_head(lambda z: rope(_rms_norm_rows(z, kn_ref[...]))))

    @pl.when((jp >= W_V_FIRST) & (jp < W_IQ_FIRST))
    def _():
        def finish(y):
            o_ref[...] = y.astype(o_ref.dtype)
            vt_ref[...] = jnp.transpose(y).astype(vt_ref.dtype)
        step(finish)

    @pl.when((jp >= W_IQ_FIRST) & (jp < W_LAST))
    def _():
        step(per_head(idx_rope))

    @pl.when(jp == W_LAST)
    def _():
        def finish(y):
            o_ref[...] = jnp.concatenate(
                [idx_rope(y[:, :LANES]), jnp.zeros_like(y[:, LANES:])], axis=1).astype(o_ref.dtype)
            lane = lax.broadcasted_iota(jnp.int32, (y.shape[0], LANES), 1)
            iw_ref[...] = jnp.where(lane < IDX_HEADS,
                                    y[:, LANES:] * (IDX_HEADS ** -0.5 * IDX_DIM ** -0.5), 0.0)
        step(finish, multiply=False)


def dsa_projection(x, g, w_layers, layer, q_norm, k_norm, tables, *, tm):
    t, d = x.shape
    assert w_layers.shape[1] == Q_COLS + 2 * KV_COLS + IQ_COLS + IDX_DIM + IDX_HEADS
    rc, rs, ic, ia, ib = tables
    tab_spec = pl.BlockSpec((tm, LANES), lambda i, j: (i, 0))
    vec_spec = pl.BlockSpec((1, LANES), lambda i, j: (0, 0))
    need = (2 * _nbytes((tm, d), F32) + 3 * _nbytes((d, DSA_TN), F32)
            + 2 * _nbytes((tm, DSA_TN), MXU_DTYPE) + _nbytes((tm, d), MXU_DTYPE)
            + 12 * _nbytes((tm, LANES), F32) + 3 * _nbytes((tm, DSA_TN), F32))
    return pl.pallas_call(
        _dsa_proj_kernel,
        grid=(t // tm, W_LAST + 2),
        in_specs=[
            pl.BlockSpec((tm, d), lambda i, j: (i, 0)),
            pl.BlockSpec((1, d), lambda i, j: (0, 0)),
            pl.BlockSpec((None, DSA_TN, d), lambda i, j: (layer, jnp.minimum(j, W_LAST), 0)),
            vec_spec, vec_spec, tab_spec, tab_spec, tab_spec, tab_spec, tab_spec,
        ],
        out_specs=[
            pl.BlockSpec((tm, DSA_TN), lambda i, j: (i, _dsa_out_block(jnp.maximum(j - 1, 0)))),
            pl.BlockSpec((tm, LANES), lambda i, j: (i, 0)),
            pl.BlockSpec((DSA_TN, tm), lambda i, j: (jnp.clip(j - 1 - W_V_FIRST, 0, KV_BLOCKS - 1), i)),
        ],
        out_shape=[
            jax.ShapeDtypeStruct((t, DSA_COLS), MXU_DTYPE),
            jax.ShapeDtypeStruct((t, LANES), F32),
            jax.ShapeDtypeStruct((KV_COLS, t), MXU_DTYPE),
        ],
        scratch_shapes=[pltpu.VMEM((tm, d), MXU_DTYPE), pltpu.VMEM((tm, DSA_TN), F32)],
        compiler_params=pltpu.CompilerParams(
            dimension_semantics=("parallel", "arbitrary"),
            vmem_limit_bytes=_vmem_limit(need)),
        name="dsa_projection",
    )(x, g.reshape(1, d), w_layers, q_norm.reshape(1, LANES), k_norm.reshape(1, LANES), rc, rs, ic, ia, ib)


def _float_key(v):
    b = lax.bitcast_convert_type(v, jnp.int32)
    return b ^ ((b >> 31) & 0x7FFFFFFF)


def _key_float(k):
    return lax.bitcast_convert_type(k ^ ((k >> 31) & 0x7FFFFFFF), F32)


_KLO, _KHI, _CLO, _CHI, _JCUT = range(5)
_SEARCH_CAP = 96
ONES_ROWS = 2 * SUBLANES
ATTN_PAD_FROM = 2
ATTN_LOOKAHEAD = 1


def _dsa_core_kernel(q_ref, iq_ref, iw_ref, k_ref, vt_ref, ik_ref, o_ref,
                     key_ref, bias_ref, s_ref, wt_ref, st_ref, m_ref, l_ref, acc_ref, *, tq, topk, kt, ka):
    i = pl.program_id(0)
    nc = i + 1
    group = N_HEADS // N_KV_HEADS
    nt = (((1,), (1,)), ((), ()))

    wt_ref[...] = jnp.transpose(iw_ref[...])

    krow = lax.broadcasted_iota(jnp.int32, (tq, tq), 0)
    qcol = lax.broadcasted_iota(jnp.int32, (tq, tq), 1)
    diag_causal = krow <= qcol

    def chunk_start(c):
        return pl.multiple_of(c * tq, tq)

    def over_chunks(body, k, init, pad_from=None):
        n_tiles = (nc + k - min(pad_from or k, k)) // k
        carry = lax.fori_loop(0, n_tiles, lambda ti, c: body(ti * k, k, c), init)
        if k > 1:
            carry = lax.fori_loop(n_tiles * k, nc, lambda c0, c: body(c0, 1, c), carry)
        return carry

    def index_tile(first, k, carry):
        smin, smax = carry
        iks = [ik_ref[pl.ds(chunk_start(first + u), tq), :] for u in range(k)]
        scores = [jnp.zeros((tq, tq), F32) for _ in range(k)]
        for h in range(IDX_HEADS):
            iqh = iq_ref[:, h * IDX_DIM:(h + 1) * IDX_DIM]
            w = jnp.broadcast_to(wt_ref[h:h + 1, :], (tq, tq))
            for u in range(k):
                logits = lax.dot_general(iks[u], iqh, nt, preferred_element_type=F32)
                scores[u] = scores[u] + w * jnp.maximum(logits, 0.0)
        for u in range(k):
            c = first + u
            valid = (c < i) | diag_causal
            key_ref[c] = jnp.where(valid, _float_key(scores[u]), INT_MIN)
            smin = jnp.minimum(smin, jnp.min(jnp.where(valid, scores[u], jnp.inf), axis=0, keepdims=True))
            smax = jnp.maximum(smax, jnp.max(jnp.where(valid, scores[u], -jnp.inf), axis=0, keepdims=True))
        return smin, smax

    smin, smax = over_chunks(
        index_tile, kt, (jnp.full((1, tq), jnp.inf, F32), jnp.full((1, tq), -jnp.inf, F32)))

    def mask_chunk(c, carry):
        key_ref[c] = jnp.full((tq, tq), INT_MIN, jnp.int32)
        return carry

    lax.fori_loop(nc, (nc + ka - min(ATTN_PAD_FROM, ka)) // ka * ka, mask_chunk, 0)

    def count_ge(cand):
        cand_b = jnp.broadcast_to(cand, (tq, tq))

        def body(first, k, cnt):
            for u in range(k):
                hit = jnp.where(key_ref[first + u] >= cand_b, 1.0, 0.0)
                cnt = cnt + jnp.sum(hit, axis=0, keepdims=True)
            return cnt

        return over_chunks(body, kt, jnp.zeros((1, tq), F32))

    st_ref[_KLO] = _float_key(smin)
    st_ref[_KHI] = _float_key(smax) + 1
    st_ref[_CLO] = lax.broadcasted_iota(jnp.int32, (1, tq), 1) + (i * tq + 1)
    st_ref[_CHI] = jnp.zeros((1, tq), jnp.int32)

    def unfinished(klo, khi, clo):
        return (clo > topk) & ((khi - klo) != 1)

    def search_cond(st):
        step, pending = st
        return (step < _SEARCH_CAP) & (pending > 0)

    def any_lane(mask):
        return jnp.max(jnp.where(mask, 1.0, 0.0))

    def search_body(st):
        step, _ = st
        klo, khi, clo = st_ref[_KLO], st_ref[_KHI], st_ref[_CLO]
        live = unfinished(klo, khi, clo)
        pending = any_lane(live)
        vmid = 0.5 * _key_float(klo) + 0.5 * _key_float(khi)
        cmid = _float_key(vmid)
        kmid = klo + lax.shift_right_logical(khi - klo, 1)
        cand = jnp.where((cmid > klo) & (cmid < khi), cmid, kmid)
        cand = jnp.where(live, cand, klo)
        cnt = count_ge(cand).astype(jnp.int32)
        take = cnt >= topk
        klo = jnp.where(live & take, cand, klo)
        clo = jnp.where(live & take, cnt, clo)
        drop = live & jnp.logical_not(take)
        st_ref[_KLO], st_ref[_KHI], st_ref[_CLO] = klo, jnp.where(drop, cand, khi), clo
        st_ref[_CHI] = jnp.where(drop, cnt, st_ref[_CHI])
        return step + 1, pending

    lax.while_loop(search_cond, search_body, (jnp.int32(0), jnp.float32(1.0)))
    thr_b = jnp.broadcast_to(st_ref[_KLO], (tq, tq))

    n_keys = key_ref.shape[0] * tq
    tied = st_ref[_CLO] > topk
    has_ties = any_lane(tied) > 0
    st_ref[_JCUT] = jnp.full((1, tq), n_keys, jnp.int32)

    @pl.when(has_ties)
    def _():
        want = topk - st_ref[_CHI]

        def count_tied_upto(pos):
            pos_b = jnp.broadcast_to(pos, (tq, tq))

            def body(c, cnt):
                hit = (key_ref[c] == thr_b) & (krow + c * tq <= pos_b)
                return cnt + jnp.sum(jnp.where(hit, 1.0, 0.0), axis=0, keepdims=True)

            return lax.fori_loop(0, nc, body, jnp.zeros((1, tq), F32)).astype(jnp.int32)

        def cut_step(_, st):
            lo, hi = st
            mid = lo + ((hi - lo) >> 1)
            enough = count_tied_upto(mid) >= want
            return jnp.where(enough, lo, mid), jnp.where(enough, mid, hi)

        _, cut = lax.fori_loop(0, n_keys.bit_length(), cut_step,
                               (jnp.full((1, tq), -1, jnp.int32), jnp.full((1, tq), n_keys - 1, jnp.int32)))
        st_ref[_JCUT] = jnp.where(tied, cut, n_keys)

    cut_b = jnp.broadcast_to(st_ref[_JCUT], (tq, tq))

    m_ref[...] = jnp.full(m_ref.shape, NEG_BIG, F32)
    l_ref[...] = jnp.zeros(l_ref.shape, F32)
    acc_ref[...] = jnp.zeros(acc_ref.shape, F32)

    def attend_tile(first, k, carry, *, with_ties):
        for u in range(k):
            c = first + u
            key = key_ref[c]
            if with_ties:
                keep = (key > thr_b) | ((key == thr_b) & (krow + c * tq <= cut_b))
            else:
                keep = key >= thr_b
            bias_ref[u] = jnp.where(keep, 0.0, NEG_BIG)

        starts = [chunk_start(first + u) for u in range(k)]
        slots = ATTN_LOOKAHEAD + 1
        dyn0 = jnp.minimum(first, 0)
        ones_rows = jnp.ones((ONES_ROWS, tq), vt_ref.dtype)

        def logits(h):
            cols = slice(h // group * HEAD_DIM, (h // group + 1) * HEAD_DIM)
            qh = q_ref[:, h * HEAD_DIM:(h + 1) * HEAD_DIM]
            top = None
            for u in range(k):
                s = lax.dot_general(k_ref[pl.ds(starts[u], tq), cols], qh, nt,
                                    preferred_element_type=F32) + bias_ref[u]
                s_ref[h % slots + dyn0, u] = s
                top = s if top is None else jnp.maximum(top, s)
            return jnp.max(top, axis=0, keepdims=True)

        queued = [logits(h) for h in range(ATTN_LOOKAHEAD)]
        for h in range(N_HEADS):
            tile_max = queued.pop(0)
            if h + ATTN_LOOKAHEAD < N_HEADS:
                queued.append(logits(h + ATTN_LOOKAHEAD))
            cols = slice(h // group * HEAD_DIM, (h // group + 1) * HEAD_DIM)
            m_prev = m_ref[h]
            m_cur = jnp.maximum(m_prev, tile_max)
            alpha = jnp.exp2(m_prev - m_cur)
            pv = jnp.zeros((HEAD_DIM + ONES_ROWS, tq), F32)
            for u in range(k):
                p = jnp.exp2(s_ref[h % slots + dyn0, u] - m_cur)
                v_aug = jnp.concatenate([vt_ref[cols, pl.ds(starts[u], tq)], ones_rows], axis=0)
                pv = pv + jnp.dot(v_aug, p.astype(vt_ref.dtype), preferred_element_type=F32)
            l_ref[h] = alpha * l_ref[h] + pv[HEAD_DIM:HEAD_DIM + 1, :]
            pv = pv[:HEAD_DIM, :]
            acc_ref[h] = alpha * acc_ref[h] + pv
            m_ref[h] = m_cur
        return carry

    @pl.when(jnp.logical_not(has_ties))
    def _():
        over_chunks(functools.partial(attend_tile, with_ties=False), ka, 0, pad_from=ATTN_PAD_FROM)

    @pl.when(has_ties)
    def _():
        over_chunks(functools.partial(attend_tile, with_ties=True), 1, 0)

    for h in range(N_HEADS):
        o_ref[:, h * HEAD_DIM:(h + 1) * HEAD_DIM] = jnp.transpose(acc_ref[h] / l_ref[h]).astype(o_ref.dtype)


def dsa_core(proj, iw, vt, *, tq, topk, kt, ka):
    t = proj.shape[0]
    assert (t // tq) % kt == 0 and kt % ka == 0
    k_block = (Q_COLS + IQ_COLS) // KV_COLS
    ik_block = (Q_COLS + IQ_COLS + 2 * KV_COLS) // IDX_DIM
    resident = dict(pipeline_mode=pl.Buffered(1))
    need = (4 * _nbytes((tq, Q_COLS), proj.dtype) + 2 * _nbytes((tq, LANES), F32)
            + 2 * _nbytes((t, KV_COLS), proj.dtype) + _nbytes((t, IDX_DIM), proj.dtype)
            + 2 * _nbytes((tq, Q_COLS), proj.dtype)
            + _nbytes((t // tq, tq, tq), jnp.int32) + _nbytes((kt, tq, tq), F32)
            + _nbytes((LANES, tq), F32) + (3 + 2 * N_HEADS) * _nbytes((SUBLANES, tq), F32)
            + _nbytes((N_HEADS, HEAD_DIM, tq), F32)
            + (ATTN_LOOKAHEAD + 3) * _nbytes((ka, tq, tq), F32))
    return pl.pallas_call(
        functools.partial(_dsa_core_kernel, tq=tq, topk=topk, kt=kt, ka=ka),
        grid=(t // tq,),
        in_specs=[
            pl.BlockSpec((tq, Q_COLS), lambda i: (i, 0)),
            pl.BlockSpec((tq, IQ_COLS), lambda i: (i, 1)),
            pl.BlockSpec((tq, LANES), lambda i: (i, 0)),
            pl.BlockSpec((t, KV_COLS), lambda i: (0, k_block), **resident),
            pl.BlockSpec((KV_COLS, t), lambda i: (0, 0), **resident),
            pl.BlockSpec((t, IDX_DIM), lambda i: (0, ik_block), **resident),
        ],
        out_specs=pl.BlockSpec((tq, Q_COLS), lambda i: (i, 0)),
        out_shape=jax.ShapeDtypeStruct((t, Q_COLS), proj.dtype),
        scratch_shapes=[
            pltpu.VMEM((t // tq, tq, tq), jnp.int32),
            pltpu.VMEM((ka, tq, tq), F32),
            pltpu.VMEM((ATTN_LOOKAHEAD + 1, ka, tq, tq), F32),
            pltpu.VMEM((LANES, tq), F32),
            pltpu.VMEM((5, 1, tq), jnp.int32),
            pltpu.VMEM((N_HEADS, 1, tq), F32),
            pltpu.VMEM((N_HEADS, 1, tq), F32),
            pltpu.VMEM((N_HEADS, HEAD_DIM, tq), F32),
        ],
        compiler_params=pltpu.CompilerParams(
            dimension_semantics=("arbitrary",),
            vmem_limit_bytes=_vmem_limit(need)),
        name="dsa_core",
    )(proj, proj, iw, proj, vt, proj)


def _rope_tables(pos):
    def angles(dim):
        inv = 1.0 / (ROPE_THETA ** (jnp.arange(0, dim, 2, dtype=F32) / dim))
        return pos.astype(F32)[:, None] * inv

    ang = angles(HEAD_DIM)
    cos, sin = jnp.cos(ang), jnp.sin(ang)
    rc = jnp.concatenate([cos, cos], axis=1)
    rs = jnp.concatenate([-sin, sin], axis=1)
    iang = angles(IDX_ROPE_DIM)
    icos, isin = jnp.cos(iang), jnp.sin(iang)
    rest = LANES - IDX_ROPE_DIM
    zeros = jnp.zeros_like(isin)
    ic = jnp.concatenate([icos, icos, jnp.ones((pos.shape[0], rest), F32)], axis=1)
    ia = jnp.concatenate([-isin, zeros, jnp.zeros((pos.shape[0], rest), F32)], axis=1)
    ib = jnp.concatenate([zeros, isin, jnp.zeros((pos.shape[0], rest), F32)], axis=1)
    return rc, rs, ic, ia, ib


def dsa_layer(x, pos, norm_g, w_in_layers, layer, q_norm, k_norm, w_out, tiles):
    t = x.shape[0]
    proj, iw, vt = dsa_projection(x, norm_g, jnp.swapaxes(w_in_layers, 1, 2), layer, q_norm, k_norm,
                                  _rope_tables(pos), tm=tiles["proj_tm"])
    attn = dsa_core(proj, iw, vt, tq=tiles["tq"], topk=min(TOPK_MAX, t // 4), kt=tiles["kt"], ka=tiles["ka"])
    return matmul_residual(attn, w_out.astype(MXU_DTYPE), x, tm=tiles["res_tm"], tn=tiles["res_tn"])


def _rglru_kernel(gate_ref, xr_ref, cw_ref, cb_ref, wa_ref, ba_ref, wx_ref, bx_ref, lam_ref,
                  y_ref, hcar_ref, xprev_ref, *, tt):
    @pl.when(pl.program_id(1) == 0)
    def _():
        hcar_ref[...] = jnp.zeros_like(hcar_ref)
        xprev_ref[...] = jnp.zeros_like(xprev_ref)

    xr = xr_ref[...]
    ext = jnp.concatenate([xprev_ref[...], xr], axis=0)
    cw = cw_ref[...]
    xc = cb_ref[...] + xr * cw[CONV_WIDTH - 1:CONV_WIDTH, :]
    for d in range(1, CONV_WIDTH):
        xc = xc + pltpu.roll(ext, d, 0)[SUBLANES:, :] * cw[CONV_WIDTH - 1 - d:CONV_WIDTH - d, :]
    xprev_ref[...] = xr[tt - SUBLANES:, :]

    xcb = xc.astype(wa_ref.dtype)
    r = jax.nn.sigmoid(jnp.dot(xcb, wa_ref[...], preferred_element_type=F32) + ba_ref[...])
    ig = jax.nn.sigmoid(jnp.dot(xcb, wx_ref[...], preferred_element_type=F32) + bx_ref[...])
    nlam = -lam_ref[...]
    softplus = jnp.maximum(nlam, 0.0) + jnp.log(1.0 + jnp.exp(-jnp.abs(nlam)))
    log_a = -LRU_C * r * softplus
    a = jnp.exp(log_a)
    mult = jnp.sqrt(1.0 - jnp.exp(2.0 * log_a))
    b = xc * ig * mult

    n_groups = tt // SUBLANES
    a = a.reshape(n_groups, SUBLANES, a.shape[1])
    b = b.reshape(a.shape)
    in_group = lax.broadcasted_iota(jnp.int32, a.shape, 1)
    d = 1
    while d < SUBLANES:
        b = a * jnp.where(in_group >= d, pltpu.roll(b, d, 1), 0.0) + b
        a = a * jnp.where(in_group >= d, pltpu.roll(a, d, 1), 1.0)
        d *= 2
    carry = hcar_ref[...]
    groups = []
    for gi in range(n_groups):
        hg = b[gi] + a[gi] * carry
        carry = hg[SUBLANES - 1:SUBLANES, :]
        groups.append(hg)
    hcar_ref[...] = carry
    h = jnp.concatenate(groups, axis=0)
    y_ref[...] = (h * jax.nn.gelu(gate_ref[...])).astype(y_ref.dtype)


def rglru_scan(proj, conv_w, conv_b, wa, ba, wx, bx, lam, *, tt):
    t = proj.shape[0]
    width = conv_w.shape[1]
    nb, blk, _ = wa.shape
    vec = lambda: pl.BlockSpec((1, blk), lambda n, s: (0, n))
    need = (4 * _nbytes((tt, blk), F32) + 2 * _nbytes((tt, blk), MXU_DTYPE)
            + 4 * _nbytes((blk, blk), wa.dtype) + 24 * _nbytes((tt, blk), F32))
    return pl.pallas_call(
        functools.partial(_rglru_kernel, tt=tt),
        grid=(nb, t // tt),
        in_specs=[
            pl.BlockSpec((tt, blk), lambda n, s: (s, n)),
            pl.BlockSpec((tt, blk), lambda n, s: (s, nb + n)),
            pl.BlockSpec((CONV_WIDTH, blk), lambda n, s: (0, n)),
            vec(),
            pl.BlockSpec((None, blk, blk), lambda n, s: (n, 0, 0)),
            vec(),
            pl.BlockSpec((None, blk, blk), lambda n, s: (n, 0, 0)),
            vec(),
            vec(),
        ],
        out_specs=pl.BlockSpec((tt, blk), lambda n, s: (s, n)),
        out_shape=jax.ShapeDtypeStruct((t, width), MXU_DTYPE),
        scratch_shapes=[pltpu.VMEM((1, blk), F32), pltpu.VMEM((SUBLANES, blk), F32)],
        compiler_params=pltpu.CompilerParams(
            dimension_semantics=("parallel", "arbitrary"),
            vmem_limit_bytes=_vmem_limit(need)),
        name="rglru_scan",
    )(proj, proj, conv_w, conv_b.reshape(1, width), wa, ba.reshape(1, width),
      wx, bx.reshape(1, width), lam.reshape(1, width))


def rglru_layer(x, norm_g, w_in, conv_w, conv_b, wa, ba, wx, bx, lam, w_out, tiles):
    proj = norm_matmul(x, norm_g, w_in.astype(MXU_DTYPE), tm=tiles["proj_tm"], tn=tiles["proj_tn"],
                       out_dtype=F32)
    y = rglru_scan(proj, conv_w, conv_b, wa.astype(MXU_DTYPE), ba, wx.astype(MXU_DTYPE), bx, lam,
                   tt=tiles["scan_tt"])
    return matmul_residual(y, w_out.astype(MXU_DTYPE), x, tm=tiles["res_tm"], tn=tiles["res_tn"])


POOL_HALO = max(POOL_WINDOWS)


def _pool_kernel(x_ref, g_ref, w_ref, b_ref, s_ref, o_ref, halo_ref, *, tt):
    blk = pl.program_id(0)

    @pl.when(blk == 0)
    def _():
        halo_ref[...] = jnp.zeros_like(halo_ref)

    x = x_ref[...]
    h = _rms_norm_rows(x, g_ref[...])
    ext = jnp.concatenate([halo_ref[...], h], axis=0)
    halo_ref[...] = h[tt - POOL_HALO:, :]

    gw = h.shape[1] // len(POOL_WINDOWS)
    t1 = (lax.broadcasted_iota(jnp.int32, (tt, gw), 0) + (blk * tt + 1)).astype(F32)
    for gi, win in enumerate(POOL_WINDOWS):
        cols = slice(gi * gw, (gi + 1) * gw)
        acc = ext[:, cols]
        d = 1
        while d < win:
            acc = acc + pltpu.roll(acc, d, 0)
            d *= 2
        mean = acc[POOL_HALO:, :] / jnp.minimum(t1, float(win))
        y = (mean - h[:, cols]).astype(w_ref.dtype)
        z = jnp.dot(y, w_ref[gi], preferred_element_type=F32) + b_ref[gi:gi + 1, :]
        o_ref[:, cols] = x[:, cols] + z * s_ref[:, cols]


def pool_layer(x, norm_g, w_group, b_group, scale, tiles):
    t, d = x.shape
    tt = tiles["pool_tt"]
    ng, gw, _ = w_group.shape
    need = (4 * _nbytes((tt, d), F32) + 2 * _nbytes((ng, gw, gw), MXU_DTYPE)
            + 6 * _nbytes((tt, d), F32))
    return pl.pallas_call(
        functools.partial(_pool_kernel, tt=tt),
        grid=(t // tt,),
        in_specs=[
            pl.BlockSpec((tt, d), lambda i: (i, 0)),
            pl.BlockSpec((1, d), lambda i: (0, 0)),
            pl.BlockSpec((ng, gw, gw), lambda i: (0, 0, 0)),
            pl.BlockSpec((ng, gw), lambda i: (0, 0)),
            pl.BlockSpec((1, d), lambda i: (0, 0)),
        ],
        out_specs=pl.BlockSpec((tt, d), lambda i: (i, 0)),
        out_shape=jax.ShapeDtypeStruct((t, d), F32),
        scratch_shapes=[pltpu.VMEM((POOL_HALO, d), F32)],
        compiler_params=pltpu.CompilerParams(
            dimension_semantics=("arbitrary",),
            vmem_limit_bytes=_vmem_limit(need)),
        name="pool_mixer",
    )(x, norm_g.reshape(1, d), w_group.astype(MXU_DTYPE), b_group, scale.reshape(1, d))


def _tiles(t):
    big = min(t, 512)
    tall = min(t, 1024)
    return {
        "proj_tm": tall, "proj_tn": 512,
        "res_tm": tall, "res_tn": 512,
        "tq": min(t, 256), "kt": min(4, t // min(t, 256)), "ka": min(4, t // min(t, 256)),
        "scan_tt": min(t, 256),
        "pool_tt": min(t, 256),
        "mlp_tm": big, "mlp_tf": 512,
    }


def kernel(x, positions, attn_norm, attn_w_in, attn_q_norm, attn_k_norm, attn_w_out, rnn_norm, rnn_w_in, rnn_conv_w, rnn_conv_b, rnn_gate_a_w, rnn_gate_a_b, rnn_gate_x_w, rnn_gate_x_b, rnn_lambda, rnn_w_out, pool_norm, pool_w, pool_b, pool_scale, mlp_norm, mlp_w_up, mlp_w_down):
    batch, t, d = x.shape
    depth = mlp_norm.shape[0]
    tiles = _tiles(t)
    w_up_all = mlp_w_up.astype(MXU_DTYPE)
    w_down_all = mlp_w_down.astype(MXU_DTYPE)
    outs = []
    for bi in range(batch):
        xb = x[bi]
        pos = positions[bi]
        for i in range(depth):
            kind, j = i % N_MIXERS, i // N_MIXERS
            if kind == 0:
                xb = dsa_layer(xb, pos, attn_norm[j], attn_w_in, j, attn_q_norm[j], attn_k_norm[j],
                               attn_w_out[j], tiles)
            elif kind == 1:
                xb = rglru_layer(xb, rnn_norm[j], rnn_w_in[j], rnn_conv_w[j], rnn_conv_b[j],
                                 rnn_gate_a_w[j], rnn_gate_a_b[j], rnn_gate_x_w[j], rnn_gate_x_b[j],
                                 rnn_lambda[j], rnn_w_out[j], tiles)
            else:
                xb = pool_layer(xb, pool_norm[j], pool_w[j], pool_b[j], pool_scale[j], tiles)
            xb = mlp_block(xb, mlp_norm[i], w_up_all, w_down_all, i,
                           tm=tiles["mlp_tm"], tf=tiles["mlp_tf"])
        outs.append(xb)
    return outs[0][None] if batch == 1 else jnp.stack(outs, axis=0)
```

```python
import functools
import math

import jax
import jax.numpy as jnp
from jax import lax
from jax.experimental import pallas as pl
from jax.experimental.pallas import tpu as pltpu

F32 = jnp.float32
MXU_DTYPE = jnp.bfloat16

N_MIXERS = 3
EPS = 1e-6
ROPE_THETA = 10000.0
HEAD_DIM = 128
N_HEADS = 16
N_KV_HEADS = 4
IDX_HEADS = 16
IDX_DIM = 128
IDX_ROPE_DIM = 64
TOPK_MAX = 256
CONV_WIDTH = 4
LRU_C = 8.0
POOL_WINDOWS = (2, 4, 8, 16)

LANES = 128
SUBLANES = 8
VMEM_BYTES_V7X = 64 * 1024 * 1024
VMEM_CAP_BYTES = VMEM_BYTES_V7X - 8 * 1024 * 1024

INT_MIN = -(2 ** 31)
NEG_BIG = -1e30


def _vmem_limit(block_bytes):
    return int(min(VMEM_CAP_BYTES, block_bytes * 3 // 2 + (4 << 20)))


def _nbytes(shape, dtype):
    return math.prod(shape) * jnp.dtype(dtype).itemsize


def _rms_norm_rows(x, g):
    ms = jnp.mean(x * x, axis=-1, keepdims=True)
    return x * lax.rsqrt(ms + EPS) * g


def _tile_lanes(x, width):
    reps = width // LANES
    return x if reps == 1 else jnp.concatenate([x] * reps, axis=1)


def _norm_matmul_kernel(x_ref, g_ref, w_ref, o_ref, h_ref):
    @pl.when(pl.program_id(1) == 0)
    def _():
        h_ref[...] = _rms_norm_rows(x_ref[...], g_ref[...]).astype(h_ref.dtype)

    o_ref[...] = jnp.dot(h_ref[...], w_ref[...], preferred_element_type=F32).astype(o_ref.dtype)


def norm_matmul(x, g, w, *, tm, tn, out_dtype):
    t, d = x.shape
    n = w.shape[1]
    need = (2 * _nbytes((tm, d), F32) + 2 * _nbytes((d, tn), w.dtype)
            + 2 * _nbytes((tm, tn), out_dtype) + _nbytes((tm, d), w.dtype))
    return pl.pallas_call(
        _norm_matmul_kernel,
        grid=(t // tm, n // tn),
        in_specs=[
            pl.BlockSpec((tm, d), lambda i, j: (i, 0)),
            pl.BlockSpec((1, d), lambda i, j: (0, 0)),
            pl.BlockSpec((d, tn), lambda i, j: (0, j)),
        ],
        out_specs=pl.BlockSpec((tm, tn), lambda i, j: (i, j)),
        out_shape=jax.ShapeDtypeStruct((t, n), out_dtype),
        scratch_shapes=[pltpu.VMEM((tm, d), w.dtype)],
        compiler_params=pltpu.CompilerParams(
            dimension_semantics=("parallel", "arbitrary"),
            vmem_limit_bytes=_vmem_limit(need)),
        name="norm_matmul",
    )(x, g.reshape(1, d), w)


def _matmul_residual_kernel(a_ref, w_ref, x_ref, o_ref):
    o_ref[...] = x_ref[...] + jnp.dot(a_ref[...], w_ref[...], preferred_element_type=F32)


def matmul_residual(a, w, x, *, tm, tn):
    t, k = a.shape
    n = w.shape[1]
    need = (2 * _nbytes((tm, k), a.dtype) + 2 * _nbytes((k, tn), w.dtype)
            + 4 * _nbytes((tm, tn), F32))
    return pl.pallas_call(
        _matmul_residual_kernel,
        grid=(t // tm, n // tn),
        in_specs=[
            pl.BlockSpec((tm, k), lambda i, j: (i, 0)),
            pl.BlockSpec((k, tn), lambda i, j: (0, j)),
            pl.BlockSpec((tm, tn), lambda i, j: (i, j)),
        ],
        out_specs=pl.BlockSpec((tm, tn), lambda i, j: (i, j)),
        out_shape=jax.ShapeDtypeStruct((t, n), F32),
        compiler_params=pltpu.CompilerParams(
            dimension_semantics=("parallel", "arbitrary"),
            vmem_limit_bytes=_vmem_limit(need)),
        name="matmul_residual",
    )(a, w, x)


def _mlp_kernel(x_ref, g_ref, wu_ref, wd_ref, o_ref, h_ref, acc_ref):
    f = pl.program_id(1)

    @pl.when(f == 0)
    def _():
        h_ref[...] = _rms_norm_rows(x_ref[...], g_ref[...]).astype(h_ref.dtype)
        acc_ref[...] = jnp.zeros_like(acc_ref)

    u = jnp.dot(h_ref[...], wu_ref[...], preferred_element_type=F32)
    u = jnp.square(jnp.maximum(u, 0.0)).astype(wd_ref.dtype)
    acc_ref[...] += jnp.dot(u, wd_ref[...], preferred_element_type=F32)

    @pl.when(f == pl.num_programs(1) - 1)
    def _():
        o_ref[...] = x_ref[...] + acc_ref[...]


def mlp_block(x, g, w_up, w_down, layer, *, tm, tf):
    t, d = x.shape
    ff = w_up.shape[2]
    need = (4 * _nbytes((tm, d), F32) + 2 * _nbytes((d, tf), w_up.dtype)
            + 2 * _nbytes((tf, d), w_down.dtype) + _nbytes((tm, d), w_up.dtype)
            + _nbytes((tm, d), F32) + 2 * _nbytes((tm, tf), F32))
    return pl.pallas_call(
        _mlp_kernel,
        grid=(t // tm, ff // tf),
        in_specs=[
            pl.BlockSpec((tm, d), lambda i, f: (i, 0)),
            pl.BlockSpec((1, d), lambda i, f: (0, 0)),
            pl.BlockSpec((None, d, tf), lambda i, f: (layer, 0, f)),
            pl.BlockSpec((None, tf, d), lambda i, f: (layer, f, 0)),
        ],
        out_specs=pl.BlockSpec((tm, d), lambda i, f: (i, 0)),
        out_shape=jax.ShapeDtypeStruct((t, d), F32),
        scratch_shapes=[pltpu.VMEM((tm, d), w_up.dtype), pltpu.VMEM((tm, d), F32)],
        compiler_params=pltpu.CompilerParams(
            dimension_semantics=("parallel", "arbitrary"),
            vmem_limit_bytes=_vmem_limit(need)),
        name="mlp_block",
    )(x, g.reshape(1, d), w_up, w_down)


DSA_TN = 4 * LANES
Q_COLS = N_HEADS * HEAD_DIM
IQ_COLS = IDX_HEADS * IDX_DIM
KV_COLS = N_KV_HEADS * HEAD_DIM
DSA_COLS = Q_COLS + IQ_COLS + 2 * KV_COLS + DSA_TN
Q_BLOCKS = Q_COLS // DSA_TN
IQ_BLOCKS = IQ_COLS // DSA_TN
KV_BLOCKS = KV_COLS // DSA_TN


W_K_FIRST = Q_BLOCKS
W_V_FIRST = W_K_FIRST + KV_BLOCKS
W_IQ_FIRST = W_V_FIRST + KV_BLOCKS
W_LAST = W_IQ_FIRST + IQ_BLOCKS
OUT_K_FIRST = Q_BLOCKS + IQ_BLOCKS
OUT_V_FIRST = OUT_K_FIRST + KV_BLOCKS


def _dsa_out_block(jw):
    return jnp.where(jw < W_K_FIRST, jw,
                     jnp.where(jw < W_IQ_FIRST, jw + (OUT_K_FIRST - W_K_FIRST),
                               jnp.where(jw < W_LAST, jw - (W_IQ_FIRST - Q_BLOCKS), W_LAST)))


def _dsa_proj_kernel(x_ref, g_ref, w_ref, qn_ref, kn_ref, rc_ref, rs_ref,
                     ic_ref, ia_ref, ib_ref, o_ref, iw_ref, vt_ref, h_ref, y_ref):
    j = pl.program_id(1)
    jp = j - 1

    @pl.when(j == 0)
    def _():
        h_ref[...] = _rms_norm_rows(x_ref[...], g_ref[...]).astype(h_ref.dtype)

    def rope(z):
        return z * rc_ref[...] + pltpu.roll(z, HEAD_DIM // 2, 1) * rs_ref[...]

    def idx_rope(z):
        half = IDX_ROPE_DIM // 2
        return (z * ic_ref[...] + pltpu.roll(z, LANES - half, 1) * ia_ref[...]
                + pltpu.roll(z, half, 1) * ib_ref[...])

    def per_head(fn):
        def finish(y):
            o_ref[...] = jnp.concatenate(
                [fn(y[:, s * LANES:(s + 1) * LANES]) for s in range(DSA_TN // LANES)],
                axis=1).astype(o_ref.dtype)
        return finish

    def step(finish, multiply=True):
        y_prev = y_ref[...]
        if multiply:
            y_ref[...] = lax.dot_general(
                h_ref[...], w_ref[...].astype(h_ref.dtype), (((1,), (1,)), ((), ())),
                preferred_element_type=F32)
        if finish is not None:
            finish(y_prev)

    @pl.when(j == 0)
    def _():
        step(None)

    @pl.when((jp >= 0) & (jp < W_K_FIRST))
    def _():
        scale = HEAD_DIM ** -0.5 * math.log2(math.e)
        step(per_head(lambda z: rope(_rms_norm_rows(z, qn_ref[...])) * scale))

    @pl.when((jp >= W_K_FIRST) & (jp < W_V_FIRST))
    def _():
        step(per_head(lambda z: rope(_rms_norm_rows(z, kn_ref[...]))))

    @pl.when((jp >= W_V_FIRST) & (jp < W_IQ_FIRST))
    def _():
        def finish(y):
            o_ref[...] = y.astype(o_ref.dtype)
            vt_ref[...] = jnp.transpose(y).astype(vt_ref.dtype)
        step(finish)

    @pl.when((jp >= W_IQ_FIRST) & (jp < W_LAST))
    def _():
        step(per_head(idx_rope))

    @pl.when(jp == W_LAST)
    def _():
        def finish(y):
            o_ref[...] = jnp.concatenate(
                [idx_rope(y[:, :LANES]), jnp.zeros_like(y[:, LANES:])], axis=1).astype(o_ref.dtype)
            lane = lax.broadcasted_iota(jnp.int32, (y.shape[0], LANES), 1)
            iw_ref[...] = jnp.where(lane < IDX_HEADS,
                                    y[:, LANES:2 * LANES] * (IDX_HEADS ** -0.5 * IDX_DIM ** -0.5), 0.0)
        step(finish, multiply=False)


def dsa_projection(x, g, w_layers, layer, q_norm, k_norm, tables, *, tm):
    t, d = x.shape
    assert w_layers.shape[1] == Q_COLS + 2 * KV_COLS + IQ_COLS + IDX_DIM + IDX_HEADS
    rc, rs, ic, ia, ib = tables
    tab_spec = pl.BlockSpec((tm, LANES), lambda i, j: (i, 0))
    vec_spec = pl.BlockSpec((1, LANES), lambda i, j: (0, 0))
    need = (2 * _nbytes((tm, d), F32) + 3 * _nbytes((d, DSA_TN), F32)
            + 2 * _nbytes((tm, DSA_TN), MXU_DTYPE) + _nbytes((tm, d), MXU_DTYPE)
            + 12 * _nbytes((tm, LANES), F32) + 3 * _nbytes((tm, DSA_TN), F32))
    return pl.pallas_call(
        _dsa_proj_kernel,
        grid=(t // tm, W_LAST + 2),
        in_specs=[
            pl.BlockSpec((tm, d), lambda i, j: (i, 0)),
            pl.BlockSpec((1, d), lambda i, j: (0, 0)),
            pl.BlockSpec((None, DSA_TN, d), lambda i, j: (layer, jnp.minimum(j, W_LAST), 0)),
            vec_spec, vec_spec, tab_spec, tab_spec, tab_spec, tab_spec, tab_spec,
        ],
        out_specs=[
            pl.BlockSpec((tm, DSA_TN), lambda i, j: (i, _dsa_out_block(jnp.maximum(j - 1, 0)))),
            pl.BlockSpec((tm, LANES), lambda i, j: (i, 0)),
            pl.BlockSpec((DSA_TN, tm), lambda i, j: (jnp.clip(j - 1 - W_V_FIRST, 0, KV_BLOCKS - 1), i)),
        ],
        out_shape=[
            jax.ShapeDtypeStruct((t, DSA_COLS), MXU_DTYPE),
            jax.ShapeDtypeStruct((t, LANES), F32),
            jax.ShapeDtypeStruct((KV_COLS, t), MXU_DTYPE),
        ],
        scratch_shapes=[pltpu.VMEM((tm, d), MXU_DTYPE), pltpu.VMEM((tm, DSA_TN), F32)],
        compiler_params=pltpu.CompilerParams(
            dimension_semantics=("parallel", "arbitrary"),
            vmem_limit_bytes=_vmem_limit(need)),
        name="dsa_projection",
    )(x, g.reshape(1, d), w_layers, q_norm.reshape(1, LANES), k_norm.reshape(1, LANES), rc, rs, ic, ia, ib)


def _float_key(v):
    b = lax.bitcast_convert_type(v, jnp.int32)
    return b ^ ((b >> 31) & 0x7FFFFFFF)


def _key_float(k):
    return lax.bitcast_convert_type(k ^ ((k >> 31) & 0x7FFFFFFF), F32)


_KLO, _KHI, _CLO, _CHI, _JCUT = range(5)
_SEARCH_CAP = 96
ONES_ROWS = 2 * SUBLANES
ATTN_PAD_FROM = 2
ATTN_LOOKAHEAD = 1


def _dsa_core_kernel(q_ref, iq_ref, iw_ref, k_ref, vt_ref, ik_ref, o_ref,
                     key_ref, bias_ref, s_ref, wt_ref, st_ref, m_ref, l_ref, acc_ref, *, tq, topk, kt, ka):
    i = pl.program_id(0)
    nc = i + 1
    group = N_HEADS // N_KV_HEADS
    nt = (((1,), (1,)), ((), ()))

    wt_ref[...] = jnp.transpose(iw_ref[...])

    krow = lax.broadcasted_iota(jnp.int32, (tq, tq), 0)
    qcol = lax.broadcasted_iota(jnp.int32, (tq, tq), 1)
    diag_causal = krow <= qcol

    def chunk_start(c):
        return pl.multiple_of(c * tq, tq)

    def over_chunks(body, k, init, pad_from=None):
        n_tiles = (nc + k - min(pad_from or k, k)) // k
        carry = lax.fori_loop(0, n_tiles, lambda ti, c: body(ti * k, k, c), init)
        if k > 1:
            carry = lax.fori_loop(n_tiles * k, nc, lambda c0, c: body(c0, 1, c), carry)
        return carry

    def index_tile(first, k, carry):
        smin, smax = carry
        iks = [ik_ref[pl.ds(chunk_start(first + u), tq), :] for u in range(k)]
        scores = [jnp.zeros((tq, tq), F32) for _ in range(k)]
        for h in range(IDX_HEADS):
            iqh = iq_ref[:, h * IDX_DIM:(h + 1) * IDX_DIM]
            w = jnp.broadcast_to(wt_ref[h:h + 1, :], (tq, tq))
            for u in range(k):
                logits = lax.dot_general(iks[u], iqh, nt, preferred_element_type=F32)
                scores[u] = scores[u] + w * jnp.maximum(logits, 0.0)
        for u in range(k):
            c = first + u
            valid = (c < i) | diag_causal
            key_ref[c] = jnp.where(valid, _float_key(scores[u]), INT_MIN)
            smin = jnp.minimum(smin, jnp.min(jnp.where(valid, scores[u], jnp.inf), axis=0, keepdims=True))
            smax = jnp.maximum(smax, jnp.max(jnp.where(valid, scores[u], -jnp.inf), axis=0, keepdims=True))
        return smin, smax

    smin, smax = over_chunks(
        index_tile, kt, (jnp.full((1, tq), jnp.inf, F32), jnp.full((1, tq), -jnp.inf, F32)))

    def mask_chunk(c, carry):
        key_ref[c] = jnp.full((tq, tq), INT_MIN, jnp.int32)
        return carry

    lax.fori_loop(nc, (nc + ka - min(ATTN_PAD_FROM, ka)) // ka * ka, mask_chunk, 0)

    def count_ge(cand):
        cand_b = jnp.broadcast_to(cand, (tq, tq))

        def body(first, k, cnt):
            for u in range(k):
                hit = jnp.where(key_ref[first + u] >= cand_b, 1.0, 0.0)
                cnt = cnt + jnp.sum(hit, axis=0, keepdims=True)
            return cnt

        return over_chunks(body, kt, jnp.zeros((1, tq), F32))

    st_ref[_KLO] = _float_key(smin)
    st_ref[_KHI] = _float_key(smax) + 1
    st_ref[_CLO] = lax.broadcasted_iota(jnp.int32, (1, tq), 1) + (i * tq + 1)
    st_ref[_CHI] = jnp.zeros((1, tq), jnp.int32)

    def unfinished(klo, khi, clo):
        return (clo > topk) & ((khi - klo) != 1)

    def search_cond(st):
        step, pending = st
        return (step < _SEARCH_CAP) & (pending > 0)

    def any_lane(mask):
        return jnp.max(jnp.where(mask, 1.0, 0.0))

    def search_body(st):
        step, _ = st
        klo, khi, clo = st_ref[_KLO], st_ref[_KHI], st_ref[_CLO]
        live = unfinished(klo, khi, clo)
        pending = any_lane(live)
        vmid = 0.5 * _key_float(klo) + 0.5 * _key_float(khi)
        cmid = _float_key(vmid)
        kmid = klo + lax.shift_right_logical(khi - klo, 1)
        cand = jnp.where((cmid > klo) & (cmid < khi), cmid, kmid)
        cand = jnp.where(live, cand, klo)
        cnt = count_ge(cand).astype(jnp.int32)
        take = cnt >= topk
        klo = jnp.where(live & take, cand, klo)
        clo = jnp.where(live & take, cnt, clo)
        drop = live & jnp.logical_not(take)
        st_ref[_KLO], st_ref[_KHI], st_ref[_CLO] = klo, jnp.where(drop, cand, khi), clo
        st_ref[_CHI] = jnp.where(drop, cnt, st_ref[_CHI])
        return step + 1, pending

    lax.while_loop(search_cond, search_body, (jnp.int32(0), jnp.float32(1.0)))
    thr_b = jnp.broadcast_to(st_ref[_KLO], (tq, tq))

    n_keys = key_ref.shape[0] * tq
    tied = st_ref[_CLO] > topk
    has_ties = any_lane(tied) > 0
    st_ref[_JCUT] = jnp.full((1, tq), n_keys, jnp.int32)

    @pl.when(has_ties)
    def _():
        want = topk - st_ref[_CHI]

        def count_tied_upto(pos):
            pos_b = jnp.broadcast_to(pos, (tq, tq))

            def body(c, cnt):
                hit = (key_ref[c] == thr_b) & (krow + c * tq <= pos_b)
                return cnt + jnp.sum(jnp.where(hit, 1.0, 0.0), axis=0, keepdims=True)

            return lax.fori_loop(0, nc, body, jnp.zeros((1, tq), F32)).astype(jnp.int32)

        def cut_step(_, st):
            lo, hi = st
            mid = lo + ((hi - lo) >> 1)
            enough = count_tied_upto(mid) >= want
            return jnp.where(enough, lo, mid), jnp.where(enough, mid, hi)

        _, cut = lax.fori_loop(0, n_keys.bit_length(), cut_step,
                               (jnp.full((1, tq), -1, jnp.int32), jnp.full((1, tq), n_keys - 1, jnp.int32)))
        st_ref[_JCUT] = jnp.where(tied, cut, n_keys)

    cut_b = jnp.broadcast_to(st_ref[_JCUT], (tq, tq))

    m_ref[...] = jnp.full(m_ref.shape, NEG_BIG, F32)
    l_ref[...] = jnp.zeros(l_ref.shape, F32)
    acc_ref[...] = jnp.zeros(acc_ref.shape, F32)

    def attend_tile(first, k, carry, *, with_ties):
        for u in range(k):
            c = first + u
            key = key_ref[c]
            if with_ties:
                keep = (key > thr_b) | ((key == thr_b) & (krow + c * tq <= cut_b))
            else:
                keep = key >= thr_b
            bias_ref[u] = jnp.where(keep, 0.0, NEG_BIG)

        starts = [chunk_start(first + u) for u in range(k)]
        slots = ATTN_LOOKAHEAD + 1
        dyn0 = jnp.minimum(first, 0)
        ones_rows = jnp.ones((ONES_ROWS, tq), vt_ref.dtype)

        def logits(h):
            cols = slice(h // group * HEAD_DIM, (h // group + 1) * HEAD_DIM)
            qh = q_ref[:, h * HEAD_DIM:(h + 1) * HEAD_DIM]
            top = None
            for u in range(k):
                s = lax.dot_general(k_ref[pl.ds(starts[u], tq), cols], qh, nt,
                                    preferred_element_type=F32) + bias_ref[u]
                s_ref[h % slots + dyn0, u] = s
                top = s if top is None else jnp.maximum(top, s)
            return jnp.max(top, axis=0, keepdims=True)

        queued = [logits(h) for h in range(ATTN_LOOKAHEAD)]
        for h in range(N_HEADS):
            tile_max = queued.pop(0)
            if h + ATTN_LOOKAHEAD < N_HEADS:
                queued.append(logits(h + ATTN_LOOKAHEAD))
            cols = slice(h // group * HEAD_DIM, (h // group + 1) * HEAD_DIM)
            m_prev = m_ref[h]
            m_cur = jnp.maximum(m_prev, tile_max)
            alpha = jnp.exp2(m_prev - m_cur)
            pv = jnp.zeros((HEAD_DIM + ONES_ROWS, tq), F32)
            for u in range(k):
                p = jnp.exp2(s_ref[h % slots + dyn0, u] - m_cur)
                v_aug = jnp.concatenate([vt_ref[cols, pl.ds(starts[u], tq)], ones_rows], axis=0)
                pv = pv + jnp.dot(v_aug, p.astype(vt_ref.dtype), preferred_element_type=F32)
            l_ref[h] = alpha * l_ref[h] + pv[HEAD_DIM:HEAD_DIM + 1, :]
            pv = pv[:HEAD_DIM, :]
            acc_ref[h] = alpha * acc_ref[h] + pv
            m_ref[h] = m_cur
        return carry

    @pl.when(jnp.logical_not(has_ties))
    def _():
        over_chunks(functools.partial(attend_tile, with_ties=False), ka, 0, pad_from=ATTN_PAD_FROM)

    @pl.when(has_ties)
    def _():
        over_chunks(functools.partial(attend_tile, with_ties=True), 1, 0)

    for h in range(N_HEADS):
        o_ref[:, h * HEAD_DIM:(h + 1) * HEAD_DIM] = jnp.transpose(acc_ref[h] / l_ref[h]).astype(o_ref.dtype)


def dsa_core(proj, iw, vt, *, tq, topk, kt, ka):
    t = proj.shape[0]
    assert (t // tq) % kt == 0 and kt % ka == 0
    k_block = (Q_COLS + IQ_COLS) // KV_COLS
    ik_block = (Q_COLS + IQ_COLS + 2 * KV_COLS) // IDX_DIM
    resident = dict(pipeline_mode=pl.Buffered(1))
    need = (4 * _nbytes((tq, Q_COLS), proj.dtype) + 2 * _nbytes((tq, LANES), F32)
            + 2 * _nbytes((t, KV_COLS), proj.dtype) + _nbytes((t, IDX_DIM), proj.dtype)
            + 2 * _nbytes((tq, Q_COLS), proj.dtype)
            + _nbytes((t // tq, tq, tq), jnp.int32) + _nbytes((kt, tq, tq), F32)
            + _nbytes((LANES, tq), F32) + (3 + 2 * N_HEADS) * _nbytes((SUBLANES, tq), F32)
            + _nbytes((N_HEADS, HEAD_DIM, tq), F32)
            + (ATTN_LOOKAHEAD + 3) * _nbytes((ka, tq, tq), F32))
    return pl.pallas_call(
        functools.partial(_dsa_core_kernel, tq=tq, topk=topk, kt=kt, ka=ka),
        grid=(t // tq,),
        in_specs=[
            pl.BlockSpec((tq, Q_COLS), lambda i: (i, 0)),
            pl.BlockSpec((tq, IQ_COLS), lambda i: (i, 1)),
            pl.BlockSpec((tq, LANES), lambda i: (i, 0)),
            pl.BlockSpec((t, KV_COLS), lambda i: (0, k_block), **resident),
            pl.BlockSpec((KV_COLS, t), lambda i: (0, 0), **resident),
            pl.BlockSpec((t, IDX_DIM), lambda i: (0, ik_block), **resident),
        ],
        out_specs=pl.BlockSpec((tq, Q_COLS), lambda i: (i, 0)),
        out_shape=jax.ShapeDtypeStruct((t, Q_COLS), proj.dtype),
        scratch_shapes=[
            pltpu.VMEM((t // tq, tq, tq), jnp.int32),
            pltpu.VMEM((ka, tq, tq), F32),
            pltpu.VMEM((ATTN_LOOKAHEAD + 1, ka, tq, tq), F32),
            pltpu.VMEM((LANES, tq), F32),
            pltpu.VMEM((5, 1, tq), jnp.int32),
            pltpu.VMEM((N_HEADS, 1, tq), F32),
            pltpu.VMEM((N_HEADS, 1, tq), F32),
            pltpu.VMEM((N_HEADS, HEAD_DIM, tq), F32),
        ],
        compiler_params=pltpu.CompilerParams(
            dimension_semantics=("arbitrary",),
            vmem_limit_bytes=_vmem_limit(need)),
        name="dsa_core",
    )(proj, proj, iw, proj, vt, proj)


def _rope_tables(pos):
    def angles(dim):
        inv = 1.0 / (ROPE_THETA ** (jnp.arange(0, dim, 2, dtype=F32) / dim))
        return pos.astype(F32)[:, None] * inv

    ang = angles(HEAD_DIM)
    cos, sin = jnp.cos(ang), jnp.sin(ang)
    rc = jnp.concatenate([cos, cos], axis=1)
    rs = jnp.concatenate([-sin, sin], axis=1)
    iang = angles(IDX_ROPE_DIM)
    icos, isin = jnp.cos(iang), jnp.sin(iang)
    rest = LANES - IDX_ROPE_DIM
    zeros = jnp.zeros_like(isin)
    ic = jnp.concatenate([icos, icos, jnp.ones((pos.shape[0], rest), F32)], axis=1)
    ia = jnp.concatenate([-isin, zeros, jnp.zeros((pos.shape[0], rest), F32)], axis=1)
    ib = jnp.concatenate([zeros, isin, jnp.zeros((pos.shape[0], rest), F32)], axis=1)
    return rc, rs, ic, ia, ib


def dsa_layer(x, pos, norm_g, w_in_layers, layer, q_norm, k_norm, w_out, tiles):
    t = x.shape[0]
    proj, iw, vt = dsa_projection(x, norm_g, jnp.swapaxes(w_in_layers, 1, 2), layer, q_norm, k_norm,
                                  _rope_tables(pos), tm=tiles["proj_tm"])
    attn = dsa_core(proj, iw, vt, tq=tiles["tq"], topk=min(TOPK_MAX, t // 4), kt=tiles["kt"], ka=tiles["ka"])
    return matmul_residual(attn, w_out.astype(MXU_DTYPE), x, tm=tiles["res_tm"], tn=tiles["res_tn"])


def _rglru_kernel(gate_ref, xr_ref, cw_ref, cb_ref, wa_ref, ba_ref, wx_ref, bx_ref, lam_ref,
                  y_ref, hcar_ref, xprev_ref, *, tt):
    @pl.when(pl.program_id(1) == 0)
    def _():
        hcar_ref[...] = jnp.zeros_like(hcar_ref)
        xprev_ref[...] = jnp.zeros_like(xprev_ref)

    xr = xr_ref[...]
    ext = jnp.concatenate([xprev_ref[...], xr], axis=0)
    cw = cw_ref[...]
    xc = cb_ref[...] + xr * cw[CONV_WIDTH - 1:CONV_WIDTH, :]
    for d in range(1, CONV_WIDTH):
        xc = xc + pltpu.roll(ext, d, 0)[SUBLANES:, :] * cw[CONV_WIDTH - 1 - d:CONV_WIDTH - d, :]
    xprev_ref[...] = xr[tt - SUBLANES:, :]

    xcb = xc.astype(wa_ref.dtype)
    r = jax.nn.sigmoid(jnp.dot(xcb, wa_ref[...], preferred_element_type=F32) + ba_ref[...])
    ig = jax.nn.sigmoid(jnp.dot(xcb, wx_ref[...], preferred_element_type=F32) + bx_ref[...])
    nlam = -lam_ref[...]
    softplus = jnp.maximum(nlam, 0.0) + jnp.log(1.0 + jnp.exp(-jnp.abs(nlam)))
    log_a = -LRU_C * r * softplus
    a = jnp.exp(log_a)
    mult = jnp.sqrt(1.0 - jnp.exp(2.0 * log_a))
    b = xc * ig * mult

    n_groups = tt // SUBLANES
    a = a.reshape(n_groups, SUBLANES, a.shape[1])
    b = b.reshape(a.shape)
    in_group = lax.broadcasted_iota(jnp.int32, a.shape, 1)
    d = 1
    while d < SUBLANES:
        b = a * jnp.where(in_group >= d, pltpu.roll(b, d, 1), 0.0) + b
        a = a * jnp.where(in_group >= d, pltpu.roll(a, d, 1), 1.0)
        d *= 2
    carry = hcar_ref[...]
    groups = []
    for gi in range(n_groups):
        hg = b[gi] + a[gi] * carry
        carry = hg[SUBLANES - 1:SUBLANES, :]
        groups.append(hg)
    hcar_ref[...] = carry
    h = jnp.concatenate(groups, axis=0)
    y_ref[...] = (h * jax.nn.gelu(gate_ref[...])).astype(y_ref.dtype)


def rglru_scan(proj, conv_w, conv_b, wa, ba, wx, bx, lam, *, tt):
    t = proj.shape[0]
    width = conv_w.shape[1]
    nb, blk, _ = wa.shape
    vec = lambda: pl.BlockSpec((1, blk), lambda n, s: (0, n))
    need = (4 * _nbytes((tt, blk), F32) + 2 * _nbytes((tt, blk), MXU_DTYPE)
            + 4 * _nbytes((blk, blk), wa.dtype) + 24 * _nbytes((tt, blk), F32))
    return pl.pallas_call(
        functools.partial(_rglru_kernel, tt=tt),
        grid=(nb, t // tt),
        in_specs=[
            pl.BlockSpec((tt, blk), lambda n, s: (s, n)),
            pl.BlockSpec((tt, blk), lambda n, s: (s, nb + n)),
            pl.BlockSpec((CONV_WIDTH, blk), lambda n, s: (0, n)),
            vec(),
            pl.BlockSpec((None, blk, blk), lambda n, s: (n, 0, 0)),
            vec(),
            pl.BlockSpec((None, blk, blk), lambda n, s: (n, 0, 0)),
            vec(),
            vec(),
        ],
        out_specs=pl.BlockSpec((tt, blk), lambda n, s: (s, n)),
        out_shape=jax.ShapeDtypeStruct((t, width), MXU_DTYPE),
        scratch_shapes=[pltpu.VMEM((1, blk), F32), pltpu.VMEM((SUBLANES, blk), F32)],
        compiler_params=pltpu.CompilerParams(
            dimension_semantics=("parallel", "arbitrary"),
            vmem_limit_bytes=_vmem_limit(need)),
        name="rglru_scan",
    )(proj, proj, conv_w, conv_b.reshape(1, width), wa, ba.reshape(1, width),
      wx, bx.reshape(1, width), lam.reshape(1, width))


def rglru_layer(x, norm_g, w_in, conv_w, conv_b, wa, ba, wx, bx, lam, w_out, tiles):
    proj = norm_matmul(x, norm_g, w_in.astype(MXU_DTYPE), tm=tiles["proj_tm"], tn=tiles["proj_tn"],
                       out_dtype=F32)
    y = rglru_scan(proj, conv_w, conv_b, wa.astype(MXU_DTYPE), ba, wx.astype(MXU_DTYPE), bx, lam,
                   tt=tiles["scan_tt"])
    return matmul_residual(y, w_out.astype(MXU_DTYPE), x, tm=tiles["res_tm"], tn=tiles["res_tn"])


POOL_HALO = max(POOL_WINDOWS)


def _pool_kernel(x_ref, g_ref, w_ref, b_ref, s_ref, o_ref, halo_ref, *, tt):
    blk = pl.program_id(0)

    @pl.when(blk == 0)
    def _():
        halo_ref[...] = jnp.zeros_like(halo_ref)

    x = x_ref[...]
    h = _rms_norm_rows(x, g_ref[...])
    ext = jnp.concatenate([halo_ref[...], h], axis=0)
    halo_ref[...] = h[tt - POOL_HALO:, :]

    gw = h.shape[1] // len(POOL_WINDOWS)
    t1 = (lax.broadcasted_iota(jnp.int32, (tt, gw), 0) + (blk * tt + 1)).astype(F32)
    for gi, win in enumerate(POOL_WINDOWS):
        cols = slice(gi * gw, (gi + 1) * gw)
        acc = ext[:, cols]
        d = 1
        while d < win:
            acc = acc + pltpu.roll(acc, d, 0)
            d *= 2
        mean = acc[POOL_HALO:, :] / jnp.minimum(t1, float(win))
        y = (mean - h[:, cols]).astype(w_ref.dtype)
        z = jnp.dot(y, w_ref[gi], preferred_element_type=F32) + b_ref[gi:gi + 1, :]
        o_ref[:, cols] = x[:, cols] + z * s_ref[:, cols]


def pool_layer(x, norm_g, w_group, b_group, scale, tiles):
    t, d = x.shape
    tt = tiles["pool_tt"]
    ng, gw, _ = w_group.shape
    need = (4 * _nbytes((tt, d), F32) + 2 * _nbytes((ng, gw, gw), MXU_DTYPE)
            + 6 * _nbytes((tt, d), F32))
    return pl.pallas_call(
        functools.partial(_pool_kernel, tt=tt),
        grid=(t // tt,),
        in_specs=[
            pl.BlockSpec((tt, d), lambda i: (i, 0)),
            pl.BlockSpec((1, d), lambda i: (0, 0)),
            pl.BlockSpec((ng, gw, gw), lambda i: (0, 0, 0)),
            pl.BlockSpec((ng, gw), lambda i: (0, 0)),
            pl.BlockSpec((1, d), lambda i: (0, 0)),
        ],
        out_specs=pl.BlockSpec((tt, d), lambda i: (i, 0)),
        out_shape=jax.ShapeDtypeStruct((t, d), F32),
        scratch_shapes=[pltpu.VMEM((POOL_HALO, d), F32)],
        compiler_params=pltpu.CompilerParams(
            dimension_semantics=("arbitrary",),
            vmem_limit_bytes=_vmem_limit(need)),
        name="pool_mixer",
    )(x, norm_g.reshape(1, d), w_group.astype(MXU_DTYPE), b_group, scale.reshape(1, d))


def _tiles(t):
    big = min(t, 512)
    tall = min(t, 1024)
    return {
        "proj_tm": tall, "proj_tn": 1024,
        "res_tm": tall, "res_tn": 1024,
        "tq": min(t, 256), "kt": min(4, t // min(t, 256)), "ka": min(4, t // min(t, 256)),
        "scan_tt": min(t, 512),
        "pool_tt": min(t, 256),
        "mlp_tm": big, "mlp_tf": 1024,
    }


def kernel(x, positions, attn_norm, attn_w_in, attn_q_norm, attn_k_norm, attn_w_out, rnn_norm, rnn_w_in, rnn_conv_w, rnn_conv_b, rnn_gate_a_w, rnn_gate_a_b, rnn_gate_x_w, rnn_gate_x_b, rnn_lambda, rnn_w_out, pool_norm, pool_w, pool_b, pool_scale, mlp_norm, mlp_w_up, mlp_w_down):
    batch, t, d = x.shape
    depth = mlp_norm.shape[0]
    tiles = _tiles(t)
    w_up_all = mlp_w_up.astype(MXU_DTYPE)
    w_down_all = mlp_w_down.astype(MXU_DTYPE)
    outs = []
    for bi in range(batch):
        xb = x[bi]
        pos = positions[bi]
        for i in range(depth):
            kind, j = i % N_MIXERS, i // N_MIXERS
            if kind == 0:
                xb = dsa_layer(xb, pos, attn_norm[j], attn_w_in, j, attn_q_norm[j], attn_k_norm[j],
                               attn_w_out[j], tiles)
            elif kind == 1:
                xb = rglru_layer(xb, rnn_norm[j], rnn_w_in[j], rnn_conv_w[j], rnn_conv_b[j],
                                 rnn_gate_a_w[j], rnn_gate_a_b[j], rnn_gate_x_w[j], rnn_gate_x_b[j],
                                 rnn_lambda[j], rnn_w_out[j], tiles)
            else:
                xb = pool_layer(xb, pool_norm[j], pool_w[j], pool_b[j], pool_scale[j], tiles)
            xb = mlp_block(xb, mlp_norm[i], w_up_all, w_down_all, i,
                           tm=tiles["mlp_tm"], tf=tiles["mlp_tf"])
        outs.append(xb)
    return outs[0][None] if batch == 1 else jnp.stack(outs, axis=0)
```

```python
import functools
import math

import jax
import jax.numpy as jnp
from jax import lax
from jax.experimental import pallas as pl
from jax.experimental.pallas import tpu as pltpu

F32 = jnp.float32
MXU_DTYPE = jnp.bfloat16

N_MIXERS = 3
EPS = 1e-6
ROPE_THETA = 10000.0
HEAD_DIM = 128
N_HEADS = 16
N_KV_HEADS = 4
IDX_HEADS = 16
IDX_DIM = 128
IDX_ROPE_DIM = 64
TOPK_MAX = 256
CONV_WIDTH = 4
LRU_C = 8.0
POOL_WINDOWS = (2, 4, 8, 16)

LANES = 128
SUBLANES = 8
VMEM_BYTES_V7X = 64 * 1024 * 1024
VMEM_CAP_BYTES = VMEM_BYTES_V7X - 8 * 1024 * 1024

INT_MIN = -(2 ** 31)
NEG_BIG = -1e30


def _vmem_limit(block_bytes):
    return int(min(VMEM_CAP_BYTES, block_bytes * 3 // 2 + (4 << 20)))


def _nbytes(shape, dtype):
    return math.prod(shape) * jnp.dtype(dtype).itemsize


def _rms_norm_rows(x, g):
    ms = jnp.mean(x * x, axis=-1, keepdims=True)
    return x * lax.rsqrt(ms + EPS) * g


def _norm_matmul_kernel(x_ref, g_ref, w_ref, o_ref, h_ref):
    @pl.when(pl.program_id(1) == 0)
    def _():
        h_ref[...] = _rms_norm_rows(x_ref[...], g_ref[...]).astype(h_ref.dtype)

    o_ref[...] = jnp.dot(h_ref[...], w_ref[...], preferred_element_type=F32).astype(o_ref.dtype)


def norm_matmul(x, g, w, *, tm, tn, out_dtype):
    t, d = x.shape
    n = w.shape[1]
    need = (2 * _nbytes((tm, d), F32) + 2 * _nbytes((d, tn), w.dtype)
            + 2 * _nbytes((tm, tn), out_dtype) + _nbytes((tm, d), w.dtype))
    return pl.pallas_call(
        _norm_matmul_kernel,
        grid=(t // tm, n // tn),
        in_specs=[
            pl.BlockSpec((tm, d), lambda i, j: (i, 0)),
            pl.BlockSpec((1, d), lambda i, j: (0, 0)),
            pl.BlockSpec((d, tn), lambda i, j: (0, j)),
        ],
        out_specs=pl.BlockSpec((tm, tn), lambda i, j: (i, j)),
        out_shape=jax.ShapeDtypeStruct((t, n), out_dtype),
        scratch_shapes=[pltpu.VMEM((tm, d), w.dtype)],
        compiler_params=pltpu.CompilerParams(
            dimension_semantics=("parallel", "arbitrary"),
            vmem_limit_bytes=_vmem_limit(need)),
        name="norm_matmul",
    )(x, g.reshape(1, d), w)


def _matmul_residual_kernel(a_ref, w_ref, x_ref, o_ref):
    o_ref[...] = x_ref[...] + jnp.dot(a_ref[...], w_ref[...], preferred_element_type=F32)


def matmul_residual(a, w, x, *, tm, tn):
    t, k = a.shape
    n = w.shape[1]
    need = (2 * _nbytes((tm, k), a.dtype) + 2 * _nbytes((k, tn), w.dtype)
            + 4 * _nbytes((tm, tn), F32))
    return pl.pallas_call(
        _matmul_residual_kernel,
        grid=(t // tm, n // tn),
        in_specs=[
            pl.BlockSpec((tm, k), lambda i, j: (i, 0)),
            pl.BlockSpec((k, tn), lambda i, j: (0, j)),
            pl.BlockSpec((tm, tn), lambda i, j: (i, j)),
        ],
        out_specs=pl.BlockSpec((tm, tn), lambda i, j: (i, j)),
        out_shape=jax.ShapeDtypeStruct((t, n), F32),
        compiler_params=pltpu.CompilerParams(
            dimension_semantics=("parallel", "arbitrary"),
            vmem_limit_bytes=_vmem_limit(need)),
        name="matmul_residual",
    )(a, w, x)


def _mlp_kernel(x_ref, g_ref, wu_ref, wd_ref, o_ref, h_ref, acc_ref):
    f = pl.program_id(1)

    @pl.when(f == 0)
    def _():
        h_ref[...] = _rms_norm_rows(x_ref[...], g_ref[...]).astype(h_ref.dtype)
        acc_ref[...] = jnp.zeros_like(acc_ref)

    u = jnp.dot(h_ref[...], wu_ref[...], preferred_element_type=F32)
    u = jnp.square(jnp.maximum(u, 0.0)).astype(wd_ref.dtype)
    acc_ref[...] += jnp.dot(u, wd_ref[...], preferred_element_type=F32)

    @pl.when(f == pl.num_programs(1) - 1)
    def _():
        o_ref[...] = x_ref[...] + acc_ref[...]


def mlp_block(x, g, w_up, w_down, layer, *, tm, tf):
    t, d = x.shape
    ff = w_up.shape[2]
    need = (4 * _nbytes((tm, d), F32) + 2 * _nbytes((d, tf), w_up.dtype)
            + 2 * _nbytes((tf, d), w_down.dtype) + _nbytes((tm, d), w_up.dtype)
            + _nbytes((tm, d), F32) + 2 * _nbytes((tm, tf), F32))
    return pl.pallas_call(
        _mlp_kernel,
        grid=(t // tm, ff // tf),
        in_specs=[
            pl.BlockSpec((tm, d), lambda i, f: (i, 0)),
            pl.BlockSpec((1, d), lambda i, f: (0, 0)),
            pl.BlockSpec((None, d, tf), lambda i, f: (layer, 0, f)),
            pl.BlockSpec((None, tf, d), lambda i, f: (layer, f, 0)),
        ],
        out_specs=pl.BlockSpec((tm, d), lambda i, f: (i, 0)),
        out_shape=jax.ShapeDtypeStruct((t, d), F32),
        scratch_shapes=[pltpu.VMEM((tm, d), w_up.dtype), pltpu.VMEM((tm, d), F32)],
        compiler_params=pltpu.CompilerParams(
            dimension_semantics=("parallel", "arbitrary"),
            vmem_limit_bytes=_vmem_limit(need)),
        name="mlp_block",
    )(x, g.reshape(1, d), w_up, w_down)


DSA_TN = 4 * LANES
Q_COLS = N_HEADS * HEAD_DIM
IQ_COLS = IDX_HEADS * IDX_DIM
KV_COLS = N_KV_HEADS * HEAD_DIM
DSA_COLS = Q_COLS + IQ_COLS + 2 * KV_COLS + DSA_TN
Q_BLOCKS = Q_COLS // DSA_TN
IQ_BLOCKS = IQ_COLS // DSA_TN
KV_BLOCKS = KV_COLS // DSA_TN


W_K_FIRST = Q_BLOCKS
W_V_FIRST = W_K_FIRST + KV_BLOCKS
W_IQ_FIRST = W_V_FIRST + KV_BLOCKS
W_LAST = W_IQ_FIRST + IQ_BLOCKS
OUT_K_FIRST = Q_BLOCKS + IQ_BLOCKS


def _dsa_out_block(jw):
    return jnp.where(jw < W_K_FIRST, jw,
                     jnp.where(jw < W_IQ_FIRST, jw + (OUT_K_FIRST - W_K_FIRST),
                               jnp.where(jw < W_LAST, jw - (W_IQ_FIRST - Q_BLOCKS), W_LAST)))


def _dsa_proj_kernel(x_ref, g_ref, w_ref, qn_ref, kn_ref, rc_ref, rs_ref,
                     ic_ref, ia_ref, ib_ref, o_ref, iw_ref, vt_ref, h_ref, y_ref):
    j = pl.program_id(1)
    jp = j - 1

    @pl.when(j == 0)
    def _():
        h_ref[...] = _rms_norm_rows(x_ref[...], g_ref[...]).astype(h_ref.dtype)

    def rope(z):
        return z * rc_ref[...] + pltpu.roll(z, HEAD_DIM // 2, 1) * rs_ref[...]

    def idx_rope(z):
        half = IDX_ROPE_DIM // 2
        return (z * ic_ref[...] + pltpu.roll(z, LANES - half, 1) * ia_ref[...]
                + pltpu.roll(z, half, 1) * ib_ref[...])

    def per_head(fn):
        def finish(y):
            o_ref[...] = jnp.concatenate(
                [fn(y[:, s * LANES:(s + 1) * LANES]) for s in range(DSA_TN // LANES)],
                axis=1).astype(o_ref.dtype)
        return finish

    def step(finish, multiply=True):
        y_prev = y_ref[...]
        if multiply:
            y_ref[...] = lax.dot_general(
                h_ref[...], w_ref[...].astype(h_ref.dtype), (((1,), (1,)), ((), ())),
                preferred_element_type=F32)
        if finish is not None:
            finish(y_prev)

    @pl.when(j == 0)
    def _():
        step(None)

    @pl.when((jp >= 0) & (jp < W_K_FIRST))
    def _():
        scale = HEAD_DIM ** -0.5 * math.log2(math.e)
        step(per_head(lambda z: rope(_rms_norm_rows(z, qn_ref[...])) * scale))

    @pl.when((jp >= W_K_FIRST) & (jp < W_V_FIRST))
    def _():
        step(per_head(lambda z: rope(_rms_norm_rows(z, kn_ref[...]))))

    @pl.when((jp >= W_V_FIRST) & (jp < W_IQ_FIRST))
    def _():
        def finish(y):
            o_ref[...] = y.astype(o_ref.dtype)
            vt_ref[...] = jnp.transpose(y).astype(vt_ref.dtype)
        step(finish)

    @pl.when((jp >= W_IQ_FIRST) & (jp < W_LAST))
    def _():
        step(per_head(idx_rope))

    @pl.when(jp == W_LAST)
    def _():
        def finish(y):
            o_ref[...] = jnp.concatenate(
                [idx_rope(y[:, :LANES]), jnp.zeros_like(y[:, LANES:])], axis=1).astype(o_ref.dtype)
            lane = lax.broadcasted_iota(jnp.int32, (y.shape[0], LANES), 1)
            iw_ref[...] = jnp.where(lane < IDX_HEADS,
                                    y[:, LANES:2 * LANES] * (IDX_HEADS ** -0.5 * IDX_DIM ** -0.5), 0.0)
        step(finish, multiply=False)


def dsa_projection(x, g, w_layers, layer, q_norm, k_norm, tables, *, tm):
    t, d = x.shape
    assert w_layers.shape[1] == Q_COLS + 2 * KV_COLS + IQ_COLS + IDX_DIM + IDX_HEADS
    rc, rs, ic, ia, ib = tables
    tab_spec = pl.BlockSpec((tm, LANES), lambda i, j: (i, 0))
    vec_spec = pl.BlockSpec((1, LANES), lambda i, j: (0, 0))
    need = (2 * _nbytes((tm, d), F32) + 3 * _nbytes((d, DSA_TN), F32)
            + 2 * _nbytes((tm, DSA_TN), MXU_DTYPE) + _nbytes((tm, d), MXU_DTYPE)
            + 12 * _nbytes((tm, LANES), F32) + 3 * _nbytes((tm, DSA_TN), F32))
    return pl.pallas_call(
        _dsa_proj_kernel,
        grid=(t // tm, W_LAST + 2),
        in_specs=[
            pl.BlockSpec((tm, d), lambda i, j: (i, 0)),
            pl.BlockSpec((1, d), lambda i, j: (0, 0)),
            pl.BlockSpec((None, DSA_TN, d), lambda i, j: (layer, jnp.minimum(j, W_LAST), 0)),
            vec_spec, vec_spec, tab_spec, tab_spec, tab_spec, tab_spec, tab_spec,
        ],
        out_specs=[
            pl.BlockSpec((tm, DSA_TN), lambda i, j: (i, _dsa_out_block(jnp.maximum(j - 1, 0)))),
            pl.BlockSpec((tm, LANES), lambda i, j: (i, 0)),
            pl.BlockSpec((DSA_TN, tm), lambda i, j: (jnp.clip(j - 1 - W_V_FIRST, 0, KV_BLOCKS - 1), i)),
        ],
        out_shape=[
            jax.ShapeDtypeStruct((t, DSA_COLS), MXU_DTYPE),
            jax.ShapeDtypeStruct((t, LANES), F32),
            jax.ShapeDtypeStruct((KV_COLS, t), MXU_DTYPE),
        ],
        scratch_shapes=[pltpu.VMEM((tm, d), MXU_DTYPE), pltpu.VMEM((tm, DSA_TN), F32)],
        compiler_params=pltpu.CompilerParams(
            dimension_semantics=("parallel", "arbitrary"),
            vmem_limit_bytes=_vmem_limit(need)),
        name="dsa_projection",
    )(x, g.reshape(1, d), w_layers, q_norm.reshape(1, LANES), k_norm.reshape(1, LANES), rc, rs, ic, ia, ib)


def _float_key(v):
    b = lax.bitcast_convert_type(v, jnp.int32)
    return b ^ ((b >> 31) & 0x7FFFFFFF)


def _key_float(k):
    return lax.bitcast_convert_type(k ^ ((k >> 31) & 0x7FFFFFFF), F32)


_KLO, _KHI, _CLO, _CHI, _CAND = range(5)
_THR, _CUT = range(2)
_SEARCH_CAP = 96
ONES_ROWS = 2 * SUBLANES
ATTN_PAD_FROM = 2
ATTN_LOOKAHEAD = 1
SEARCH_STEP_EVERY = 1


def _dsa_core_kernel(q_ref, iq_ref, iw_ref, k_ref, vt_ref, ik_ref, o_ref,
                     key_ref, bias_ref, s_ref, wt_ref, st_ref, sel_ref, m_ref, l_ref, acc_ref,
                     *, tq, topk, kt, ka, n_blocks):
    i = pl.program_id(0)
    n_chunks = key_ref.shape[0] // 2
    n_keys = n_chunks * tq
    searching = i < n_blocks
    nc = jnp.minimum(i + 1, n_chunks)
    pc = i
    cur = (i % 2) * n_chunks
    prev = ((i + 1) % 2) * n_chunks
    group = N_HEADS // N_KV_HEADS
    nt = (((1,), (1,)), ((), ()))

    krow = lax.broadcasted_iota(jnp.int32, (tq, tq), 0)
    qcol = lax.broadcasted_iota(jnp.int32, (tq, tq), 1)
    diag_causal = krow <= qcol

    def chunk_start(c):
        return pl.multiple_of(c * tq, tq)

    def over_chunks(body, k, n, init, pad_from=None):
        n_tiles = (n + k - min(pad_from or k, k)) // k
        carry = lax.fori_loop(0, n_tiles, lambda ti, c: body(ti * k, k, c), init)
        if k > 1:
            carry = lax.fori_loop(n_tiles * k, n, lambda c0, c: body(c0, 1, c), carry)
        return carry

    def any_lane(mask):
        return jnp.max(jnp.where(mask, 1.0, 0.0))

    def unfinished(klo, khi, clo):
        return (clo > topk) & ((khi - klo) != 1)

    def candidate(klo, khi, clo):
        vmid = 0.5 * _key_float(klo) + 0.5 * _key_float(khi)
        cmid = _float_key(vmid)
        kmid = klo + lax.shift_right_logical(khi - klo, 1)
        cand = jnp.where((cmid > klo) & (cmid < khi), cmid, kmid)
        return jnp.where(unfinished(klo, khi, clo), cand, klo)

    def finish_pass(cnt, enable):
        klo, khi, clo, chi = st_ref[_KLO], st_ref[_KHI], st_ref[_CLO], st_ref[_CHI]
        cand = st_ref[_CAND]
        live = unfinished(klo, khi, clo) & enable
        cnt = cnt.astype(jnp.int32)
        take = cnt >= topk
        up = live & take
        down = live & jnp.logical_not(take)
        klo = jnp.where(up, cand, klo)
        clo = jnp.where(up, cnt, clo)
        khi = jnp.where(down, cand, khi)
        st_ref[_KLO], st_ref[_KHI], st_ref[_CLO] = klo, khi, clo
        st_ref[_CHI] = jnp.where(down, cnt, chi)
        st_ref[_CAND] = candidate(klo, khi, clo)

    def count_chunks(first, k, cnt, cand_b):
        for u in range(k):
            hit = jnp.where(key_ref[cur + first + u] >= cand_b, 1.0, 0.0)
            cnt = cnt + jnp.sum(hit, axis=0, keepdims=True)
        return cnt

    @pl.when(searching)
    def _():
        wt_ref[...] = jnp.transpose(iw_ref[...])

        def index_tile(first, k, carry):
            smin, smax = carry
            iks = [ik_ref[pl.ds(chunk_start(first + u), tq), :] for u in range(k)]
            scores = [jnp.zeros((tq, tq), F32) for _ in range(k)]
            for h in range(IDX_HEADS):
                iqh = iq_ref[:, h * IDX_DIM:(h + 1) * IDX_DIM]
                w = jnp.broadcast_to(wt_ref[h:h + 1, :], (tq, tq))
                for u in range(k):
                    logits = lax.dot_general(iks[u], iqh, nt, preferred_element_type=F32)
                    scores[u] = scores[u] + w * jnp.maximum(logits, 0.0)
            for u in range(k):
                c = first + u
                valid = (c < i) | diag_causal
                key_ref[cur + c] = jnp.where(valid, _float_key(scores[u]), INT_MIN)
                smin = jnp.minimum(smin, jnp.min(jnp.where(valid, scores[u], jnp.inf), axis=0, keepdims=True))
                smax = jnp.maximum(smax, jnp.max(jnp.where(valid, scores[u], -jnp.inf), axis=0, keepdims=True))
            return smin, smax

        smin, smax = over_chunks(
            index_tile, kt, nc, (jnp.full((1, tq), jnp.inf, F32), jnp.full((1, tq), -jnp.inf, F32)))

        def mask_chunk(c, carry):
            key_ref[cur + c] = jnp.full((tq, tq), INT_MIN, jnp.int32)
            return carry

        lax.fori_loop(nc, (nc + ka - 1) // ka * ka, mask_chunk, 0)

        klo = _float_key(smin)
        khi = _float_key(smax) + 1
        clo = lax.broadcasted_iota(jnp.int32, (1, tq), 1) + (i * tq + 1)
        st_ref[_KLO], st_ref[_KHI], st_ref[_CLO] = klo, khi, clo
        st_ref[_CHI] = jnp.zeros((1, tq), jnp.int32)
        st_ref[_CAND] = candidate(klo, khi, clo)

    @pl.when(i == 0)
    def _():
        sel_ref[_THR] = jnp.zeros((1, tq), jnp.int32)
        sel_ref[_CUT] = jnp.full((1, tq), n_keys, jnp.int32)

    thr_b = jnp.broadcast_to(sel_ref[_THR], (tq, tq))
    cut_b = jnp.broadcast_to(sel_ref[_CUT], (tq, tq))
    has_ties = any_lane(sel_ref[_CUT] != n_keys) > 0
    m_ref[...] = jnp.full(m_ref.shape, NEG_BIG, F32)
    l_ref[...] = jnp.zeros(l_ref.shape, F32)
    acc_ref[...] = jnp.zeros(acc_ref.shape, F32)
    count_tiles = (nc + ka - 1) // ka

    def search_step(carry):
        cursor, cnt = carry
        tile = jnp.minimum(cursor, count_tiles - 1)
        cnt = count_chunks(tile * ka, ka, cnt, jnp.broadcast_to(st_ref[_CAND], (tq, tq)))
        cursor = cursor + 1
        done = cursor >= count_tiles
        finish_pass(cnt, done & searching)
        return jnp.where(done, 0, cursor), jnp.where(done, 0.0, cnt)

    def attend_tile(first, k, carry, *, with_ties, with_search):
        for u in range(k):
            c = first + u
            key = key_ref[prev + c]
            if with_ties:
                keep = (key > thr_b) | ((key == thr_b) & (krow + c * tq <= cut_b))
            else:
                keep = key >= thr_b
            bias_ref[u] = jnp.where(keep, 0.0, NEG_BIG)

        starts = [chunk_start(first + u) for u in range(k)]
        slots = ATTN_LOOKAHEAD + 1
        dyn0 = jnp.minimum(first, 0)
        ones_rows = jnp.ones((ONES_ROWS, tq), vt_ref.dtype)

        def logits(h):
            cols = slice(h // group * HEAD_DIM, (h // group + 1) * HEAD_DIM)
            qh = q_ref[:, h * HEAD_DIM:(h + 1) * HEAD_DIM]
            top = None
            for u in range(k):
                s = lax.dot_general(k_ref[pl.ds(starts[u], tq), cols], qh, nt,
                                    preferred_element_type=F32) + bias_ref[u]
                s_ref[h % slots + dyn0, u] = s
                top = s if top is None else jnp.maximum(top, s)
            return jnp.max(top, axis=0, keepdims=True)

        queued = [logits(h) for h in range(ATTN_LOOKAHEAD)]
        for h in range(N_HEADS):
            tile_max = queued.pop(0)
            if h + ATTN_LOOKAHEAD < N_HEADS:
                queued.append(logits(h + ATTN_LOOKAHEAD))
            cols = slice(h // group * HEAD_DIM, (h // group + 1) * HEAD_DIM)
            m_prev = m_ref[h]
            m_cur = jnp.maximum(m_prev, tile_max)
            alpha = jnp.exp2(m_prev - m_cur)
            pv = jnp.zeros((HEAD_DIM + ONES_ROWS, tq), F32)
            for u in range(k):
                p = jnp.exp2(s_ref[h % slots + dyn0, u] - m_cur)
                v_aug = jnp.concatenate([vt_ref[cols, pl.ds(starts[u], tq)], ones_rows], axis=0)
                pv = pv + jnp.dot(v_aug, p.astype(vt_ref.dtype), preferred_element_type=F32)
            l_ref[h] = alpha * l_ref[h] + pv[HEAD_DIM:HEAD_DIM + 1, :]
            pv = pv[:HEAD_DIM, :]
            acc_ref[h] = alpha * acc_ref[h] + pv
            m_ref[h] = m_cur
            if with_search and h % SEARCH_STEP_EVERY == SEARCH_STEP_EVERY - 1:
                carry = search_step(carry)
        return carry

    idle = (jnp.int32(0), jnp.zeros((1, tq), F32))

    @pl.when(jnp.logical_not(has_ties))
    def _():
        def tile(first, k, carry):
            return attend_tile(first, k, carry, with_ties=False, with_search=(k == ka))
        over_chunks(tile, ka, pc, idle, pad_from=ATTN_PAD_FROM)

    @pl.when(has_ties)
    def _():
        def tile(first, k, carry):
            return attend_tile(first, k, carry, with_ties=True, with_search=False)
        over_chunks(tile, 1, pc, idle)

    @pl.when(i >= 1)
    def _():
        for h in range(N_HEADS):
            o_ref[:, h * HEAD_DIM:(h + 1) * HEAD_DIM] = jnp.transpose(acc_ref[h] / l_ref[h]).astype(o_ref.dtype)

    @pl.when(searching)
    def _():
        def count_ge(cand):
            cand_b = jnp.broadcast_to(cand, (tq, tq))
            return over_chunks(lambda first, k, cnt: count_chunks(first, k, cnt, cand_b),
                               kt, nc, jnp.zeros((1, tq), F32))

        def search_cond(st):
            step, pending = st
            return (step < _SEARCH_CAP) & (pending > 0)

        def search_body(st):
            step, _ = st
            pending = any_lane(unfinished(st_ref[_KLO], st_ref[_KHI], st_ref[_CLO]))
            finish_pass(count_ge(st_ref[_CAND]), True)
            return step + 1, pending

        pending0 = any_lane(unfinished(st_ref[_KLO], st_ref[_KHI], st_ref[_CLO]))
        lax.while_loop(search_cond, search_body, (jnp.int32(0), pending0))

        thr = st_ref[_KLO]
        tied = st_ref[_CLO] > topk
        sel_ref[_THR] = thr
        sel_ref[_CUT] = jnp.full((1, tq), n_keys, jnp.int32)

        @pl.when(any_lane(tied) > 0)
        def _():
            want = topk - st_ref[_CHI]
            thr_t = jnp.broadcast_to(thr, (tq, tq))

            def count_tied_upto(pos):
                pos_b = jnp.broadcast_to(pos, (tq, tq))

                def body(c, cnt):
                    hit = (key_ref[cur + c] == thr_t) & (krow + c * tq <= pos_b)
                    return cnt + jnp.sum(jnp.where(hit, 1.0, 0.0), axis=0, keepdims=True)

                return lax.fori_loop(0, nc, body, jnp.zeros((1, tq), F32)).astype(jnp.int32)

            def cut_step(_, st):
                lo, hi = st
                mid = lo + ((hi - lo) >> 1)
                enough = count_tied_upto(mid) >= want
                return jnp.where(enough, lo, mid), jnp.where(enough, mid, hi)

            _, cut = lax.fori_loop(0, n_keys.bit_length(), cut_step,
                                   (jnp.full((1, tq), -1, jnp.int32), jnp.full((1, tq), n_keys - 1, jnp.int32)))
            sel_ref[_CUT] = jnp.where(tied, cut, n_keys)


def dsa_core(proj, iw, vt, *, tq, topk, kt, ka):
    t = proj.shape[0]
    n_blocks = t // tq
    assert n_blocks % kt == 0 and kt % ka == 0
    k_block = (Q_COLS + IQ_COLS) // KV_COLS
    ik_block = (Q_COLS + IQ_COLS + 2 * KV_COLS) // IDX_DIM
    resident = dict(pipeline_mode=pl.Buffered(1))
    need = (4 * _nbytes((tq, Q_COLS), proj.dtype) + 2 * _nbytes((tq, LANES), F32)
            + 2 * _nbytes((t, KV_COLS), proj.dtype) + _nbytes((t, IDX_DIM), proj.dtype)
            + 2 * _nbytes((tq, Q_COLS), proj.dtype)
            + 2 * _nbytes((n_blocks, tq, tq), jnp.int32)
            + _nbytes((LANES, tq), F32) + (7 + 2 * N_HEADS) * _nbytes((SUBLANES, tq), F32)
            + _nbytes((N_HEADS, HEAD_DIM, tq), F32)
            + (ATTN_LOOKAHEAD + 3) * _nbytes((ka, tq, tq), F32))
    return pl.pallas_call(
        functools.partial(_dsa_core_kernel, tq=tq, topk=topk, kt=kt, ka=ka, n_blocks=n_blocks),
        grid=(n_blocks + 1,),
        in_specs=[
            pl.BlockSpec((tq, Q_COLS), lambda i: (jnp.maximum(i - 1, 0), 0)),
            pl.BlockSpec((tq, IQ_COLS), lambda i: (jnp.minimum(i, n_blocks - 1), 1)),
            pl.BlockSpec((tq, LANES), lambda i: (jnp.minimum(i, n_blocks - 1), 0)),
            pl.BlockSpec((t, KV_COLS), lambda i: (0, k_block), **resident),
            pl.BlockSpec((KV_COLS, t), lambda i: (0, 0), **resident),
            pl.BlockSpec((t, IDX_DIM), lambda i: (0, ik_block), **resident),
        ],
        out_specs=pl.BlockSpec((tq, Q_COLS), lambda i: (jnp.maximum(i - 1, 0), 0)),
        out_shape=jax.ShapeDtypeStruct((t, Q_COLS), proj.dtype),
        scratch_shapes=[
            pltpu.VMEM((2 * n_blocks, tq, tq), jnp.int32),
            pltpu.VMEM((ka, tq, tq), F32),
            pltpu.VMEM((ATTN_LOOKAHEAD + 1, ka, tq, tq), F32),
            pltpu.VMEM((LANES, tq), F32),
            pltpu.VMEM((5, 1, tq), jnp.int32),
            pltpu.VMEM((2, 1, tq), jnp.int32),
            pltpu.VMEM((N_HEADS, 1, tq), F32),
            pltpu.VMEM((N_HEADS, 1, tq), F32),
            pltpu.VMEM((N_HEADS, HEAD_DIM, tq), F32),
        ],
        compiler_params=pltpu.CompilerParams(
            dimension_semantics=("arbitrary",),
            vmem_limit_bytes=_vmem_limit(need)),
        name="dsa_core",
    )(proj, proj, iw, proj, vt, proj)


def _rope_tables(pos):
    def angles(dim):
        inv = 1.0 / (ROPE_THETA ** (jnp.arange(0, dim, 2, dtype=F32) / dim))
        return pos.astype(F32)[:, None] * inv

    ang = angles(HEAD_DIM)
    cos, sin = jnp.cos(ang), jnp.sin(ang)
    rc = jnp.concatenate([cos, cos], axis=1)
    rs = jnp.concatenate([-sin, sin], axis=1)
    iang = angles(IDX_ROPE_DIM)
    icos, isin = jnp.cos(iang), jnp.sin(iang)
    rest = LANES - IDX_ROPE_DIM
    zeros = jnp.zeros_like(isin)
    ic = jnp.concatenate([icos, icos, jnp.ones((pos.shape[0], rest), F32)], axis=1)
    ia = jnp.concatenate([-isin, zeros, jnp.zeros((pos.shape[0], rest), F32)], axis=1)
    ib = jnp.concatenate([zeros, isin, jnp.zeros((pos.shape[0], rest), F32)], axis=1)
    return rc, rs, ic, ia, ib


def dsa_layer(x, pos, norm_g, w_in_layers, layer, q_norm, k_norm, w_out, tiles):
    t = x.shape[0]
    proj, iw, vt = dsa_projection(x, norm_g, jnp.swapaxes(w_in_layers, 1, 2), layer, q_norm, k_norm,
                                  _rope_tables(pos), tm=tiles["proj_tm"])
    attn = dsa_core(proj, iw, vt, tq=tiles["tq"], topk=min(TOPK_MAX, t // 4), kt=tiles["kt"], ka=tiles["ka"])
    return matmul_residual(attn, w_out.astype(MXU_DTYPE), x, tm=tiles["res_tm"], tn=tiles["res_tn"])


def _rglru_kernel(gate_ref, xr_ref, cw_ref, cb_ref, wa_ref, ba_ref, wx_ref, bx_ref, lam_ref,
                  y_ref, hcar_ref, xprev_ref, *, tt):
    @pl.when(pl.program_id(1) == 0)
    def _():
        hcar_ref[...] = jnp.zeros_like(hcar_ref)
        xprev_ref[...] = jnp.zeros_like(xprev_ref)

    xr = xr_ref[...]
    ext = jnp.concatenate([xprev_ref[...], xr], axis=0)
    cw = cw_ref[...]
    xc = cb_ref[...] + xr * cw[CONV_WIDTH - 1:CONV_WIDTH, :]
    for d in range(1, CONV_WIDTH):
        xc = xc + pltpu.roll(ext, d, 0)[SUBLANES:, :] * cw[CONV_WIDTH - 1 - d:CONV_WIDTH - d, :]
    xprev_ref[...] = xr[tt - SUBLANES:, :]

    xcb = xc.astype(wa_ref.dtype)
    r = jax.nn.sigmoid(jnp.dot(xcb, wa_ref[...], preferred_element_type=F32) + ba_ref[...])
    ig = jax.nn.sigmoid(jnp.dot(xcb, wx_ref[...], preferred_element_type=F32) + bx_ref[...])
    nlam = -lam_ref[...]
    softplus = jnp.maximum(nlam, 0.0) + jnp.log(1.0 + jnp.exp(-jnp.abs(nlam)))
    log_a = -LRU_C * r * softplus
    a = jnp.exp(log_a)
    mult = jnp.sqrt(1.0 - jnp.exp(2.0 * log_a))
    b = xc * ig * mult

    n_groups = tt // SUBLANES
    a = a.reshape(n_groups, SUBLANES, a.shape[1])
    b = b.reshape(a.shape)
    in_group = lax.broadcasted_iota(jnp.int32, a.shape, 1)
    d = 1
    while d < SUBLANES:
        b = a * jnp.where(in_group >= d, pltpu.roll(b, d, 1), 0.0) + b
        a = a * jnp.where(in_group >= d, pltpu.roll(a, d, 1), 1.0)
        d *= 2
    carry = hcar_ref[...]
    groups = []
    for gi in range(n_groups):
        hg = b[gi] + a[gi] * carry
        carry = hg[SUBLANES - 1:SUBLANES, :]
        groups.append(hg)
    hcar_ref[...] = carry
    h = jnp.concatenate(groups, axis=0)
    y_ref[...] = (h * jax.nn.gelu(gate_ref[...])).astype(y_ref.dtype)


def rglru_scan(proj, conv_w, conv_b, wa, ba, wx, bx, lam, *, tt):
    t = proj.shape[0]
    width = conv_w.shape[1]
    nb, blk, _ = wa.shape
    vec = lambda: pl.BlockSpec((1, blk), lambda n, s: (0, n))
    need = (4 * _nbytes((tt, blk), F32) + 2 * _nbytes((tt, blk), MXU_DTYPE)
            + 4 * _nbytes((blk, blk), wa.dtype) + 24 * _nbytes((tt, blk), F32))
    return pl.pallas_call(
        functools.partial(_rglru_kernel, tt=tt),
        grid=(nb, t // tt),
        in_specs=[
            pl.BlockSpec((tt, blk), lambda n, s: (s, n)),
            pl.BlockSpec((tt, blk), lambda n, s: (s, nb + n)),
            pl.BlockSpec((CONV_WIDTH, blk), lambda n, s: (0, n)),
            vec(),
            pl.BlockSpec((None, blk, blk), lambda n, s: (n, 0, 0)),
            vec(),
            pl.BlockSpec((None, blk, blk), lambda n, s: (n, 0, 0)),
            vec(),
            vec(),
        ],
        out_specs=pl.BlockSpec((tt, blk), lambda n, s: (s, n)),
        out_shape=jax.ShapeDtypeStruct((t, width), MXU_DTYPE),
        scratch_shapes=[pltpu.VMEM((1, blk), F32), pltpu.VMEM((SUBLANES, blk), F32)],
        compiler_params=pltpu.CompilerParams(
            dimension_semantics=("parallel", "arbitrary"),
            vmem_limit_bytes=_vmem_limit(need)),
        name="rglru_scan",
    )(proj, proj, conv_w, conv_b.reshape(1, width), wa, ba.reshape(1, width),
      wx, bx.reshape(1, width), lam.reshape(1, width))


def rglru_layer(x, norm_g, w_in, conv_w, conv_b, wa, ba, wx, bx, lam, w_out, tiles):
    proj = norm_matmul(x, norm_g, w_in.astype(MXU_DTYPE), tm=tiles["proj_tm"], tn=tiles["proj_tn"],
                       out_dtype=F32)
    y = rglru_scan(proj, conv_w, conv_b, wa.astype(MXU_DTYPE), ba, wx.astype(MXU_DTYPE), bx, lam,
                   tt=tiles["scan_tt"])
    return matmul_residual(y, w_out.astype(MXU_DTYPE), x, tm=tiles["res_tm"], tn=tiles["res_tn"])


POOL_HALO = max(POOL_WINDOWS)


def _pool_kernel(x_ref, g_ref, w_ref, b_ref, s_ref, o_ref, halo_ref, *, tt):
    blk = pl.program_id(0)

    @pl.when(blk == 0)
    def _():
        halo_ref[...] = jnp.zeros_like(halo_ref)

    x = x_ref[...]
    h = _rms_norm_rows(x, g_ref[...])
    ext = jnp.concatenate([halo_ref[...], h], axis=0)
    halo_ref[...] = h[tt - POOL_HALO:, :]

    gw = h.shape[1] // len(POOL_WINDOWS)
    t1 = (lax.broadcasted_iota(jnp.int32, (tt, gw), 0) + (blk * tt + 1)).astype(F32)
    for gi, win in enumerate(POOL_WINDOWS):
        cols = slice(gi * gw, (gi + 1) * gw)
        acc = ext[:, cols]
        d = 1
        while d < win:
            acc = acc + pltpu.roll(acc, d, 0)
            d *= 2
        mean = acc[POOL_HALO:, :] / jnp.minimum(t1, float(win))
        y = (mean - h[:, cols]).astype(w_ref.dtype)
        z = jnp.dot(y, w_ref[gi], preferred_element_type=F32) + b_ref[gi:gi + 1, :]
        o_ref[:, cols] = x[:, cols] + z * s_ref[:, cols]


def pool_layer(x, norm_g, w_group, b_group, scale, tiles):
    t, d = x.shape
    tt = tiles["pool_tt"]
    ng, gw, _ = w_group.shape
    need = (4 * _nbytes((tt, d), F32) + 2 * _nbytes((ng, gw, gw), MXU_DTYPE)
            + 6 * _nbytes((tt, d), F32))
    return pl.pallas_call(
        functools.partial(_pool_kernel, tt=tt),
        grid=(t // tt,),
        in_specs=[
            pl.BlockSpec((tt, d), lambda i: (i, 0)),
            pl.BlockSpec((1, d), lambda i: (0, 0)),
            pl.BlockSpec((ng, gw, gw), lambda i: (0, 0, 0)),
            pl.BlockSpec((ng, gw), lambda i: (0, 0)),
            pl.BlockSpec((1, d), lambda i: (0, 0)),
        ],
        out_specs=pl.BlockSpec((tt, d), lambda i: (i, 0)),
        out_shape=jax.ShapeDtypeStruct((t, d), F32),
        scratch_shapes=[pltpu.VMEM((POOL_HALO, d), F32)],
        compiler_params=pltpu.CompilerParams(
            dimension_semantics=("arbitrary",),
            vmem_limit_bytes=_vmem_limit(need)),
        name="pool_mixer",
    )(x, norm_g.reshape(1, d), w_group.astype(MXU_DTYPE), b_group, scale.reshape(1, d))


def _tiles(t):
    big = min(t, 512)
    tall = min(t, 1024)
    return {
        "proj_tm": tall, "proj_tn": 1024,
        "res_tm": tall, "res_tn": 1024,
        "tq": min(t, 256), "kt": min(4, t // min(t, 256)), "ka": min(4, t // min(t, 256)),
        "scan_tt": min(t, 512),
        "pool_tt": min(t, 256),
        "mlp_tm": big, "mlp_tf": 1024,
    }


def kernel(x, positions, attn_norm, attn_w_in, attn_q_norm, attn_k_norm, attn_w_out, rnn_norm, rnn_w_in, rnn_conv_w, rnn_conv_b, rnn_gate_a_w, rnn_gate_a_b, rnn_gate_x_w, rnn_gate_x_b, rnn_lambda, rnn_w_out, pool_norm, pool_w, pool_b, pool_scale, mlp_norm, mlp_w_up, mlp_w_down):
    batch, t, d = x.shape
    depth = mlp_norm.shape[0]
    tiles = _tiles(t)
    w_up_all = mlp_w_up.astype(MXU_DTYPE)
    w_down_all = mlp_w_down.astype(MXU_DTYPE)
    outs = []
    for bi in range(batch):
        xb = x[bi]
        pos = positions[bi]
        for i in range(depth):
            kind, j = i % N_MIXERS, i // N_MIXERS
            if kind == 0:
                xb = dsa_layer(xb, pos, attn_norm[j], attn_w_in, j, attn_q_norm[j], attn_k_norm[j],
                               attn_w_out[j], tiles)
            elif kind == 1:
                xb = rglru_layer(xb, rnn_norm[j], rnn_w_in[j], rnn_conv_w[j], rnn_conv_b[j],
                                 rnn_gate_a_w[j], rnn_gate_a_b[j], rnn_gate_x_w[j], rnn_gate_x_b[j],
                                 rnn_lambda[j], rnn_w_out[j], tiles)
            else:
                xb = pool_layer(xb, pool_norm[j], pool_w[j], pool_b[j], pool_scale[j], tiles)
            xb = mlp_block(xb, mlp_norm[i], w_up_all, w_down_all, i,
                           tm=tiles["mlp_tm"], tf=tiles["mlp_tf"])
        outs.append(xb)
    return outs[0][None] if batch == 1 else jnp.stack(outs, axis=0)
```

```python
import functools
import math

import jax
import jax.numpy as jnp
from jax import lax
from jax.experimental import pallas as pl
from jax.experimental.pallas import tpu as pltpu

F32 = jnp.float32
MXU_DTYPE = jnp.bfloat16

N_MIXERS = 3
EPS = 1e-6
ROPE_THETA = 10000.0
HEAD_DIM = 128
N_HEADS = 16
N_KV_HEADS = 4
IDX_HEADS = 16
IDX_DIM = 128
IDX_ROPE_DIM = 64
TOPK_MAX = 256
CONV_WIDTH = 4
LRU_C = 8.0
POOL_WINDOWS = (2, 4, 8, 16)

LANES = 128
SUBLANES = 8
VMEM_BYTES_V7X = 64 * 1024 * 1024
VMEM_CAP_BYTES = VMEM_BYTES_V7X - 8 * 1024 * 1024

INT_MIN = -(2 ** 31)
NEG_BIG = -1e30


def _vmem_limit(block_bytes):
    return int(min(VMEM_CAP_BYTES, block_bytes * 3 // 2 + (4 << 20)))


def _nbytes(shape, dtype):
    return math.prod(shape) * jnp.dtype(dtype).itemsize


def _rms_norm_rows(x, g):
    ms = jnp.mean(x * x, axis=-1, keepdims=True)
    return x * lax.rsqrt(ms + EPS) * g


def _norm_matmul_kernel(x_ref, g_ref, w_ref, o_ref, h_ref):
    @pl.when(pl.program_id(1) == 0)
    def _():
        h_ref[...] = _rms_norm_rows(x_ref[...], g_ref[...]).astype(h_ref.dtype)

    o_ref[...] = jnp.dot(h_ref[...], w_ref[...], preferred_element_type=F32).astype(o_ref.dtype)


def norm_matmul(x, g, w, *, tm, tn, out_dtype):
    t, d = x.shape
    n = w.shape[1]
    need = (2 * _nbytes((tm, d), F32) + 2 * _nbytes((d, tn), w.dtype)
            + 2 * _nbytes((tm, tn), out_dtype) + _nbytes((tm, d), w.dtype))
    return pl.pallas_call(
        _norm_matmul_kernel,
        grid=(t // tm, n // tn),
        in_specs=[
            pl.BlockSpec((tm, d), lambda i, j: (i, 0)),
            pl.BlockSpec((1, d), lambda i, j: (0, 0)),
            pl.BlockSpec((d, tn), lambda i, j: (0, j)),
        ],
        out_specs=pl.BlockSpec((tm, tn), lambda i, j: (i, j)),
        out_shape=jax.ShapeDtypeStruct((t, n), out_dtype),
        scratch_shapes=[pltpu.VMEM((tm, d), w.dtype)],
        compiler_params=pltpu.CompilerParams(
            dimension_semantics=("parallel", "arbitrary"),
            vmem_limit_bytes=_vmem_limit(need)),
        name="norm_matmul",
    )(x, g.reshape(1, d), w)


def _matmul_residual_kernel(a_ref, w_ref, x_ref, o_ref):
    o_ref[...] = x_ref[...] + jnp.dot(a_ref[...], w_ref[...], preferred_element_type=F32)


def matmul_residual(a, w, x, *, tm, tn):
    t, k = a.shape
    n = w.shape[1]
    need = (2 * _nbytes((tm, k), a.dtype) + 2 * _nbytes((k, tn), w.dtype)
            + 4 * _nbytes((tm, tn), F32))
    return pl.pallas_call(
        _matmul_residual_kernel,
        grid=(t // tm, n // tn),
        in_specs=[
            pl.BlockSpec((tm, k), lambda i, j: (i, 0)),
            pl.BlockSpec((k, tn), lambda i, j: (0, j)),
            pl.BlockSpec((tm, tn), lambda i, j: (i, j)),
        ],
        out_specs=pl.BlockSpec((tm, tn), lambda i, j: (i, j)),
        out_shape=jax.ShapeDtypeStruct((t, n), F32),
        compiler_params=pltpu.CompilerParams(
            dimension_semantics=("parallel", "arbitrary"),
            vmem_limit_bytes=_vmem_limit(need)),
        name="matmul_residual",
    )(a, w, x)


def _mlp_kernel(x_ref, g_ref, wu_ref, wd_ref, o_ref, h_ref, acc_ref):
    f = pl.program_id(1)

    @pl.when(f == 0)
    def _():
        h_ref[...] = _rms_norm_rows(x_ref[...], g_ref[...]).astype(h_ref.dtype)
        acc_ref[...] = jnp.zeros_like(acc_ref)

    u = jnp.dot(h_ref[...], wu_ref[...], preferred_element_type=F32)
    u = jnp.square(jnp.maximum(u, 0.0)).astype(wd_ref.dtype)
    acc_ref[...] += jnp.dot(u, wd_ref[...], preferred_element_type=F32)

    @pl.when(f == pl.num_programs(1) - 1)
    def _():
        o_ref[...] = x_ref[...] + acc_ref[...]


def mlp_block(x, g, w_up, w_down, layer, *, tm, tf):
    t, d = x.shape
    ff = w_up.shape[2]
    need = (4 * _nbytes((tm, d), F32) + 2 * _nbytes((d, tf), w_up.dtype)
            + 2 * _nbytes((tf, d), w_down.dtype) + _nbytes((tm, d), w_up.dtype)
            + _nbytes((tm, d), F32) + 2 * _nbytes((tm, tf), F32))
    return pl.pallas_call(
        _mlp_kernel,
        grid=(t // tm, ff // tf),
        in_specs=[
            pl.BlockSpec((tm, d), lambda i, f: (i, 0)),
            pl.BlockSpec((1, d), lambda i, f: (0, 0)),
            pl.BlockSpec((None, d, tf), lambda i, f: (layer, 0, f)),
            pl.BlockSpec((None, tf, d), lambda i, f: (layer, f, 0)),
        ],
        out_specs=pl.BlockSpec((tm, d), lambda i, f: (i, 0)),
        out_shape=jax.ShapeDtypeStruct((t, d), F32),
        scratch_shapes=[pltpu.VMEM((tm, d), w_up.dtype), pltpu.VMEM((tm, d), F32)],
        compiler_params=pltpu.CompilerParams(
            dimension_semantics=("parallel", "arbitrary"),
            vmem_limit_bytes=_vmem_limit(need)),
        name="mlp_block",
    )(x, g.reshape(1, d), w_up, w_down)


DSA_TN = 4 * LANES
Q_COLS = N_HEADS * HEAD_DIM
IQ_COLS = IDX_HEADS * IDX_DIM
KV_COLS = N_KV_HEADS * HEAD_DIM
DSA_COLS = Q_COLS + IQ_COLS + 2 * KV_COLS + DSA_TN
Q_BLOCKS = Q_COLS // DSA_TN
IQ_BLOCKS = IQ_COLS // DSA_TN
KV_BLOCKS = KV_COLS // DSA_TN


W_K_FIRST = Q_BLOCKS
W_V_FIRST = W_K_FIRST + KV_BLOCKS
W_IQ_FIRST = W_V_FIRST + KV_BLOCKS
W_LAST = W_IQ_FIRST + IQ_BLOCKS
OUT_K_FIRST = Q_BLOCKS + IQ_BLOCKS


def _dsa_out_block(jw):
    return jnp.where(jw < W_K_FIRST, jw,
                     jnp.where(jw < W_IQ_FIRST, jw + (OUT_K_FIRST - W_K_FIRST),
                               jnp.where(jw < W_LAST, jw - (W_IQ_FIRST - Q_BLOCKS), W_LAST)))


def _dsa_proj_kernel(x_ref, g_ref, w_ref, qn_ref, kn_ref, rc_ref, rs_ref,
                     ic_ref, ia_ref, ib_ref, o_ref, iw_ref, vt_ref, h_ref, y_ref):
    j = pl.program_id(1)
    jp = j - 1

    @pl.when(j == 0)
    def _():
        h_ref[...] = _rms_norm_rows(x_ref[...], g_ref[...]).astype(h_ref.dtype)

    def rope(z):
        return z * rc_ref[...] + pltpu.roll(z, HEAD_DIM // 2, 1) * rs_ref[...]

    def idx_rope(z):
        half = IDX_ROPE_DIM // 2
        return (z * ic_ref[...] + pltpu.roll(z, LANES - half, 1) * ia_ref[...]
                + pltpu.roll(z, half, 1) * ib_ref[...])

    def per_head(fn):
        def finish(y):
            o_ref[...] = jnp.concatenate(
                [fn(y[:, s * LANES:(s + 1) * LANES]) for s in range(DSA_TN // LANES)],
                axis=1).astype(o_ref.dtype)
        return finish

    def step(finish, multiply=True):
        y_prev = y_ref[...]
        if multiply:
            y_ref[...] = lax.dot_general(
                h_ref[...], w_ref[...].astype(h_ref.dtype), (((1,), (1,)), ((), ())),
                preferred_element_type=F32)
        if finish is not None:
            finish(y_prev)

    @pl.when(j == 0)
    def _():
        step(None)

    @pl.when((jp >= 0) & (jp < W_K_FIRST))
    def _():
        scale = HEAD_DIM ** -0.5 * math.log2(math.e)
        step(per_head(lambda z: rope(_rms_norm_rows(z, qn_ref[...])) * scale))

    @pl.when((jp >= W_K_FIRST) & (jp < W_V_FIRST))
    def _():
        step(per_head(lambda z: rope(_rms_norm_rows(z, kn_ref[...]))))

    @pl.when((jp >= W_V_FIRST) & (jp < W_IQ_FIRST))
    def _():
        def finish(y):
            o_ref[...] = y.astype(o_ref.dtype)
            vt_ref[...] = jnp.transpose(y).astype(vt_ref.dtype)
        step(finish)

    @pl.when((jp >= W_IQ_FIRST) & (jp < W_LAST))
    def _():
        step(per_head(idx_rope))

    @pl.when(jp == W_LAST)
    def _():
        def finish(y):
            o_ref[...] = jnp.concatenate(
                [idx_rope(y[:, :LANES]), jnp.zeros_like(y[:, LANES:])], axis=1).astype(o_ref.dtype)
            lane = lax.broadcasted_iota(jnp.int32, (y.shape[0], LANES), 1)
            iw_ref[...] = jnp.where(lane < IDX_HEADS,
                                    y[:, LANES:2 * LANES] * (IDX_HEADS ** -0.5 * IDX_DIM ** -0.5), 0.0)
        step(finish, multiply=False)


def dsa_projection(x, g, w_layers, layer, q_norm, k_norm, tables, *, tm):
    t, d = x.shape
    assert w_layers.shape[1] == Q_COLS + 2 * KV_COLS + IQ_COLS + IDX_DIM + IDX_HEADS
    rc, rs, ic, ia, ib = tables
    tab_spec = pl.BlockSpec((tm, LANES), lambda i, j: (i, 0))
    vec_spec = pl.BlockSpec((1, LANES), lambda i, j: (0, 0))
    need = (2 * _nbytes((tm, d), F32) + 3 * _nbytes((d, DSA_TN), F32)
            + 2 * _nbytes((tm, DSA_TN), MXU_DTYPE) + _nbytes((tm, d), MXU_DTYPE)
            + 12 * _nbytes((tm, LANES), F32) + 3 * _nbytes((tm, DSA_TN), F32))
    return pl.pallas_call(
        _dsa_proj_kernel,
        grid=(t // tm, W_LAST + 2),
        in_specs=[
            pl.BlockSpec((tm, d), lambda i, j: (i, 0)),
            pl.BlockSpec((1, d), lambda i, j: (0, 0)),
            pl.BlockSpec((None, DSA_TN, d), lambda i, j: (layer, jnp.minimum(j, W_LAST), 0)),
            vec_spec, vec_spec, tab_spec, tab_spec, tab_spec, tab_spec, tab_spec,
        ],
        out_specs=[
            pl.BlockSpec((tm, DSA_TN), lambda i, j: (i, _dsa_out_block(jnp.maximum(j - 1, 0)))),
            pl.BlockSpec((tm, LANES), lambda i, j: (i, 0)),
            pl.BlockSpec((DSA_TN, tm), lambda i, j: (jnp.clip(j - 1 - W_V_FIRST, 0, KV_BLOCKS - 1), i)),
        ],
        out_shape=[
            jax.ShapeDtypeStruct((t, DSA_COLS), MXU_DTYPE),
            jax.ShapeDtypeStruct((t, LANES), F32),
            jax.ShapeDtypeStruct((KV_COLS, t), MXU_DTYPE),
        ],
        scratch_shapes=[pltpu.VMEM((tm, d), MXU_DTYPE), pltpu.VMEM((tm, DSA_TN), F32)],
        compiler_params=pltpu.CompilerParams(
            dimension_semantics=("parallel", "arbitrary"),
            vmem_limit_bytes=_vmem_limit(need)),
        name="dsa_projection",
    )(x, g.reshape(1, d), w_layers, q_norm.reshape(1, LANES), k_norm.reshape(1, LANES), rc, rs, ic, ia, ib)


def _float_key(v):
    b = lax.bitcast_convert_type(v, jnp.int32)
    return b ^ ((b >> 31) & 0x7FFFFFFF)


def _key_float(k):
    return lax.bitcast_convert_type(k ^ ((k >> 31) & 0x7FFFFFFF), F32)


_KLO, _KHI, _CLO, _CHI, _JCUT = range(5)
_SEARCH_CAP = 96
ONES_ROWS = 2 * SUBLANES
ATTN_PAD_FROM = 2
ATTN_LOOKAHEAD = 1


def _dsa_core_kernel(q_ref, iq_ref, iw_ref, k_ref, vt_ref, ik_ref, o_ref,
                     key_ref, bias_ref, s_ref, wt_ref, st_ref, m_ref, l_ref, acc_ref, *, tq, topk, kt, ka):
    i = pl.program_id(0)
    nc = i + 1
    group = N_HEADS // N_KV_HEADS
    nt = (((1,), (1,)), ((), ()))

    wt_ref[...] = jnp.transpose(iw_ref[...])

    krow = lax.broadcasted_iota(jnp.int32, (tq, tq), 0)
    qcol = lax.broadcasted_iota(jnp.int32, (tq, tq), 1)
    diag_causal = krow <= qcol

    def chunk_start(c):
        return pl.multiple_of(c * tq, tq)

    def over_chunks(body, k, init, pad_from=None):
        n_tiles = (nc + k - min(pad_from or k, k)) // k
        carry = lax.fori_loop(0, n_tiles, lambda ti, c: body(ti * k, k, c), init)
        if k > 1:
            carry = lax.fori_loop(n_tiles * k, nc, lambda c0, c: body(c0, 1, c), carry)
        return carry

    def index_tile(first, k, carry):
        smin, smax = carry
        iks = [ik_ref[pl.ds(chunk_start(first + u), tq), :] for u in range(k)]
        scores = [jnp.zeros((tq, tq), F32) for _ in range(k)]
        for h in range(IDX_HEADS):
            iqh = iq_ref[:, h * IDX_DIM:(h + 1) * IDX_DIM]
            w = jnp.broadcast_to(wt_ref[h:h + 1, :], (tq, tq))
            for u in range(k):
                logits = lax.dot_general(iks[u], iqh, nt, preferred_element_type=F32)
                scores[u] = scores[u] + w * jnp.maximum(logits, 0.0)
        for u in range(k):
            c = first + u
            valid = (c < i) | diag_causal
            key_ref[c] = jnp.where(valid, _float_key(scores[u]), INT_MIN)
            smin = jnp.minimum(smin, jnp.min(jnp.where(valid, scores[u], jnp.inf), axis=0, keepdims=True))
            smax = jnp.maximum(smax, jnp.max(jnp.where(valid, scores[u], -jnp.inf), axis=0, keepdims=True))
        return smin, smax

    smin, smax = over_chunks(
        index_tile, kt, (jnp.full((1, tq), jnp.inf, F32), jnp.full((1, tq), -jnp.inf, F32)))

    def mask_chunk(c, carry):
        key_ref[c] = jnp.full((tq, tq), INT_MIN, jnp.int32)
        return carry

    lax.fori_loop(nc, (nc + ka - min(ATTN_PAD_FROM, ka)) // ka * ka, mask_chunk, 0)

    def count_ge(cand):
        cand_b = jnp.broadcast_to(cand, (tq, tq))

        def body(first, k, cnt):
            for u in range(k):
                hit = jnp.where(key_ref[first + u] >= cand_b, 1.0, 0.0)
                cnt = cnt + jnp.sum(hit, axis=0, keepdims=True)
            return cnt

        return over_chunks(body, kt, jnp.zeros((1, tq), F32))

    st_ref[_KLO] = _float_key(smin)
    st_ref[_KHI] = _float_key(smax) + 1
    st_ref[_CLO] = lax.broadcasted_iota(jnp.int32, (1, tq), 1) + (i * tq + 1)
    st_ref[_CHI] = jnp.zeros((1, tq), jnp.int32)

    def unfinished(klo, khi, clo):
        return (clo > topk) & ((khi - klo) != 1)

    def search_cond(st):
        step, pending = st
        return (step < _SEARCH_CAP) & (pending > 0)

    def any_lane(mask):
        return jnp.max(jnp.where(mask, 1.0, 0.0))

    def search_body(st):
        step, _ = st
        klo, khi, clo = st_ref[_KLO], st_ref[_KHI], st_ref[_CLO]
        live = unfinished(klo, khi, clo)
        pending = any_lane(live)
        vmid = 0.5 * _key_float(klo) + 0.5 * _key_float(khi)
        cmid = _float_key(vmid)
        kmid = klo + lax.shift_right_logical(khi - klo, 1)
        cand = jnp.where((cmid > klo) & (cmid < khi), cmid, kmid)
        cand = jnp.where(live, cand, klo)
        cnt = count_ge(cand).astype(jnp.int32)
        take = cnt >= topk
        klo = jnp.where(live & take, cand, klo)
        clo = jnp.where(live & take, cnt, clo)
        drop = live & jnp.logical_not(take)
        st_ref[_KLO], st_ref[_KHI], st_ref[_CLO] = klo, jnp.where(drop, cand, khi), clo
        st_ref[_CHI] = jnp.where(drop, cnt, st_ref[_CHI])
        return step + 1, pending

    lax.while_loop(search_cond, search_body, (jnp.int32(0), jnp.float32(1.0)))
    thr_b = jnp.broadcast_to(st_ref[_KLO], (tq, tq))

    n_keys = key_ref.shape[0] * tq
    tied = st_ref[_CLO] > topk
    has_ties = any_lane(tied) > 0
    st_ref[_JCUT] = jnp.full((1, tq), n_keys, jnp.int32)

    @pl.when(has_ties)
    def _():
        want = topk - st_ref[_CHI]

        def count_tied_upto(pos):
            pos_b = jnp.broadcast_to(pos, (tq, tq))

            def body(c, cnt):
                hit = (key_ref[c] == thr_b) & (krow + c * tq <= pos_b)
                return cnt + jnp.sum(jnp.where(hit, 1.0, 0.0), axis=0, keepdims=True)

            return lax.fori_loop(0, nc, body, jnp.zeros((1, tq), F32)).astype(jnp.int32)

        def cut_step(_, st):
            lo, hi = st
            mid = lo + ((hi - lo) >> 1)
            enough = count_tied_upto(mid) >= want
            return jnp.where(enough, lo, mid), jnp.where(enough, mid, hi)

        _, cut = lax.fori_loop(0, n_keys.bit_length(), cut_step,
                               (jnp.full((1, tq), -1, jnp.int32), jnp.full((1, tq), n_keys - 1, jnp.int32)))
        st_ref[_JCUT] = jnp.where(tied, cut, n_keys)

    cut_b = jnp.broadcast_to(st_ref[_JCUT], (tq, tq))

    m_ref[...] = jnp.full(m_ref.shape, NEG_BIG, F32)
    l_ref[...] = jnp.zeros(l_ref.shape, F32)
    acc_ref[...] = jnp.zeros(acc_ref.shape, F32)

    def attend_tile(first, k, carry, *, with_ties):
        for u in range(k):
            c = first + u
            key = key_ref[c]
            if with_ties:
                keep = (key > thr_b) | ((key == thr_b) & (krow + c * tq <= cut_b))
            else:
                keep = key >= thr_b
            bias_ref[u] = jnp.where(keep, 0.0, NEG_BIG)

        starts = [chunk_start(first + u) for u in range(k)]
        slots = ATTN_LOOKAHEAD + 1
        dyn0 = jnp.minimum(first, 0)
        ones_rows = jnp.ones((ONES_ROWS, tq), vt_ref.dtype)

        def logits(h):
            cols = slice(h // group * HEAD_DIM, (h // group + 1) * HEAD_DIM)
            qh = q_ref[:, h * HEAD_DIM:(h + 1) * HEAD_DIM]
            top = None
            for u in range(k):
                s = lax.dot_general(k_ref[pl.ds(starts[u], tq), cols], qh, nt,
                                    preferred_element_type=F32) + bias_ref[u]
                s_ref[h % slots + dyn0, u] = s
                top = s if top is None else jnp.maximum(top, s)
            return jnp.max(top, axis=0, keepdims=True)

        queued = [logits(h) for h in range(ATTN_LOOKAHEAD)]
        for h in range(N_HEADS):
            tile_max = queued.pop(0)
            if h + ATTN_LOOKAHEAD < N_HEADS:
                queued.append(logits(h + ATTN_LOOKAHEAD))
            cols = slice(h // group * HEAD_DIM, (h // group + 1) * HEAD_DIM)
            m_prev = m_ref[h]
            m_cur = jnp.maximum(m_prev, tile_max)
            alpha = jnp.exp2(m_prev - m_cur)
            pv = jnp.zeros((HEAD_DIM + ONES_ROWS, tq), F32)
            for u in range(k):
                p = jnp.exp2(s_ref[h % slots + dyn0, u] - m_cur)
                v_aug = jnp.concatenate([vt_ref[cols, pl.ds(starts[u], tq)], ones_rows], axis=0)
                pv = pv + jnp.dot(v_aug, p.astype(vt_ref.dtype), preferred_element_type=F32)
            l_ref[h] = alpha * l_ref[h] + pv[HEAD_DIM:HEAD_DIM + 1, :]
            pv = pv[:HEAD_DIM, :]
            acc_ref[h] = alpha * acc_ref[h] + pv
            m_ref[h] = m_cur
        return carry

    @pl.when(jnp.logical_not(has_ties))
    def _():
        over_chunks(functools.partial(attend_tile, with_ties=False), ka, 0, pad_from=ATTN_PAD_FROM)

    @pl.when(has_ties)
    def _():
        over_chunks(functools.partial(attend_tile, with_ties=True), 1, 0)

    for h in range(N_HEADS):
        o_ref[:, h * HEAD_DIM:(h + 1) * HEAD_DIM] = jnp.transpose(acc_ref[h] / l_ref[h]).astype(o_ref.dtype)


def dsa_core(proj, iw, vt, *, tq, topk, kt, ka):
    t = proj.shape[0]
    assert (t // tq) % kt == 0 and kt % ka == 0
    k_block = (Q_COLS + IQ_COLS) // KV_COLS
    ik_block = (Q_COLS + IQ_COLS + 2 * KV_COLS) // IDX_DIM
    resident = dict(pipeline_mode=pl.Buffered(1))
    need = (4 * _nbytes((tq, Q_COLS), proj.dtype) + 2 * _nbytes((tq, LANES), F32)
            + 2 * _nbytes((t, KV_COLS), proj.dtype) + _nbytes((t, IDX_DIM), proj.dtype)
            + 2 * _nbytes((tq, Q_COLS), proj.dtype)
            + _nbytes((t // tq, tq, tq), jnp.int32) + _nbytes((kt, tq, tq), F32)
            + _nbytes((LANES, tq), F32) + (3 + 2 * N_HEADS) * _nbytes((SUBLANES, tq), F32)
            + _nbytes((N_HEADS, HEAD_DIM, tq), F32)
            + (ATTN_LOOKAHEAD + 3) * _nbytes((ka, tq, tq), F32))
    return pl.pallas_call(
        functools.partial(_dsa_core_kernel, tq=tq, topk=topk, kt=kt, ka=ka),
        grid=(t // tq,),
        in_specs=[
            pl.BlockSpec((tq, Q_COLS), lambda i: (i, 0)),
            pl.BlockSpec((tq, IQ_COLS), lambda i: (i, 1)),
            pl.BlockSpec((tq, LANES), lambda i: (i, 0)),
            pl.BlockSpec((t, KV_COLS), lambda i: (0, k_block), **resident),
            pl.BlockSpec((KV_COLS, t), lambda i: (0, 0), **resident),
            pl.BlockSpec((t, IDX_DIM), lambda i: (0, ik_block), **resident),
        ],
        out_specs=pl.BlockSpec((tq, Q_COLS), lambda i: (i, 0)),
        out_shape=jax.ShapeDtypeStruct((t, Q_COLS), proj.dtype),
        scratch_shapes=[
            pltpu.VMEM((t // tq, tq, tq), jnp.int32),
            pltpu.VMEM((ka, tq, tq), F32),
            pltpu.VMEM((ATTN_LOOKAHEAD + 1, ka, tq, tq), F32),
            pltpu.VMEM((LANES, tq), F32),
            pltpu.VMEM((5, 1, tq), jnp.int32),
            pltpu.VMEM((N_HEADS, 1, tq), F32),
            pltpu.VMEM((N_HEADS, 1, tq), F32),
            pltpu.VMEM((N_HEADS, HEAD_DIM, tq), F32),
        ],
        compiler_params=pltpu.CompilerParams(
            dimension_semantics=("arbitrary",),
            vmem_limit_bytes=_vmem_limit(need)),
        name="dsa_core",
    )(proj, proj, iw, proj, vt, proj)


def _rope_tables(pos):
    def angles(dim):
        inv = 1.0 / (ROPE_THETA ** (jnp.arange(0, dim, 2, dtype=F32) / dim))
        return pos.astype(F32)[:, None] * inv

    ang = angles(HEAD_DIM)
    cos, sin = jnp.cos(ang), jnp.sin(ang)
    rc = jnp.concatenate([cos, cos], axis=1)
    rs = jnp.concatenate([-sin, sin], axis=1)
    iang = angles(IDX_ROPE_DIM)
    icos, isin = jnp.cos(iang), jnp.sin(iang)
    rest = LANES - IDX_ROPE_DIM
    zeros = jnp.zeros_like(isin)
    ic = jnp.concatenate([icos, icos, jnp.ones((pos.shape[0], rest), F32)], axis=1)
    ia = jnp.concatenate([-isin, zeros, jnp.zeros((pos.shape[0], rest), F32)], axis=1)
    ib = jnp.concatenate([zeros, isin, jnp.zeros((pos.shape[0], rest), F32)], axis=1)
    return rc, rs, ic, ia, ib


def dsa_layer(x, pos, norm_g, w_in_layers, layer, q_norm, k_norm, w_out, tiles):
    t = x.shape[0]
    proj, iw, vt = dsa_projection(x, norm_g, jnp.swapaxes(w_in_layers, 1, 2), layer, q_norm, k_norm,
                                  _rope_tables(pos), tm=tiles["proj_tm"])
    attn = dsa_core(proj, iw, vt, tq=tiles["tq"], topk=min(TOPK_MAX, t // 4), kt=tiles["kt"], ka=tiles["ka"])
    return matmul_residual(attn, w_out.astype(MXU_DTYPE), x, tm=tiles["res_tm"], tn=tiles["res_tn"])


def _rglru_kernel(gate_ref, xr_ref, cw_ref, cb_ref, wa_ref, ba_ref, wx_ref, bx_ref, lam_ref,
                  y_ref, hcar_ref, xprev_ref, *, tt):
    @pl.when(pl.program_id(1) == 0)
    def _():
        hcar_ref[...] = jnp.zeros_like(hcar_ref)
        xprev_ref[...] = jnp.zeros_like(xprev_ref)

    xr = xr_ref[...]
    ext = jnp.concatenate([xprev_ref[...], xr], axis=0)
    cw = cw_ref[...]
    xc = cb_ref[...] + xr * cw[CONV_WIDTH - 1:CONV_WIDTH, :]
    for d in range(1, CONV_WIDTH):
        xc = xc + pltpu.roll(ext, d, 0)[SUBLANES:, :] * cw[CONV_WIDTH - 1 - d:CONV_WIDTH - d, :]
    xprev_ref[...] = xr[tt - SUBLANES:, :]

    xcb = xc.astype(wa_ref.dtype)
    r = jax.nn.sigmoid(jnp.dot(xcb, wa_ref[...], preferred_element_type=F32) + ba_ref[...])
    ig = jax.nn.sigmoid(jnp.dot(xcb, wx_ref[...], preferred_element_type=F32) + bx_ref[...])
    nlam = -lam_ref[...]
    softplus = jnp.maximum(nlam, 0.0) + jnp.log(1.0 + jnp.exp(-jnp.abs(nlam)))
    log_a = -LRU_C * r * softplus
    a = jnp.exp(log_a)
    mult = jnp.sqrt(1.0 - a * a)
    b = xc * ig * mult

    n_groups = tt // SUBLANES
    a = a.reshape(n_groups, SUBLANES, a.shape[1])
    b = b.reshape(a.shape)
    in_group = lax.broadcasted_iota(jnp.int32, a.shape, 1)
    d = 1
    while d < SUBLANES:
        b = a * jnp.where(in_group >= d, pltpu.roll(b, d, 1), 0.0) + b
        a = a * jnp.where(in_group >= d, pltpu.roll(a, d, 1), 1.0)
        d *= 2
    carry = hcar_ref[...]
    groups = []
    for gi in range(n_groups):
        hg = b[gi] + a[gi] * carry
        carry = hg[SUBLANES - 1:SUBLANES, :]
        groups.append(hg)
    hcar_ref[...] = carry
    h = jnp.concatenate(groups, axis=0)
    y_ref[...] = (h * jax.nn.gelu(gate_ref[...])).astype(y_ref.dtype)


def rglru_scan(proj, conv_w, conv_b, wa, ba, wx, bx, lam, *, tt):
    t = proj.shape[0]
    width = conv_w.shape[1]
    nb, blk, _ = wa.shape
    vec = lambda: pl.BlockSpec((1, blk), lambda n, s: (0, n))
    need = (4 * _nbytes((tt, blk), F32) + 2 * _nbytes((tt, blk), MXU_DTYPE)
            + 4 * _nbytes((blk, blk), wa.dtype) + 24 * _nbytes((tt, blk), F32))
    return pl.pallas_call(
        functools.partial(_rglru_kernel, tt=tt),
        grid=(nb, t // tt),
        in_specs=[
            pl.BlockSpec((tt, blk), lambda n, s: (s, n)),
            pl.BlockSpec((tt, blk), lambda n, s: (s, nb + n)),
            pl.BlockSpec((CONV_WIDTH, blk), lambda n, s: (0, n)),
            vec(),
            pl.BlockSpec((None, blk, blk), lambda n, s: (n, 0, 0)),
            vec(),
            pl.BlockSpec((None, blk, blk), lambda n, s: (n, 0, 0)),
            vec(),
            vec(),
        ],
        out_specs=pl.BlockSpec((tt, blk), lambda n, s: (s, n)),
        out_shape=jax.ShapeDtypeStruct((t, width), MXU_DTYPE),
        scratch_shapes=[pltpu.VMEM((1, blk), F32), pltpu.VMEM((SUBLANES, blk), F32)],
        compiler_params=pltpu.CompilerParams(
            dimension_semantics=("parallel", "arbitrary"),
            vmem_limit_bytes=_vmem_limit(need)),
        name="rglru_scan",
    )(proj, proj, conv_w, conv_b.reshape(1, width), wa, ba.reshape(1, width),
      wx, bx.reshape(1, width), lam.reshape(1, width))


def rglru_layer(x, norm_g, w_in, conv_w, conv_b, wa, ba, wx, bx, lam, w_out, tiles):
    proj = norm_matmul(x, norm_g, w_in.astype(MXU_DTYPE), tm=tiles["proj_tm"], tn=tiles["proj_tn"],
                       out_dtype=F32)
    y = rglru_scan(proj, conv_w, conv_b, wa.astype(MXU_DTYPE), ba, wx.astype(MXU_DTYPE), bx, lam,
                   tt=tiles["scan_tt"])
    return matmul_residual(y, w_out.astype(MXU_DTYPE), x, tm=tiles["res_tm"], tn=tiles["res_tn"])


POOL_HALO = max(POOL_WINDOWS)


def _pool_kernel(x_ref, g_ref, w_ref, b_ref, s_ref, o_ref, halo_ref, *, tt):
    blk = pl.program_id(0)

    @pl.when(blk == 0)
    def _():
        halo_ref[...] = jnp.zeros_like(halo_ref)

    x = x_ref[...]
    h = _rms_norm_rows(x, g_ref[...])
    ext = jnp.concatenate([halo_ref[...], h], axis=0)
    halo_ref[...] = h[tt - POOL_HALO:, :]

    gw = h.shape[1] // len(POOL_WINDOWS)
    t1 = (lax.broadcasted_iota(jnp.int32, (tt, gw), 0) + (blk * tt + 1)).astype(F32)
    for gi, win in enumerate(POOL_WINDOWS):
        cols = slice(gi * gw, (gi + 1) * gw)
        acc = ext[:, cols]
        d = 1
        while d < win:
            acc = acc + pltpu.roll(acc, d, 0)
            d *= 2
        mean = acc[POOL_HALO:, :] / jnp.minimum(t1, float(win))
        y = (mean - h[:, cols]).astype(w_ref.dtype)
        z = jnp.dot(y, w_ref[gi], preferred_element_type=F32) + b_ref[gi:gi + 1, :]
        o_ref[:, cols] = x[:, cols] + z * s_ref[:, cols]


def pool_layer(x, norm_g, w_group, b_group, scale, tiles):
    t, d = x.shape
    tt = tiles["pool_tt"]
    ng, gw, _ = w_group.shape
    need = (4 * _nbytes((tt, d), F32) + 2 * _nbytes((ng, gw, gw), MXU_DTYPE)
            + 6 * _nbytes((tt, d), F32))
    return pl.pallas_call(
        functools.partial(_pool_kernel, tt=tt),
        grid=(t // tt,),
        in_specs=[
            pl.BlockSpec((tt, d), lambda i: (i, 0)),
            pl.BlockSpec((1, d), lambda i: (0, 0)),
            pl.BlockSpec((ng, gw, gw), lambda i: (0, 0, 0)),
            pl.BlockSpec((ng, gw), lambda i: (0, 0)),
            pl.BlockSpec((1, d), lambda i: (0, 0)),
        ],
        out_specs=pl.BlockSpec((tt, d), lambda i: (i, 0)),
        out_shape=jax.ShapeDtypeStruct((t, d), F32),
        scratch_shapes=[pltpu.VMEM((POOL_HALO, d), F32)],
        compiler_params=pltpu.CompilerParams(
            dimension_semantics=("arbitrary",),
            vmem_limit_bytes=_vmem_limit(need)),
        name="pool_mixer",
    )(x, norm_g.reshape(1, d), w_group.astype(MXU_DTYPE), b_group, scale.reshape(1, d))


def _tiles(t):
    big = min(t, 512)
    tall = min(t, 1024)
    return {
        "proj_tm": tall, "proj_tn": 1024,
        "res_tm": tall, "res_tn": 1024,
        "tq": min(t, 256), "kt": min(4, t // min(t, 256)), "ka": min(4, t // min(t, 256)),
        "scan_tt": min(t, 512),
        "pool_tt": min(t, 256),
        "mlp_tm": big, "mlp_tf": 1024,
    }


def kernel(x, positions, attn_norm, attn_w_in, attn_q_norm, attn_k_norm, attn_w_out, rnn_norm, rnn_w_in, rnn_conv_w, rnn_conv_b, rnn_gate_a_w, rnn_gate_a_b, rnn_gate_x_w, rnn_gate_x_b, rnn_lambda, rnn_w_out, pool_norm, pool_w, pool_b, pool_scale, mlp_norm, mlp_w_up, mlp_w_down):
    batch, t, d = x.shape
    depth = mlp_norm.shape[0]
    tiles = _tiles(t)
    w_up_all = mlp_w_up.astype(MXU_DTYPE)
    w_down_all = mlp_w_down.astype(MXU_DTYPE)
    outs = []
    for bi in range(batch):
        xb = x[bi]
        pos = positions[bi]
        for i in range(depth):
            kind, j = i % N_MIXERS, i // N_MIXERS
            if kind == 0:
                xb = dsa_layer(xb, pos, attn_norm[j], attn_w_in, j, attn_q_norm[j], attn_k_norm[j],
                               attn_w_out[j], tiles)
            elif kind == 1:
                xb = rglru_layer(xb, rnn_norm[j], rnn_w_in[j], rnn_conv_w[j], rnn_conv_b[j],
                                 rnn_gate_a_w[j], rnn_gate_a_b[j], rnn_gate_x_w[j], rnn_gate_x_b[j],
                                 rnn_lambda[j], rnn_w_out[j], tiles)
            else:
                xb = pool_layer(xb, pool_norm[j], pool_w[j], pool_b[j], pool_scale[j], tiles)
            xb = mlp_block(xb, mlp_norm[i], w_up_all, w_down_all, i,
                           tm=tiles["mlp_tm"], tf=tiles["mlp_tf"])
        outs.append(xb)
    return outs[0][None] if batch == 1 else jnp.stack(outs, axis=0)
```

```python
import functools
import math

import jax
import jax.numpy as jnp
from jax import lax
from jax.experimental import pallas as pl
from jax.experimental.pallas import tpu as pltpu

F32 = jnp.float32
MXU_DTYPE = jnp.bfloat16

N_MIXERS = 3
EPS = 1e-6
ROPE_THETA = 10000.0
HEAD_DIM = 128
N_HEADS = 16
N_KV_HEADS = 4
IDX_HEADS = 16
IDX_DIM = 128
IDX_ROPE_DIM = 64
TOPK_MAX = 256
CONV_WIDTH = 4
LRU_C = 8.0
POOL_WINDOWS = (2, 4, 8, 16)

LANES = 128
SUBLANES = 8
VMEM_BYTES_V7X = 64 * 1024 * 1024
VMEM_CAP_BYTES = VMEM_BYTES_V7X - 8 * 1024 * 1024

INT_MIN = -(2 ** 31)
NEG_BIG = -1e30


def _vmem_limit(block_bytes):
    return int(min(VMEM_CAP_BYTES, block_bytes * 3 // 2 + (4 << 20)))


def _nbytes(shape, dtype):
    return math.prod(shape) * jnp.dtype(dtype).itemsize


def _rms_norm_rows(x, g):
    ms = jnp.mean(x * x, axis=-1, keepdims=True)
    return x * lax.rsqrt(ms + EPS) * g


def _norm_matmul_kernel(x_ref, g_ref, w_ref, o_ref, h_ref):
    @pl.when(pl.program_id(1) == 0)
    def _():
        h_ref[...] = _rms_norm_rows(x_ref[...], g_ref[...]).astype(h_ref.dtype)

    o_ref[...] = jnp.dot(h_ref[...], w_ref[...], preferred_element_type=F32).astype(o_ref.dtype)


def norm_matmul(x, g, w, *, tm, tn, out_dtype):
    t, d = x.shape
    n = w.shape[1]
    need = (2 * _nbytes((tm, d), F32) + 2 * _nbytes((d, tn), w.dtype)
            + 2 * _nbytes((tm, tn), out_dtype) + _nbytes((tm, d), w.dtype))
    return pl.pallas_call(
        _norm_matmul_kernel,
        grid=(t // tm, n // tn),
        in_specs=[
            pl.BlockSpec((tm, d), lambda i, j: (i, 0)),
            pl.BlockSpec((1, d), lambda i, j: (0, 0)),
            pl.BlockSpec((d, tn), lambda i, j: (0, j)),
        ],
        out_specs=pl.BlockSpec((tm, tn), lambda i, j: (i, j)),
        out_shape=jax.ShapeDtypeStruct((t, n), out_dtype),
        scratch_shapes=[pltpu.VMEM((tm, d), w.dtype)],
        compiler_params=pltpu.CompilerParams(
            dimension_semantics=("parallel", "arbitrary"),
            vmem_limit_bytes=_vmem_limit(need)),
        name="norm_matmul",
    )(x, g.reshape(1, d), w)


def _matmul_residual_kernel(a_ref, w_ref, x_ref, o_ref):
    o_ref[...] = x_ref[...] + jnp.dot(a_ref[...], w_ref[...], preferred_element_type=F32)


def matmul_residual(a, w, x, *, tm, tn):
    t, k = a.shape
    n = w.shape[1]
    need = (2 * _nbytes((tm, k), a.dtype) + 2 * _nbytes((k, tn), w.dtype)
            + 4 * _nbytes((tm, tn), F32))
    return pl.pallas_call(
        _matmul_residual_kernel,
        grid=(t // tm, n // tn),
        in_specs=[
            pl.BlockSpec((tm, k), lambda i, j: (i, 0)),
            pl.BlockSpec((k, tn), lambda i, j: (0, j)),
            pl.BlockSpec((tm, tn), lambda i, j: (i, j)),
        ],
        out_specs=pl.BlockSpec((tm, tn), lambda i, j: (i, j)),
        out_shape=jax.ShapeDtypeStruct((t, n), F32),
        compiler_params=pltpu.CompilerParams(
            dimension_semantics=("parallel", "arbitrary"),
            vmem_limit_bytes=_vmem_limit(need)),
        name="matmul_residual",
    )(a, w, x)


def _mlp_kernel(x_ref, g_ref, wu_ref, wd_ref, o_ref, h_ref, acc_ref):
    f = pl.program_id(1)

    @pl.when(f == 0)
    def _():
        h_ref[...] = _rms_norm_rows(x_ref[...], g_ref[...]).astype(h_ref.dtype)
        acc_ref[...] = jnp.zeros_like(acc_ref)

    u = jnp.dot(h_ref[...], wu_ref[...], preferred_element_type=F32)
    u = jnp.square(jnp.maximum(u, 0.0)).astype(wd_ref.dtype)
    acc_ref[...] += jnp.dot(u, wd_ref[...], preferred_element_type=F32)

    @pl.when(f == pl.num_programs(1) - 1)
    def _():
        o_ref[...] = x_ref[...] + acc_ref[...]


def mlp_block(x, g, w_up, w_down, layer, *, tm, tf):
    t, d = x.shape
    ff = w_up.shape[2]
    need = (4 * _nbytes((tm, d), F32) + 2 * _nbytes((d, tf), w_up.dtype)
            + 2 * _nbytes((tf, d), w_down.dtype) + _nbytes((tm, d), w_up.dtype)
            + _nbytes((tm, d), F32) + 2 * _nbytes((tm, tf), F32))
    return pl.pallas_call(
        _mlp_kernel,
        grid=(t // tm, ff // tf),
        in_specs=[
            pl.BlockSpec((tm, d), lambda i, f: (i, 0)),
            pl.BlockSpec((1, d), lambda i, f: (0, 0)),
            pl.BlockSpec((None, d, tf), lambda i, f: (layer, 0, f)),
            pl.BlockSpec((None, tf, d), lambda i, f: (layer, f, 0)),
        ],
        out_specs=pl.BlockSpec((tm, d), lambda i, f: (i, 0)),
        out_shape=jax.ShapeDtypeStruct((t, d), F32),
        scratch_shapes=[pltpu.VMEM((tm, d), w_up.dtype), pltpu.VMEM((tm, d), F32)],
        compiler_params=pltpu.CompilerParams(
            dimension_semantics=("parallel", "arbitrary"),
            vmem_limit_bytes=_vmem_limit(need)),
        name="mlp_block",
    )(x, g.reshape(1, d), w_up, w_down)


DSA_TN = 4 * LANES
Q_COLS = N_HEADS * HEAD_DIM
IQ_COLS = IDX_HEADS * IDX_DIM
KV_COLS = N_KV_HEADS * HEAD_DIM
DSA_COLS = Q_COLS + IQ_COLS + 2 * KV_COLS + DSA_TN
Q_BLOCKS = Q_COLS // DSA_TN
IQ_BLOCKS = IQ_COLS // DSA_TN
KV_BLOCKS = KV_COLS // DSA_TN


W_K_FIRST = Q_BLOCKS
W_V_FIRST = W_K_FIRST + KV_BLOCKS
W_IQ_FIRST = W_V_FIRST + KV_BLOCKS
W_LAST = W_IQ_FIRST + IQ_BLOCKS
W_TAIL_ROWS = IDX_DIM + IDX_HEADS
OUT_K_FIRST = Q_BLOCKS + IQ_BLOCKS


def _dsa_out_block(jw):
    return jnp.where(jw < W_K_FIRST, jw,
                     jnp.where(jw < W_IQ_FIRST, jw + (OUT_K_FIRST - W_K_FIRST),
                               jnp.where(jw < W_LAST, jw - (W_IQ_FIRST - Q_BLOCKS), W_LAST)))


def _dsa_proj_kernel(x_ref, g_ref, w_ref, qn_ref, kn_ref, rc_ref, rs_ref,
                     ic_ref, ia_ref, ib_ref, o_ref, iw_ref, vt_ref, h_ref, y_ref):
    j = pl.program_id(1)
    jp = j - 1

    @pl.when(j == 0)
    def _():
        h_ref[...] = _rms_norm_rows(x_ref[...], g_ref[...]).astype(h_ref.dtype)

    def rope(z):
        return z * rc_ref[...] + pltpu.roll(z, HEAD_DIM // 2, 1) * rs_ref[...]

    def idx_rope(z):
        half = IDX_ROPE_DIM // 2
        return (z * ic_ref[...] + pltpu.roll(z, LANES - half, 1) * ia_ref[...]
                + pltpu.roll(z, half, 1) * ib_ref[...])

    def per_head(fn):
        def finish(y):
            o_ref[...] = jnp.concatenate(
                [fn(y[:, s * LANES:(s + 1) * LANES]) for s in range(DSA_TN // LANES)],
                axis=1).astype(o_ref.dtype)
        return finish

    def step(finish, multiply=True, last_block_possible=False):
        if finish is not None:
            y_prev = y_ref[...]
        if multiply:
            w = w_ref[...]
            if last_block_possible:
                row = lax.broadcasted_iota(jnp.int32, w.shape, 0)
                w = jnp.where((j == W_LAST) & (row >= W_TAIL_ROWS), 0.0, w)
            y_ref[...] = lax.dot_general(
                h_ref[...], w.astype(h_ref.dtype), (((1,), (1,)), ((), ())),
                preferred_element_type=F32)
        if finish is not None:
            finish(y_prev)

    @pl.when(j == 0)
    def _():
        step(None)

    @pl.when((jp >= 0) & (jp < W_K_FIRST))
    def _():
        scale = HEAD_DIM ** -0.5 * math.log2(math.e)
        step(per_head(lambda z: rope(_rms_norm_rows(z, qn_ref[...])) * scale))

    @pl.when((jp >= W_K_FIRST) & (jp < W_V_FIRST))
    def _():
        step(per_head(lambda z: rope(_rms_norm_rows(z, kn_ref[...]))))

    @pl.when((jp >= W_V_FIRST) & (jp < W_IQ_FIRST))
    def _():
        def finish(y):
            o_ref[...] = y.astype(o_ref.dtype)
            vt_ref[...] = jnp.transpose(y).astype(vt_ref.dtype)
        step(finish)

    @pl.when((jp >= W_IQ_FIRST) & (jp < W_LAST))
    def _():
        step(per_head(idx_rope), last_block_possible=True)

    @pl.when(jp == W_LAST)
    def _():
        def finish(y):
            o_ref[...] = jnp.concatenate(
                [idx_rope(y[:, :LANES]), jnp.zeros_like(y[:, LANES:])], axis=1).astype(o_ref.dtype)
            iw_ref[...] = y[:, LANES:2 * LANES] * (IDX_HEADS ** -0.5 * IDX_DIM ** -0.5)
        step(finish, multiply=False)


def dsa_projection(x, g, w_layers, layer, q_norm, k_norm, tables, *, tm):
    t, d = x.shape
    assert w_layers.shape[1] == Q_COLS + 2 * KV_COLS + IQ_COLS + IDX_DIM + IDX_HEADS
    rc, rs, ic, ia, ib = tables
    tab_spec = pl.BlockSpec((tm, LANES), lambda i, j: (i, 0))
    vec_spec = pl.BlockSpec((1, LANES), lambda i, j: (0, 0))
    need = (2 * _nbytes((tm, d), F32) + 3 * _nbytes((d, DSA_TN), F32)
            + 2 * _nbytes((tm, DSA_TN), MXU_DTYPE) + _nbytes((tm, d), MXU_DTYPE)
            + 12 * _nbytes((tm, LANES), F32) + 3 * _nbytes((tm, DSA_TN), F32))
    return pl.pallas_call(
        _dsa_proj_kernel,
        grid=(t // tm, W_LAST + 2),
        in_specs=[
            pl.BlockSpec((tm, d), lambda i, j: (i, 0)),
            pl.BlockSpec((1, d), lambda i, j: (0, 0)),
            pl.BlockSpec((None, DSA_TN, d), lambda i, j: (layer, jnp.minimum(j, W_LAST), 0)),
            vec_spec, vec_spec, tab_spec, tab_spec, tab_spec, tab_spec, tab_spec,
        ],
        out_specs=[
            pl.BlockSpec((tm, DSA_TN), lambda i, j: (i, _dsa_out_block(jnp.maximum(j - 1, 0)))),
            pl.BlockSpec((tm, LANES), lambda i, j: (i, 0)),
            pl.BlockSpec((DSA_TN, tm), lambda i, j: (jnp.clip(j - 1 - W_V_FIRST, 0, KV_BLOCKS - 1), i)),
        ],
        out_shape=[
            jax.ShapeDtypeStruct((t, DSA_COLS), MXU_DTYPE),
            jax.ShapeDtypeStruct((t, LANES), F32),
            jax.ShapeDtypeStruct((KV_COLS, t), MXU_DTYPE),
        ],
        scratch_shapes=[pltpu.VMEM((tm, d), MXU_DTYPE), pltpu.VMEM((tm, DSA_TN), F32)],
        compiler_params=pltpu.CompilerParams(
            dimension_semantics=("parallel", "arbitrary"),
            vmem_limit_bytes=_vmem_limit(need)),
        name="dsa_projection",
    )(x, g.reshape(1, d), w_layers, q_norm.reshape(1, LANES), k_norm.reshape(1, LANES), rc, rs, ic, ia, ib)


def _float_key(v):
    b = lax.bitcast_convert_type(v, jnp.int32)
    return b ^ ((b >> 31) & 0x7FFFFFFF)


def _key_float(k):
    return lax.bitcast_convert_type(k ^ ((k >> 31) & 0x7FFFFFFF), F32)


_KLO, _KHI, _CLO, _CHI, _JCUT = range(5)
_SEARCH_CAP = 96
ONES_ROWS = 2 * SUBLANES
ATTN_PAD_FROM = 2
ATTN_LOOKAHEAD = 1


def _dsa_core_kernel(q_ref, iq_ref, iw_ref, k_ref, vt_ref, ik_ref, o_ref,
                     key_ref, bias_ref, s_ref, wt_ref, st_ref, m_ref, l_ref, acc_ref, *, tq, topk, kt, ka):
    i = pl.program_id(0)
    nc = i + 1
    group = N_HEADS // N_KV_HEADS
    nt = (((1,), (1,)), ((), ()))

    wt_ref[...] = jnp.transpose(iw_ref[...])

    krow = lax.broadcasted_iota(jnp.int32, (tq, tq), 0)
    qcol = lax.broadcasted_iota(jnp.int32, (tq, tq), 1)
    diag_causal = krow <= qcol

    def chunk_start(c):
        return pl.multiple_of(c * tq, tq)

    def over_chunks(body, k, init, pad_from=None):
        n_tiles = (nc + k - min(pad_from or k, k)) // k
        carry = lax.fori_loop(0, n_tiles, lambda ti, c: body(ti * k, k, c), init)
        if k > 1:
            carry = lax.fori_loop(n_tiles * k, nc, lambda c0, c: body(c0, 1, c), carry)
        return carry

    def index_tile(first, k, carry):
        smin, smax = carry
        iks = [ik_ref[pl.ds(chunk_start(first + u), tq), :] for u in range(k)]
        scores = [jnp.zeros((tq, tq), F32) for _ in range(k)]
        for h in range(IDX_HEADS):
            iqh = iq_ref[:, h * IDX_DIM:(h + 1) * IDX_DIM]
            w = jnp.broadcast_to(wt_ref[h:h + 1, :], (tq, tq))
            for u in range(k):
                logits = lax.dot_general(iks[u], iqh, nt, preferred_element_type=F32)
                scores[u] = scores[u] + w * jnp.maximum(logits, 0.0)
        for u in range(k):
            c = first + u
            valid = (c < i) | diag_causal
            key_ref[c] = jnp.where(valid, _float_key(scores[u]), INT_MIN)
            smin = jnp.minimum(smin, jnp.min(jnp.where(valid, scores[u], jnp.inf), axis=0, keepdims=True))
            smax = jnp.maximum(smax, jnp.max(jnp.where(valid, scores[u], -jnp.inf), axis=0, keepdims=True))
        return smin, smax

    smin, smax = over_chunks(
        index_tile, kt, (jnp.full((1, tq), jnp.inf, F32), jnp.full((1, tq), -jnp.inf, F32)))

    def mask_chunk(c, carry):
        key_ref[c] = jnp.full((tq, tq), INT_MIN, jnp.int32)
        return carry

    lax.fori_loop(nc, (nc + ka - min(ATTN_PAD_FROM, ka)) // ka * ka, mask_chunk, 0)

    def count_ge(cand):
        cand_b = jnp.broadcast_to(cand, (tq, tq))

        def body(first, k, cnt):
            for u in range(k):
                hit = jnp.where(key_ref[first + u] >= cand_b, 1.0, 0.0)
                cnt = cnt + jnp.sum(hit, axis=0, keepdims=True)
            return cnt

        return over_chunks(body, kt, jnp.zeros((1, tq), F32))

    st_ref[_KLO] = _float_key(smin)
    st_ref[_KHI] = _float_key(smax) + 1
    st_ref[_CLO] = lax.broadcasted_iota(jnp.int32, (1, tq), 1) + (i * tq + 1)
    st_ref[_CHI] = jnp.zeros((1, tq), jnp.int32)

    def unfinished(klo, khi, clo):
        return (clo > topk) & ((khi - klo) != 1)

    def search_cond(st):
        step, pending = st
        return (step < _SEARCH_CAP) & (pending > 0)

    def any_lane(mask):
        return jnp.max(jnp.where(mask, 1.0, 0.0))

    def search_body(st):
        step, _ = st
        klo, khi, clo = st_ref[_KLO], st_ref[_KHI], st_ref[_CLO]
        live = unfinished(klo, khi, clo)
        pending = any_lane(live)
        vmid = 0.5 * _key_float(klo) + 0.5 * _key_float(khi)
        cmid = _float_key(vmid)
        kmid = klo + lax.shift_right_logical(khi - klo, 1)
        cand = jnp.where((cmid > klo) & (cmid < khi), cmid, kmid)
        cand = jnp.where(live, cand, klo)
        cnt = count_ge(cand).astype(jnp.int32)
        take = cnt >= topk
        klo = jnp.where(live & take, cand, klo)
        clo = jnp.where(live & take, cnt, clo)
        drop = live & jnp.logical_not(take)
        st_ref[_KLO], st_ref[_KHI], st_ref[_CLO] = klo, jnp.where(drop, cand, khi), clo
        st_ref[_CHI] = jnp.where(drop, cnt, st_ref[_CHI])
        return step + 1, pending

    lax.while_loop(search_cond, search_body, (jnp.int32(0), jnp.float32(1.0)))
    thr_b = jnp.broadcast_to(st_ref[_KLO], (tq, tq))

    n_keys = key_ref.shape[0] * tq
    tied = st_ref[_CLO] > topk
    has_ties = any_lane(tied) > 0
    st_ref[_JCUT] = jnp.full((1, tq), n_keys, jnp.int32)

    @pl.when(has_ties)
    def _():
        want = topk - st_ref[_CHI]

        def count_tied_upto(pos):
            pos_b = jnp.broadcast_to(pos, (tq, tq))

            def body(c, cnt):
                hit = (key_ref[c] == thr_b) & (krow + c * tq <= pos_b)
                return cnt + jnp.sum(jnp.where(hit, 1.0, 0.0), axis=0, keepdims=True)

            return lax.fori_loop(0, nc, body, jnp.zeros((1, tq), F32)).astype(jnp.int32)

        def cut_step(_, st):
            lo, hi = st
            mid = lo + ((hi - lo) >> 1)
            enough = count_tied_upto(mid) >= want
            return jnp.where(enough, lo, mid), jnp.where(enough, mid, hi)

        _, cut = lax.fori_loop(0, n_keys.bit_length(), cut_step,
                               (jnp.full((1, tq), -1, jnp.int32), jnp.full((1, tq), n_keys - 1, jnp.int32)))
        st_ref[_JCUT] = jnp.where(tied, cut, n_keys)

    cut_b = jnp.broadcast_to(st_ref[_JCUT], (tq, tq))

    m_ref[...] = jnp.full(m_ref.shape, NEG_BIG, F32)
    l_ref[...] = jnp.zeros(l_ref.shape, F32)
    acc_ref[...] = jnp.zeros(acc_ref.shape, F32)

    def attend_tile(first, k, carry, *, with_ties):
        for u in range(k):
            c = first + u
            key = key_ref[c]
            if with_ties:
                keep = (key > thr_b) | ((key == thr_b) & (krow + c * tq <= cut_b))
            else:
                keep = key >= thr_b
            bias_ref[u] = jnp.where(keep, 0.0, NEG_BIG)

        starts = [chunk_start(first + u) for u in range(k)]
        slots = ATTN_LOOKAHEAD + 1
        dyn0 = jnp.minimum(first, 0)
        ones_rows = jnp.ones((ONES_ROWS, tq), vt_ref.dtype)

        def logits(h):
            cols = slice(h // group * HEAD_DIM, (h // group + 1) * HEAD_DIM)
            qh = q_ref[:, h * HEAD_DIM:(h + 1) * HEAD_DIM]
            top = None
            for u in range(k):
                s = lax.dot_general(k_ref[pl.ds(starts[u], tq), cols], qh, nt,
                                    preferred_element_type=F32) + bias_ref[u]
                s_ref[h % slots + dyn0, u] = s
                top = s if top is None else jnp.maximum(top, s)
            return jnp.max(top, axis=0, keepdims=True)

        queued = [logits(h) for h in range(ATTN_LOOKAHEAD)]
        for h in range(N_HEADS):
            tile_max = queued.pop(0)
            if h + ATTN_LOOKAHEAD < N_HEADS:
                queued.append(logits(h + ATTN_LOOKAHEAD))
            cols = slice(h // group * HEAD_DIM, (h // group + 1) * HEAD_DIM)
            m_prev = m_ref[h]
            m_cur = jnp.maximum(m_prev, tile_max)
            alpha = jnp.exp2(m_prev - m_cur)
            pv = jnp.zeros((HEAD_DIM + ONES_ROWS, tq), F32)
            for u in range(k):
                p = jnp.exp2(s_ref[h % slots + dyn0, u] - m_cur)
                v_aug = jnp.concatenate([vt_ref[cols, pl.ds(starts[u], tq)], ones_rows], axis=0)
                pv = pv + jnp.dot(v_aug, p.astype(vt_ref.dtype), preferred_element_type=F32)
            l_ref[h] = alpha * l_ref[h] + pv[HEAD_DIM:HEAD_DIM + 1, :]
            pv = pv[:HEAD_DIM, :]
            acc_ref[h] = alpha * acc_ref[h] + pv
            m_ref[h] = m_cur
        return carry

    @pl.when(jnp.logical_not(has_ties))
    def _():
        over_chunks(functools.partial(attend_tile, with_ties=False), ka, 0, pad_from=ATTN_PAD_FROM)

    @pl.when(has_ties)
    def _():
        over_chunks(functools.partial(attend_tile, with_ties=True), 1, 0)

    for h in range(N_HEADS):
        o_ref[:, h * HEAD_DIM:(h + 1) * HEAD_DIM] = jnp.transpose(acc_ref[h] / l_ref[h]).astype(o_ref.dtype)


def dsa_core(proj, iw, vt, *, tq, topk, kt, ka):
    t = proj.shape[0]
    assert (t // tq) % kt == 0 and kt % ka == 0
    k_block = (Q_COLS + IQ_COLS) // KV_COLS
    ik_block = (Q_COLS + IQ_COLS + 2 * KV_COLS) // IDX_DIM
    resident = dict(pipeline_mode=pl.Buffered(1))
    need = (4 * _nbytes((tq, Q_COLS), proj.dtype) + 2 * _nbytes((tq, LANES), F32)
            + 2 * _nbytes((t, KV_COLS), proj.dtype) + _nbytes((t, IDX_DIM), proj.dtype)
            + 2 * _nbytes((tq, Q_COLS), proj.dtype)
            + _nbytes((t // tq, tq, tq), jnp.int32) + _nbytes((kt, tq, tq), F32)
            + _nbytes((LANES, tq), F32) + (3 + 2 * N_HEADS) * _nbytes((SUBLANES, tq), F32)
            + _nbytes((N_HEADS, HEAD_DIM, tq), F32)
            + (ATTN_LOOKAHEAD + 3) * _nbytes((ka, tq, tq), F32))
    return pl.pallas_call(
        functools.partial(_dsa_core_kernel, tq=tq, topk=topk, kt=kt, ka=ka),
        grid=(t // tq,),
        in_specs=[
            pl.BlockSpec((tq, Q_COLS), lambda i: (i, 0)),
            pl.BlockSpec((tq, IQ_COLS), lambda i: (i, 1)),
            pl.BlockSpec((tq, LANES), lambda i: (i, 0)),
            pl.BlockSpec((t, KV_COLS), lambda i: (0, k_block), **resident),
            pl.BlockSpec((KV_COLS, t), lambda i: (0, 0), **resident),
            pl.BlockSpec((t, IDX_DIM), lambda i: (0, ik_block), **resident),
        ],
        out_specs=pl.BlockSpec((tq, Q_COLS), lambda i: (i, 0)),
        out_shape=jax.ShapeDtypeStruct((t, Q_COLS), proj.dtype),
        scratch_shapes=[
            pltpu.VMEM((t // tq, tq, tq), jnp.int32),
            pltpu.VMEM((ka, tq, tq), F32),
            pltpu.VMEM((ATTN_LOOKAHEAD + 1, ka, tq, tq), F32),
            pltpu.VMEM((LANES, tq), F32),
            pltpu.VMEM((5, 1, tq), jnp.int32),
            pltpu.VMEM((N_HEADS, 1, tq), F32),
            pltpu.VMEM((N_HEADS, 1, tq), F32),
            pltpu.VMEM((N_HEADS, HEAD_DIM, tq), F32),
        ],
        compiler_params=pltpu.CompilerParams(
            dimension_semantics=("arbitrary",),
            vmem_limit_bytes=_vmem_limit(need)),
        name="dsa_core",
    )(proj, proj, iw, proj, vt, proj)


def _rope_tables(pos):
    def angles(dim):
        inv = 1.0 / (ROPE_THETA ** (jnp.arange(0, dim, 2, dtype=F32) / dim))
        return pos.astype(F32)[:, None] * inv

    ang = angles(HEAD_DIM)
    cos, sin = jnp.cos(ang), jnp.sin(ang)
    rc = jnp.concatenate([cos, cos], axis=1)
    rs = jnp.concatenate([-sin, sin], axis=1)
    iang = angles(IDX_ROPE_DIM)
    icos, isin = jnp.cos(iang), jnp.sin(iang)
    rest = LANES - IDX_ROPE_DIM
    zeros = jnp.zeros_like(isin)
    ic = jnp.concatenate([icos, icos, jnp.ones((pos.shape[0], rest), F32)], axis=1)
    ia = jnp.concatenate([-isin, zeros, jnp.zeros((pos.shape[0], rest), F32)], axis=1)
    ib = jnp.concatenate([zeros, isin, jnp.zeros((pos.shape[0], rest), F32)], axis=1)
    return rc, rs, ic, ia, ib


def dsa_layer(x, pos, norm_g, w_in_layers, layer, q_norm, k_norm, w_out, tiles):
    t = x.shape[0]
    proj, iw, vt = dsa_projection(x, norm_g, jnp.swapaxes(w_in_layers, 1, 2), layer, q_norm, k_norm,
                                  _rope_tables(pos), tm=tiles["proj_tm"])
    attn = dsa_core(proj, iw, vt, tq=tiles["tq"], topk=min(TOPK_MAX, t // 4), kt=tiles["kt"], ka=tiles["ka"])
    return matmul_residual(attn, w_out.astype(MXU_DTYPE), x, tm=tiles["res_tm"], tn=tiles["res_tn"])


def _rglru_kernel(gate_ref, xr_ref, cw_ref, cb_ref, wa_ref, ba_ref, wx_ref, bx_ref, lam_ref,
                  y_ref, hcar_ref, xprev_ref, *, tt):
    @pl.when(pl.program_id(1) == 0)
    def _():
        hcar_ref[...] = jnp.zeros_like(hcar_ref)
        xprev_ref[...] = jnp.zeros_like(xprev_ref)

    xr = xr_ref[...]
    ext = jnp.concatenate([xprev_ref[...], xr], axis=0)
    cw = cw_ref[...]
    xc = cb_ref[...] + xr * cw[CONV_WIDTH - 1:CONV_WIDTH, :]
    for d in range(1, CONV_WIDTH):
        xc = xc + pltpu.roll(ext, d, 0)[SUBLANES:, :] * cw[CONV_WIDTH - 1 - d:CONV_WIDTH - d, :]
    xprev_ref[...] = xr[tt - SUBLANES:, :]

    xcb = xc.astype(wa_ref.dtype)
    r = jax.nn.sigmoid(jnp.dot(xcb, wa_ref[...], preferred_element_type=F32) + ba_ref[...])
    ig = jax.nn.sigmoid(jnp.dot(xcb, wx_ref[...], preferred_element_type=F32) + bx_ref[...])
    nlam = -lam_ref[...]
    softplus = jnp.maximum(nlam, 0.0) + jnp.log(1.0 + jnp.exp(-jnp.abs(nlam)))
    log_a = -LRU_C * r * softplus
    a = jnp.exp(log_a)
    mult = jnp.sqrt(1.0 - a * a)
    b = xc * ig * mult

    n_groups = tt // SUBLANES
    a = a.reshape(n_groups, SUBLANES, a.shape[1])
    b = b.reshape(a.shape)
    in_group = lax.broadcasted_iota(jnp.int32, a.shape, 1)
    d = 1
    while d < SUBLANES:
        b = a * jnp.where(in_group >= d, pltpu.roll(b, d, 1), 0.0) + b
        a = a * jnp.where(in_group >= d, pltpu.roll(a, d, 1), 1.0)
        d *= 2
    carry = hcar_ref[...]
    groups = []
    for gi in range(n_groups):
        hg = b[gi] + a[gi] * carry
        carry = hg[SUBLANES - 1:SUBLANES, :]
        groups.append(hg)
    hcar_ref[...] = carry
    h = jnp.concatenate(groups, axis=0)
    y_ref[...] = (h * jax.nn.gelu(gate_ref[...])).astype(y_ref.dtype)


def rglru_scan(proj, conv_w, conv_b, wa, ba, wx, bx, lam, *, tt):
    t = proj.shape[0]
    width = conv_w.shape[1]
    nb, blk, _ = wa.shape
    vec = lambda: pl.BlockSpec((1, blk), lambda n, s: (0, n))
    need = (4 * _nbytes((tt, blk), F32) + 2 * _nbytes((tt, blk), MXU_DTYPE)
            + 4 * _nbytes((blk, blk), wa.dtype) + 24 * _nbytes((tt, blk), F32))
    return pl.pallas_call(
        functools.partial(_rglru_kernel, tt=tt),
        grid=(nb, t // tt),
        in_specs=[
            pl.BlockSpec((tt, blk), lambda n, s: (s, n)),
            pl.BlockSpec((tt, blk), lambda n, s: (s, nb + n)),
            pl.BlockSpec((CONV_WIDTH, blk), lambda n, s: (0, n)),
            vec(),
            pl.BlockSpec((None, blk, blk), lambda n, s: (n, 0, 0)),
            vec(),
            pl.BlockSpec((None, blk, blk), lambda n, s: (n, 0, 0)),
            vec(),
            vec(),
        ],
        out_specs=pl.BlockSpec((tt, blk), lambda n, s: (s, n)),
        out_shape=jax.ShapeDtypeStruct((t, width), MXU_DTYPE),
        scratch_shapes=[pltpu.VMEM((1, blk), F32), pltpu.VMEM((SUBLANES, blk), F32)],
        compiler_params=pltpu.CompilerParams(
            dimension_semantics=("parallel", "arbitrary"),
            vmem_limit_bytes=_vmem_limit(need)),
        name="rglru_scan",
    )(proj, proj, conv_w, conv_b.reshape(1, width), wa, ba.reshape(1, width),
      wx, bx.reshape(1, width), lam.reshape(1, width))


def rglru_layer(x, norm_g, w_in, conv_w, conv_b, wa, ba, wx, bx, lam, w_out, tiles):
    proj = norm_matmul(x, norm_g, w_in.astype(MXU_DTYPE), tm=tiles["proj_tm"], tn=tiles["proj_tn"],
                       out_dtype=F32)
    y = rglru_scan(proj, conv_w, conv_b, wa.astype(MXU_DTYPE), ba, wx.astype(MXU_DTYPE), bx, lam,
                   tt=tiles["scan_tt"])
    return matmul_residual(y, w_out.astype(MXU_DTYPE), x, tm=tiles["res_tm"], tn=tiles["res_tn"])


POOL_HALO = max(POOL_WINDOWS)


def _pool_kernel(x_ref, g_ref, w_ref, b_ref, s_ref, o_ref, halo_ref, *, tt):
    blk = pl.program_id(0)

    @pl.when(blk == 0)
    def _():
        halo_ref[...] = jnp.zeros_like(halo_ref)

    x = x_ref[...]
    h = _rms_norm_rows(x, g_ref[...])
    ext = jnp.concatenate([halo_ref[...], h], axis=0)
    halo_ref[...] = h[tt - POOL_HALO:, :]

    gw = h.shape[1] // len(POOL_WINDOWS)
    t1 = (lax.broadcasted_iota(jnp.int32, (tt, gw), 0) + (blk * tt + 1)).astype(F32)
    for gi, win in enumerate(POOL_WINDOWS):
        cols = slice(gi * gw, (gi + 1) * gw)
        acc = ext[:, cols]
        d = 1
        while d < win:
            acc = acc + pltpu.roll(acc, d, 0)
            d *= 2
        mean = acc[POOL_HALO:, :] / jnp.minimum(t1, float(win))
        y = (mean - h[:, cols]).astype(w_ref.dtype)
        z = jnp.dot(y, w_ref[gi], preferred_element_type=F32) + b_ref[gi:gi + 1, :]
        o_ref[:, cols] = x[:, cols] + z * s_ref[:, cols]


def pool_layer(x, norm_g, w_group, b_group, scale, tiles):
    t, d = x.shape
    tt = tiles["pool_tt"]
    ng, gw, _ = w_group.shape
    need = (4 * _nbytes((tt, d), F32) + 2 * _nbytes((ng, gw, gw), MXU_DTYPE)
            + 6 * _nbytes((tt, d), F32))
    return pl.pallas_call(
        functools.partial(_pool_kernel, tt=tt),
        grid=(t // tt,),
        in_specs=[
            pl.BlockSpec((tt, d), lambda i: (i, 0)),
            pl.BlockSpec((1, d), lambda i: (0, 0)),
            pl.BlockSpec((ng, gw, gw), lambda i: (0, 0, 0)),
            pl.BlockSpec((ng, gw), lambda i: (0, 0)),
            pl.BlockSpec((1, d), lambda i: (0, 0)),
        ],
        out_specs=pl.BlockSpec((tt, d), lambda i: (i, 0)),
        out_shape=jax.ShapeDtypeStruct((t, d), F32),
        scratch_shapes=[pltpu.VMEM((POOL_HALO, d), F32)],
        compiler_params=pltpu.CompilerParams(
            dimension_semantics=("arbitrary",),
            vmem_limit_bytes=_vmem_limit(need)),
        name="pool_mixer",
    )(x, norm_g.reshape(1, d), w_group.astype(MXU_DTYPE), b_group, scale.reshape(1, d))


def _tiles(t):
    big = min(t, 512)
    tall = min(t, 1024)
    return {
        "proj_tm": tall, "proj_tn": 1024,
        "res_tm": tall, "res_tn": 1024,
        "tq": min(t, 256), "kt": min(4, t // min(t, 256)), "ka": min(4, t // min(t, 256)),
        "scan_tt": min(t, 512),
        "pool_tt": min(t, 256),
        "mlp_tm": big, "mlp_tf": 1024,
    }


def kernel(x, positions, attn_norm, attn_w_in, attn_q_norm, attn_k_norm, attn_w_out, rnn_norm, rnn_w_in, rnn_conv_w, rnn_conv_b, rnn_gate_a_w, rnn_gate_a_b, rnn_gate_x_w, rnn_gate_x_b, rnn_lambda, rnn_w_out, pool_norm, pool_w, pool_b, pool_scale, mlp_norm, mlp_w_up, mlp_w_down):
    batch, t, d = x.shape
    depth = mlp_norm.shape[0]
    tiles = _tiles(t)
    w_up_all = mlp_w_up.astype(MXU_DTYPE)
    w_down_all = mlp_w_down.astype(MXU_DTYPE)
    outs = []
    for bi in range(batch):
        xb = x[bi]
        pos = positions[bi]
        for i in range(depth):
            kind, j = i % N_MIXERS, i // N_MIXERS
            if kind == 0:
                xb = dsa_layer(xb, pos, attn_norm[j], attn_w_in, j, attn_q_norm[j], attn_k_norm[j],
                               attn_w_out[j], tiles)
            elif kind == 1:
                xb = rglru_layer(xb, rnn_norm[j], rnn_w_in[j], rnn_conv_w[j], rnn_conv_b[j],
                                 rnn_gate_a_w[j], rnn_gate_a_b[j], rnn_gate_x_w[j], rnn_gate_x_b[j],
                                 rnn_lambda[j], rnn_w_out[j], tiles)
            else:
                xb = pool_layer(xb, pool_norm[j], pool_w[j], pool_b[j], pool_scale[j], tiles)
            xb = mlp_block(xb, mlp_norm[i], w_up_all, w_down_all, i,
                           tm=tiles["mlp_tm"], tf=tiles["mlp_tf"])
        outs.append(xb)
    return outs[0][None] if batch == 1 else jnp.stack(outs, axis=0)
```

```python
import functools
import math

import jax
import jax.numpy as jnp
from jax import lax
from jax.experimental import pallas as pl
from jax.experimental.pallas import tpu as pltpu

F32 = jnp.float32
MXU_DTYPE = jnp.bfloat16

N_MIXERS = 3
EPS = 1e-6
ROPE_THETA = 10000.0
HEAD_DIM = 128
N_HEADS = 16
N_KV_HEADS = 4
IDX_HEADS = 16
IDX_DIM = 128
IDX_ROPE_DIM = 64
TOPK_MAX = 256
CONV_WIDTH = 4
LRU_C = 8.0
POOL_WINDOWS = (2, 4, 8, 16)

LANES = 128
SUBLANES = 8
VMEM_BYTES_V7X = 64 * 1024 * 1024
VMEM_CAP_BYTES = VMEM_BYTES_V7X - 8 * 1024 * 1024

INT_MIN = -(2 ** 31)
NEG_BIG = -1e30


def _vmem_limit(block_bytes):
    return int(min(VMEM_CAP_BYTES, block_bytes * 3 // 2 + (4 << 20)))


def _nbytes(shape, dtype):
    return math.prod(shape) * jnp.dtype(dtype).itemsize


def _rms_norm_rows(x, g):
    ms = jnp.mean(x * x, axis=-1, keepdims=True)
    return x * lax.rsqrt(ms + EPS) * g


def _norm_matmul_kernel(x_ref, g_ref, w_ref, o_ref, h_ref):
    @pl.when(pl.program_id(1) == 0)
    def _():
        h_ref[...] = _rms_norm_rows(x_ref[...], g_ref[...]).astype(h_ref.dtype)

    o_ref[...] = jnp.dot(h_ref[...], w_ref[...], preferred_element_type=F32).astype(o_ref.dtype)


def norm_matmul(x, g, w, *, tm, tn, out_dtype):
    t, d = x.shape
    n = w.shape[1]
    need = (2 * _nbytes((tm, d), F32) + 2 * _nbytes((d, tn), w.dtype)
            + 2 * _nbytes((tm, tn), out_dtype) + _nbytes((tm, d), w.dtype))
    return pl.pallas_call(
        _norm_matmul_kernel,
        grid=(t // tm, n // tn),
        in_specs=[
            pl.BlockSpec((tm, d), lambda i, j: (i, 0)),
            pl.BlockSpec((1, d), lambda i, j: (0, 0)),
            pl.BlockSpec((d, tn), lambda i, j: (0, j)),
        ],
        out_specs=pl.BlockSpec((tm, tn), lambda i, j: (i, j)),
        out_shape=jax.ShapeDtypeStruct((t, n), out_dtype),
        scratch_shapes=[pltpu.VMEM((tm, d), w.dtype)],
        compiler_params=pltpu.CompilerParams(
            dimension_semantics=("parallel", "arbitrary"),
            vmem_limit_bytes=_vmem_limit(need)),
        name="norm_matmul",
    )(x, g.reshape(1, d), w)


def _matmul_residual_kernel(a_ref, w_ref, x_ref, o_ref):
    o_ref[...] = x_ref[...] + jnp.dot(a_ref[...], w_ref[...], preferred_element_type=F32)


def matmul_residual(a, w, x, *, tm, tn):
    t, k = a.shape
    n = w.shape[1]
    need = (2 * _nbytes((tm, k), a.dtype) + 2 * _nbytes((k, tn), w.dtype)
            + 4 * _nbytes((tm, tn), F32))
    return pl.pallas_call(
        _matmul_residual_kernel,
        grid=(t // tm, n // tn),
        in_specs=[
            pl.BlockSpec((tm, k), lambda i, j: (i, 0)),
            pl.BlockSpec((k, tn), lambda i, j: (0, j)),
            pl.BlockSpec((tm, tn), lambda i, j: (i, j)),
        ],
        out_specs=pl.BlockSpec((tm, tn), lambda i, j: (i, j)),
        out_shape=jax.ShapeDtypeStruct((t, n), F32),
        compiler_params=pltpu.CompilerParams(
            dimension_semantics=("parallel", "arbitrary"),
            vmem_limit_bytes=_vmem_limit(need)),
        name="matmul_residual",
    )(a, w, x)


def _mlp_kernel(x_ref, g_ref, wu_ref, wd_ref, o_ref, h_ref, acc_ref):
    f = pl.program_id(1)

    @pl.when(f == 0)
    def _():
        h_ref[...] = _rms_norm_rows(x_ref[...], g_ref[...]).astype(h_ref.dtype)
        acc_ref[...] = jnp.zeros_like(acc_ref)

    u = jnp.dot(h_ref[...], wu_ref[...], preferred_element_type=F32)
    u = jnp.square(jnp.maximum(u, 0.0)).astype(wd_ref.dtype)
    acc_ref[...] += jnp.dot(u, wd_ref[...], preferred_element_type=F32)

    @pl.when(f == pl.num_programs(1) - 1)
    def _():
        o_ref[...] = x_ref[...] + acc_ref[...]


def mlp_block(x, g, w_up, w_down, layer, *, tm, tf):
    t, d = x.shape
    ff = w_up.shape[2]
    need = (4 * _nbytes((tm, d), F32) + 2 * _nbytes((d, tf), w_up.dtype)
            + 2 * _nbytes((tf, d), w_down.dtype) + _nbytes((tm, d), w_up.dtype)
            + _nbytes((tm, d), F32) + 2 * _nbytes((tm, tf), F32))
    return pl.pallas_call(
        _mlp_kernel,
        grid=(t // tm, ff // tf),
        in_specs=[
            pl.BlockSpec((tm, d), lambda i, f: (i, 0)),
            pl.BlockSpec((1, d), lambda i, f: (0, 0)),
            pl.BlockSpec((None, d, tf), lambda i, f: (layer, 0, f)),
            pl.BlockSpec((None, tf, d), lambda i, f: (layer, f, 0)),
        ],
        out_specs=pl.BlockSpec((tm, d), lambda i, f: (i, 0)),
        out_shape=jax.ShapeDtypeStruct((t, d), F32),
        scratch_shapes=[pltpu.VMEM((tm, d), w_up.dtype), pltpu.VMEM((tm, d), F32)],
        compiler_params=pltpu.CompilerParams(
            dimension_semantics=("parallel", "arbitrary"),
            vmem_limit_bytes=_vmem_limit(need)),
        name="mlp_block",
    )(x, g.reshape(1, d), w_up, w_down)


DSA_TN = 4 * LANES
Q_COLS = N_HEADS * HEAD_DIM
IQ_COLS = IDX_HEADS * IDX_DIM
KV_COLS = N_KV_HEADS * HEAD_DIM
DSA_COLS = Q_COLS + IQ_COLS + 2 * KV_COLS + DSA_TN
Q_BLOCKS = Q_COLS // DSA_TN
IQ_BLOCKS = IQ_COLS // DSA_TN
KV_BLOCKS = KV_COLS // DSA_TN


W_K_FIRST = Q_BLOCKS
W_V_FIRST = W_K_FIRST + KV_BLOCKS
W_IQ_FIRST = W_V_FIRST + KV_BLOCKS
W_LAST = W_IQ_FIRST + IQ_BLOCKS
W_TAIL_ROWS = IDX_DIM + IDX_HEADS
OUT_K_FIRST = Q_BLOCKS + IQ_BLOCKS


def _dsa_out_block(jw):
    return jnp.where(jw < W_K_FIRST, jw,
                     jnp.where(jw < W_IQ_FIRST, jw + (OUT_K_FIRST - W_K_FIRST),
                               jnp.where(jw < W_LAST, jw - (W_IQ_FIRST - Q_BLOCKS), W_LAST)))


def _dsa_proj_kernel(x_ref, g_ref, w_ref, qn_ref, kn_ref, rc_ref, rs_ref,
                     ic_ref, ia_ref, ib_ref, o_ref, iw_ref, vt_ref, h_ref, y_ref):
    j = pl.program_id(1)
    jp = j - 1

    @pl.when(j == 0)
    def _():
        h_ref[...] = _rms_norm_rows(x_ref[...], g_ref[...]).astype(h_ref.dtype)

    def rope(z):
        return z * rc_ref[...] + pltpu.roll(z, HEAD_DIM // 2, 1) * rs_ref[...]

    def idx_rope(z):
        half = IDX_ROPE_DIM // 2
        return (z * ic_ref[...] + pltpu.roll(z, LANES - half, 1) * ia_ref[...]
                + pltpu.roll(z, half, 1) * ib_ref[...])

    def per_head(fn):
        def finish(y):
            o_ref[...] = jnp.concatenate(
                [fn(y[:, s * LANES:(s + 1) * LANES]) for s in range(DSA_TN // LANES)],
                axis=1).astype(o_ref.dtype)
        return finish

    def step(finish, multiply=True, last_block_possible=False):
        if finish is not None:
            y_prev = y_ref[...]
        if multiply:
            w = w_ref[...]
            if last_block_possible:
                row = lax.broadcasted_iota(jnp.int32, w.shape, 0)
                w = jnp.where((j == W_LAST) & (row >= W_TAIL_ROWS), 0.0, w)
            y_ref[...] = lax.dot_general(
                h_ref[...], w.astype(h_ref.dtype), (((1,), (1,)), ((), ())),
                preferred_element_type=F32)
        if finish is not None:
            finish(y_prev)

    @pl.when(j == 0)
    def _():
        step(None)

    @pl.when((jp >= 0) & (jp < W_K_FIRST))
    def _():
        scale = HEAD_DIM ** -0.5 * math.log2(math.e)
        step(per_head(lambda z: rope(_rms_norm_rows(z, qn_ref[...])) * scale))

    @pl.when((jp >= W_K_FIRST) & (jp < W_V_FIRST))
    def _():
        step(per_head(lambda z: rope(_rms_norm_rows(z, kn_ref[...]))))

    @pl.when((jp >= W_V_FIRST) & (jp < W_IQ_FIRST))
    def _():
        def finish(y):
            o_ref[...] = y.astype(o_ref.dtype)
            vt_ref[...] = jnp.transpose(y).astype(vt_ref.dtype)
        step(finish)

    @pl.when((jp >= W_IQ_FIRST) & (jp < W_LAST))
    def _():
        step(per_head(idx_rope), last_block_possible=True)

    @pl.when(jp == W_LAST)
    def _():
        def finish(y):
            o_ref[...] = jnp.concatenate(
                [idx_rope(y[:, :LANES]), jnp.zeros_like(y[:, LANES:])], axis=1).astype(o_ref.dtype)
            iw_ref[...] = y[:, LANES:2 * LANES] * (IDX_HEADS ** -0.5 * IDX_DIM ** -0.5)
        step(finish, multiply=False)


def dsa_projection(x, g, w_layers, layer, q_norm, k_norm, tables, *, tm):
    t, d = x.shape
    assert w_layers.shape[1] == Q_COLS + 2 * KV_COLS + IQ_COLS + IDX_DIM + IDX_HEADS
    rc, rs, ic, ia, ib = tables
    tab_spec = pl.BlockSpec((tm, LANES), lambda i, j: (i, 0))
    vec_spec = pl.BlockSpec((1, LANES), lambda i, j: (0, 0))
    need = (2 * _nbytes((tm, d), F32) + 3 * _nbytes((d, DSA_TN), F32)
            + 2 * _nbytes((tm, DSA_TN), MXU_DTYPE) + _nbytes((tm, d), MXU_DTYPE)
            + 12 * _nbytes((tm, LANES), F32) + 3 * _nbytes((tm, DSA_TN), F32))
    return pl.pallas_call(
        _dsa_proj_kernel,
        grid=(t // tm, W_LAST + 2),
        in_specs=[
            pl.BlockSpec((tm, d), lambda i, j: (i, 0)),
            pl.BlockSpec((1, d), lambda i, j: (0, 0)),
            pl.BlockSpec((None, DSA_TN, d), lambda i, j: (layer, jnp.minimum(j, W_LAST), 0)),
            vec_spec, vec_spec, tab_spec, tab_spec, tab_spec, tab_spec, tab_spec,
        ],
        out_specs=[
            pl.BlockSpec((tm, DSA_TN), lambda i, j: (i, _dsa_out_block(jnp.maximum(j - 1, 0)))),
            pl.BlockSpec((tm, LANES), lambda i, j: (i, 0)),
            pl.BlockSpec((DSA_TN, tm), lambda i, j: (jnp.clip(j - 1 - W_V_FIRST, 0, KV_BLOCKS - 1), i)),
        ],
        out_shape=[
            jax.ShapeDtypeStruct((t, DSA_COLS), MXU_DTYPE),
            jax.ShapeDtypeStruct((t, LANES), F32),
            jax.ShapeDtypeStruct((KV_COLS, t), MXU_DTYPE),
        ],
        scratch_shapes=[pltpu.VMEM((tm, d), MXU_DTYPE), pltpu.VMEM((tm, DSA_TN), F32)],
        compiler_params=pltpu.CompilerParams(
            dimension_semantics=("parallel", "arbitrary"),
            vmem_limit_bytes=_vmem_limit(need)),
        name="dsa_projection",
    )(x, g.reshape(1, d), w_layers, q_norm.reshape(1, LANES), k_norm.reshape(1, LANES), rc, rs, ic, ia, ib)


def _float_key(v):
    b = lax.bitcast_convert_type(v, jnp.int32)
    return b ^ ((b >> 31) & 0x7FFFFFFF)


def _key_float(k):
    return lax.bitcast_convert_type(k ^ ((k >> 31) & 0x7FFFFFFF), F32)


_KLO, _KHI, _CLO, _CHI, _JCUT, _SEED, _SEED_UP, _SEED_DOWN = range(8)
SEED_STEP = 0.25
_SEARCH_CAP = 96
ONES_ROWS = 2 * SUBLANES
ATTN_PAD_FROM = 2
ATTN_LOOKAHEAD = 1


def _dsa_core_kernel(q_ref, iq_ref, iw_ref, k_ref, vt_ref, ik_ref, o_ref,
                     key_ref, bias_ref, s_ref, wt_ref, st_ref, m_ref, l_ref, acc_ref, *, tq, topk, kt, ka):
    i = pl.program_id(0)
    nc = i + 1
    group = N_HEADS // N_KV_HEADS
    nt = (((1,), (1,)), ((), ()))

    wt_ref[...] = jnp.transpose(iw_ref[...])

    krow = lax.broadcasted_iota(jnp.int32, (tq, tq), 0)
    qcol = lax.broadcasted_iota(jnp.int32, (tq, tq), 1)
    diag_causal = krow <= qcol

    def chunk_start(c):
        return pl.multiple_of(c * tq, tq)

    def over_chunks(body, k, init, pad_from=None):
        n_tiles = (nc + k - min(pad_from or k, k)) // k
        carry = lax.fori_loop(0, n_tiles, lambda ti, c: body(ti * k, k, c), init)
        if k > 1:
            carry = lax.fori_loop(n_tiles * k, nc, lambda c0, c: body(c0, 1, c), carry)
        return carry

    def index_tile(first, k, carry):
        smin, smax, ssum, ssq, sampled = carry
        iks = [ik_ref[pl.ds(chunk_start(first + u), tq), :] for u in range(k)]
        scores = [jnp.zeros((tq, tq), F32) for _ in range(k)]
        for h in range(IDX_HEADS):
            iqh = iq_ref[:, h * IDX_DIM:(h + 1) * IDX_DIM]
            w = jnp.broadcast_to(wt_ref[h:h + 1, :], (tq, tq))
            for u in range(k):
                logits = lax.dot_general(iks[u], iqh, nt, preferred_element_type=F32)
                scores[u] = scores[u] + w * jnp.maximum(logits, 0.0)
        for u in range(k):
            c = first + u
            valid = (c < i) | diag_causal
            key_ref[c] = jnp.where(valid, _float_key(scores[u]), INT_MIN)
            smin = jnp.minimum(smin, jnp.min(jnp.where(valid, scores[u], jnp.inf), axis=0, keepdims=True))
            smax = jnp.maximum(smax, jnp.max(jnp.where(valid, scores[u], -jnp.inf), axis=0, keepdims=True))
            if u == 0:
                sample = jnp.where(c < i, scores[u], 0.0)
                ssum = ssum + jnp.sum(sample, axis=0, keepdims=True)
                ssq = ssq + jnp.sum(sample * sample, axis=0, keepdims=True)
                sampled = sampled + jnp.where(c < i, float(tq), 0.0)
        return smin, smax, ssum, ssq, sampled

    zero_row = jnp.zeros((1, tq), F32)
    smin, smax, ssum, ssq, sampled = over_chunks(
        index_tile, kt, (jnp.full((1, tq), jnp.inf, F32), jnp.full((1, tq), -jnp.inf, F32),
                         zero_row, zero_row, jnp.float32(0.0)))

    def mask_chunk(c, carry):
        key_ref[c] = jnp.full((tq, tq), INT_MIN, jnp.int32)
        return carry

    lax.fori_loop(nc, (nc + ka - min(ATTN_PAD_FROM, ka)) // ka * ka, mask_chunk, 0)

    def count_ge(cand):
        cand_b = jnp.broadcast_to(cand, (tq, tq))

        def body(first, k, cnt):
            for u in range(k):
                hit = jnp.where(key_ref[first + u] >= cand_b, 1.0, 0.0)
                cnt = cnt + jnp.sum(hit.reshape(tq // SUBLANES, SUBLANES, tq), axis=0)
            return cnt

        cnt = over_chunks(body, kt, jnp.zeros((SUBLANES, tq), F32))
        return jnp.sum(cnt, axis=0, keepdims=True)

    st_ref[_KLO] = _float_key(smin)
    st_ref[_KHI] = _float_key(smax) + 1
    n_causal = lax.broadcasted_iota(jnp.int32, (1, tq), 1) + (i * tq + 1)
    st_ref[_CLO] = n_causal
    st_ref[_CHI] = jnp.zeros((1, tq), jnp.int32)

    mean = ssum / sampled
    dev = jnp.sqrt(jnp.maximum(ssq / sampled - mean * mean, 0.0))
    upper = jnp.minimum(topk / n_causal.astype(F32), 1.0)
    tail = jnp.clip(jnp.minimum(upper, 1.0 - upper), 1e-6, 0.5)
    r = jnp.sqrt(-2.0 * jnp.log(tail))
    z = r - (2.515517 + 0.802853 * r + 0.010328 * r * r) / (
        1.0 + 1.432788 * r + 0.189269 * r * r + 0.001308 * r * r * r)
    guess = mean + jnp.where(upper <= 0.5, z, -z) * dev
    st_ref[_SEED] = _float_key(guess)
    st_ref[_SEED_UP] = _float_key(guess + SEED_STEP * dev)
    st_ref[_SEED_DOWN] = _float_key(guess - SEED_STEP * dev)

    def unfinished(klo, khi, clo):
        return (clo > topk) & ((khi - klo) != 1)

    def search_cond(st):
        step, pending = st
        return (step < _SEARCH_CAP) & (pending > 0)

    def any_lane(mask):
        return jnp.max(jnp.where(mask, 1.0, 0.0))

    def search_body(st):
        step, _ = st
        klo, khi, clo = st_ref[_KLO], st_ref[_KHI], st_ref[_CLO]
        live = unfinished(klo, khi, clo)
        pending = any_lane(live)
        vmid = 0.5 * _key_float(klo) + 0.5 * _key_float(khi)
        cmid = _float_key(vmid)
        kmid = klo + lax.shift_right_logical(khi - klo, 1)
        cand = jnp.where((cmid > klo) & (cmid < khi), cmid, kmid)
        seed = st_ref[_SEED]
        second = jnp.where(klo == seed, st_ref[_SEED_UP], st_ref[_SEED_DOWN])
        seeded = jnp.where(step == 0, seed, second)
        cand = jnp.where((step < 2) & (seeded > klo) & (seeded < khi), seeded, cand)
        cand = jnp.where(live, cand, klo)
        cnt = count_ge(cand).astype(jnp.int32)
        take = cnt >= topk
        klo = jnp.where(live & take, cand, klo)
        clo = jnp.where(live & take, cnt, clo)
        drop = live & jnp.logical_not(take)
        st_ref[_KLO], st_ref[_KHI], st_ref[_CLO] = klo, jnp.where(drop, cand, khi), clo
        st_ref[_CHI] = jnp.where(drop, cnt, st_ref[_CHI])
        return step + 1, pending

    lax.while_loop(search_cond, search_body, (jnp.int32(0), jnp.float32(1.0)))
    thr_b = jnp.broadcast_to(st_ref[_KLO], (tq, tq))

    n_keys = key_ref.shape[0] * tq
    tied = st_ref[_CLO] > topk
    has_ties = any_lane(tied) > 0
    st_ref[_JCUT] = jnp.full((1, tq), n_keys, jnp.int32)

    @pl.when(has_ties)
    def _():
        want = topk - st_ref[_CHI]

        def count_tied_upto(pos):
            pos_b = jnp.broadcast_to(pos, (tq, tq))

            def body(c, cnt):
                hit = (key_ref[c] == thr_b) & (krow + c * tq <= pos_b)
                return cnt + jnp.sum(jnp.where(hit, 1.0, 0.0), axis=0, keepdims=True)

            return lax.fori_loop(0, nc, body, jnp.zeros((1, tq), F32)).astype(jnp.int32)

        def cut_step(_, st):
            lo, hi = st
            mid = lo + ((hi - lo) >> 1)
            enough = count_tied_upto(mid) >= want
            return jnp.where(enough, lo, mid), jnp.where(enough, mid, hi)

        _, cut = lax.fori_loop(0, n_keys.bit_length(), cut_step,
                               (jnp.full((1, tq), -1, jnp.int32), jnp.full((1, tq), n_keys - 1, jnp.int32)))
        st_ref[_JCUT] = jnp.where(tied, cut, n_keys)

    cut_b = jnp.broadcast_to(st_ref[_JCUT], (tq, tq))

    m_ref[...] = jnp.full(m_ref.shape, NEG_BIG, F32)
    l_ref[...] = jnp.zeros(l_ref.shape, F32)
    acc_ref[...] = jnp.zeros(acc_ref.shape, F32)

    def attend_tile(first, k, carry, *, with_ties):
        for u in range(k):
            c = first + u
            key = key_ref[c]
            if with_ties:
                keep = (key > thr_b) | ((key == thr_b) & (krow + c * tq <= cut_b))
            else:
                keep = key >= thr_b
            bias_ref[u] = jnp.where(keep, 0.0, NEG_BIG)

        starts = [chunk_start(first + u) for u in range(k)]
        slots = ATTN_LOOKAHEAD + 1
        dyn0 = jnp.minimum(first, 0)
        ones_rows = jnp.ones((ONES_ROWS, tq), vt_ref.dtype)

        def logits(h):
            cols = slice(h // group * HEAD_DIM, (h // group + 1) * HEAD_DIM)
            qh = q_ref[:, h * HEAD_DIM:(h + 1) * HEAD_DIM]
            top = None
            for u in range(k):
                s = lax.dot_general(k_ref[pl.ds(starts[u], tq), cols], qh, nt,
                                    preferred_element_type=F32) + bias_ref[u]
                s_ref[h % slots + dyn0, u] = s
                top = s if top is None else jnp.maximum(top, s)
            return jnp.max(top, axis=0, keepdims=True)

        queued = [logits(h) for h in range(ATTN_LOOKAHEAD)]
        for h in range(N_HEADS):
            tile_max = queued.pop(0)
            if h + ATTN_LOOKAHEAD < N_HEADS:
                queued.append(logits(h + ATTN_LOOKAHEAD))
            cols = slice(h // group * HEAD_DIM, (h // group + 1) * HEAD_DIM)
            m_prev = m_ref[h]
            m_cur = jnp.maximum(m_prev, tile_max)
            alpha = jnp.exp2(m_prev - m_cur)
            pv = jnp.zeros((HEAD_DIM + ONES_ROWS, tq), F32)
            for u in range(k):
                p = jnp.exp2(s_ref[h % slots + dyn0, u] - m_cur)
                v_aug = jnp.concatenate([vt_ref[cols, pl.ds(starts[u], tq)], ones_rows], axis=0)
                pv = pv + jnp.dot(v_aug, p.astype(vt_ref.dtype), preferred_element_type=F32)
            l_ref[h] = alpha * l_ref[h] + pv[HEAD_DIM:HEAD_DIM + 1, :]
            pv = pv[:HEAD_DIM, :]
            acc_ref[h] = alpha * acc_ref[h] + pv
            m_ref[h] = m_cur
        return carry

    @pl.when(jnp.logical_not(has_ties))
    def _():
        over_chunks(functools.partial(attend_tile, with_ties=False), ka, 0, pad_from=ATTN_PAD_FROM)

    @pl.when(has_ties)
    def _():
        over_chunks(functools.partial(attend_tile, with_ties=True), 1, 0)

    for h in range(N_HEADS):
        o_ref[:, h * HEAD_DIM:(h + 1) * HEAD_DIM] = jnp.transpose(acc_ref[h] / l_ref[h]).astype(o_ref.dtype)


def dsa_core(proj, iw, vt, *, tq, topk, kt, ka):
    t = proj.shape[0]
    assert (t // tq) % kt == 0 and kt % ka == 0
    k_block = (Q_COLS + IQ_COLS) // KV_COLS
    ik_block = (Q_COLS + IQ_COLS + 2 * KV_COLS) // IDX_DIM
    resident = dict(pipeline_mode=pl.Buffered(1))
    need = (4 * _nbytes((tq, Q_COLS), proj.dtype) + 2 * _nbytes((tq, LANES), F32)
            + 2 * _nbytes((t, KV_COLS), proj.dtype) + _nbytes((t, IDX_DIM), proj.dtype)
            + 2 * _nbytes((tq, Q_COLS), proj.dtype)
            + _nbytes((t // tq, tq, tq), jnp.int32) + _nbytes((kt, tq, tq), F32)
            + _nbytes((LANES, tq), F32) + (3 + 2 * N_HEADS) * _nbytes((SUBLANES, tq), F32)
            + _nbytes((N_HEADS, HEAD_DIM, tq), F32)
            + (ATTN_LOOKAHEAD + 3) * _nbytes((ka, tq, tq), F32))
    return pl.pallas_call(
        functools.partial(_dsa_core_kernel, tq=tq, topk=topk, kt=kt, ka=ka),
        grid=(t // tq,),
        in_specs=[
            pl.BlockSpec((tq, Q_COLS), lambda i: (i, 0)),
            pl.BlockSpec((tq, IQ_COLS), lambda i: (i, 1)),
            pl.BlockSpec((tq, LANES), lambda i: (i, 0)),
            pl.BlockSpec((t, KV_COLS), lambda i: (0, k_block), **resident),
            pl.BlockSpec((KV_COLS, t), lambda i: (0, 0), **resident),
            pl.BlockSpec((t, IDX_DIM), lambda i: (0, ik_block), **resident),
        ],
        out_specs=pl.BlockSpec((tq, Q_COLS), lambda i: (i, 0)),
        out_shape=jax.ShapeDtypeStruct((t, Q_COLS), proj.dtype),
        scratch_shapes=[
            pltpu.VMEM((t // tq, tq, tq), jnp.int32),
            pltpu.VMEM((ka, tq, tq), F32),
            pltpu.VMEM((ATTN_LOOKAHEAD + 1, ka, tq, tq), F32),
            pltpu.VMEM((LANES, tq), F32),
            pltpu.VMEM((8, 1, tq), jnp.int32),
            pltpu.VMEM((N_HEADS, 1, tq), F32),
            pltpu.VMEM((N_HEADS, 1, tq), F32),
            pltpu.VMEM((N_HEADS, HEAD_DIM, tq), F32),
        ],
        compiler_params=pltpu.CompilerParams(
            dimension_semantics=("arbitrary",),
            vmem_limit_bytes=_vmem_limit(need)),
        name="dsa_core",
    )(proj, proj, iw, proj, vt, proj)


def _rope_tables(pos):
    def angles(dim):
        inv = 1.0 / (ROPE_THETA ** (jnp.arange(0, dim, 2, dtype=F32) / dim))
        return pos.astype(F32)[:, None] * inv

    ang = angles(HEAD_DIM)
    cos, sin = jnp.cos(ang), jnp.sin(ang)
    rc = jnp.concatenate([cos, cos], axis=1)
    rs = jnp.concatenate([-sin, sin], axis=1)
    iang = angles(IDX_ROPE_DIM)
    icos, isin = jnp.cos(iang), jnp.sin(iang)
    rest = LANES - IDX_ROPE_DIM
    zeros = jnp.zeros_like(isin)
    ic = jnp.concatenate([icos, icos, jnp.ones((pos.shape[0], rest), F32)], axis=1)
    ia = jnp.concatenate([-isin, zeros, jnp.zeros((pos.shape[0], rest), F32)], axis=1)
    ib = jnp.concatenate([zeros, isin, jnp.zeros((pos.shape[0], rest), F32)], axis=1)
    return rc, rs, ic, ia, ib


def dsa_layer(x, pos, norm_g, w_in_layers, layer, q_norm, k_norm, w_out, tiles):
    t = x.shape[0]
    proj, iw, vt = dsa_projection(x, norm_g, jnp.swapaxes(w_in_layers, 1, 2), layer, q_norm, k_norm,
                                  _rope_tables(pos), tm=tiles["proj_tm"])
    attn = dsa_core(proj, iw, vt, tq=tiles["tq"], topk=min(TOPK_MAX, t // 4), kt=tiles["kt"], ka=tiles["ka"])
    return matmul_residual(attn, w_out.astype(MXU_DTYPE), x, tm=tiles["res_tm"], tn=tiles["res_tn"])


def _rglru_kernel(gate_ref, xr_ref, cw_ref, cb_ref, wa_ref, ba_ref, wx_ref, bx_ref, lam_ref,
                  y_ref, hcar_ref, xprev_ref, *, tt):
    @pl.when(pl.program_id(1) == 0)
    def _():
        hcar_ref[...] = jnp.zeros_like(hcar_ref)
        xprev_ref[...] = jnp.zeros_like(xprev_ref)

    xr = xr_ref[...]
    ext = jnp.concatenate([xprev_ref[...], xr], axis=0)
    cw = cw_ref[...]
    xc = cb_ref[...] + xr * cw[CONV_WIDTH - 1:CONV_WIDTH, :]
    for d in range(1, CONV_WIDTH):
        xc = xc + pltpu.roll(ext, d, 0)[SUBLANES:, :] * cw[CONV_WIDTH - 1 - d:CONV_WIDTH - d, :]
    xprev_ref[...] = xr[tt - SUBLANES:, :]

    xcb = xc.astype(wa_ref.dtype)
    r = jax.nn.sigmoid(jnp.dot(xcb, wa_ref[...], preferred_element_type=F32) + ba_ref[...])
    ig = jax.nn.sigmoid(jnp.dot(xcb, wx_ref[...], preferred_element_type=F32) + bx_ref[...])
    nlam = -lam_ref[...]
    softplus = jnp.maximum(nlam, 0.0) + jnp.log(1.0 + jnp.exp(-jnp.abs(nlam)))
    log_a = -LRU_C * r * softplus
    a = jnp.exp(log_a)
    mult = jnp.sqrt(1.0 - a * a)
    b = xc * ig * mult

    n_groups = tt // SUBLANES
    a = a.reshape(n_groups, SUBLANES, a.shape[1])
    b = b.reshape(a.shape)
    in_group = lax.broadcasted_iota(jnp.int32, a.shape, 1)
    d = 1
    while d < SUBLANES:
        b = a * jnp.where(in_group >= d, pltpu.roll(b, d, 1), 0.0) + b
        a = a * jnp.where(in_group >= d, pltpu.roll(a, d, 1), 1.0)
        d *= 2
    carry = hcar_ref[...]
    groups = []
    for gi in range(n_groups):
        hg = b[gi] + a[gi] * carry
        carry = hg[SUBLANES - 1:SUBLANES, :]
        groups.append(hg)
    hcar_ref[...] = carry
    h = jnp.concatenate(groups, axis=0)
    y_ref[...] = (h * jax.nn.gelu(gate_ref[...])).astype(y_ref.dtype)


def rglru_scan(proj, conv_w, conv_b, wa, ba, wx, bx, lam, *, tt):
    t = proj.shape[0]
    width = conv_w.shape[1]
    nb, blk, _ = wa.shape
    vec = lambda: pl.BlockSpec((1, blk), lambda n, s: (0, n))
    need = (4 * _nbytes((tt, blk), F32) + 2 * _nbytes((tt, blk), MXU_DTYPE)
            + 4 * _nbytes((blk, blk), wa.dtype) + 24 * _nbytes((tt, blk), F32))
    return pl.pallas_call(
        functools.partial(_rglru_kernel, tt=tt),
        grid=(nb, t // tt),
        in_specs=[
            pl.BlockSpec((tt, blk), lambda n, s: (s, n)),
            pl.BlockSpec((tt, blk), lambda n, s: (s, nb + n)),
            pl.BlockSpec((CONV_WIDTH, blk), lambda n, s: (0, n)),
            vec(),
            pl.BlockSpec((None, blk, blk), lambda n, s: (n, 0, 0)),
            vec(),
            pl.BlockSpec((None, blk, blk), lambda n, s: (n, 0, 0)),
            vec(),
            vec(),
        ],
        out_specs=pl.BlockSpec((tt, blk), lambda n, s: (s, n)),
        out_shape=jax.ShapeDtypeStruct((t, width), MXU_DTYPE),
        scratch_shapes=[pltpu.VMEM((1, blk), F32), pltpu.VMEM((SUBLANES, blk), F32)],
        compiler_params=pltpu.CompilerParams(
            dimension_semantics=("parallel", "arbitrary"),
            vmem_limit_bytes=_vmem_limit(need)),
        name="rglru_scan",
    )(proj, proj, conv_w, conv_b.reshape(1, width), wa, ba.reshape(1, width),
      wx, bx.reshape(1, width), lam.reshape(1, width))


def rglru_layer(x, norm_g, w_in, conv_w, conv_b, wa, ba, wx, bx, lam, w_out, tiles):
    proj = norm_matmul(x, norm_g, w_in.astype(MXU_DTYPE), tm=tiles["proj_tm"], tn=tiles["proj_tn"],
                       out_dtype=F32)
    y = rglru_scan(proj, conv_w, conv_b, wa.astype(MXU_DTYPE), ba, wx.astype(MXU_DTYPE), bx, lam,
                   tt=tiles["scan_tt"])
    return matmul_residual(y, w_out.astype(MXU_DTYPE), x, tm=tiles["res_tm"], tn=tiles["res_tn"])


POOL_HALO = max(POOL_WINDOWS)


def _pool_kernel(x_ref, g_ref, w_ref, b_ref, s_ref, o_ref, halo_ref, *, tt):
    blk = pl.program_id(0)

    @pl.when(blk == 0)
    def _():
        halo_ref[...] = jnp.zeros_like(halo_ref)

    x = x_ref[...]
    h = _rms_norm_rows(x, g_ref[...])
    ext = jnp.concatenate([halo_ref[...], h], axis=0)
    halo_ref[...] = h[tt - POOL_HALO:, :]

    gw = h.shape[1] // len(POOL_WINDOWS)
    t1 = (lax.broadcasted_iota(jnp.int32, (tt, gw), 0) + (blk * tt + 1)).astype(F32)
    for gi, win in enumerate(POOL_WINDOWS):
        cols = slice(gi * gw, (gi + 1) * gw)
        acc = ext[:, cols]
        d = 1
        while d < win:
            acc = acc + pltpu.roll(acc, d, 0)
            d *= 2
        mean = acc[POOL_HALO:, :] / jnp.minimum(t1, float(win))
        y = (mean - h[:, cols]).astype(w_ref.dtype)
        z = jnp.dot(y, w_ref[gi], preferred_element_type=F32) + b_ref[gi:gi + 1, :]
        o_ref[:, cols] = x[:, cols] + z * s_ref[:, cols]


def pool_layer(x, norm_g, w_group, b_group, scale, tiles):
    t, d = x.shape
    tt = tiles["pool_tt"]
    ng, gw, _ = w_group.shape
    need = (4 * _nbytes((tt, d), F32) + 2 * _nbytes((ng, gw, gw), MXU_DTYPE)
            + 6 * _nbytes((tt, d), F32))
    return pl.pallas_call(
        functools.partial(_pool_kernel, tt=tt),
        grid=(t // tt,),
        in_specs=[
            pl.BlockSpec((tt, d), lambda i: (i, 0)),
            pl.BlockSpec((1, d), lambda i: (0, 0)),
            pl.BlockSpec((ng, gw, gw), lambda i: (0, 0, 0)),
            pl.BlockSpec((ng, gw), lambda i: (0, 0)),
            pl.BlockSpec((1, d), lambda i: (0, 0)),
        ],
        out_specs=pl.BlockSpec((tt, d), lambda i: (i, 0)),
        out_shape=jax.ShapeDtypeStruct((t, d), F32),
        scratch_shapes=[pltpu.VMEM((POOL_HALO, d), F32)],
        compiler_params=pltpu.CompilerParams(
            dimension_semantics=("arbitrary",),
            vmem_limit_bytes=_vmem_limit(need)),
        name="pool_mixer",
    )(x, norm_g.reshape(1, d), w_group.astype(MXU_DTYPE), b_group, scale.reshape(1, d))


def _tiles(t):
    big = min(t, 512)
    tall = min(t, 1024)
    return {
        "proj_tm": tall, "proj_tn": 1024,
        "res_tm": tall, "res_tn": 1024,
        "tq": min(t, 256), "kt": min(4, t // min(t, 256)), "ka": min(4, t // min(t, 256)),
        "scan_tt": min(t, 512),
        "pool_tt": min(t, 256),
        "mlp_tm": big, "mlp_tf": 1024,
    }


def kernel(x, positions, attn_norm, attn_w_in, attn_q_norm, attn_k_norm, attn_w_out, rnn_norm, rnn_w_in, rnn_conv_w, rnn_conv_b, rnn_gate_a_w, rnn_gate_a_b, rnn_gate_x_w, rnn_gate_x_b, rnn_lambda, rnn_w_out, pool_norm, pool_w, pool_b, pool_scale, mlp_norm, mlp_w_up, mlp_w_down):
    batch, t, d = x.shape
    depth = mlp_norm.shape[0]
    tiles = _tiles(t)
    w_up_all = mlp_w_up.astype(MXU_DTYPE)
    w_down_all = mlp_w_down.astype(MXU_DTYPE)
    outs = []
    for bi in range(batch):
        xb = x[bi]
        pos = positions[bi]
        for i in range(depth):
            kind, j = i % N_MIXERS, i // N_MIXERS
            if kind == 0:
                xb = dsa_layer(xb, pos, attn_norm[j], attn_w_in, j, attn_q_norm[j], attn_k_norm[j],
                               attn_w_out[j], tiles)
            elif kind == 1:
                xb = rglru_layer(xb, rnn_norm[j], rnn_w_in[j], rnn_conv_w[j], rnn_conv_b[j],
                                 rnn_gate_a_w[j], rnn_gate_a_b[j], rnn_gate_x_w[j], rnn_gate_x_b[j],
                                 rnn_lambda[j], rnn_w_out[j], tiles)
            else:
                xb = pool_layer(xb, pool_norm[j], pool_w[j], pool_b[j], pool_scale[j], tiles)
            xb = mlp_block(xb, mlp_norm[i], w_up_all, w_down_all, i,
                           tm=tiles["mlp_tm"], tf=tiles["mlp_tf"])
        outs.append(xb)
    return outs[0][None] if batch == 1 else jnp.stack(outs, axis=0)
```

```python
import functools
import math

import jax
import jax.numpy as jnp
from jax import lax
from jax.experimental import pallas as pl
from jax.experimental.pallas import tpu as pltpu

F32 = jnp.float32
MXU_DTYPE = jnp.bfloat16

N_MIXERS = 3
EPS = 1e-6
ROPE_THETA = 10000.0
HEAD_DIM = 128
N_HEADS = 16
N_KV_HEADS = 4
IDX_HEADS = 16
IDX_DIM = 128
IDX_ROPE_DIM = 64
TOPK_MAX = 256
CONV_WIDTH = 4
LRU_C = 8.0
POOL_WINDOWS = (2, 4, 8, 16)

LANES = 128
SUBLANES = 8
VMEM_BYTES_V7X = 64 * 1024 * 1024
VMEM_CAP_BYTES = VMEM_BYTES_V7X - 8 * 1024 * 1024

INT_MIN = -(2 ** 31)
NEG_BIG = -1e30


def _vmem_limit(block_bytes):
    return int(min(VMEM_CAP_BYTES, block_bytes * 3 // 2 + (4 << 20)))


def _nbytes(shape, dtype):
    return math.prod(shape) * jnp.dtype(dtype).itemsize


def _rms_norm_rows(x, g):
    ms = jnp.mean(x * x, axis=-1, keepdims=True)
    return x * lax.rsqrt(ms + EPS) * g


def _norm_matmul_kernel(x_ref, g_ref, w_ref, o_ref, h_ref):
    @pl.when(pl.program_id(1) == 0)
    def _():
        h_ref[...] = _rms_norm_rows(x_ref[...], g_ref[...]).astype(h_ref.dtype)

    o_ref[...] = jnp.dot(h_ref[...], w_ref[...], preferred_element_type=F32).astype(o_ref.dtype)


def norm_matmul(x, g, w, *, tm, tn, out_dtype):
    t, d = x.shape
    n = w.shape[1]
    need = (2 * _nbytes((tm, d), F32) + 2 * _nbytes((d, tn), w.dtype)
            + 2 * _nbytes((tm, tn), out_dtype) + _nbytes((tm, d), w.dtype))
    return pl.pallas_call(
        _norm_matmul_kernel,
        grid=(t // tm, n // tn),
        in_specs=[
            pl.BlockSpec((tm, d), lambda i, j: (i, 0)),
            pl.BlockSpec((1, d), lambda i, j: (0, 0)),
            pl.BlockSpec((d, tn), lambda i, j: (0, j)),
        ],
        out_specs=pl.BlockSpec((tm, tn), lambda i, j: (i, j)),
        out_shape=jax.ShapeDtypeStruct((t, n), out_dtype),
        scratch_shapes=[pltpu.VMEM((tm, d), w.dtype)],
        compiler_params=pltpu.CompilerParams(
            dimension_semantics=("parallel", "arbitrary"),
            vmem_limit_bytes=_vmem_limit(need)),
        name="norm_matmul",
    )(x, g.reshape(1, d), w)


def _matmul_residual_kernel(a_ref, w_ref, x_ref, o_ref):
    o_ref[...] = x_ref[...] + jnp.dot(a_ref[...], w_ref[...], preferred_element_type=F32)


def matmul_residual(a, w, x, *, tm, tn):
    t, k = a.shape
    n = w.shape[1]
    need = (2 * _nbytes((tm, k), a.dtype) + 2 * _nbytes((k, tn), w.dtype)
            + 4 * _nbytes((tm, tn), F32))
    return pl.pallas_call(
        _matmul_residual_kernel,
        grid=(t // tm, n // tn),
        in_specs=[
            pl.BlockSpec((tm, k), lambda i, j: (i, 0)),
            pl.BlockSpec((k, tn), lambda i, j: (0, j)),
            pl.BlockSpec((tm, tn), lambda i, j: (i, j)),
        ],
        out_specs=pl.BlockSpec((tm, tn), lambda i, j: (i, j)),
        out_shape=jax.ShapeDtypeStruct((t, n), F32),
        compiler_params=pltpu.CompilerParams(
            dimension_semantics=("parallel", "arbitrary"),
            vmem_limit_bytes=_vmem_limit(need)),
        name="matmul_residual",
    )(a, w, x)


def _mlp_kernel(x_ref, g_ref, wu_ref, wd_ref, o_ref, h_ref, acc_ref):
    f = pl.program_id(1)

    @pl.when(f == 0)
    def _():
        h_ref[...] = _rms_norm_rows(x_ref[...], g_ref[...]).astype(h_ref.dtype)
        acc_ref[...] = jnp.zeros_like(acc_ref)

    u = jnp.dot(h_ref[...], wu_ref[...], preferred_element_type=F32)
    u = jnp.square(jnp.maximum(u, 0.0)).astype(wd_ref.dtype)
    acc_ref[...] += jnp.dot(u, wd_ref[...], preferred_element_type=F32)

    @pl.when(f == pl.num_programs(1) - 1)
    def _():
        o_ref[...] = x_ref[...] + acc_ref[...]


def mlp_block(x, g, w_up, w_down, layer, *, tm, tf):
    t, d = x.shape
    ff = w_up.shape[2]
    need = (4 * _nbytes((tm, d), F32) + 2 * _nbytes((d, tf), w_up.dtype)
            + 2 * _nbytes((tf, d), w_down.dtype) + _nbytes((tm, d), w_up.dtype)
            + _nbytes((tm, d), F32) + 2 * _nbytes((tm, tf), F32))
    return pl.pallas_call(
        _mlp_kernel,
        grid=(t // tm, ff // tf),
        in_specs=[
            pl.BlockSpec((tm, d), lambda i, f: (i, 0)),
            pl.BlockSpec((1, d), lambda i, f: (0, 0)),
            pl.BlockSpec((None, d, tf), lambda i, f: (layer, 0, f)),
            pl.BlockSpec((None, tf, d), lambda i, f: (layer, f, 0)),
        ],
        out_specs=pl.BlockSpec((tm, d), lambda i, f: (i, 0)),
        out_shape=jax.ShapeDtypeStruct((t, d), F32),
        scratch_shapes=[pltpu.VMEM((tm, d), w_up.dtype), pltpu.VMEM((tm, d), F32)],
        compiler_params=pltpu.CompilerParams(
            dimension_semantics=("parallel", "arbitrary"),
            vmem_limit_bytes=_vmem_limit(need)),
        name="mlp_block",
    )(x, g.reshape(1, d), w_up, w_down)


DSA_TN = 4 * LANES
Q_COLS = N_HEADS * HEAD_DIM
IQ_COLS = IDX_HEADS * IDX_DIM
KV_COLS = N_KV_HEADS * HEAD_DIM
DSA_COLS = Q_COLS + IQ_COLS + 2 * KV_COLS + DSA_TN
Q_BLOCKS = Q_COLS // DSA_TN
IQ_BLOCKS = IQ_COLS // DSA_TN
KV_BLOCKS = KV_COLS // DSA_TN


W_K_FIRST = Q_BLOCKS
W_V_FIRST = W_K_FIRST + KV_BLOCKS
W_IQ_FIRST = W_V_FIRST + KV_BLOCKS
W_LAST = W_IQ_FIRST + IQ_BLOCKS
W_TAIL_ROWS = IDX_DIM + IDX_HEADS
OUT_K_FIRST = Q_BLOCKS + IQ_BLOCKS


def _dsa_out_block(jw):
    return jnp.where(jw < W_K_FIRST, jw,
                     jnp.where(jw < W_IQ_FIRST, jw + (OUT_K_FIRST - W_K_FIRST),
                               jnp.where(jw < W_LAST, jw - (W_IQ_FIRST - Q_BLOCKS), W_LAST)))


def _dsa_proj_kernel(x_ref, g_ref, w_ref, qn_ref, kn_ref, rc_ref, rs_ref,
                     ic_ref, ia_ref, ib_ref, o_ref, iw_ref, vt_ref, h_ref, y_ref):
    j = pl.program_id(1)
    jp = j - 1

    @pl.when(j == 0)
    def _():
        h_ref[...] = _rms_norm_rows(x_ref[...], g_ref[...]).astype(h_ref.dtype)

    def rope(z):
        return z * rc_ref[...] + pltpu.roll(z, HEAD_DIM // 2, 1) * rs_ref[...]

    def idx_rope(z):
        half = IDX_ROPE_DIM // 2
        return (z * ic_ref[...] + pltpu.roll(z, LANES - half, 1) * ia_ref[...]
                + pltpu.roll(z, half, 1) * ib_ref[...])

    def per_head(fn):
        def finish(y):
            o_ref[...] = jnp.concatenate(
                [fn(y[:, s * LANES:(s + 1) * LANES]) for s in range(DSA_TN // LANES)],
                axis=1).astype(o_ref.dtype)
        return finish

    def step(finish, multiply=True, last_block_possible=False):
        if finish is not None:
            y_prev = y_ref[...]
        if multiply:
            w = w_ref[...]
            if last_block_possible:
                row = lax.broadcasted_iota(jnp.int32, w.shape, 0)
                w = jnp.where((j == W_LAST) & (row >= W_TAIL_ROWS), 0.0, w)
            y_ref[...] = lax.dot_general(
                h_ref[...], w.astype(h_ref.dtype), (((1,), (1,)), ((), ())),
                preferred_element_type=F32)
        if finish is not None:
            finish(y_prev)

    @pl.when(j == 0)
    def _():
        step(None)

    @pl.when((jp >= 0) & (jp < W_K_FIRST))
    def _():
        scale = HEAD_DIM ** -0.5 * math.log2(math.e)
        step(per_head(lambda z: rope(_rms_norm_rows(z, qn_ref[...])) * scale))

    @pl.when((jp >= W_K_FIRST) & (jp < W_V_FIRST))
    def _():
        step(per_head(lambda z: rope(_rms_norm_rows(z, kn_ref[...]))))

    @pl.when((jp >= W_V_FIRST) & (jp < W_IQ_FIRST))
    def _():
        def finish(y):
            o_ref[...] = y.astype(o_ref.dtype)
            vt_ref[...] = jnp.transpose(y).astype(vt_ref.dtype)
        step(finish)

    @pl.when((jp >= W_IQ_FIRST) & (jp < W_LAST))
    def _():
        step(per_head(idx_rope), last_block_possible=True)

    @pl.when(jp == W_LAST)
    def _():
        def finish(y):
            o_ref[...] = jnp.concatenate(
                [idx_rope(y[:, :LANES]), jnp.zeros_like(y[:, LANES:])], axis=1).astype(o_ref.dtype)
            iw_ref[...] = y[:, LANES:2 * LANES] * (IDX_HEADS ** -0.5 * IDX_DIM ** -0.5)
        step(finish, multiply=False)


def dsa_projection(x, g, w_layers, layer, q_norm, k_norm, tables, *, tm):
    t, d = x.shape
    assert w_layers.shape[1] == Q_COLS + 2 * KV_COLS + IQ_COLS + IDX_DIM + IDX_HEADS
    rc, rs, ic, ia, ib = tables
    tab_spec = pl.BlockSpec((tm, LANES), lambda i, j: (i, 0))
    vec_spec = pl.BlockSpec((1, LANES), lambda i, j: (0, 0))
    need = (2 * _nbytes((tm, d), F32) + 3 * _nbytes((d, DSA_TN), F32)
            + 2 * _nbytes((tm, DSA_TN), MXU_DTYPE) + _nbytes((tm, d), MXU_DTYPE)
            + 12 * _nbytes((tm, LANES), F32) + 3 * _nbytes((tm, DSA_TN), F32))
    return pl.pallas_call(
        _dsa_proj_kernel,
        grid=(t // tm, W_LAST + 2),
        in_specs=[
            pl.BlockSpec((tm, d), lambda i, j: (i, 0)),
            pl.BlockSpec((1, d), lambda i, j: (0, 0)),
            pl.BlockSpec((None, DSA_TN, d), lambda i, j: (layer, jnp.minimum(j, W_LAST), 0)),
            vec_spec, vec_spec, tab_spec, tab_spec, tab_spec, tab_spec, tab_spec,
        ],
        out_specs=[
            pl.BlockSpec((tm, DSA_TN), lambda i, j: (i, _dsa_out_block(jnp.maximum(j - 1, 0)))),
            pl.BlockSpec((tm, LANES), lambda i, j: (i, 0)),
            pl.BlockSpec((DSA_TN, tm), lambda i, j: (jnp.clip(j - 1 - W_V_FIRST, 0, KV_BLOCKS - 1), i)),
        ],
        out_shape=[
            jax.ShapeDtypeStruct((t, DSA_COLS), MXU_DTYPE),
            jax.ShapeDtypeStruct((t, LANES), F32),
            jax.ShapeDtypeStruct((KV_COLS, t), MXU_DTYPE),
        ],
        scratch_shapes=[pltpu.VMEM((tm, d), MXU_DTYPE), pltpu.VMEM((tm, DSA_TN), F32)],
        compiler_params=pltpu.CompilerParams(
            dimension_semantics=("parallel", "arbitrary"),
            vmem_limit_bytes=_vmem_limit(need)),
        name="dsa_projection",
    )(x, g.reshape(1, d), w_layers, q_norm.reshape(1, LANES), k_norm.reshape(1, LANES), rc, rs, ic, ia, ib)


def _float_key(v):
    b = lax.bitcast_convert_type(v, jnp.int32)
    return b ^ ((b >> 31) & 0x7FFFFFFF)


def _key_float(k):
    return lax.bitcast_convert_type(k ^ ((k >> 31) & 0x7FFFFFFF), F32)


_KLO, _KHI, _CLO, _CHI, _JCUT, _SEED, _SEED_UP, _SEED_DOWN, _SEED_UP3, _SEED_DOWN3 = range(10)
SEED_STEP = 0.15
_SEARCH_CAP = 96
ONES_ROWS = 2 * SUBLANES
ATTN_PAD_FROM = 2
ATTN_LOOKAHEAD = 1


def _dsa_core_kernel(q_ref, iq_ref, iw_ref, k_ref, vt_ref, ik_ref, o_ref,
                     key_ref, bias_ref, s_ref, wt_ref, st_ref, m_ref, l_ref, acc_ref, *, tq, topk, kt, ka):
    i = pl.program_id(0)
    nc = i + 1
    group = N_HEADS // N_KV_HEADS
    nt = (((1,), (1,)), ((), ()))

    wt_ref[...] = jnp.transpose(iw_ref[...])

    krow = lax.broadcasted_iota(jnp.int32, (tq, tq), 0)
    qcol = lax.broadcasted_iota(jnp.int32, (tq, tq), 1)
    diag_causal = krow <= qcol

    def chunk_start(c):
        return pl.multiple_of(c * tq, tq)

    def over_chunks(body, k, init, pad_from=None):
        n_tiles = (nc + k - min(pad_from or k, k)) // k
        carry = lax.fori_loop(0, n_tiles, lambda ti, c: body(ti * k, k, c), init)
        if k > 1:
            carry = lax.fori_loop(n_tiles * k, nc, lambda c0, c: body(c0, 1, c), carry)
        return carry

    def index_tile(first, k, carry):
        smin, smax, ssum, ssq, sampled = carry
        iks = [ik_ref[pl.ds(chunk_start(first + u), tq), :] for u in range(k)]
        scores = [jnp.zeros((tq, tq), F32) for _ in range(k)]
        for h in range(IDX_HEADS):
            iqh = iq_ref[:, h * IDX_DIM:(h + 1) * IDX_DIM]
            w = jnp.broadcast_to(wt_ref[h:h + 1, :], (tq, tq))
            for u in range(k):
                logits = lax.dot_general(iks[u], iqh, nt, preferred_element_type=F32)
                scores[u] = scores[u] + w * jnp.maximum(logits, 0.0)
        for u in range(k):
            c = first + u
            valid = (c < i) | diag_causal
            key_ref[c] = jnp.where(valid, _float_key(scores[u]), INT_MIN)
            smin = jnp.minimum(smin, jnp.min(jnp.where(valid, scores[u], jnp.inf), axis=0, keepdims=True))
            smax = jnp.maximum(smax, jnp.max(jnp.where(valid, scores[u], -jnp.inf), axis=0, keepdims=True))
            if u == 0:
                sample = jnp.where(c < i, scores[u], 0.0)
                ssum = ssum + jnp.sum(sample, axis=0, keepdims=True)
                ssq = ssq + jnp.sum(sample * sample, axis=0, keepdims=True)
                sampled = sampled + jnp.where(c < i, float(tq), 0.0)
        return smin, smax, ssum, ssq, sampled

    zero_row = jnp.zeros((1, tq), F32)
    smin, smax, ssum, ssq, sampled = over_chunks(
        index_tile, kt, (jnp.full((1, tq), jnp.inf, F32), jnp.full((1, tq), -jnp.inf, F32),
                         zero_row, zero_row, jnp.float32(0.0)))

    def mask_chunk(c, carry):
        key_ref[c] = jnp.full((tq, tq), INT_MIN, jnp.int32)
        return carry

    lax.fori_loop(nc, (nc + ka - min(ATTN_PAD_FROM, ka)) // ka * ka, mask_chunk, 0)

    def count_ge(cand):
        cand_b = jnp.broadcast_to(cand, (tq, tq))

        def body(first, k, cnt):
            for u in range(k):
                hit = jnp.where(key_ref[first + u] >= cand_b, 1.0, 0.0)
                cnt = cnt + jnp.sum(hit.reshape(tq // SUBLANES, SUBLANES, tq), axis=0)
            return cnt

        cnt = over_chunks(body, kt, jnp.zeros((SUBLANES, tq), F32))
        return jnp.sum(cnt, axis=0, keepdims=True)

    st_ref[_KLO] = _float_key(smin)
    st_ref[_KHI] = _float_key(smax) + 1
    n_causal = lax.broadcasted_iota(jnp.int32, (1, tq), 1) + (i * tq + 1)
    st_ref[_CLO] = n_causal
    st_ref[_CHI] = jnp.zeros((1, tq), jnp.int32)

    mean = ssum / sampled
    dev = jnp.sqrt(jnp.maximum(ssq / sampled - mean * mean, 0.0))
    upper = jnp.minimum(topk / n_causal.astype(F32), 1.0)
    tail = jnp.clip(jnp.minimum(upper, 1.0 - upper), 1e-6, 0.5)
    r = jnp.sqrt(-2.0 * jnp.log(tail))
    z = r - (2.515517 + 0.802853 * r + 0.010328 * r * r) / (
        1.0 + 1.432788 * r + 0.189269 * r * r + 0.001308 * r * r * r)
    guess = mean + jnp.where(upper <= 0.5, z, -z) * dev
    st_ref[_SEED] = _float_key(guess)
    st_ref[_SEED_UP] = _float_key(guess + SEED_STEP * dev)
    st_ref[_SEED_DOWN] = _float_key(guess - SEED_STEP * dev)
    st_ref[_SEED_UP3] = _float_key(guess + 3 * SEED_STEP * dev)
    st_ref[_SEED_DOWN3] = _float_key(guess - 3 * SEED_STEP * dev)

    def unfinished(klo, khi, clo):
        return (clo > topk) & ((khi - klo) != 1) & (st_ref[_CHI] != topk - 1)

    def search_cond(st):
        step, pending = st
        return (step < _SEARCH_CAP) & (pending > 0)

    def any_lane(mask):
        return jnp.max(jnp.where(mask, 1.0, 0.0))

    def search_body(st):
        step, _ = st
        klo, khi, clo = st_ref[_KLO], st_ref[_KHI], st_ref[_CLO]
        live = unfinished(klo, khi, clo)
        pending = any_lane(live)
        vmid = 0.5 * _key_float(klo) + 0.5 * _key_float(khi)
        cmid = _float_key(vmid)
        kmid = klo + lax.shift_right_logical(khi - klo, 1)
        cand = jnp.where((cmid > klo) & (cmid < khi), cmid, kmid)
        seed, up, down = st_ref[_SEED], st_ref[_SEED_UP], st_ref[_SEED_DOWN]
        second = jnp.where(klo == seed, up, down)
        third = jnp.where(klo == up, st_ref[_SEED_UP3], jnp.where(khi == down, st_ref[_SEED_DOWN3], cand))
        seeded = jnp.where(step == 0, seed, jnp.where(step == 1, second, third))
        cand = jnp.where((step < 3) & (seeded > klo) & (seeded < khi), seeded, cand)
        cand = jnp.where(live, cand, klo)
        cnt = count_ge(cand).astype(jnp.int32)
        take = cnt >= topk
        klo = jnp.where(live & take, cand, klo)
        clo = jnp.where(live & take, cnt, clo)
        drop = live & jnp.logical_not(take)
        st_ref[_KLO], st_ref[_KHI], st_ref[_CLO] = klo, jnp.where(drop, cand, khi), clo
        st_ref[_CHI] = jnp.where(drop, cnt, st_ref[_CHI])
        return step + 1, pending

    lax.while_loop(search_cond, search_body, (jnp.int32(0), jnp.float32(1.0)))

    short = ((st_ref[_CLO] > topk) & (st_ref[_CHI] == topk - 1)
             & ((st_ref[_KHI] - st_ref[_KLO]) != 1))

    @pl.when(any_lane(short) > 0)
    def _():
        khi_b = jnp.broadcast_to(st_ref[_KHI], (tq, tq))

        def body(first, k, best):
            for u in range(k):
                key = key_ref[first + u]
                below = jnp.where(key < khi_b, key, INT_MIN)
                best = jnp.maximum(best, jnp.max(below.reshape(tq // SUBLANES, SUBLANES, tq), axis=0))
            return best

        best = over_chunks(body, kt, jnp.full((SUBLANES, tq), INT_MIN, jnp.int32))
        largest = jnp.max(best, axis=0, keepdims=True)
        st_ref[_KLO] = jnp.where(short, largest, st_ref[_KLO])
        st_ref[_CLO] = jnp.where(short, count_ge(largest).astype(jnp.int32), st_ref[_CLO])

    thr_b = jnp.broadcast_to(st_ref[_KLO], (tq, tq))

    n_keys = key_ref.shape[0] * tq
    tied = st_ref[_CLO] > topk
    has_ties = any_lane(tied) > 0
    st_ref[_JCUT] = jnp.full((1, tq), n_keys, jnp.int32)

    @pl.when(has_ties)
    def _():
        want = topk - st_ref[_CHI]

        def count_tied_upto(pos):
            pos_b = jnp.broadcast_to(pos, (tq, tq))

            def body(c, cnt):
                hit = (key_ref[c] == thr_b) & (krow + c * tq <= pos_b)
                return cnt + jnp.sum(jnp.where(hit, 1.0, 0.0), axis=0, keepdims=True)

            return lax.fori_loop(0, nc, body, jnp.zeros((1, tq), F32)).astype(jnp.int32)

        def cut_step(_, st):
            lo, hi = st
            mid = lo + ((hi - lo) >> 1)
            enough = count_tied_upto(mid) >= want
            return jnp.where(enough, lo, mid), jnp.where(enough, mid, hi)

        _, cut = lax.fori_loop(0, n_keys.bit_length(), cut_step,
                               (jnp.full((1, tq), -1, jnp.int32), jnp.full((1, tq), n_keys - 1, jnp.int32)))
        st_ref[_JCUT] = jnp.where(tied, cut, n_keys)

    cut_b = jnp.broadcast_to(st_ref[_JCUT], (tq, tq))

    m_ref[...] = jnp.full(m_ref.shape, NEG_BIG, F32)
    l_ref[...] = jnp.zeros(l_ref.shape, F32)
    acc_ref[...] = jnp.zeros(acc_ref.shape, F32)

    def attend_tile(first, k, carry, *, with_ties):
        for u in range(k):
            c = first + u
            key = key_ref[c]
            if with_ties:
                keep = (key > thr_b) | ((key == thr_b) & (krow + c * tq <= cut_b))
            else:
                keep = key >= thr_b
            bias_ref[u] = jnp.where(keep, 0.0, NEG_BIG)

        starts = [chunk_start(first + u) for u in range(k)]
        slots = ATTN_LOOKAHEAD + 1
        dyn0 = jnp.minimum(first, 0)
        ones_rows = jnp.ones((ONES_ROWS, tq), vt_ref.dtype)

        def logits(h):
            cols = slice(h // group * HEAD_DIM, (h // group + 1) * HEAD_DIM)
            qh = q_ref[:, h * HEAD_DIM:(h + 1) * HEAD_DIM]
            top = None
            for u in range(k):
                s = lax.dot_general(k_ref[pl.ds(starts[u], tq), cols], qh, nt,
                                    preferred_element_type=F32) + bias_ref[u]
                s_ref[h % slots + dyn0, u] = s
                top = s if top is None else jnp.maximum(top, s)
            return jnp.max(top, axis=0, keepdims=True)

        queued = [logits(h) for h in range(ATTN_LOOKAHEAD)]
        for h in range(N_HEADS):
            tile_max = queued.pop(0)
            if h + ATTN_LOOKAHEAD < N_HEADS:
                queued.append(logits(h + ATTN_LOOKAHEAD))
            cols = slice(h // group * HEAD_DIM, (h // group + 1) * HEAD_DIM)
            m_prev = m_ref[h]
            m_cur = jnp.maximum(m_prev, tile_max)
            alpha = jnp.exp2(m_prev - m_cur)
            pv = jnp.zeros((HEAD_DIM + ONES_ROWS, tq), F32)
            for u in range(k):
                p = jnp.exp2(s_ref[h % slots + dyn0, u] - m_cur)
                v_aug = jnp.concatenate([vt_ref[cols, pl.ds(starts[u], tq)], ones_rows], axis=0)
                pv = pv + jnp.dot(v_aug, p.astype(vt_ref.dtype), preferred_element_type=F32)
            l_ref[h] = alpha * l_ref[h] + pv[HEAD_DIM:HEAD_DIM + 1, :]
            pv = pv[:HEAD_DIM, :]
            acc_ref[h] = alpha * acc_ref[h] + pv
            m_ref[h] = m_cur
        return carry

    @pl.when(jnp.logical_not(has_ties))
    def _():
        over_chunks(functools.partial(attend_tile, with_ties=False), ka, 0, pad_from=ATTN_PAD_FROM)

    @pl.when(has_ties)
    def _():
        over_chunks(functools.partial(attend_tile, with_ties=True), 1, 0)

    for h in range(N_HEADS):
        o_ref[:, h * HEAD_DIM:(h + 1) * HEAD_DIM] = jnp.transpose(acc_ref[h] / l_ref[h]).astype(o_ref.dtype)


def dsa_core(proj, iw, vt, *, tq, topk, kt, ka):
    t = proj.shape[0]
    assert (t // tq) % kt == 0 and kt % ka == 0
    k_block = (Q_COLS + IQ_COLS) // KV_COLS
    ik_block = (Q_COLS + IQ_COLS + 2 * KV_COLS) // IDX_DIM
    resident = dict(pipeline_mode=pl.Buffered(1))
    need = (4 * _nbytes((tq, Q_COLS), proj.dtype) + 2 * _nbytes((tq, LANES), F32)
            + 2 * _nbytes((t, KV_COLS), proj.dtype) + _nbytes((t, IDX_DIM), proj.dtype)
            + 2 * _nbytes((tq, Q_COLS), proj.dtype)
            + _nbytes((t // tq, tq, tq), jnp.int32) + _nbytes((kt, tq, tq), F32)
            + _nbytes((LANES, tq), F32) + (3 + 2 * N_HEADS) * _nbytes((SUBLANES, tq), F32)
            + _nbytes((N_HEADS, HEAD_DIM, tq), F32)
            + (ATTN_LOOKAHEAD + 3) * _nbytes((ka, tq, tq), F32))
    return pl.pallas_call(
        functools.partial(_dsa_core_kernel, tq=tq, topk=topk, kt=kt, ka=ka),
        grid=(t // tq,),
        in_specs=[
            pl.BlockSpec((tq, Q_COLS), lambda i: (i, 0)),
            pl.BlockSpec((tq, IQ_COLS), lambda i: (i, 1)),
            pl.BlockSpec((tq, LANES), lambda i: (i, 0)),
            pl.BlockSpec((t, KV_COLS), lambda i: (0, k_block), **resident),
            pl.BlockSpec((KV_COLS, t), lambda i: (0, 0), **resident),
            pl.BlockSpec((t, IDX_DIM), lambda i: (0, ik_block), **resident),
        ],
        out_specs=pl.BlockSpec((tq, Q_COLS), lambda i: (i, 0)),
        out_shape=jax.ShapeDtypeStruct((t, Q_COLS), proj.dtype),
        scratch_shapes=[
            pltpu.VMEM((t // tq, tq, tq), jnp.int32),
            pltpu.VMEM((ka, tq, tq), F32),
            pltpu.VMEM((ATTN_LOOKAHEAD + 1, ka, tq, tq), F32),
            pltpu.VMEM((LANES, tq), F32),
            pltpu.VMEM((10, 1, tq), jnp.int32),
            pltpu.VMEM((N_HEADS, 1, tq), F32),
            pltpu.VMEM((N_HEADS, 1, tq), F32),
            pltpu.VMEM((N_HEADS, HEAD_DIM, tq), F32),
        ],
        compiler_params=pltpu.CompilerParams(
            dimension_semantics=("arbitrary",),
            vmem_limit_bytes=_vmem_limit(need)),
        name="dsa_core",
    )(proj, proj, iw, proj, vt, proj)


def _rope_tables(pos):
    def angles(dim):
        inv = 1.0 / (ROPE_THETA ** (jnp.arange(0, dim, 2, dtype=F32) / dim))
        return pos.astype(F32)[:, None] * inv

    ang = angles(HEAD_DIM)
    cos, sin = jnp.cos(ang), jnp.sin(ang)
    rc = jnp.concatenate([cos, cos], axis=1)
    rs = jnp.concatenate([-sin, sin], axis=1)
    iang = angles(IDX_ROPE_DIM)
    icos, isin = jnp.cos(iang), jnp.sin(iang)
    rest = LANES - IDX_ROPE_DIM
    zeros = jnp.zeros_like(isin)
    ic = jnp.concatenate([icos, icos, jnp.ones((pos.shape[0], rest), F32)], axis=1)
    ia = jnp.concatenate([-isin, zeros, jnp.zeros((pos.shape[0], rest), F32)], axis=1)
    ib = jnp.concatenate([zeros, isin, jnp.zeros((pos.shape[0], rest), F32)], axis=1)
    return rc, rs, ic, ia, ib


def dsa_layer(x, pos, norm_g, w_in_layers, layer, q_norm, k_norm, w_out, tiles):
    t = x.shape[0]
    proj, iw, vt = dsa_projection(x, norm_g, jnp.swapaxes(w_in_layers, 1, 2), layer, q_norm, k_norm,
                                  _rope_tables(pos), tm=tiles["proj_tm"])
    attn = dsa_core(proj, iw, vt, tq=tiles["tq"], topk=min(TOPK_MAX, t // 4), kt=tiles["kt"], ka=tiles["ka"])
    return matmul_residual(attn, w_out.astype(MXU_DTYPE), x, tm=tiles["res_tm"], tn=tiles["res_tn"])


def _rglru_kernel(gate_ref, xr_ref, cw_ref, cb_ref, wa_ref, ba_ref, wx_ref, bx_ref, lam_ref,
                  y_ref, hcar_ref, xprev_ref, *, tt):
    @pl.when(pl.program_id(1) == 0)
    def _():
        hcar_ref[...] = jnp.zeros_like(hcar_ref)
        xprev_ref[...] = jnp.zeros_like(xprev_ref)

    xr = xr_ref[...]
    ext = jnp.concatenate([xprev_ref[...], xr], axis=0)
    cw = cw_ref[...]
    xc = cb_ref[...] + xr * cw[CONV_WIDTH - 1:CONV_WIDTH, :]
    for d in range(1, CONV_WIDTH):
        xc = xc + pltpu.roll(ext, d, 0)[SUBLANES:, :] * cw[CONV_WIDTH - 1 - d:CONV_WIDTH - d, :]
    xprev_ref[...] = xr[tt - SUBLANES:, :]

    xcb = xc.astype(wa_ref.dtype)
    r = jax.nn.sigmoid(jnp.dot(xcb, wa_ref[...], preferred_element_type=F32) + ba_ref[...])
    ig = jax.nn.sigmoid(jnp.dot(xcb, wx_ref[...], preferred_element_type=F32) + bx_ref[...])
    nlam = -lam_ref[...]
    softplus = jnp.maximum(nlam, 0.0) + jnp.log(1.0 + jnp.exp(-jnp.abs(nlam)))
    log_a = -LRU_C * r * softplus
    a = jnp.exp(log_a)
    mult = jnp.sqrt(1.0 - a * a)
    b = xc * ig * mult

    n_groups = tt // SUBLANES
    a = a.reshape(n_groups, SUBLANES, a.shape[1])
    b = b.reshape(a.shape)
    in_group = lax.broadcasted_iota(jnp.int32, a.shape, 1)
    d = 1
    while d < SUBLANES:
        b = a * jnp.where(in_group >= d, pltpu.roll(b, d, 1), 0.0) + b
        a = a * jnp.where(in_group >= d, pltpu.roll(a, d, 1), 1.0)
        d *= 2
    carry = hcar_ref[...]
    groups = []
    for gi in range(n_groups):
        hg = b[gi] + a[gi] * carry
        carry = hg[SUBLANES - 1:SUBLANES, :]
        groups.append(hg)
    hcar_ref[...] = carry
    h = jnp.concatenate(groups, axis=0)
    y_ref[...] = (h * jax.nn.gelu(gate_ref[...])).astype(y_ref.dtype)


def rglru_scan(proj, conv_w, conv_b, wa, ba, wx, bx, lam, *, tt):
    t = proj.shape[0]
    width = conv_w.shape[1]
    nb, blk, _ = wa.shape
    vec = lambda: pl.BlockSpec((1, blk), lambda n, s: (0, n))
    need = (4 * _nbytes((tt, blk), F32) + 2 * _nbytes((tt, blk), MXU_DTYPE)
            + 4 * _nbytes((blk, blk), wa.dtype) + 24 * _nbytes((tt, blk), F32))
    return pl.pallas_call(
        functools.partial(_rglru_kernel, tt=tt),
        grid=(nb, t // tt),
        in_specs=[
            pl.BlockSpec((tt, blk), lambda n, s: (s, n)),
            pl.BlockSpec((tt, blk), lambda n, s: (s, nb + n)),
            pl.BlockSpec((CONV_WIDTH, blk), lambda n, s: (0, n)),
            vec(),
            pl.BlockSpec((None, blk, blk), lambda n, s: (n, 0, 0)),
            vec(),
            pl.BlockSpec((None, blk, blk), lambda n, s: (n, 0, 0)),
            vec(),
            vec(),
        ],
        out_specs=pl.BlockSpec((tt, blk), lambda n, s: (s, n)),
        out_shape=jax.ShapeDtypeStruct((t, width), MXU_DTYPE),
        scratch_shapes=[pltpu.VMEM((1, blk), F32), pltpu.VMEM((SUBLANES, blk), F32)],
        compiler_params=pltpu.CompilerParams(
            dimension_semantics=("parallel", "arbitrary"),
            vmem_limit_bytes=_vmem_limit(need)),
        name="rglru_scan",
    )(proj, proj, conv_w, conv_b.reshape(1, width), wa, ba.reshape(1, width),
      wx, bx.reshape(1, width), lam.reshape(1, width))


def rglru_layer(x, norm_g, w_in, conv_w, conv_b, wa, ba, wx, bx, lam, w_out, tiles):
    proj = norm_matmul(x, norm_g, w_in.astype(MXU_DTYPE), tm=tiles["proj_tm"], tn=tiles["proj_tn"],
                       out_dtype=F32)
    y = rglru_scan(proj, conv_w, conv_b, wa.astype(MXU_DTYPE), ba, wx.astype(MXU_DTYPE), bx, lam,
                   tt=tiles["scan_tt"])
    return matmul_residual(y, w_out.astype(MXU_DTYPE), x, tm=tiles["res_tm"], tn=tiles["res_tn"])


POOL_HALO = max(POOL_WINDOWS)


def _pool_kernel(x_ref, g_ref, w_ref, b_ref, s_ref, o_ref, halo_ref, *, tt):
    blk = pl.program_id(0)

    @pl.when(blk == 0)
    def _():
        halo_ref[...] = jnp.zeros_like(halo_ref)

    x = x_ref[...]
    h = _rms_norm_rows(x, g_ref[...])
    ext = jnp.concatenate([halo_ref[...], h], axis=0)
    halo_ref[...] = h[tt - POOL_HALO:, :]

    gw = h.shape[1] // len(POOL_WINDOWS)
    t1 = (lax.broadcasted_iota(jnp.int32, (tt, gw), 0) + (blk * tt + 1)).astype(F32)
    for gi, win in enumerate(POOL_WINDOWS):
        cols = slice(gi * gw, (gi + 1) * gw)
        acc = ext[:, cols]
        d = 1
        while d < win:
            acc = acc + pltpu.roll(acc, d, 0)
            d *= 2
        mean = acc[POOL_HALO:, :] / jnp.minimum(t1, float(win))
        y = (mean - h[:, cols]).astype(w_ref.dtype)
        z = jnp.dot(y, w_ref[gi], preferred_element_type=F32) + b_ref[gi:gi + 1, :]
        o_ref[:, cols] = x[:, cols] + z * s_ref[:, cols]


def pool_layer(x, norm_g, w_group, b_group, scale, tiles):
    t, d = x.shape
    tt = tiles["pool_tt"]
    ng, gw, _ = w_group.shape
    need = (4 * _nbytes((tt, d), F32) + 2 * _nbytes((ng, gw, gw), MXU_DTYPE)
            + 6 * _nbytes((tt, d), F32))
    return pl.pallas_call(
        functools.partial(_pool_kernel, tt=tt),
        grid=(t // tt,),
        in_specs=[
            pl.BlockSpec((tt, d), lambda i: (i, 0)),
            pl.BlockSpec((1, d), lambda i: (0, 0)),
            pl.BlockSpec((ng, gw, gw), lambda i: (0, 0, 0)),
            pl.BlockSpec((ng, gw), lambda i: (0, 0)),
            pl.BlockSpec((1, d), lambda i: (0, 0)),
        ],
        out_specs=pl.BlockSpec((tt, d), lambda i: (i, 0)),
        out_shape=jax.ShapeDtypeStruct((t, d), F32),
        scratch_shapes=[pltpu.VMEM((POOL_HALO, d), F32)],
        compiler_params=pltpu.CompilerParams(
            dimension_semantics=("arbitrary",),
            vmem_limit_bytes=_vmem_limit(need)),
        name="pool_mixer",
    )(x, norm_g.reshape(1, d), w_group.astype(MXU_DTYPE), b_group, scale.reshape(1, d))


def _tiles(t):
    big = min(t, 512)
    tall = min(t, 1024)
    return {
        "proj_tm": tall, "proj_tn": 1024,
        "res_tm": tall, "res_tn": 1024,
        "tq": min(t, 256), "kt": min(4, t // min(t, 256)), "ka": min(4, t // min(t, 256)),
        "scan_tt": min(t, 512),
        "pool_tt": min(t, 256),
        "mlp_tm": big, "mlp_tf": 1024,
    }


def kernel(x, positions, attn_norm, attn_w_in, attn_q_norm, attn_k_norm, attn_w_out, rnn_norm, rnn_w_in, rnn_conv_w, rnn_conv_b, rnn_gate_a_w, rnn_gate_a_b, rnn_gate_x_w, rnn_gate_x_b, rnn_lambda, rnn_w_out, pool_norm, pool_w, pool_b, pool_scale, mlp_norm, mlp_w_up, mlp_w_down):
    batch, t, d = x.shape
    depth = mlp_norm.shape[0]
    tiles = _tiles(t)
    w_up_all = mlp_w_up.astype(MXU_DTYPE)
    w_down_all = mlp_w_down.astype(MXU_DTYPE)
    outs = []
    for bi in range(batch):
        xb = x[bi]
        pos = positions[bi]
        for i in range(depth):
            kind, j = i % N_MIXERS, i // N_MIXERS
            if kind == 0:
                xb = dsa_layer(xb, pos, attn_norm[j], attn_w_in, j, attn_q_norm[j], attn_k_norm[j],
                               attn_w_out[j], tiles)
            elif kind == 1:
                xb = rglru_layer(xb, rnn_norm[j], rnn_w_in[j], rnn_conv_w[j], rnn_conv_b[j],
                                 rnn_gate_a_w[j], rnn_gate_a_b[j], rnn_gate_x_w[j], rnn_gate_x_b[j],
                                 rnn_lambda[j], rnn_w_out[j], tiles)
            else:
                xb = pool_layer(xb, pool_norm[j], pool_w[j], pool_b[j], pool_scale[j], tiles)
            xb = mlp_block(xb, mlp_norm[i], w_up_all, w_down_all, i,
                           tm=tiles["mlp_tm"], tf=tiles["mlp_tf"])
        outs.append(xb)
    return outs[0][None] if batch == 1 else jnp.stack(outs, axis=0)
```

```python
import functools
import math

import jax
import jax.numpy as jnp
from jax import lax
from jax.experimental import pallas as pl
from jax.experimental.pallas import tpu as pltpu

F32 = jnp.float32
MXU_DTYPE = jnp.bfloat16

N_MIXERS = 3
EPS = 1e-6
ROPE_THETA = 10000.0
HEAD_DIM = 128
N_HEADS = 16
N_KV_HEADS = 4
IDX_HEADS = 16
IDX_DIM = 128
IDX_ROPE_DIM = 64
TOPK_MAX = 256
CONV_WIDTH = 4
LRU_C = 8.0
POOL_WINDOWS = (2, 4, 8, 16)

LANES = 128
SUBLANES = 8
VMEM_BYTES_V7X = 64 * 1024 * 1024
VMEM_CAP_BYTES = VMEM_BYTES_V7X - 8 * 1024 * 1024

INT_MIN = -(2 ** 31)
NEG_BIG = -1e30


def _vmem_limit(block_bytes):
    return int(min(VMEM_CAP_BYTES, block_bytes * 3 // 2 + (4 << 20)))


def _nbytes(shape, dtype):
    return math.prod(shape) * jnp.dtype(dtype).itemsize


def _rms_norm_rows(x, g):
    ms = jnp.mean(x * x, axis=-1, keepdims=True)
    return x * lax.rsqrt(ms + EPS) * g


def _norm_matmul_kernel(x_ref, g_ref, w_ref, o_ref, h_ref):
    @pl.when(pl.program_id(1) == 0)
    def _():
        h_ref[...] = _rms_norm_rows(x_ref[...], g_ref[...]).astype(h_ref.dtype)

    o_ref[...] = jnp.dot(h_ref[...], w_ref[...], preferred_element_type=F32).astype(o_ref.dtype)


def norm_matmul(x, g, w, *, tm, tn, out_dtype):
    t, d = x.shape
    n = w.shape[1]
    need = (2 * _nbytes((tm, d), F32) + 2 * _nbytes((d, tn), w.dtype)
            + 2 * _nbytes((tm, tn), out_dtype) + _nbytes((tm, d), w.dtype))
    return pl.pallas_call(
        _norm_matmul_kernel,
        grid=(t // tm, n // tn),
        in_specs=[
            pl.BlockSpec((tm, d), lambda i, j: (i, 0)),
            pl.BlockSpec((1, d), lambda i, j: (0, 0)),
            pl.BlockSpec((d, tn), lambda i, j: (0, j)),
        ],
        out_specs=pl.BlockSpec((tm, tn), lambda i, j: (i, j)),
        out_shape=jax.ShapeDtypeStruct((t, n), out_dtype),
        scratch_shapes=[pltpu.VMEM((tm, d), w.dtype)],
        compiler_params=pltpu.CompilerParams(
            dimension_semantics=("parallel", "arbitrary"),
            vmem_limit_bytes=_vmem_limit(need)),
        name="norm_matmul",
    )(x, g.reshape(1, d), w)


def _matmul_residual_kernel(a_ref, w_ref, x_ref, o_ref):
    o_ref[...] = x_ref[...] + jnp.dot(a_ref[...], w_ref[...], preferred_element_type=F32)


def matmul_residual(a, w, x, *, tm, tn):
    t, k = a.shape
    n = w.shape[1]
    need = (2 * _nbytes((tm, k), a.dtype) + 2 * _nbytes((k, tn), w.dtype)
            + 4 * _nbytes((tm, tn), F32))
    return pl.pallas_call(
        _matmul_residual_kernel,
        grid=(t // tm, n // tn),
        in_specs=[
            pl.BlockSpec((tm, k), lambda i, j: (i, 0)),
            pl.BlockSpec((k, tn), lambda i, j: (0, j)),
            pl.BlockSpec((tm, tn), lambda i, j: (i, j)),
        ],
        out_specs=pl.BlockSpec((tm, tn), lambda i, j: (i, j)),
        out_shape=jax.ShapeDtypeStruct((t, n), F32),
        compiler_params=pltpu.CompilerParams(
            dimension_semantics=("parallel", "arbitrary"),
            vmem_limit_bytes=_vmem_limit(need)),
        name="matmul_residual",
    )(a, w, x)


def _mlp_kernel(x_ref, g_ref, wu_ref, wd_ref, o_ref, h_ref, acc_ref):
    f = pl.program_id(1)

    @pl.when(f == 0)
    def _():
        h_ref[...] = _rms_norm_rows(x_ref[...], g_ref[...]).astype(h_ref.dtype)
        acc_ref[...] = jnp.zeros_like(acc_ref)

    u = jnp.dot(h_ref[...], wu_ref[...], preferred_element_type=F32)
    u = jnp.square(jnp.maximum(u, 0.0)).astype(wd_ref.dtype)
    acc_ref[...] += jnp.dot(u, wd_ref[...], preferred_element_type=F32)

    @pl.when(f == pl.num_programs(1) - 1)
    def _():
        o_ref[...] = x_ref[...] + acc_ref[...]


def mlp_block(x, g, w_up, w_down, layer, *, tm, tf):
    t, d = x.shape
    ff = w_up.shape[2]
    need = (4 * _nbytes((tm, d), F32) + 2 * _nbytes((d, tf), w_up.dtype)
            + 2 * _nbytes((tf, d), w_down.dtype) + _nbytes((tm, d), w_up.dtype)
            + _nbytes((tm, d), F32) + 2 * _nbytes((tm, tf), F32))
    return pl.pallas_call(
        _mlp_kernel,
        grid=(t // tm, ff // tf),
        in_specs=[
            pl.BlockSpec((tm, d), lambda i, f: (i, 0)),
            pl.BlockSpec((1, d), lambda i, f: (0, 0)),
            pl.BlockSpec((None, d, tf), lambda i, f: (layer, 0, f)),
            pl.BlockSpec((None, tf, d), lambda i, f: (layer, f, 0)),
        ],
        out_specs=pl.BlockSpec((tm, d), lambda i, f: (i, 0)),
        out_shape=jax.ShapeDtypeStruct((t, d), F32),
        scratch_shapes=[pltpu.VMEM((tm, d), w_up.dtype), pltpu.VMEM((tm, d), F32)],
        compiler_params=pltpu.CompilerParams(
            dimension_semantics=("parallel", "arbitrary"),
            vmem_limit_bytes=_vmem_limit(need)),
        name="mlp_block",
    )(x, g.reshape(1, d), w_up, w_down)


DSA_TN = 4 * LANES
Q_COLS = N_HEADS * HEAD_DIM
IQ_COLS = IDX_HEADS * IDX_DIM
KV_COLS = N_KV_HEADS * HEAD_DIM
DSA_COLS = Q_COLS + IQ_COLS + 2 * KV_COLS + DSA_TN
Q_BLOCKS = Q_COLS // DSA_TN
IQ_BLOCKS = IQ_COLS // DSA_TN
KV_BLOCKS = KV_COLS // DSA_TN


W_K_FIRST = Q_BLOCKS
W_V_FIRST = W_K_FIRST + KV_BLOCKS
W_IQ_FIRST = W_V_FIRST + KV_BLOCKS
W_LAST = W_IQ_FIRST + IQ_BLOCKS
W_TAIL_ROWS = IDX_DIM + IDX_HEADS
OUT_K_FIRST = Q_BLOCKS + IQ_BLOCKS


def _dsa_out_block(jw):
    return jnp.where(jw < W_K_FIRST, jw,
                     jnp.where(jw < W_IQ_FIRST, jw + (OUT_K_FIRST - W_K_FIRST),
                               jnp.where(jw < W_LAST, jw - (W_IQ_FIRST - Q_BLOCKS), W_LAST)))


def _dsa_proj_kernel(x_ref, g_ref, w_ref, qn_ref, kn_ref, rc_ref, rs_ref,
                     ic_ref, ia_ref, ib_ref, o_ref, iw_ref, vt_ref, h_ref, y_ref):
    j = pl.program_id(1)
    jp = j - 1

    @pl.when(j == 0)
    def _():
        h_ref[...] = _rms_norm_rows(x_ref[...], g_ref[...]).astype(h_ref.dtype)

    def rope(z):
        return z * rc_ref[...] + pltpu.roll(z, HEAD_DIM // 2, 1) * rs_ref[...]

    def idx_rope(z):
        half = IDX_ROPE_DIM // 2
        return (z * ic_ref[...] + pltpu.roll(z, LANES - half, 1) * ia_ref[...]
                + pltpu.roll(z, half, 1) * ib_ref[...])

    def per_head(fn):
        def finish(y):
            o_ref[...] = jnp.concatenate(
                [fn(y[:, s * LANES:(s + 1) * LANES]) for s in range(DSA_TN // LANES)],
                axis=1).astype(o_ref.dtype)
        return finish

    def step(finish, multiply=True, last_block_possible=False):
        if finish is not None:
            y_prev = y_ref[...]
        if multiply:
            w = w_ref[...]
            if last_block_possible:
                row = lax.broadcasted_iota(jnp.int32, w.shape, 0)
                w = jnp.where((j == W_LAST) & (row >= W_TAIL_ROWS), 0.0, w)
            y_ref[...] = lax.dot_general(
                h_ref[...], w.astype(h_ref.dtype), (((1,), (1,)), ((), ())),
                preferred_element_type=F32)
        if finish is not None:
            finish(y_prev)

    @pl.when(j == 0)
    def _():
        step(None)

    @pl.when((jp >= 0) & (jp < W_K_FIRST))
    def _():
        scale = HEAD_DIM ** -0.5 * math.log2(math.e)
        step(per_head(lambda z: rope(_rms_norm_rows(z, qn_ref[...])) * scale))

    @pl.when((jp >= W_K_FIRST) & (jp < W_V_FIRST))
    def _():
        step(per_head(lambda z: rope(_rms_norm_rows(z, kn_ref[...]))))

    @pl.when((jp >= W_V_FIRST) & (jp < W_IQ_FIRST))
    def _():
        def finish(y):
            o_ref[...] = y.astype(o_ref.dtype)
            vt_ref[...] = jnp.transpose(y).astype(vt_ref.dtype)
        step(finish)

    @pl.when((jp >= W_IQ_FIRST) & (jp < W_LAST))
    def _():
        step(per_head(idx_rope), last_block_possible=True)

    @pl.when(jp == W_LAST)
    def _():
        def finish(y):
            o_ref[...] = jnp.concatenate(
                [idx_rope(y[:, :LANES]), jnp.zeros_like(y[:, LANES:])], axis=1).astype(o_ref.dtype)
            iw_ref[...] = y[:, LANES:2 * LANES] * (IDX_HEADS ** -0.5 * IDX_DIM ** -0.5)
        step(finish, multiply=False)


def dsa_projection(x, g, w_layers, layer, q_norm, k_norm, tables, *, tm):
    t, d = x.shape
    assert w_layers.shape[1] == Q_COLS + 2 * KV_COLS + IQ_COLS + IDX_DIM + IDX_HEADS
    rc, rs, ic, ia, ib = tables
    tab_spec = pl.BlockSpec((tm, LANES), lambda i, j: (i, 0))
    vec_spec = pl.BlockSpec((1, LANES), lambda i, j: (0, 0))
    need = (2 * _nbytes((tm, d), F32) + 3 * _nbytes((d, DSA_TN), F32)
            + 2 * _nbytes((tm, DSA_TN), MXU_DTYPE) + _nbytes((tm, d), MXU_DTYPE)
            + 12 * _nbytes((tm, LANES), F32) + 3 * _nbytes((tm, DSA_TN), F32))
    return pl.pallas_call(
        _dsa_proj_kernel,
        grid=(t // tm, W_LAST + 2),
        in_specs=[
            pl.BlockSpec((tm, d), lambda i, j: (i, 0)),
            pl.BlockSpec((1, d), lambda i, j: (0, 0)),
            pl.BlockSpec((None, DSA_TN, d), lambda i, j: (layer, jnp.minimum(j, W_LAST), 0)),
            vec_spec, vec_spec, tab_spec, tab_spec, tab_spec, tab_spec, tab_spec,
        ],
        out_specs=[
            pl.BlockSpec((tm, DSA_TN), lambda i, j: (i, _dsa_out_block(jnp.maximum(j - 1, 0)))),
            pl.BlockSpec((tm, LANES), lambda i, j: (i, 0)),
            pl.BlockSpec((DSA_TN, tm), lambda i, j: (jnp.clip(j - 1 - W_V_FIRST, 0, KV_BLOCKS - 1), i)),
        ],
        out_shape=[
            jax.ShapeDtypeStruct((t, DSA_COLS), MXU_DTYPE),
            jax.ShapeDtypeStruct((t, LANES), F32),
            jax.ShapeDtypeStruct((KV_COLS, t), MXU_DTYPE),
        ],
        scratch_shapes=[pltpu.VMEM((tm, d), MXU_DTYPE), pltpu.VMEM((tm, DSA_TN), F32)],
        compiler_params=pltpu.CompilerParams(
            dimension_semantics=("parallel", "arbitrary"),
            vmem_limit_bytes=_vmem_limit(need)),
        name="dsa_projection",
    )(x, g.reshape(1, d), w_layers, q_norm.reshape(1, LANES), k_norm.reshape(1, LANES), rc, rs, ic, ia, ib)


def _float_key(v):
    b = lax.bitcast_convert_type(v, jnp.int32)
    return b ^ ((b >> 31) & 0x7FFFFFFF)


def _key_float(k):
    return lax.bitcast_convert_type(k ^ ((k >> 31) & 0x7FFFFFFF), F32)


_KLO, _KHI, _CLO, _CHI, _JCUT, _SEED, _SEED_UP, _SEED_DOWN, _SEED_UP3, _SEED_DOWN3 = range(10)
SEED_STEP = 0.15
_SEARCH_CAP = 96
ONES_ROWS = 2 * SUBLANES
ATTN_PAD_FROM = 2
ATTN_LOOKAHEAD = 1


def _dsa_core_kernel(q_ref, iq_ref, iw_ref, k_ref, vt_ref, ik_ref, o_ref,
                     key_ref, bias_ref, s_ref, wt_ref, st_ref, m_ref, l_ref, acc_ref, *, tq, topk, kt, ka):
    i = pl.program_id(0)
    nc = i + 1
    group = N_HEADS // N_KV_HEADS
    nt = (((1,), (1,)), ((), ()))

    wt_ref[...] = jnp.transpose(iw_ref[...])

    krow = lax.broadcasted_iota(jnp.int32, (tq, tq), 0)
    qcol = lax.broadcasted_iota(jnp.int32, (tq, tq), 1)
    diag_causal = krow <= qcol

    def chunk_start(c):
        return pl.multiple_of(c * tq, tq)

    def over_chunks(body, k, init, pad_from=None):
        n_tiles = (nc + k - min(pad_from or k, k)) // k
        carry = lax.fori_loop(0, n_tiles, lambda ti, c: body(ti * k, k, c), init)
        if k > 1:
            carry = lax.fori_loop(n_tiles * k, nc, lambda c0, c: body(c0, 1, c), carry)
        return carry

    def index_tile(first, k, carry):
        smin, smax, ssum, ssq, sampled = carry
        iks = [ik_ref[pl.ds(chunk_start(first + u), tq), :] for u in range(k)]
        scores = [jnp.zeros((tq, tq), F32) for _ in range(k)]
        for h in range(IDX_HEADS):
            iqh = iq_ref[:, h * IDX_DIM:(h + 1) * IDX_DIM]
            w = jnp.broadcast_to(wt_ref[h:h + 1, :], (tq, tq))
            for u in range(k):
                logits = lax.dot_general(iks[u], iqh, nt, preferred_element_type=F32)
                scores[u] = scores[u] + w * jnp.maximum(logits, 0.0)
        for u in range(k):
            c = first + u
            valid = (c < i) | diag_causal
            key_ref[c] = jnp.where(valid, _float_key(scores[u]), INT_MIN)
            smin = jnp.minimum(smin, jnp.min(jnp.where(valid, scores[u], jnp.inf), axis=0, keepdims=True))
            smax = jnp.maximum(smax, jnp.max(jnp.where(valid, scores[u], -jnp.inf), axis=0, keepdims=True))
            if u == 0:
                sample = jnp.where(c < i, scores[u], 0.0)
                ssum = ssum + jnp.sum(sample, axis=0, keepdims=True)
                ssq = ssq + jnp.sum(sample * sample, axis=0, keepdims=True)
                sampled = sampled + jnp.where(c < i, float(tq), 0.0)
        return smin, smax, ssum, ssq, sampled

    zero_row = jnp.zeros((1, tq), F32)
    smin, smax, ssum, ssq, sampled = over_chunks(
        index_tile, kt, (jnp.full((1, tq), jnp.inf, F32), jnp.full((1, tq), -jnp.inf, F32),
                         zero_row, zero_row, jnp.float32(0.0)))

    def mask_chunk(c, carry):
        key_ref[c] = jnp.full((tq, tq), INT_MIN, jnp.int32)
        return carry

    lax.fori_loop(nc, (nc + ka - min(ATTN_PAD_FROM, ka)) // ka * ka, mask_chunk, 0)

    def count_ge(cand):
        cand_b = jnp.broadcast_to(cand, (tq, tq))

        def body(first, k, cnt):
            for u in range(k):
                hit = jnp.where(key_ref[first + u] >= cand_b, 1.0, 0.0)
                cnt = cnt + jnp.sum(hit.reshape(tq // SUBLANES, SUBLANES, tq), axis=0)
            return cnt

        cnt = over_chunks(body, kt, jnp.zeros((SUBLANES, tq), F32))
        return jnp.sum(cnt, axis=0, keepdims=True)

    st_ref[_KLO] = _float_key(smin)
    st_ref[_KHI] = _float_key(smax) + 1
    n_causal = lax.broadcasted_iota(jnp.int32, (1, tq), 1) + (i * tq + 1)
    st_ref[_CLO] = n_causal
    st_ref[_CHI] = jnp.zeros((1, tq), jnp.int32)

    mean = ssum / sampled
    dev = jnp.sqrt(jnp.maximum(ssq / sampled - mean * mean, 0.0))
    upper = jnp.minimum(topk / n_causal.astype(F32), 1.0)
    tail = jnp.clip(jnp.minimum(upper, 1.0 - upper), 1e-6, 0.5)
    r = jnp.sqrt(-2.0 * jnp.log(tail))
    z = r - (2.515517 + 0.802853 * r + 0.010328 * r * r) / (
        1.0 + 1.432788 * r + 0.189269 * r * r + 0.001308 * r * r * r)
    guess = mean + jnp.where(upper <= 0.5, z, -z) * dev
    st_ref[_SEED] = _float_key(guess)
    st_ref[_SEED_UP] = _float_key(guess + SEED_STEP * dev)
    st_ref[_SEED_DOWN] = _float_key(guess - SEED_STEP * dev)
    st_ref[_SEED_UP3] = _float_key(guess + 3 * SEED_STEP * dev)
    st_ref[_SEED_DOWN3] = _float_key(guess - 3 * SEED_STEP * dev)

    def unfinished(klo, khi, clo):
        return (clo > topk) & ((khi - klo) != 1) & (st_ref[_CHI] != topk - 1)

    def search_cond(st):
        step, pending = st
        return (step < _SEARCH_CAP) & (pending > 0)

    def any_lane(mask):
        return jnp.max(jnp.where(mask, 1.0, 0.0))

    def search_body(st):
        step, _ = st
        klo, khi, clo = st_ref[_KLO], st_ref[_KHI], st_ref[_CLO]
        live = unfinished(klo, khi, clo)
        pending = any_lane(live)
        vmid = 0.5 * _key_float(klo) + 0.5 * _key_float(khi)
        cmid = _float_key(vmid)
        kmid = klo + lax.shift_right_logical(khi - klo, 1)
        cand = jnp.where((cmid > klo) & (cmid < khi), cmid, kmid)
        seed, up, down = st_ref[_SEED], st_ref[_SEED_UP], st_ref[_SEED_DOWN]
        second = jnp.where(klo == seed, up, down)
        third = jnp.where(klo == up, st_ref[_SEED_UP3], jnp.where(khi == down, st_ref[_SEED_DOWN3], cand))
        seeded = jnp.where(step == 0, seed, jnp.where(step == 1, second, third))
        cand = jnp.where((step < 3) & (seeded > klo) & (seeded < khi), seeded, cand)
        cand = jnp.where(live, cand, klo)
        cnt = count_ge(cand).astype(jnp.int32)
        take = cnt >= topk
        klo = jnp.where(live & take, cand, klo)
        clo = jnp.where(live & take, cnt, clo)
        drop = live & jnp.logical_not(take)
        st_ref[_KLO], st_ref[_KHI], st_ref[_CLO] = klo, jnp.where(drop, cand, khi), clo
        st_ref[_CHI] = jnp.where(drop, cnt, st_ref[_CHI])
        return step + 1, pending

    lax.while_loop(search_cond, search_body, (jnp.int32(0), jnp.float32(1.0)))

    short = ((st_ref[_CLO] > topk) & (st_ref[_CHI] == topk - 1)
             & ((st_ref[_KHI] - st_ref[_KLO]) != 1))

    @pl.when(any_lane(short) > 0)
    def _():
        khi_b = jnp.broadcast_to(st_ref[_KHI], (tq, tq))

        def body(first, k, best):
            for u in range(k):
                key = key_ref[first + u]
                below = jnp.where(key < khi_b, key, INT_MIN)
                best = jnp.maximum(best, jnp.max(below.reshape(tq // SUBLANES, SUBLANES, tq), axis=0))
            return best

        best = over_chunks(body, kt, jnp.full((SUBLANES, tq), INT_MIN, jnp.int32))
        largest = jnp.max(best, axis=0, keepdims=True)
        st_ref[_KLO] = jnp.where(short, largest, st_ref[_KLO])
        st_ref[_CLO] = jnp.where(short, count_ge(largest).astype(jnp.int32), st_ref[_CLO])

    thr_b = jnp.broadcast_to(st_ref[_KLO], (tq, tq))

    n_keys = key_ref.shape[0] * tq
    tied = st_ref[_CLO] > topk
    has_ties = any_lane(tied) > 0
    st_ref[_JCUT] = jnp.full((1, tq), n_keys, jnp.int32)

    @pl.when(has_ties)
    def _():
        want = topk - st_ref[_CHI]

        def count_tied_upto(pos):
            pos_b = jnp.broadcast_to(pos, (tq, tq))

            def body(c, cnt):
                hit = (key_ref[c] == thr_b) & (krow + c * tq <= pos_b)
                return cnt + jnp.sum(jnp.where(hit, 1.0, 0.0), axis=0, keepdims=True)

            return lax.fori_loop(0, nc, body, jnp.zeros((1, tq), F32)).astype(jnp.int32)

        def cut_step(_, st):
            lo, hi = st
            mid = lo + ((hi - lo) >> 1)
            enough = count_tied_upto(mid) >= want
            return jnp.where(enough, lo, mid), jnp.where(enough, mid, hi)

        _, cut = lax.fori_loop(0, n_keys.bit_length(), cut_step,
                               (jnp.full((1, tq), -1, jnp.int32), jnp.full((1, tq), n_keys - 1, jnp.int32)))
        st_ref[_JCUT] = jnp.where(tied, cut, n_keys)

    cut_b = jnp.broadcast_to(st_ref[_JCUT], (tq, tq))

    m_ref[...] = jnp.full(m_ref.shape, NEG_BIG, F32)
    l_ref[...] = jnp.zeros(l_ref.shape, F32)
    acc_ref[...] = jnp.zeros(acc_ref.shape, F32)

    def attend_tile(first, k, carry, *, with_ties):
        for u in range(k):
            c = first + u
            key = key_ref[c]
            if with_ties:
                keep = (key > thr_b) | ((key == thr_b) & (krow + c * tq <= cut_b))
            else:
                keep = key >= thr_b
            bias_ref[u] = jnp.where(keep, 0.0, NEG_BIG)

        starts = [chunk_start(first + u) for u in range(k)]
        slots = ATTN_LOOKAHEAD + 1
        dyn0 = jnp.minimum(first, 0)
        ones_rows = jnp.ones((ONES_ROWS, tq), vt_ref.dtype)

        def logits(h):
            cols = slice(h // group * HEAD_DIM, (h // group + 1) * HEAD_DIM)
            qh = q_ref[:, h * HEAD_DIM:(h + 1) * HEAD_DIM]
            top = None
            for u in range(k):
                s = lax.dot_general(k_ref[pl.ds(starts[u], tq), cols], qh, nt,
                                    preferred_element_type=F32) + bias_ref[u]
                s_ref[h % slots + dyn0, u] = s
                top = s if top is None else jnp.maximum(top, s)
            return jnp.max(top, axis=0, keepdims=True)

        queued = [logits(h) for h in range(ATTN_LOOKAHEAD)]
        for h in range(N_HEADS):
            tile_max = queued.pop(0)
            if h + ATTN_LOOKAHEAD < N_HEADS:
                queued.append(logits(h + ATTN_LOOKAHEAD))
            cols = slice(h // group * HEAD_DIM, (h // group + 1) * HEAD_DIM)
            m_prev = m_ref[h]
            m_cur = jnp.maximum(m_prev, tile_max)
            alpha = jnp.exp2(m_prev - m_cur)
            pv = jnp.zeros((HEAD_DIM + ONES_ROWS, tq), F32)
            for u in range(k):
                p = jnp.exp2(s_ref[h % slots + dyn0, u] - m_cur)
                v_aug = jnp.concatenate([vt_ref[cols, pl.ds(starts[u], tq)], ones_rows], axis=0)
                pv = pv + jnp.dot(v_aug, p.astype(vt_ref.dtype), preferred_element_type=F32)
            l_ref[h] = alpha * l_ref[h] + pv[HEAD_DIM:HEAD_DIM + 1, :]
            pv = pv[:HEAD_DIM, :]
            acc_ref[h] = alpha * acc_ref[h] + pv
            m_ref[h] = m_cur
        return carry

    @pl.when(jnp.logical_not(has_ties))
    def _():
        over_chunks(functools.partial(attend_tile, with_ties=False), ka, 0, pad_from=ATTN_PAD_FROM)

    @pl.when(has_ties)
    def _():
        over_chunks(functools.partial(attend_tile, with_ties=True), 1, 0)

    for h in range(N_HEADS):
        o_ref[:, h * HEAD_DIM:(h + 1) * HEAD_DIM] = jnp.transpose(acc_ref[h] / l_ref[h]).astype(o_ref.dtype)


def dsa_core(proj, iw, vt, *, tq, topk, kt, ka):
    t = proj.shape[0]
    assert (t // tq) % kt == 0 and kt % ka == 0
    k_block = (Q_COLS + IQ_COLS) // KV_COLS
    ik_block = (Q_COLS + IQ_COLS + 2 * KV_COLS) // IDX_DIM
    resident = dict(pipeline_mode=pl.Buffered(1))
    need = (4 * _nbytes((tq, Q_COLS), proj.dtype) + 2 * _nbytes((tq, LANES), F32)
            + 2 * _nbytes((t, KV_COLS), proj.dtype) + _nbytes((t, IDX_DIM), proj.dtype)
            + 2 * _nbytes((tq, Q_COLS), proj.dtype)
            + _nbytes((t // tq, tq, tq), jnp.int32) + _nbytes((kt, tq, tq), F32)
            + _nbytes((LANES, tq), F32) + (3 + 2 * N_HEADS) * _nbytes((SUBLANES, tq), F32)
            + _nbytes((N_HEADS, HEAD_DIM, tq), F32)
            + (ATTN_LOOKAHEAD + 3) * _nbytes((ka, tq, tq), F32))
    return pl.pallas_call(
        functools.partial(_dsa_core_kernel, tq=tq, topk=topk, kt=kt, ka=ka),
        grid=(t // tq,),
        in_specs=[
            pl.BlockSpec((tq, Q_COLS), lambda i: (i, 0)),
            pl.BlockSpec((tq, IQ_COLS), lambda i: (i, 1)),
            pl.BlockSpec((tq, LANES), lambda i: (i, 0)),
            pl.BlockSpec((t, KV_COLS), lambda i: (0, k_block), **resident),
            pl.BlockSpec((KV_COLS, t), lambda i: (0, 0), **resident),
            pl.BlockSpec((t, IDX_DIM), lambda i: (0, ik_block), **resident),
        ],
        out_specs=pl.BlockSpec((tq, Q_COLS), lambda i: (i, 0)),
        out_shape=jax.ShapeDtypeStruct((t, Q_COLS), proj.dtype),
        scratch_shapes=[
            pltpu.VMEM((t // tq, tq, tq), jnp.int32),
            pltpu.VMEM((ka, tq, tq), F32),
            pltpu.VMEM((ATTN_LOOKAHEAD + 1, ka, tq, tq), F32),
            pltpu.VMEM((LANES, tq), F32),
            pltpu.VMEM((10, 1, tq), jnp.int32),
            pltpu.VMEM((N_HEADS, 1, tq), F32),
            pltpu.VMEM((N_HEADS, 1, tq), F32),
            pltpu.VMEM((N_HEADS, HEAD_DIM, tq), F32),
        ],
        compiler_params=pltpu.CompilerParams(
            dimension_semantics=("arbitrary",),
            vmem_limit_bytes=_vmem_limit(need)),
        name="dsa_core",
    )(proj, proj, iw, proj, vt, proj)


def _rope_tables(pos):
    def angles(dim):
        inv = 1.0 / (ROPE_THETA ** (jnp.arange(0, dim, 2, dtype=F32) / dim))
        return pos.astype(F32)[:, None] * inv

    ang = angles(HEAD_DIM)
    cos, sin = jnp.cos(ang), jnp.sin(ang)
    rc = jnp.concatenate([cos, cos], axis=1)
    rs = jnp.concatenate([-sin, sin], axis=1)
    iang = angles(IDX_ROPE_DIM)
    icos, isin = jnp.cos(iang), jnp.sin(iang)
    rest = LANES - IDX_ROPE_DIM
    zeros = jnp.zeros_like(isin)
    ic = jnp.concatenate([icos, icos, jnp.ones((pos.shape[0], rest), F32)], axis=1)
    ia = jnp.concatenate([-isin, zeros, jnp.zeros((pos.shape[0], rest), F32)], axis=1)
    ib = jnp.concatenate([zeros, isin, jnp.zeros((pos.shape[0], rest), F32)], axis=1)
    return rc, rs, ic, ia, ib


def dsa_layer(x, pos, norm_g, w_in_layers, layer, q_norm, k_norm, w_out, tiles):
    t = x.shape[0]
    proj, iw, vt = dsa_projection(x, norm_g, jnp.swapaxes(w_in_layers, 1, 2), layer, q_norm, k_norm,
                                  _rope_tables(pos), tm=tiles["proj_tm"])
    attn = dsa_core(proj, iw, vt, tq=tiles["tq"], topk=min(TOPK_MAX, t // 4), kt=tiles["kt"], ka=tiles["ka"])
    return matmul_residual(attn, w_out.astype(MXU_DTYPE), x, tm=tiles["res_tm"], tn=tiles["res_tn"])


def _rglru_kernel(gate_ref, xr_ref, cw_ref, cb_ref, wa_ref, ba_ref, wx_ref, bx_ref, lam_ref,
                  y_ref, hcar_ref, xprev_ref, *, tt):
    @pl.when(pl.program_id(1) == 0)
    def _():
        hcar_ref[...] = jnp.zeros_like(hcar_ref)
        xprev_ref[...] = jnp.zeros_like(xprev_ref)

    xr = xr_ref[...]
    ext = jnp.concatenate([xprev_ref[...], xr], axis=0)
    cw = cw_ref[...]
    xc = cb_ref[...] + xr * cw[CONV_WIDTH - 1:CONV_WIDTH, :]
    for d in range(1, CONV_WIDTH):
        xc = xc + pltpu.roll(ext, d, 0)[SUBLANES:, :] * cw[CONV_WIDTH - 1 - d:CONV_WIDTH - d, :]
    xprev_ref[...] = xr[tt - SUBLANES:, :]

    xcb = xc.astype(wa_ref.dtype)
    r = jax.nn.sigmoid(jnp.dot(xcb, wa_ref[...], preferred_element_type=F32) + ba_ref[...])
    ig = jax.nn.sigmoid(jnp.dot(xcb, wx_ref[...], preferred_element_type=F32) + bx_ref[...])
    nlam = -lam_ref[...]
    softplus = jnp.maximum(nlam, 0.0) + jnp.log(1.0 + jnp.exp(-jnp.abs(nlam)))
    log_a = -LRU_C * r * softplus
    a = jnp.exp(log_a)
    mult = jnp.sqrt(1.0 - a * a)
    b = xc * ig * mult

    n_groups = tt // SUBLANES
    a = a.reshape(n_groups, SUBLANES, a.shape[1])
    b = b.reshape(a.shape)
    in_group = lax.broadcasted_iota(jnp.int32, a.shape, 1)
    d = 1
    while d < SUBLANES:
        b = a * jnp.where(in_group >= d, pltpu.roll(b, d, 1), 0.0) + b
        a = a * jnp.where(in_group >= d, pltpu.roll(a, d, 1), 1.0)
        d *= 2
    carry = hcar_ref[...]
    groups = []
    for gi in range(n_groups):
        hg = b[gi] + a[gi] * carry
        carry = hg[SUBLANES - 1:SUBLANES, :]
        groups.append(hg)
    hcar_ref[...] = carry
    h = jnp.concatenate(groups, axis=0)
    y_ref[...] = (h * jax.nn.gelu(gate_ref[...])).astype(y_ref.dtype)


def rglru_scan(proj, conv_w, conv_b, wa, ba, wx, bx, lam, *, tt):
    t = proj.shape[0]
    width = conv_w.shape[1]
    nb, blk, _ = wa.shape
    vec = lambda: pl.BlockSpec((1, blk), lambda n, s: (0, n))
    need = (4 * _nbytes((tt, blk), F32) + 2 * _nbytes((tt, blk), MXU_DTYPE)
            + 4 * _nbytes((blk, blk), wa.dtype) + 24 * _nbytes((tt, blk), F32))
    return pl.pallas_call(
        functools.partial(_rglru_kernel, tt=tt),
        grid=(nb, t // tt),
        in_specs=[
            pl.BlockSpec((tt, blk), lambda n, s: (s, n)),
            pl.BlockSpec((tt, blk), lambda n, s: (s, nb + n)),
            pl.BlockSpec((CONV_WIDTH, blk), lambda n, s: (0, n)),
            vec(),
            pl.BlockSpec((None, blk, blk), lambda n, s: (n, 0, 0)),
            vec(),
            pl.BlockSpec((None, blk, blk), lambda n, s: (n, 0, 0)),
            vec(),
            vec(),
        ],
        out_specs=pl.BlockSpec((tt, blk), lambda n, s: (s, n)),
        out_shape=jax.ShapeDtypeStruct((t, width), MXU_DTYPE),
        scratch_shapes=[pltpu.VMEM((1, blk), F32), pltpu.VMEM((SUBLANES, blk), F32)],
        compiler_params=pltpu.CompilerParams(
            dimension_semantics=("parallel", "arbitrary"),
            vmem_limit_bytes=_vmem_limit(need)),
        name="rglru_scan",
    )(proj, proj, conv_w, conv_b.reshape(1, width), wa, ba.reshape(1, width),
      wx, bx.reshape(1, width), lam.reshape(1, width))


def rglru_layer(x, norm_g, w_in, conv_w, conv_b, wa, ba, wx, bx, lam, w_out, tiles):
    proj = norm_matmul(x, norm_g, w_in.astype(MXU_DTYPE), tm=tiles["proj_tm"], tn=tiles["proj_tn"],
                       out_dtype=F32)
    y = rglru_scan(proj, conv_w, conv_b, wa.astype(MXU_DTYPE), ba, wx.astype(MXU_DTYPE), bx, lam,
                   tt=tiles["scan_tt"])
    return matmul_residual(y, w_out.astype(MXU_DTYPE), x, tm=tiles["res_tm"], tn=tiles["res_tn"])


POOL_HALO = max(POOL_WINDOWS)


def _pool_kernel(x_ref, g_ref, w_ref, b_ref, s_ref, o_ref, halo_ref, *, tt):
    blk = pl.program_id(0)

    @pl.when(blk == 0)
    def _():
        halo_ref[...] = jnp.zeros_like(halo_ref)

    x = x_ref[...]
    h = _rms_norm_rows(x, g_ref[...])
    ext = jnp.concatenate([halo_ref[...], h], axis=0)
    halo_ref[...] = h[tt - POOL_HALO:, :]

    gw = h.shape[1] // len(POOL_WINDOWS)
    t1 = (lax.broadcasted_iota(jnp.int32, (tt, gw), 0) + (blk * tt + 1)).astype(F32)
    for gi, win in enumerate(POOL_WINDOWS):
        cols = slice(gi * gw, (gi + 1) * gw)
        acc = ext[:, cols]
        d = 1
        while d < win:
            acc = acc + pltpu.roll(acc, d, 0)
            d *= 2
        mean = acc[POOL_HALO:, :] / jnp.minimum(t1, float(win))
        y = (mean - h[:, cols]).astype(w_ref.dtype)
        z = jnp.dot(y, w_ref[gi], preferred_element_type=F32) + b_ref[gi:gi + 1, :]
        o_ref[:, cols] = x[:, cols] + z * s_ref[:, cols]


def pool_layer(x, norm_g, w_group, b_group, scale, tiles):
    t, d = x.shape
    tt = tiles["pool_tt"]
    ng, gw, _ = w_group.shape
    need = (4 * _nbytes((tt, d), F32) + 2 * _nbytes((ng, gw, gw), MXU_DTYPE)
            + 6 * _nbytes((tt, d), F32))
    return pl.pallas_call(
        functools.partial(_pool_kernel, tt=tt),
        grid=(t // tt,),
        in_specs=[
            pl.BlockSpec((tt, d), lambda i: (i, 0)),
            pl.BlockSpec((1, d), lambda i: (0, 0)),
            pl.BlockSpec((ng, gw, gw), lambda i: (0, 0, 0)),
            pl.BlockSpec((ng, gw), lambda i: (0, 0)),
            pl.BlockSpec((1, d), lambda i: (0, 0)),
        ],
        out_specs=pl.BlockSpec((tt, d), lambda i: (i, 0)),
        out_shape=jax.ShapeDtypeStruct((t, d), F32),
        scratch_shapes=[pltpu.VMEM((POOL_HALO, d), F32)],
        compiler_params=pltpu.CompilerParams(
            dimension_semantics=("arbitrary",),
            vmem_limit_bytes=_vmem_limit(need)),
        name="pool_mixer",
    )(x, norm_g.reshape(1, d), w_group.astype(MXU_DTYPE), b_group, scale.reshape(1, d))


def _tiles(t):
    big = min(t, 512)
    tall = min(t, 1024)
    return {
        "proj_tm": tall, "proj_tn": 1024,
        "res_tm": tall, "res_tn": 1024,
        "tq": min(t, 256), "kt": min(4, t // min(t, 256)), "ka": min(4, t // min(t, 256)),
        "scan_tt": min(t, 1024),
        "pool_tt": min(t, 256),
        "mlp_tm": big, "mlp_tf": 1024,
    }


def kernel(x, positions, attn_norm, attn_w_in, attn_q_norm, attn_k_norm, attn_w_out, rnn_norm, rnn_w_in, rnn_conv_w, rnn_conv_b, rnn_gate_a_w, rnn_gate_a_b, rnn_gate_x_w, rnn_gate_x_b, rnn_lambda, rnn_w_out, pool_norm, pool_w, pool_b, pool_scale, mlp_norm, mlp_w_up, mlp_w_down):
    batch, t, d = x.shape
    depth = mlp_norm.shape[0]
    tiles = _tiles(t)
    w_up_all = mlp_w_up.astype(MXU_DTYPE)
    w_down_all = mlp_w_down.astype(MXU_DTYPE)
    outs = []
    for bi in range(batch):
        xb = x[bi]
        pos = positions[bi]
        for i in range(depth):
            kind, j = i % N_MIXERS, i // N_MIXERS
            if kind == 0:
                xb = dsa_layer(xb, pos, attn_norm[j], attn_w_in, j, attn_q_norm[j], attn_k_norm[j],
                               attn_w_out[j], tiles)
            elif kind == 1:
                xb = rglru_layer(xb, rnn_norm[j], rnn_w_in[j], rnn_conv_w[j], rnn_conv_b[j],
                                 rnn_gate_a_w[j], rnn_gate_a_b[j], rnn_gate_x_w[j], rnn_gate_x_b[j],
                                 rnn_lambda[j], rnn_w_out[j], tiles)
            else:
                xb = pool_layer(xb, pool_norm[j], pool_w[j], pool_b[j], pool_scale[j], tiles)
            xb = mlp_block(xb, mlp_norm[i], w_up_all, w_down_all, i,
                           tm=tiles["mlp_tm"], tf=tiles["mlp_tf"])
        outs.append(xb)
    return outs[0][None] if batch == 1 else jnp.stack(outs, axis=0)
```

```python
import functools
import math

import jax
import jax.numpy as jnp
from jax import lax
from jax.experimental import pallas as pl
from jax.experimental.pallas import tpu as pltpu

F32 = jnp.float32
MXU_DTYPE = jnp.bfloat16

N_MIXERS = 3
EPS = 1e-6
ROPE_THETA = 10000.0
HEAD_DIM = 128
N_HEADS = 16
N_KV_HEADS = 4
IDX_HEADS = 16
IDX_DIM = 128
IDX_ROPE_DIM = 64
TOPK_MAX = 256
CONV_WIDTH = 4
LRU_C = 8.0
POOL_WINDOWS = (2, 4, 8, 16)

LANES = 128
SUBLANES = 8
VMEM_BYTES_V7X = 64 * 1024 * 1024
VMEM_CAP_BYTES = VMEM_BYTES_V7X - 8 * 1024 * 1024

INT_MIN = -(2 ** 31)
NEG_BIG = -1e30


def _vmem_limit(block_bytes):
    return int(min(VMEM_CAP_BYTES, block_bytes * 3 // 2 + (4 << 20)))


def _nbytes(shape, dtype):
    return math.prod(shape) * jnp.dtype(dtype).itemsize


def _rms_norm_rows(x, g):
    ms = jnp.mean(x * x, axis=-1, keepdims=True)
    return x * lax.rsqrt(ms + EPS) * g


def _matmul_residual_kernel(a_ref, w_ref, x_ref, o_ref):
    o_ref[...] = x_ref[...] + jnp.dot(a_ref[...], w_ref[...], preferred_element_type=F32)


def matmul_residual(a, w, x, *, tm, tn):
    t, k = a.shape
    n = w.shape[1]
    need = (2 * _nbytes((tm, k), a.dtype) + 2 * _nbytes((k, tn), w.dtype)
            + 4 * _nbytes((tm, tn), F32))
    return pl.pallas_call(
        _matmul_residual_kernel,
        grid=(t // tm, n // tn),
        in_specs=[
            pl.BlockSpec((tm, k), lambda i, j: (i, 0)),
            pl.BlockSpec((k, tn), lambda i, j: (0, j)),
            pl.BlockSpec((tm, tn), lambda i, j: (i, j)),
        ],
        out_specs=pl.BlockSpec((tm, tn), lambda i, j: (i, j)),
        out_shape=jax.ShapeDtypeStruct((t, n), F32),
        compiler_params=pltpu.CompilerParams(
            dimension_semantics=("parallel", "arbitrary"),
            vmem_limit_bytes=_vmem_limit(need)),
        name="matmul_residual",
    )(a, w, x)


def _mlp_kernel(x_ref, g_ref, wu_ref, wd_ref, o_ref, h_ref, acc_ref):
    f = pl.program_id(1)

    @pl.when(f == 0)
    def _():
        h_ref[...] = _rms_norm_rows(x_ref[...], g_ref[...]).astype(h_ref.dtype)
        acc_ref[...] = jnp.zeros_like(acc_ref)

    u = jnp.dot(h_ref[...], wu_ref[...], preferred_element_type=F32)
    u = jnp.square(jnp.maximum(u, 0.0)).astype(wd_ref.dtype)
    acc_ref[...] += jnp.dot(u, wd_ref[...], preferred_element_type=F32)

    @pl.when(f == pl.num_programs(1) - 1)
    def _():
        o_ref[...] = x_ref[...] + acc_ref[...]


def mlp_block(x, g, w_up, w_down, layer, *, tm, tf):
    t, d = x.shape
    ff = w_up.shape[2]
    need = (4 * _nbytes((tm, d), F32) + 2 * _nbytes((d, tf), w_up.dtype)
            + 2 * _nbytes((tf, d), w_down.dtype) + _nbytes((tm, d), w_up.dtype)
            + _nbytes((tm, d), F32) + 2 * _nbytes((tm, tf), F32))
    return pl.pallas_call(
        _mlp_kernel,
        grid=(t // tm, ff // tf),
        in_specs=[
            pl.BlockSpec((tm, d), lambda i, f: (i, 0)),
            pl.BlockSpec((1, d), lambda i, f: (0, 0)),
            pl.BlockSpec((None, d, tf), lambda i, f: (layer, 0, f)),
            pl.BlockSpec((None, tf, d), lambda i, f: (layer, f, 0)),
        ],
        out_specs=pl.BlockSpec((tm, d), lambda i, f: (i, 0)),
        out_shape=jax.ShapeDtypeStruct((t, d), F32),
        scratch_shapes=[pltpu.VMEM((tm, d), w_up.dtype), pltpu.VMEM((tm, d), F32)],
        compiler_params=pltpu.CompilerParams(
            dimension_semantics=("parallel", "arbitrary"),
            vmem_limit_bytes=_vmem_limit(need)),
        name="mlp_block",
    )(x, g.reshape(1, d), w_up, w_down)


DSA_TN = 4 * LANES
Q_COLS = N_HEADS * HEAD_DIM
IQ_COLS = IDX_HEADS * IDX_DIM
KV_COLS = N_KV_HEADS * HEAD_DIM
DSA_COLS = Q_COLS + IQ_COLS + 2 * KV_COLS + DSA_TN
Q_BLOCKS = Q_COLS // DSA_TN
IQ_BLOCKS = IQ_COLS // DSA_TN
KV_BLOCKS = KV_COLS // DSA_TN


W_K_FIRST = Q_BLOCKS
W_V_FIRST = W_K_FIRST + KV_BLOCKS
W_IQ_FIRST = W_V_FIRST + KV_BLOCKS
W_LAST = W_IQ_FIRST + IQ_BLOCKS
W_TAIL_ROWS = IDX_DIM + IDX_HEADS
OUT_K_FIRST = Q_BLOCKS + IQ_BLOCKS


def _dsa_out_block(jw):
    return jnp.where(jw < W_K_FIRST, jw,
                     jnp.where(jw < W_IQ_FIRST, jw + (OUT_K_FIRST - W_K_FIRST),
                               jnp.where(jw < W_LAST, jw - (W_IQ_FIRST - Q_BLOCKS), W_LAST)))


def _dsa_proj_kernel(x_ref, g_ref, w_ref, qn_ref, kn_ref, rc_ref, rs_ref,
                     ic_ref, ia_ref, ib_ref, o_ref, iw_ref, vt_ref, h_ref, y_ref):
    j = pl.program_id(1)
    jp = j - 1

    @pl.when(j == 0)
    def _():
        h_ref[...] = _rms_norm_rows(x_ref[...], g_ref[...]).astype(h_ref.dtype)

    def rope(z):
        return z * rc_ref[...] + pltpu.roll(z, HEAD_DIM // 2, 1) * rs_ref[...]

    def idx_rope(z):
        half = IDX_ROPE_DIM // 2
        return (z * ic_ref[...] + pltpu.roll(z, LANES - half, 1) * ia_ref[...]
                + pltpu.roll(z, half, 1) * ib_ref[...])

    def per_head(fn):
        def finish(y):
            o_ref[...] = jnp.concatenate(
                [fn(y[:, s * LANES:(s + 1) * LANES]) for s in range(DSA_TN // LANES)],
                axis=1).astype(o_ref.dtype)
        return finish

    def step(finish, multiply=True, last_block_possible=False):
        if finish is not None:
            y_prev = y_ref[...]
        if multiply:
            w = w_ref[...]
            if last_block_possible:
                row = lax.broadcasted_iota(jnp.int32, w.shape, 0)
                w = jnp.where((j == W_LAST) & (row >= W_TAIL_ROWS), 0.0, w)
            y_ref[...] = lax.dot_general(
                h_ref[...], w.astype(h_ref.dtype), (((1,), (1,)), ((), ())),
                preferred_element_type=F32)
        if finish is not None:
            finish(y_prev)

    @pl.when(j == 0)
    def _():
        step(None)

    @pl.when((jp >= 0) & (jp < W_K_FIRST))
    def _():
        scale = HEAD_DIM ** -0.5 * math.log2(math.e)
        step(per_head(lambda z: rope(_rms_norm_rows(z, qn_ref[...])) * scale))

    @pl.when((jp >= W_K_FIRST) & (jp < W_V_FIRST))
    def _():
        step(per_head(lambda z: rope(_rms_norm_rows(z, kn_ref[...]))))

    @pl.when((jp >= W_V_FIRST) & (jp < W_IQ_FIRST))
    def _():
        def finish(y):
            o_ref[...] = y.astype(o_ref.dtype)
            vt_ref[...] = jnp.transpose(y).astype(vt_ref.dtype)
        step(finish)

    @pl.when((jp >= W_IQ_FIRST) & (jp < W_LAST))
    def _():
        step(per_head(idx_rope), last_block_possible=True)

    @pl.when(jp == W_LAST)
    def _():
        def finish(y):
            o_ref[...] = jnp.concatenate(
                [idx_rope(y[:, :LANES]), jnp.zeros_like(y[:, LANES:])], axis=1).astype(o_ref.dtype)
            iw_ref[...] = y[:, LANES:2 * LANES] * (IDX_HEADS ** -0.5 * IDX_DIM ** -0.5)
        step(finish, multiply=False)


def dsa_projection(x, g, w_layers, layer, q_norm, k_norm, tables, *, tm):
    t, d = x.shape
    assert w_layers.shape[1] == Q_COLS + 2 * KV_COLS + IQ_COLS + IDX_DIM + IDX_HEADS
    rc, rs, ic, ia, ib = tables
    tab_spec = pl.BlockSpec((tm, LANES), lambda i, j: (i, 0))
    vec_spec = pl.BlockSpec((1, LANES), lambda i, j: (0, 0))
    need = (2 * _nbytes((tm, d), F32) + 3 * _nbytes((d, DSA_TN), F32)
            + 2 * _nbytes((tm, DSA_TN), MXU_DTYPE) + _nbytes((tm, d), MXU_DTYPE)
            + 12 * _nbytes((tm, LANES), F32) + 3 * _nbytes((tm, DSA_TN), F32))
    return pl.pallas_call(
        _dsa_proj_kernel,
        grid=(t // tm, W_LAST + 2),
        in_specs=[
            pl.BlockSpec((tm, d), lambda i, j: (i, 0)),
            pl.BlockSpec((1, d), lambda i, j: (0, 0)),
            pl.BlockSpec((None, DSA_TN, d), lambda i, j: (layer, jnp.minimum(j, W_LAST), 0)),
            vec_spec, vec_spec, tab_spec, tab_spec, tab_spec, tab_spec, tab_spec,
        ],
        out_specs=[
            pl.BlockSpec((tm, DSA_TN), lambda i, j: (i, _dsa_out_block(jnp.maximum(j - 1, 0)))),
            pl.BlockSpec((tm, LANES), lambda i, j: (i, 0)),
            pl.BlockSpec((DSA_TN, tm), lambda i, j: (jnp.clip(j - 1 - W_V_FIRST, 0, KV_BLOCKS - 1), i)),
        ],
        out_shape=[
            jax.ShapeDtypeStruct((t, DSA_COLS), MXU_DTYPE),
            jax.ShapeDtypeStruct((t, LANES), F32),
            jax.ShapeDtypeStruct((KV_COLS, t), MXU_DTYPE),
        ],
        scratch_shapes=[pltpu.VMEM((tm, d), MXU_DTYPE), pltpu.VMEM((tm, DSA_TN), F32)],
        compiler_params=pltpu.CompilerParams(
            dimension_semantics=("parallel", "arbitrary"),
            vmem_limit_bytes=_vmem_limit(need)),
        name="dsa_projection",
    )(x, g.reshape(1, d), w_layers, q_norm.reshape(1, LANES), k_norm.reshape(1, LANES), rc, rs, ic, ia, ib)


def _float_key(v):
    b = lax.bitcast_convert_type(v, jnp.int32)
    return b ^ ((b >> 31) & 0x7FFFFFFF)


def _key_float(k):
    return lax.bitcast_convert_type(k ^ ((k >> 31) & 0x7FFFFFFF), F32)


_KLO, _KHI, _CLO, _CHI, _JCUT, _SEED, _SEED_UP, _SEED_DOWN, _SEED_UP3, _SEED_DOWN3 = range(10)
SEED_STEP = 0.15
_SEARCH_CAP = 96
ONES_ROWS = 2 * SUBLANES
ATTN_PAD_FROM = 2
ATTN_LOOKAHEAD = 1


def _dsa_core_kernel(q_ref, iq_ref, iw_ref, k_ref, vt_ref, ik_ref, o_ref,
                     key_ref, bias_ref, s_ref, wt_ref, st_ref, m_ref, l_ref, acc_ref, *, tq, topk, kt, ka):
    i = pl.program_id(0)
    nc = i + 1
    group = N_HEADS // N_KV_HEADS
    nt = (((1,), (1,)), ((), ()))

    wt_ref[...] = jnp.transpose(iw_ref[...])

    krow = lax.broadcasted_iota(jnp.int32, (tq, tq), 0)
    qcol = lax.broadcasted_iota(jnp.int32, (tq, tq), 1)
    diag_causal = krow <= qcol

    def chunk_start(c):
        return pl.multiple_of(c * tq, tq)

    def over_chunks(body, k, init, pad_from=None):
        n_tiles = (nc + k - min(pad_from or k, k)) // k
        carry = lax.fori_loop(0, n_tiles, lambda ti, c: body(ti * k, k, c), init)
        if k > 1:
            carry = lax.fori_loop(n_tiles * k, nc, lambda c0, c: body(c0, 1, c), carry)
        return carry

    def index_tile(first, k, carry):
        smin, smax, ssum, ssq, sampled = carry
        iks = [ik_ref[pl.ds(chunk_start(first + u), tq), :] for u in range(k)]
        scores = [jnp.zeros((tq, tq), F32) for _ in range(k)]
        for h in range(IDX_HEADS):
            iqh = iq_ref[:, h * IDX_DIM:(h + 1) * IDX_DIM]
            w = jnp.broadcast_to(wt_ref[h:h + 1, :], (tq, tq))
            for u in range(k):
                logits = lax.dot_general(iks[u], iqh, nt, preferred_element_type=F32)
                scores[u] = scores[u] + w * jnp.maximum(logits, 0.0)
        for u in range(k):
            c = first + u
            valid = (c < i) | diag_causal
            key_ref[c] = jnp.where(valid, _float_key(scores[u]), INT_MIN)
            smin = jnp.minimum(smin, jnp.min(jnp.where(valid, scores[u], jnp.inf), axis=0, keepdims=True))
            smax = jnp.maximum(smax, jnp.max(jnp.where(valid, scores[u], -jnp.inf), axis=0, keepdims=True))
            if u == 0:
                sample = jnp.where(c < i, scores[u], 0.0)
                ssum = ssum + jnp.sum(sample, axis=0, keepdims=True)
                ssq = ssq + jnp.sum(sample * sample, axis=0, keepdims=True)
                sampled = sampled + jnp.where(c < i, float(tq), 0.0)
        return smin, smax, ssum, ssq, sampled

    zero_row = jnp.zeros((1, tq), F32)
    smin, smax, ssum, ssq, sampled = over_chunks(
        index_tile, kt, (jnp.full((1, tq), jnp.inf, F32), jnp.full((1, tq), -jnp.inf, F32),
                         zero_row, zero_row, jnp.float32(0.0)))

    def mask_chunk(c, carry):
        key_ref[c] = jnp.full((tq, tq), INT_MIN, jnp.int32)
        return carry

    lax.fori_loop(nc, (nc + ka - min(ATTN_PAD_FROM, ka)) // ka * ka, mask_chunk, 0)

    def count_ge(cand):
        cand_b = jnp.broadcast_to(cand, (tq, tq))

        def body(first, k, cnt):
            for u in range(k):
                hit = jnp.where(key_ref[first + u] >= cand_b, 1.0, 0.0)
                cnt = cnt + jnp.sum(hit.reshape(tq // SUBLANES, SUBLANES, tq), axis=0)
            return cnt

        cnt = over_chunks(body, kt, jnp.zeros((SUBLANES, tq), F32))
        return jnp.sum(cnt, axis=0, keepdims=True)

    st_ref[_KLO] = _float_key(smin)
    st_ref[_KHI] = _float_key(smax) + 1
    n_causal = lax.broadcasted_iota(jnp.int32, (1, tq), 1) + (i * tq + 1)
    st_ref[_CLO] = n_causal
    st_ref[_CHI] = jnp.zeros((1, tq), jnp.int32)

    mean = ssum / sampled
    dev = jnp.sqrt(jnp.maximum(ssq / sampled - mean * mean, 0.0))
    upper = jnp.minimum(topk / n_causal.astype(F32), 1.0)
    tail = jnp.clip(jnp.minimum(upper, 1.0 - upper), 1e-6, 0.5)
    r = jnp.sqrt(-2.0 * jnp.log(tail))
    z = r - (2.515517 + 0.802853 * r + 0.010328 * r * r) / (
        1.0 + 1.432788 * r + 0.189269 * r * r + 0.001308 * r * r * r)
    guess = mean + jnp.where(upper <= 0.5, z, -z) * dev
    st_ref[_SEED] = _float_key(guess)
    st_ref[_SEED_UP] = _float_key(guess + SEED_STEP * dev)
    st_ref[_SEED_DOWN] = _float_key(guess - SEED_STEP * dev)
    st_ref[_SEED_UP3] = _float_key(guess + 3 * SEED_STEP * dev)
    st_ref[_SEED_DOWN3] = _float_key(guess - 3 * SEED_STEP * dev)

    def unfinished(klo, khi, clo):
        return (clo > topk) & ((khi - klo) != 1) & (st_ref[_CHI] != topk - 1)

    def search_cond(st):
        step, pending = st
        return (step < _SEARCH_CAP) & (pending > 0)

    def any_lane(mask):
        return jnp.max(jnp.where(mask, 1.0, 0.0))

    def search_body(st):
        step, _ = st
        klo, khi, clo = st_ref[_KLO], st_ref[_KHI], st_ref[_CLO]
        live = unfinished(klo, khi, clo)
        pending = any_lane(live)
        vmid = 0.5 * _key_float(klo) + 0.5 * _key_float(khi)
        cmid = _float_key(vmid)
        kmid = klo + lax.shift_right_logical(khi - klo, 1)
        cand = jnp.where((cmid > klo) & (cmid < khi), cmid, kmid)
        seed, up, down = st_ref[_SEED], st_ref[_SEED_UP], st_ref[_SEED_DOWN]
        second = jnp.where(klo == seed, up, down)
        third = jnp.where(klo == up, st_ref[_SEED_UP3], jnp.where(khi == down, st_ref[_SEED_DOWN3], cand))
        seeded = jnp.where(step == 0, seed, jnp.where(step == 1, second, third))
        cand = jnp.where((step < 3) & (seeded > klo) & (seeded < khi), seeded, cand)
        cand = jnp.where(live, cand, klo)
        cnt = count_ge(cand).astype(jnp.int32)
        take = cnt >= topk
        klo = jnp.where(live & take, cand, klo)
        clo = jnp.where(live & take, cnt, clo)
        drop = live & jnp.logical_not(take)
        st_ref[_KLO], st_ref[_KHI], st_ref[_CLO] = klo, jnp.where(drop, cand, khi), clo
        st_ref[_CHI] = jnp.where(drop, cnt, st_ref[_CHI])
        return step + 1, pending

    lax.while_loop(search_cond, search_body, (jnp.int32(0), jnp.float32(1.0)))

    short = ((st_ref[_CLO] > topk) & (st_ref[_CHI] == topk - 1)
             & ((st_ref[_KHI] - st_ref[_KLO]) != 1))

    @pl.when(any_lane(short) > 0)
    def _():
        khi_b = jnp.broadcast_to(st_ref[_KHI], (tq, tq))

        def body(first, k, best):
            for u in range(k):
                key = key_ref[first + u]
                below = jnp.where(key < khi_b, key, INT_MIN)
                best = jnp.maximum(best, jnp.max(below.reshape(tq // SUBLANES, SUBLANES, tq), axis=0))
            return best

        best = over_chunks(body, kt, jnp.full((SUBLANES, tq), INT_MIN, jnp.int32))
        largest = jnp.max(best, axis=0, keepdims=True)
        st_ref[_KLO] = jnp.where(short, largest, st_ref[_KLO])
        st_ref[_CLO] = jnp.where(short, count_ge(largest).astype(jnp.int32), st_ref[_CLO])

    thr_b = jnp.broadcast_to(st_ref[_KLO], (tq, tq))

    n_keys = key_ref.shape[0] * tq
    tied = st_ref[_CLO] > topk
    has_ties = any_lane(tied) > 0
    st_ref[_JCUT] = jnp.full((1, tq), n_keys, jnp.int32)

    @pl.when(has_ties)
    def _():
        want = topk - st_ref[_CHI]

        def count_tied_upto(pos):
            pos_b = jnp.broadcast_to(pos, (tq, tq))

            def body(c, cnt):
                hit = (key_ref[c] == thr_b) & (krow + c * tq <= pos_b)
                return cnt + jnp.sum(jnp.where(hit, 1.0, 0.0), axis=0, keepdims=True)

            return lax.fori_loop(0, nc, body, jnp.zeros((1, tq), F32)).astype(jnp.int32)

        def cut_step(_, st):
            lo, hi = st
            mid = lo + ((hi - lo) >> 1)
            enough = count_tied_upto(mid) >= want
            return jnp.where(enough, lo, mid), jnp.where(enough, mid, hi)

        _, cut = lax.fori_loop(0, n_keys.bit_length(), cut_step,
                               (jnp.full((1, tq), -1, jnp.int32), jnp.full((1, tq), n_keys - 1, jnp.int32)))
        st_ref[_JCUT] = jnp.where(tied, cut, n_keys)

    cut_b = jnp.broadcast_to(st_ref[_JCUT], (tq, tq))

    m_ref[...] = jnp.full(m_ref.shape, NEG_BIG, F32)
    l_ref[...] = jnp.zeros(l_ref.shape, F32)
    acc_ref[...] = jnp.zeros(acc_ref.shape, F32)

    def attend_tile(first, k, carry, *, with_ties):
        for u in range(k):
            c = first + u
            key = key_ref[c]
            if with_ties:
                keep = (key > thr_b) | ((key == thr_b) & (krow + c * tq <= cut_b))
            else:
                keep = key >= thr_b
            bias_ref[u] = jnp.where(keep, 0.0, NEG_BIG)

        starts = [chunk_start(first + u) for u in range(k)]
        slots = ATTN_LOOKAHEAD + 1
        dyn0 = jnp.minimum(first, 0)
        ones_rows = jnp.ones((ONES_ROWS, tq), vt_ref.dtype)

        def logits(h):
            cols = slice(h // group * HEAD_DIM, (h // group + 1) * HEAD_DIM)
            qh = q_ref[:, h * HEAD_DIM:(h + 1) * HEAD_DIM]
            top = None
            for u in range(k):
                s = lax.dot_general(k_ref[pl.ds(starts[u], tq), cols], qh, nt,
                                    preferred_element_type=F32) + bias_ref[u]
                s_ref[h % slots + dyn0, u] = s
                top = s if top is None else jnp.maximum(top, s)
            return jnp.max(top, axis=0, keepdims=True)

        queued = [logits(h) for h in range(ATTN_LOOKAHEAD)]
        for h in range(N_HEADS):
            tile_max = queued.pop(0)
            if h + ATTN_LOOKAHEAD < N_HEADS:
                queued.append(logits(h + ATTN_LOOKAHEAD))
            cols = slice(h // group * HEAD_DIM, (h // group + 1) * HEAD_DIM)
            m_prev = m_ref[h]
            m_cur = jnp.maximum(m_prev, tile_max)
            alpha = jnp.exp2(m_prev - m_cur)
            pv = jnp.zeros((HEAD_DIM + ONES_ROWS, tq), F32)
            for u in range(k):
                p = jnp.exp2(s_ref[h % slots + dyn0, u] - m_cur)
                v_aug = jnp.concatenate([vt_ref[cols, pl.ds(starts[u], tq)], ones_rows], axis=0)
                pv = pv + jnp.dot(v_aug, p.astype(vt_ref.dtype), preferred_element_type=F32)
            l_ref[h] = alpha * l_ref[h] + pv[HEAD_DIM:HEAD_DIM + 1, :]
            pv = pv[:HEAD_DIM, :]
            acc_ref[h] = alpha * acc_ref[h] + pv
            m_ref[h] = m_cur
        return carry

    @pl.when(jnp.logical_not(has_ties))
    def _():
        over_chunks(functools.partial(attend_tile, with_ties=False), ka, 0, pad_from=ATTN_PAD_FROM)

    @pl.when(has_ties)
    def _():
        over_chunks(functools.partial(attend_tile, with_ties=True), 1, 0)

    for h in range(N_HEADS):
        o_ref[:, h * HEAD_DIM:(h + 1) * HEAD_DIM] = jnp.transpose(acc_ref[h] / l_ref[h]).astype(o_ref.dtype)


def dsa_core(proj, iw, vt, *, tq, topk, kt, ka):
    t = proj.shape[0]
    assert (t // tq) % kt == 0 and kt % ka == 0
    k_block = (Q_COLS + IQ_COLS) // KV_COLS
    ik_block = (Q_COLS + IQ_COLS + 2 * KV_COLS) // IDX_DIM
    resident = dict(pipeline_mode=pl.Buffered(1))
    need = (4 * _nbytes((tq, Q_COLS), proj.dtype) + 2 * _nbytes((tq, LANES), F32)
            + 2 * _nbytes((t, KV_COLS), proj.dtype) + _nbytes((t, IDX_DIM), proj.dtype)
            + 2 * _nbytes((tq, Q_COLS), proj.dtype)
            + _nbytes((t // tq, tq, tq), jnp.int32) + _nbytes((kt, tq, tq), F32)
            + _nbytes((LANES, tq), F32) + (3 + 2 * N_HEADS) * _nbytes((SUBLANES, tq), F32)
            + _nbytes((N_HEADS, HEAD_DIM, tq), F32)
            + (ATTN_LOOKAHEAD + 3) * _nbytes((ka, tq, tq), F32))
    return pl.pallas_call(
        functools.partial(_dsa_core_kernel, tq=tq, topk=topk, kt=kt, ka=ka),
        grid=(t // tq,),
        in_specs=[
            pl.BlockSpec((tq, Q_COLS), lambda i: (i, 0)),
            pl.BlockSpec((tq, IQ_COLS), lambda i: (i, 1)),
            pl.BlockSpec((tq, LANES), lambda i: (i, 0)),
            pl.BlockSpec((t, KV_COLS), lambda i: (0, k_block), **resident),
            pl.BlockSpec((KV_COLS, t), lambda i: (0, 0), **resident),
            pl.BlockSpec((t, IDX_DIM), lambda i: (0, ik_block), **resident),
        ],
        out_specs=pl.BlockSpec((tq, Q_COLS), lambda i: (i, 0)),
        out_shape=jax.ShapeDtypeStruct((t, Q_COLS), proj.dtype),
        scratch_shapes=[
            pltpu.VMEM((t // tq, tq, tq), jnp.int32),
            pltpu.VMEM((ka, tq, tq), F32),
            pltpu.VMEM((ATTN_LOOKAHEAD + 1, ka, tq, tq), F32),
            pltpu.VMEM((LANES, tq), F32),
            pltpu.VMEM((10, 1, tq), jnp.int32),
            pltpu.VMEM((N_HEADS, 1, tq), F32),
            pltpu.VMEM((N_HEADS, 1, tq), F32),
            pltpu.VMEM((N_HEADS, HEAD_DIM, tq), F32),
        ],
        compiler_params=pltpu.CompilerParams(
            dimension_semantics=("arbitrary",),
            vmem_limit_bytes=_vmem_limit(need)),
        name="dsa_core",
    )(proj, proj, iw, proj, vt, proj)


def _rope_tables(pos):
    def angles(dim):
        inv = 1.0 / (ROPE_THETA ** (jnp.arange(0, dim, 2, dtype=F32) / dim))
        return pos.astype(F32)[:, None] * inv

    ang = angles(HEAD_DIM)
    cos, sin = jnp.cos(ang), jnp.sin(ang)
    rc = jnp.concatenate([cos, cos], axis=1)
    rs = jnp.concatenate([-sin, sin], axis=1)
    iang = angles(IDX_ROPE_DIM)
    icos, isin = jnp.cos(iang), jnp.sin(iang)
    rest = LANES - IDX_ROPE_DIM
    zeros = jnp.zeros_like(isin)
    ic = jnp.concatenate([icos, icos, jnp.ones((pos.shape[0], rest), F32)], axis=1)
    ia = jnp.concatenate([-isin, zeros, jnp.zeros((pos.shape[0], rest), F32)], axis=1)
    ib = jnp.concatenate([zeros, isin, jnp.zeros((pos.shape[0], rest), F32)], axis=1)
    return rc, rs, ic, ia, ib


def dsa_layer(x, pos, norm_g, w_in_layers, layer, q_norm, k_norm, w_out, tiles):
    t = x.shape[0]
    proj, iw, vt = dsa_projection(x, norm_g, jnp.swapaxes(w_in_layers, 1, 2), layer, q_norm, k_norm,
                                  _rope_tables(pos), tm=tiles["proj_tm"])
    attn = dsa_core(proj, iw, vt, tq=tiles["tq"], topk=min(TOPK_MAX, t // 4), kt=tiles["kt"], ka=tiles["ka"])
    return matmul_residual(attn, w_out.astype(MXU_DTYPE), x, tm=tiles["res_tm"], tn=tiles["res_tn"])


def _rglru_kernel(x_ref, g_ref, wgate_ref, wxr_ref, cw_ref, cb_ref, wa_ref, ba_ref, wx_ref, bx_ref, lam_ref,
                  y_ref, hn_ref, proj_ref, hcar_ref, xprev_ref, *, tt, nb, n_steps):
    k = pl.program_id(0)
    n = (k + nb - 1) % nb

    @pl.when((k < n_steps) & (k % nb == 0))
    def _():
        hn_ref[...] = _rms_norm_rows(x_ref[...], g_ref[...]).astype(hn_ref.dtype)

    def project():
        proj_ref[k % 2, 0] = jnp.dot(hn_ref[...], wgate_ref[...], preferred_element_type=F32)
        proj_ref[k % 2, 1] = jnp.dot(hn_ref[...], wxr_ref[...], preferred_element_type=F32)

    @pl.when(k == 0)
    def _():
        hcar_ref[...] = jnp.zeros_like(hcar_ref)
        xprev_ref[...] = jnp.zeros_like(xprev_ref)
        project()

    @pl.when(k >= 1)
    def _():
        gate, xr = proj_ref[(k + 1) % 2, 0], proj_ref[(k + 1) % 2, 1]
        _rglru_recurrence(gate, xr, n, cw_ref, cb_ref, wa_ref, ba_ref, wx_ref, bx_ref, lam_ref,
                          y_ref, hcar_ref, xprev_ref, tt, project)


def _rglru_recurrence(gate, xr, n, cw_ref, cb_ref, wa_ref, ba_ref, wx_ref, bx_ref, lam_ref,
                      y_ref, hcar_ref, xprev_ref, tt, after_gate_matmuls):
    ext = jnp.concatenate([xprev_ref[n], xr], axis=0)
    cw = cw_ref[...]
    xc = cb_ref[...] + xr * cw[CONV_WIDTH - 1:CONV_WIDTH, :]
    for d in range(1, CONV_WIDTH):
        xc = xc + pltpu.roll(ext, d, 0)[SUBLANES:, :] * cw[CONV_WIDTH - 1 - d:CONV_WIDTH - d, :]
    xprev_ref[n] = xr[tt - SUBLANES:, :]

    xcb = xc.astype(wa_ref.dtype)
    r = jax.nn.sigmoid(jnp.dot(xcb, wa_ref[...], preferred_element_type=F32) + ba_ref[...])
    ig = jax.nn.sigmoid(jnp.dot(xcb, wx_ref[...], preferred_element_type=F32) + bx_ref[...])
    after_gate_matmuls()
    nlam = -lam_ref[...]
    softplus = jnp.maximum(nlam, 0.0) + jnp.log(1.0 + jnp.exp(-jnp.abs(nlam)))
    log_a = -LRU_C * r * softplus
    a = jnp.exp(log_a)
    mult = jnp.sqrt(1.0 - a * a)
    b = xc * ig * mult

    n_groups = tt // SUBLANES
    a = a.reshape(n_groups, SUBLANES, a.shape[1])
    b = b.reshape(a.shape)
    in_group = lax.broadcasted_iota(jnp.int32, a.shape, 1)
    d = 1
    while d < SUBLANES:
        b = a * jnp.where(in_group >= d, pltpu.roll(b, d, 1), 0.0) + b
        a = a * jnp.where(in_group >= d, pltpu.roll(a, d, 1), 1.0)
        d *= 2
    carry = hcar_ref[n]
    groups = []
    for gi in range(n_groups):
        hg = b[gi] + a[gi] * carry
        carry = hg[SUBLANES - 1:SUBLANES, :]
        groups.append(hg)
    hcar_ref[n] = carry
    h = jnp.concatenate(groups, axis=0)
    y_ref[...] = (h * jax.nn.gelu(gate)).astype(y_ref.dtype)


def rglru_scan(x, g, w_in, conv_w, conv_b, wa, ba, wx, bx, lam, *, tt):
    t, d = x.shape
    width = conv_w.shape[1]
    nb, blk, _ = wa.shape
    n_steps = (t // tt) * nb
    proj_k = lambda k: jnp.minimum(k, n_steps - 1)
    scan_k = lambda k: jnp.maximum(k - 1, 0)
    vec = lambda: pl.BlockSpec((1, blk), lambda k: (0, scan_k(k) % nb))
    need = (2 * _nbytes((tt, d), F32) + _nbytes((tt, d), w_in.dtype) + 4 * _nbytes((d, blk), w_in.dtype)
            + 2 * _nbytes((tt, blk), MXU_DTYPE) + 4 * _nbytes((tt, blk), F32)
            + 4 * _nbytes((blk, blk), wa.dtype) + 24 * _nbytes((tt, blk), F32))
    return pl.pallas_call(
        functools.partial(_rglru_kernel, tt=tt, nb=nb, n_steps=n_steps),
        grid=(n_steps + 1,),
        in_specs=[
            pl.BlockSpec((tt, d), lambda k: (proj_k(k) // nb, 0)),
            pl.BlockSpec((1, d), lambda k: (0, 0)),
            pl.BlockSpec((d, blk), lambda k: (0, proj_k(k) % nb)),
            pl.BlockSpec((d, blk), lambda k: (0, nb + proj_k(k) % nb)),
            pl.BlockSpec((CONV_WIDTH, blk), lambda k: (0, scan_k(k) % nb)),
            vec(),
            pl.BlockSpec((None, blk, blk), lambda k: (scan_k(k) % nb, 0, 0)),
            vec(),
            pl.BlockSpec((None, blk, blk), lambda k: (scan_k(k) % nb, 0, 0)),
            vec(),
            vec(),
        ],
        out_specs=pl.BlockSpec((tt, blk), lambda k: (scan_k(k) // nb, scan_k(k) % nb)),
        out_shape=jax.ShapeDtypeStruct((t, width), MXU_DTYPE),
        scratch_shapes=[pltpu.VMEM((tt, d), w_in.dtype), pltpu.VMEM((2, 2, tt, blk), F32),
                        pltpu.VMEM((nb, 1, blk), F32), pltpu.VMEM((nb, SUBLANES, blk), F32)],
        compiler_params=pltpu.CompilerParams(
            dimension_semantics=("arbitrary",),
            vmem_limit_bytes=_vmem_limit(need)),
        name="rglru_scan",
    )(x, g.reshape(1, d), w_in, w_in, conv_w, conv_b.reshape(1, width), wa, ba.reshape(1, width),
      wx, bx.reshape(1, width), lam.reshape(1, width))


def rglru_layer(x, norm_g, w_in, conv_w, conv_b, wa, ba, wx, bx, lam, w_out, tiles):
    y = rglru_scan(x, norm_g, w_in.astype(MXU_DTYPE), conv_w, conv_b, wa.astype(MXU_DTYPE), ba,
                   wx.astype(MXU_DTYPE), bx, lam, tt=tiles["scan_tt"])
    return matmul_residual(y, w_out.astype(MXU_DTYPE), x, tm=tiles["res_tm"], tn=tiles["res_tn"])


POOL_HALO = max(POOL_WINDOWS)


def _pool_kernel(x_ref, g_ref, w_ref, b_ref, s_ref, o_ref, halo_ref, *, tt):
    blk = pl.program_id(0)

    @pl.when(blk == 0)
    def _():
        halo_ref[...] = jnp.zeros_like(halo_ref)

    x = x_ref[...]
    h = _rms_norm_rows(x, g_ref[...])
    ext = jnp.concatenate([halo_ref[...], h], axis=0)
    halo_ref[...] = h[tt - POOL_HALO:, :]

    gw = h.shape[1] // len(POOL_WINDOWS)
    t1 = (lax.broadcasted_iota(jnp.int32, (tt, gw), 0) + (blk * tt + 1)).astype(F32)
    for gi, win in enumerate(POOL_WINDOWS):
        cols = slice(gi * gw, (gi + 1) * gw)
        acc = ext[:, cols]
        d = 1
        while d < win:
            acc = acc + pltpu.roll(acc, d, 0)
            d *= 2
        mean = acc[POOL_HALO:, :] / jnp.minimum(t1, float(win))
        y = (mean - h[:, cols]).astype(w_ref.dtype)
        z = jnp.dot(y, w_ref[gi], preferred_element_type=F32) + b_ref[gi:gi + 1, :]
        o_ref[:, cols] = x[:, cols] + z * s_ref[:, cols]


def pool_layer(x, norm_g, w_group, b_group, scale, tiles):
    t, d = x.shape
    tt = tiles["pool_tt"]
    ng, gw, _ = w_group.shape
    need = (4 * _nbytes((tt, d), F32) + 2 * _nbytes((ng, gw, gw), MXU_DTYPE)
            + 6 * _nbytes((tt, d), F32))
    return pl.pallas_call(
        functools.partial(_pool_kernel, tt=tt),
        grid=(t // tt,),
        in_specs=[
            pl.BlockSpec((tt, d), lambda i: (i, 0)),
            pl.BlockSpec((1, d), lambda i: (0, 0)),
            pl.BlockSpec((ng, gw, gw), lambda i: (0, 0, 0)),
            pl.BlockSpec((ng, gw), lambda i: (0, 0)),
            pl.BlockSpec((1, d), lambda i: (0, 0)),
        ],
        out_specs=pl.BlockSpec((tt, d), lambda i: (i, 0)),
        out_shape=jax.ShapeDtypeStruct((t, d), F32),
        scratch_shapes=[pltpu.VMEM((POOL_HALO, d), F32)],
        compiler_params=pltpu.CompilerParams(
            dimension_semantics=("arbitrary",),
            vmem_limit_bytes=_vmem_limit(need)),
        name="pool_mixer",
    )(x, norm_g.reshape(1, d), w_group.astype(MXU_DTYPE), b_group, scale.reshape(1, d))


def _tiles(t):
    big = min(t, 512)
    tall = min(t, 1024)
    return {
        "proj_tm": tall, "proj_tn": 1024,
        "res_tm": tall, "res_tn": 1024,
        "tq": min(t, 256), "kt": min(4, t // min(t, 256)), "ka": min(4, t // min(t, 256)),
        "scan_tt": min(t, 1024),
        "pool_tt": min(t, 256),
        "mlp_tm": big, "mlp_tf": 1024,
    }


def kernel(x, positions, attn_norm, attn_w_in, attn_q_norm, attn_k_norm, attn_w_out, rnn_norm, rnn_w_in, rnn_conv_w, rnn_conv_b, rnn_gate_a_w, rnn_gate_a_b, rnn_gate_x_w, rnn_gate_x_b, rnn_lambda, rnn_w_out, pool_norm, pool_w, pool_b, pool_scale, mlp_norm, mlp_w_up, mlp_w_down):
    batch, t, d = x.shape
    depth = mlp_norm.shape[0]
    tiles = _tiles(t)
    w_up_all = mlp_w_up.astype(MXU_DTYPE)
    w_down_all = mlp_w_down.astype(MXU_DTYPE)
    outs = []
    for bi in range(batch):
        xb = x[bi]
        pos = positions[bi]
        for i in range(depth):
            kind, j = i % N_MIXERS, i // N_MIXERS
            if kind == 0:
                xb = dsa_layer(xb, pos, attn_norm[j], attn_w_in, j, attn_q_norm[j], attn_k_norm[j],
                               attn_w_out[j], tiles)
            elif kind == 1:
                xb = rglru_layer(xb, rnn_norm[j], rnn_w_in[j], rnn_conv_w[j], rnn_conv_b[j],
                                 rnn_gate_a_w[j], rnn_gate_a_b[j], rnn_gate_x_w[j], rnn_gate_x_b[j],
                                 rnn_lambda[j], rnn_w_out[j], tiles)
            else:
                xb = pool_layer(xb, pool_norm[j], pool_w[j], pool_b[j], pool_scale[j], tiles)
            xb = mlp_block(xb, mlp_norm[i], w_up_all, w_down_all, i,
                           tm=tiles["mlp_tm"], tf=tiles["mlp_tf"])
        outs.append(xb)
    return outs[0][None] if batch == 1 else jnp.stack(outs, axis=0)
```
